```python
import math
import jax, jax.numpy as jnp
from jax import lax
import numpy as np

D_MODEL = 1024
BATCH = 1
SEQ = 16384
DEPTH = 1

GDN_HEADS = 8
GDN_DK = 128
GDN_DV = 128
GDN_CONV = 4
GDN_CHUNK = 64
MB_HEADS = 8
MB_KV_HEADS = 2
MB_HD = 128
MB_BLOCK = 256
MB_TOPK = 3
MB_QCHUNK = 64
ROT_DIM = MB_HD // 4
ROPE_THETA = 500000.0
N_EXPERTS = 256
TOP_K = 8
N_GROUPS = 8
TOPK_GROUPS = 4
EXPERT_DIM = 256
SHARED_DIM = 256
ROUTED_SCALE = 2.5
MOE_BLOCK = 128
DEEPNORM_ALPHA = (2.0 * DEPTH) ** 0.25
DEEPNORM_BETA = (8.0 * DEPTH) ** -0.25
LN_EPS = 1e-5
MASK_NEG = -1e30

CONV_CH = 2 * GDN_HEADS * GDN_DK + GDN_HEADS * GDN_DV
SPLIT_SIZES = (CONV_CH, GDN_HEADS * GDN_DV, GDN_HEADS, GDN_HEADS,
               MB_HEADS * MB_HD, MB_KV_HEADS * MB_HD, MB_KV_HEADS * MB_HD,
               D_MODEL, D_MODEL)
IN_COLS = sum(SPLIT_SIZES)

kernel_name = 'hybrid_gdn_moba_moe_deepnorm'


def layer_norm(x, g, b):
    xf = x.astype(jnp.float32)
    mu = jnp.mean(xf, -1, keepdims=True)
    var = jnp.mean(jnp.square(xf - mu), -1, keepdims=True)
    return ((xf - mu) * lax.rsqrt(var + LN_EPS) * g + b).astype(x.dtype)


def l2norm(x):
    return x * lax.rsqrt(jnp.sum(x * x, -1, keepdims=True) + 1e-6)


def causal_short_conv(u, w):
    c = u.shape[-1]
    return lax.conv_general_dilated(u, w[:, None, :].astype(u.dtype), window_strides=(1,),
                                    padding=[(w.shape[0] - 1, 0)],
                                    dimension_numbers=('NWC', 'WIO', 'NWC'),
                                    feature_group_count=c)


def partial_rope(x, pos):
    half = ROT_DIM // 2
    inv = ROPE_THETA ** (-jnp.arange(half, dtype=jnp.float32) / half)
    ang = pos.astype(jnp.float32)[:, None] * inv[None, :]
    cos, sin = jnp.cos(ang), jnp.sin(ang)
    xr = x[..., :ROT_DIM].astype(jnp.float32)
    x1, x2 = xr[..., :half], xr[..., half:]
    rot = jnp.concatenate([x1 * cos - x2 * sin, x2 * cos + x1 * sin], -1).astype(x.dtype)
    return jnp.concatenate([rot, x[..., ROT_DIM:]], -1)


def gated_deltanet(q, k, v, beta, g):
    f32 = jnp.float32
    q, k, v, beta, g = (a.astype(f32) for a in (q, k, v, beta, g))
    B, H, T, dk = q.shape
    dv = v.shape[-1]
    C = GDN_CHUNK
    N = T // C
    q, k, v = (a.reshape(B, H, N, C, a.shape[-1]) for a in (q, k, v))
    beta = beta.reshape(B, H, N, C)
    gc = jnp.cumsum(g.reshape(B, H, N, C), axis=-1)
    idx = jnp.arange(C)
    tril = idx[:, None] >= idx[None, :]
    strict = idx[:, None] > idx[None, :]
    diff = gc[..., :, None] - gc[..., None, :]
    L = jnp.where(tril, jnp.exp(jnp.where(tril, diff, 0.0)), 0.0)
    kb = k * beta[..., None]
    A = jnp.einsum('bhncd,bhnsd->bhncs', kb, k) * L
    M = jnp.eye(C, dtype=f32) + jnp.where(strict, A, 0.0)
    rhs = jnp.concatenate([v * beta[..., None], kb * jnp.exp(gc)[..., None]], -1)
    sol = lax.linalg.triangular_solve(M, rhs, left_side=True, lower=True, unit_diagonal=True)
    u, w = sol[..., :dv], sol[..., dv:]
    qg = q * jnp.exp(gc)[..., None]
    kd = k * jnp.exp(gc[..., -1:] - gc)[..., None]
    glast = jnp.exp(gc[..., -1])
    Aqk = jnp.einsum('bhncd,bhnsd->bhncs', q, k) * L

    def step(S, inp):
        qg_i, w_i, u_i, kd_i, A_i, gl_i = inp
        v_new = u_i - jnp.einsum('bhcd,bhde->bhce', w_i, S)
        o = jnp.einsum('bhcd,bhde->bhce', qg_i, S) + jnp.einsum('bhcs,bhse->bhce', A_i, v_new)
        S = S * gl_i[..., None, None] + jnp.einsum('bhcd,bhce->bhde', kd_i, v_new)
        return S, o

    xs = tuple(jnp.moveaxis(a, 2, 0) for a in (qg, w, u, kd, Aqk, glast))
    S0 = jnp.zeros((B, H, dk, dv), f32)
    _, o = lax.scan(step, S0, xs)
    return jnp.moveaxis(o, 0, 2).reshape(B, H, T, dv)


def moba_attention(q, k, v):
    B, Hq, T, hd = q.shape
    G = k.shape[1]
    R = Hq // G
    Tp = -(-T // MB_BLOCK) * MB_BLOCK
    pad = ((0, 0), (0, 0), (0, Tp - T), (0, 0))
    q, k, v = jnp.pad(q, pad), jnp.pad(k, pad), jnp.pad(v, pad)
    NB = Tp // MB_BLOCK
    k_sel = min(MB_TOPK, NB)
    q = q.reshape(B, G, R, Tp, hd)
    kb = k.reshape(B, G, NB, MB_BLOCK, hd)
    vb = v.reshape(B, G, NB, MB_BLOCK, hd)
    kmean = jnp.mean(kb.astype(jnp.float32), axis=3)
    scale = hd ** -0.5
    bi = jnp.arange(B)[:, None, None, None, None]
    gi = jnp.arange(G)[None, :, None, None, None]
    Q = MB_QCHUNK

    def one_chunk(c):
        t0 = c * Q
        qc = lax.dynamic_slice_in_dim(q, t0, Q, axis=3)
        blk = t0 // MB_BLOCK
        gate = jnp.einsum('bgrqd,bgnd->bgrqn', qc.astype(jnp.float32), kmean)
        gate = jnp.where(jnp.arange(NB) < blk, gate, -jnp.inf)
        _, sel = lax.top_k(gate, k_sel)
        slot_ok = jnp.arange(k_sel) < blk
        ks = kb[bi, gi, sel]
        vs = vb[bi, gi, sel]
        s_sel = jnp.einsum('bgrqd,bgrqskd->bgrqsk', qc, ks).astype(jnp.float32) * scale
        s_sel = jnp.where(slot_ok[:, None], s_sel, MASK_NEG).reshape(B, G, R, Q, k_sel * MB_BLOCK)
        k_own = lax.dynamic_slice_in_dim(kb, blk, 1, axis=2)[:, :, 0]
        v_own = lax.dynamic_slice_in_dim(vb, blk, 1, axis=2)[:, :, 0]
        s_own = jnp.einsum('bgrqd,bgkd->bgrqk', qc, k_own).astype(jnp.float32) * scale
        qpos = t0 + jnp.arange(Q)
        kpos = blk * MB_BLOCK + jnp.arange(MB_BLOCK)
        s_own = jnp.where(kpos[None, :] <= qpos[:, None], s_own, MASK_NEG)
        p = jax.nn.softmax(jnp.concatenate([s_sel, s_own], -1), axis=-1).astype(v.dtype)
        p_sel = p[..., :k_sel * MB_BLOCK].reshape(B, G, R, Q, k_sel, MB_BLOCK)
        p_own = p[..., k_sel * MB_BLOCK:]
        return (jnp.einsum('bgrqsk,bgrqskd->bgrqd', p_sel, vs)
                + jnp.einsum('bgrqk,bgkd->bgrqd', p_own, v_own))

    out = lax.map(one_chunk, jnp.arange(Tp // Q))
    out = jnp.moveaxis(out, 0, 3).reshape(B, G, R, Tp, hd)[:, :, :, :T]
    return out.reshape(B, Hq, T, hd)


def token_mixer(x, w_in, conv_w, gdn_a_log, gdn_dt_bias, gdn_norm_w,
                w_gdn_proj, w_moba_proj, w_out):
    B, T, D = x.shape
    z = x @ w_in
    offs = np.cumsum(SPLIT_SIZES)[:-1].tolist()
    conv_in, gdn_gate, gdn_b, gdn_a, mb_q, mb_k, mb_v, gate_gdn, gate_mb = jnp.split(z, offs, -1)

    def heads(a, h):
        return a.reshape(B, T, h, -1).transpose(0, 2, 1, 3)

    c = jax.nn.silu(causal_short_conv(conv_in, conv_w))
    qk = GDN_HEADS * GDN_DK
    q, k, v = jnp.split(c, [qk, 2 * qk], -1)
    q = l2norm(heads(q, GDN_HEADS).astype(jnp.float32)) * (GDN_DK ** -0.5)
    k = l2norm(heads(k, GDN_HEADS).astype(jnp.float32))
    v = heads(v, GDN_HEADS)
    beta = jax.nn.sigmoid(gdn_b.astype(jnp.float32)).transpose(0, 2, 1)
    g = (-jnp.exp(gdn_a_log.astype(jnp.float32))
         * jax.nn.softplus(gdn_a.astype(jnp.float32) + gdn_dt_bias.astype(jnp.float32))).transpose(0, 2, 1)
    o = gated_deltanet(q, k, v, beta, g)
    o = o * lax.rsqrt(jnp.mean(o * o, -1, keepdims=True) + 1e-6) * gdn_norm_w.astype(jnp.float32)
    o = o.transpose(0, 2, 1, 3).reshape(B, T, GDN_HEADS * GDN_DV).astype(x.dtype) * jax.nn.silu(gdn_gate)
    y_gdn = o @ w_gdn_proj

    pos = jnp.arange(T)
    mq = partial_rope(heads(mb_q, MB_HEADS), pos)
    mk = partial_rope(heads(mb_k, MB_KV_HEADS), pos)
    mv = heads(mb_v, MB_KV_HEADS)
    a = moba_attention(mq, mk, mv)
    y_mb = a.transpose(0, 2, 1, 3).reshape(B, T, MB_HEADS * MB_HD) @ w_moba_proj

    m = jax.nn.sigmoid(gate_gdn) * y_gdn + jax.nn.sigmoid(gate_mb) * y_mb
    return m @ w_out


def moe_ffn(h, router_w, router_bias, exp_w_gate, exp_w_up, exp_w_down,
            sh_w_gate, sh_w_up, sh_w_down):
    B, T, D = h.shape
    N = B * T
    E = N_EXPERTS
    xt = h.reshape(N, D)
    scores = jax.nn.sigmoid((xt @ router_w).astype(jnp.float32))
    choice = scores + router_bias.astype(jnp.float32)
    group_score = lax.top_k(choice.reshape(N, N_GROUPS, E // N_GROUPS), 2)[0].sum(-1)
    _, gidx = lax.top_k(group_score, TOPK_GROUPS)
    gmask = jnp.any(gidx[:, :, None] == jnp.arange(N_GROUPS)[None, None, :], axis=1)
    choice = jnp.where(jnp.repeat(gmask, E // N_GROUPS, axis=1), choice, -jnp.inf)
    _, eidx = lax.top_k(choice, TOP_K)
    wts = jnp.take_along_axis(scores, eidx, axis=1)
    wts = wts / (jnp.sum(wts, -1, keepdims=True) + 1e-20) * ROUTED_SCALE

    NK = N * TOP_K
    e_flat = eidx.reshape(NK)
    tok_flat = jnp.arange(NK) // TOP_K
    w_flat = wts.reshape(NK)
    order = jnp.argsort(e_flat)
    e_s, tok_s, w_s = e_flat[order], tok_flat[order], w_flat[order]
    counts = jnp.bincount(e_flat, length=E)
    padded = (counts + MOE_BLOCK - 1) // MOE_BLOCK * MOE_BLOCK
    start = jnp.cumsum(counts) - counts
    pend = jnp.cumsum(padded)
    pstart = pend - padded
    dest = pstart[e_s] + jnp.arange(NK) - start[e_s]
    n_blocks = -(-NK // MOE_BLOCK) + E
    R = n_blocks * MOE_BLOCK
    row_tok = jnp.full((R,), N, jnp.int32).at[dest].set(tok_s.astype(jnp.int32))
    row_w = jnp.zeros((R,), jnp.float32).at[dest].set(w_s)
    blk_exp = jnp.minimum(jnp.searchsorted(pend, jnp.arange(n_blocks) * MOE_BLOCK, side='right'), E - 1)
    x_pad = jnp.concatenate([xt, jnp.zeros((1, D), xt.dtype)], 0)
    x_rows = x_pad[row_tok].reshape(n_blocks, MOE_BLOCK, D)

    def expert_block(args):
        xb, e = args
        hb = jax.nn.silu(xb @ exp_w_gate[e]) * (xb @ exp_w_up[e])
        return hb @ exp_w_down[e]

    y_rows = lax.map(expert_block, (x_rows, blk_exp)).reshape(R, D)
    routed = jax.ops.segment_sum(y_rows * row_w[:, None].astype(y_rows.dtype), row_tok,
                                 num_segments=N + 1)[:N]
    shared = (jax.nn.silu(xt @ sh_w_gate) * (xt @ sh_w_up)) @ sh_w_down
    return (routed + shared).reshape(B, T, D)


def setup_inputs(seed: int = 0) -> dict:
    key = jax.random.key(seed)
    ks = jax.random.split(key, 24)
    d = D_MODEL
    f32 = jnp.float32

    def nrm(k, shape, fan_in, scale=1.0):
        return jax.random.normal(k, shape, f32) * (scale * fan_in ** -0.5)

    dt = jnp.exp(jax.random.uniform(ks[4], (GDN_HEADS,), f32) * (math.log(0.1) - math.log(0.001))
                 + math.log(0.001))
    return {
        'x': jax.random.normal(ks[0], (BATCH, SEQ, d), f32),
        'w_in': nrm(ks[1], (d, IN_COLS), d),
        'conv_w': nrm(ks[2], (GDN_CONV, CONV_CH), GDN_CONV),
        'gdn_a_log': jnp.log(jax.random.uniform(ks[3], (GDN_HEADS,), f32, 1.0, 16.0)),
        'gdn_dt_bias': dt + jnp.log(-jnp.expm1(-dt)),
        'gdn_norm_w': 1.0 + 0.02 * jax.random.normal(ks[5], (GDN_DV,), f32),
        'w_gdn_proj': nrm(ks[6], (GDN_HEADS * GDN_DV, d), GDN_HEADS * GDN_DV, DEEPNORM_BETA),
        'w_moba_proj': nrm(ks[7], (MB_HEADS * MB_HD, d), MB_HEADS * MB_HD, DEEPNORM_BETA),
        'w_out': nrm(ks[8], (d, d), d, DEEPNORM_BETA),
        'ln1_g': 1.0 + 0.02 * jax.random.normal(ks[9], (d,), f32),
        'ln1_b': 0.02 * jax.random.normal(ks[10], (d,), f32),
        'router_w': nrm(ks[11], (d, N_EXPERTS), d),
        'router_bias': 0.01 * jax.random.normal(ks[12], (N_EXPERTS,), f32),
        'exp_w_gate': nrm(ks[13], (N_EXPERTS, d, EXPERT_DIM), d),
        'exp_w_up': nrm(ks[14], (N_EXPERTS, d, EXPERT_DIM), d),
        'exp_w_down': nrm(ks[15], (N_EXPERTS, EXPERT_DIM, d), EXPERT_DIM, DEEPNORM_BETA),
        'sh_w_gate': nrm(ks[16], (d, SHARED_DIM), d),
        'sh_w_up': nrm(ks[17], (d, SHARED_DIM), d),
        'sh_w_down': nrm(ks[18], (SHARED_DIM, d), SHARED_DIM, DEEPNORM_BETA),
        'ln2_g': 1.0 + 0.02 * jax.random.normal(ks[19], (d,), f32),
        'ln2_b': 0.02 * jax.random.normal(ks[20], (d,), f32),
    }


def reference(x, w_in, conv_w, gdn_a_log, gdn_dt_bias, gdn_norm_w, w_gdn_proj, w_moba_proj,
              w_out, ln1_g, ln1_b, router_w, router_bias, exp_w_gate, exp_w_up, exp_w_down,
              sh_w_gate, sh_w_up, sh_w_down, ln2_g, ln2_b):
    h = x
    for _ in range(DEPTH):
        mix = token_mixer(h, w_in, conv_w, gdn_a_log, gdn_dt_bias, gdn_norm_w,
                          w_gdn_proj, w_moba_proj, w_out)
        h = layer_norm(DEEPNORM_ALPHA * h + mix, ln1_g, ln1_b)
        ffn = moe_ffn(h, router_w, router_bias, exp_w_gate, exp_w_up, exp_w_down,
                      sh_w_gate, sh_w_up, sh_w_down)
        h = layer_norm(DEEPNORM_ALPHA * h + ffn, ln2_g, ln2_b)
    return h
```

```python
import functools
import math

import jax
import jax.numpy as jnp
from jax import lax
from jax.experimental import pallas as pl
from jax.experimental.pallas import tpu as pltpu

F32 = jnp.float32
BF16 = jnp.bfloat16

D_MODEL = 1024
DEPTH = 1
GDN_HEADS = 8
GDN_DK = 128
GDN_DV = 128
GDN_CONV = 4
GDN_CHUNK = 64
MB_HEADS = 8
MB_KV_HEADS = 2
MB_REP = MB_HEADS // MB_KV_HEADS
MB_HD = 128
MB_BLOCK = 256
MB_TOPK = 3
ROT_DIM = MB_HD // 4
ROPE_THETA = 500000.0
N_EXPERTS = 256
TOP_K = 8
N_GROUPS = 8
GROUP_SIZE = N_EXPERTS // N_GROUPS
TOPK_GROUPS = 4
EXPERT_DIM = 256
SHARED_DIM = 256
ROUTED_SCALE = 2.5
MOE_BLOCK = 128
DEEPNORM_ALPHA = (2.0 * DEPTH) ** 0.25
LN_EPS = 1e-5
MASK_NEG = -1e30

QK_COLS = GDN_HEADS * GDN_DK
CONV_CH = 2 * QK_COLS + GDN_HEADS * GDN_DV
LANES = 128
SUBLANES = 8
VMEM_LIMIT = 56 * 1024 * 1024

TOK_TILE = 256

PK_BETA, PK_GC, PK_EG, PK_EGD, PK_EGL = 0, 8, 16, 24, 32


def _sigmoid(x):
    return 1.0 / (1.0 + jnp.exp(-x))


def _softplus(x):
    return jnp.maximum(x, 0.0) + jnp.log(1.0 + jnp.exp(-jnp.abs(x)))


def _bdot(a, b):
    return jnp.dot(a.astype(BF16), b.astype(BF16), preferred_element_type=F32)


def _bdot_nt(a, b):
    return lax.dot_general(a.astype(BF16), b.astype(BF16), (((1,), (1,)), ((), ())),
                           preferred_element_type=F32)


def _bdot_tn(a, b):
    return lax.dot_general(a.astype(BF16), b.astype(BF16), (((0,), (0,)), ((), ())),
                           preferred_element_type=F32)


def _layer_norm(x, g, b):
    mu = jnp.mean(x, axis=-1, keepdims=True)
    xc = x - mu
    var = jnp.mean(xc * xc, axis=-1, keepdims=True)
    return xc * lax.rsqrt(var + LN_EPS) * g + b


def _const_spec(shape):
    nd = len(shape)
    return pl.BlockSpec(shape, lambda *_: (0,) * nd, pipeline_mode=pl.Buffered(1))


def _qkv_kernel(x_ref, w_ref, wp_ref, cw_ref, alog_ref, dtb_ref,
                q_ref, k_ref, v_ref, pack_ref, gct_ref, z_ref):
    tm = x_ref.shape[0]
    halo = SUBLANES

    @pl.when(pl.program_id(0) == 0)
    def _():
        z_ref[0:halo, :] = jnp.zeros((halo, CONV_CH), F32)

    xb = x_ref[...].astype(BF16)
    z_ref[halo:halo + tm, :] = jnp.dot(xb, w_ref[...], preferred_element_type=F32)

    for c in range(CONV_CH // LANES):
        cols = slice(c * LANES, (c + 1) * LANES)
        acc = z_ref[halo:halo + tm, cols] * cw_ref[GDN_CONV - 1:GDN_CONV, cols]
        for s in range(1, GDN_CONV):
            acc = acc + z_ref[halo - s:halo - s + tm, cols] * cw_ref[GDN_CONV - 1 - s:GDN_CONV - s, cols]
        y = acc * _sigmoid(acc)
        h = c % GDN_HEADS
        if c < GDN_HEADS:
            y = y * lax.rsqrt(jnp.sum(y * y, axis=-1, keepdims=True) + 1e-6) * (GDN_DK ** -0.5)
            q_ref[h] = y
        elif c < 2 * GDN_HEADS:
            y = y * lax.rsqrt(jnp.sum(y * y, axis=-1, keepdims=True) + 1e-6)
            k_ref[h] = y
        else:
            v_ref[h] = y

    z_ref[0:halo, :] = z_ref[tm:tm + halo, :]

    zp = jnp.dot(xb, wp_ref[...], preferred_element_type=F32)
    beta = _sigmoid(zp)
    g = -jnp.exp(alog_ref[...]) * _softplus(zp + dtb_ref[...])
    row = lax.broadcasted_iota(jnp.int32, (tm, LANES), 0) % GDN_CHUNK
    gc = g
    step = 1
    while step < GDN_CHUNK:
        gc = gc + jnp.where(row >= step, pltpu.roll(gc, step, axis=0), 0.0)
        step *= 2
    gl = gc.reshape(tm // GDN_CHUNK, GDN_CHUNK, LANES)[:, GDN_CHUNK - 1:GDN_CHUNK, :]
    gl = jnp.broadcast_to(gl, (tm // GDN_CHUNK, GDN_CHUNK, LANES)).reshape(tm, LANES)
    lane = lax.broadcasted_iota(jnp.int32, (tm, LANES), 1)
    pack = jnp.where(lane < PK_GC, beta,
           jnp.where(lane < PK_EG, gc,
           jnp.where(lane < PK_EGD, jnp.exp(gc),
           jnp.where(lane < PK_EGL, jnp.exp(gl - gc), jnp.exp(gl)))))
    pack_ref[...] = pack
    gct_ref[...] = pack.T[PK_GC:PK_GC + GDN_HEADS, :]


def _qkv_proj(x2, w_qkv, w_pack, conv_w, alog_row, dtb_row):
    T = x2.shape[0]
    tm = TOK_TILE
    hd_spec = pl.BlockSpec((GDN_HEADS, tm, LANES), lambda i: (0, i, 0))
    hd_shape = jax.ShapeDtypeStruct((GDN_HEADS, T, LANES), F32)
    return pl.pallas_call(
        _qkv_kernel,
        grid=(T // tm,),
        in_specs=[
            pl.BlockSpec((tm, D_MODEL), lambda i: (i, 0)),
            _const_spec((D_MODEL, CONV_CH)),
            _const_spec((D_MODEL, LANES)),
            _const_spec((GDN_CONV, CONV_CH)),
            _const_spec((1, LANES)),
            _const_spec((1, LANES)),
        ],
        out_specs=[hd_spec, hd_spec, hd_spec,
                   pl.BlockSpec((tm, LANES), lambda i: (i, 0)),
                   pl.BlockSpec((GDN_HEADS, tm), lambda i: (0, i))],
        out_shape=[hd_shape, hd_shape, hd_shape,
                   jax.ShapeDtypeStruct((T, LANES), F32),
                   jax.ShapeDtypeStruct((GDN_HEADS, T), F32)],
        scratch_shapes=[pltpu.VMEM((tm + SUBLANES, CONV_CH), F32)],
        compiler_params=pltpu.CompilerParams(dimension_semantics=("arbitrary",),
                                             vmem_limit_bytes=VMEM_LIMIT),
        name="qkv_proj",
    )(x2, w_qkv, w_pack, conv_w, alog_row, dtb_row)


def _rope(xh, cos_t, sin_t, lane):
    half = ROT_DIM // 2
    swapped = jnp.where(lane < half, pltpu.roll(xh, LANES - half, axis=1), pltpu.roll(xh, half, axis=1))
    return xh * cos_t + swapped * sin_t


def _moba_proj_kernel(x_ref, w_ref, cos_ref, sin_ref, q_ref, k_ref, vt_ref, km_ref):
    tm = x_ref.shape[0]
    z = jnp.dot(x_ref[...].astype(BF16), w_ref[...], preferred_element_type=F32)
    cos_t = cos_ref[...]
    sin_t = sin_ref[...]
    lane = lax.broadcasted_iota(jnp.int32, (tm, LANES), 1)
    for h in range(MB_HEADS):
        q_ref[h] = _rope(z[:, h * MB_HD:(h + 1) * MB_HD], cos_t, sin_t, lane)
    koff = MB_HEADS * MB_HD
    voff = koff + MB_KV_HEADS * MB_HD
    for g in range(MB_KV_HEADS):
        kr = _rope(z[:, koff + g * MB_HD:koff + (g + 1) * MB_HD], cos_t, sin_t, lane)
        k_ref[0, g] = kr.astype(BF16)
        km_ref[0, g:g + 1, :] = jnp.mean(kr, axis=0, keepdims=True)
        vt_ref[0, g] = z[:, voff + g * MB_HD:voff + (g + 1) * MB_HD].T.astype(BF16)


def _moba_proj(x2, w_mb, cos_t, sin_t):
    T = x2.shape[0]
    tm = MB_BLOCK
    nb = T // tm
    return pl.pallas_call(
        _moba_proj_kernel,
        grid=(nb,),
        in_specs=[
            pl.BlockSpec((tm, D_MODEL), lambda i: (i, 0)),
            _const_spec(w_mb.shape),
            pl.BlockSpec((tm, LANES), lambda i: (i, 0)),
            pl.BlockSpec((tm, LANES), lambda i: (i, 0)),
        ],
        out_specs=[
            pl.BlockSpec((MB_HEADS, tm, MB_HD), lambda i: (0, i, 0)),
            pl.BlockSpec((1, MB_KV_HEADS, tm, MB_HD), lambda i: (i, 0, 0, 0)),
            pl.BlockSpec((1, MB_KV_HEADS, MB_HD, tm), lambda i: (i, 0, 0, 0)),
            pl.BlockSpec((1, MB_KV_HEADS, MB_HD), lambda i: (i, 0, 0)),
        ],
        out_shape=[
            jax.ShapeDtypeStruct((MB_HEADS, T, MB_HD), F32),
            jax.ShapeDtypeStruct((nb, MB_KV_HEADS, tm, MB_HD), BF16),
            jax.ShapeDtypeStruct((nb, MB_KV_HEADS, MB_HD, tm), BF16),
            jax.ShapeDtypeStruct((nb, MB_KV_HEADS, MB_HD), F32),
        ],
        compiler_params=pltpu.CompilerParams(dimension_semantics=("arbitrary",),
                                             vmem_limit_bytes=VMEM_LIMIT),
        name="moba_proj",
    )(x2, w_mb, cos_t, sin_t)


def _unit_lower_inverse(a):
    c = a.shape[0]
    rr = lax.broadcasted_iota(jnp.int32, (c, c), 0)
    cc = lax.broadcasted_iota(jnp.int32, (c, c), 1)
    eye = jnp.where(rr == cc, 1.0, 0.0).astype(F32)
    inv = eye - a
    p = _bdot(a, a)
    n = 2
    while True:
        inv = inv + _bdot(inv, p)
        n *= 2
        if n >= c:
            break
        p = _bdot(p, p)
    return inv


def _gdn_kernel(q_ref, k_ref, v_ref, pack_ref, gct_ref, nw_ref, o_ref, s_ref):
    C = GDN_CHUNK

    @pl.when(pl.program_id(0) == 0)
    def _():
        s_ref[...] = jnp.zeros_like(s_ref)

    rr = lax.broadcasted_iota(jnp.int32, (C, C), 0)
    cc = lax.broadcasted_iota(jnp.int32, (C, C), 1)
    tril = rr >= cc
    strict = rr > cc
    pack = pack_ref[...]
    nw = nw_ref[...]
    for h in range(GDN_HEADS):
        qh, kh, vh = q_ref[h], k_ref[h], v_ref[h]
        beta = pack[:, PK_BETA + h:PK_BETA + h + 1]
        gcc = pack[:, PK_GC + h:PK_GC + h + 1]
        eg = pack[:, PK_EG + h:PK_EG + h + 1]
        egd = pack[:, PK_EGD + h:PK_EGD + h + 1]
        egl = pack[0:1, PK_EGL + h:PK_EGL + h + 1]
        gcr = gct_ref[0, h:h + 1, :]
        kb = kh * beta
        decay = jnp.where(tril, jnp.exp(jnp.where(tril, gcc - gcr, 0.0)), 0.0)
        a = jnp.where(strict, _bdot_nt(kb, kh) * decay, 0.0)
        aqk = _bdot_nt(qh, kh) * decay
        tinv = _unit_lower_inverse(a)
        sol = _bdot(tinv, jnp.concatenate([vh * beta, kb * eg], axis=1))
        u, w = sol[:, :GDN_DV], sol[:, GDN_DV:]
        s = s_ref[h]
        v_new = u - _bdot(w, s)
        o = _bdot(qh * eg, s) + _bdot(aqk, v_new)
        s_ref[h] = s * egl + _bdot_tn(kh * egd, v_new)
        o = o * lax.rsqrt(jnp.mean(o * o, axis=-1, keepdims=True) + 1e-6) * nw
        o_ref[:, h * GDN_DV:(h + 1) * GDN_DV] = o


def _gdn(q, k, v, pack, gct3, norm_w_row):
    T = q.shape[1]
    C = GDN_CHUNK
    hd_spec = pl.BlockSpec((GDN_HEADS, C, LANES), lambda i: (0, i, 0))
    return pl.pallas_call(
        _gdn_kernel,
        grid=(T // C,),
        in_specs=[hd_spec, hd_spec, hd_spec,
                  pl.BlockSpec((C, LANES), lambda i: (i, 0)),
                  pl.BlockSpec((1, GDN_HEADS, C), lambda i: (i, 0, 0)),
                  _const_spec((1, GDN_DV))],
        out_specs=pl.BlockSpec((C, GDN_HEADS * GDN_DV), lambda i: (i, 0)),
        out_shape=jax.ShapeDtypeStruct((T, GDN_HEADS * GDN_DV), F32),
        scratch_shapes=[pltpu.VMEM((GDN_HEADS, GDN_DK, GDN_DV), F32)],
        compiler_params=pltpu.CompilerParams(dimension_semantics=("arbitrary",)),
        name="gdn",
    )(q, k, v, pack, gct3, norm_w_row)


def _moba_kernel(q_ref, k_ref, vt_ref, km_ref, o_ref, sel_ref, acc_ref, m_ref, l_ref):
    qi = pl.program_id(1)
    nb = k_ref.shape[0]
    bs = MB_BLOCK
    nq = MB_REP * bs
    scale = MB_HD ** -0.5

    qf = q_ref[...].reshape(nq, MB_HD)
    qb = qf.astype(BF16)

    gate = lax.dot_general(km_ref[...], qf, (((1,), (1,)), ((), ())),
                           precision=lax.Precision.HIGHEST, preferred_element_type=F32)
    blk = lax.broadcasted_iota(jnp.int32, (nb, nq), 0).astype(F32)
    qif = qi.astype(F32)
    gate = jnp.where(blk < qif, gate, -jnp.inf)
    sel = jnp.zeros((nb, nq), F32)
    for _ in range(MB_TOPK):
        top = jnp.max(gate, axis=0, keepdims=True)
        first = jnp.min(jnp.where(gate == top, blk, float(nb)), axis=0, keepdims=True)
        hit = (blk == first) & (top > -jnp.inf)
        sel = jnp.where(hit, 1.0, sel)
        gate = jnp.where(hit, -jnp.inf, gate)
    sel_ref[...] = sel

    st = _bdot_nt(k_ref[qi], qb) * scale
    kpos = lax.broadcasted_iota(jnp.int32, (bs, nq), 0)
    qpos = lax.broadcasted_iota(jnp.int32, (bs, nq), 1) % bs
    st = jnp.where(kpos <= qpos, st, MASK_NEG)
    m0 = jnp.max(st, axis=0, keepdims=True)
    p = jnp.exp(st - m0)
    m_ref[...] = m0
    l_ref[...] = jnp.sum(p, axis=0, keepdims=True)
    acc_ref[...] = jnp.dot(vt_ref[qi], p.astype(BF16), preferred_element_type=F32)

    def body(j, carry):
        st = _bdot_nt(k_ref[j], qb) * scale
        st = jnp.where(sel_ref[pl.ds(j, 1), :] > 0.0, st, MASK_NEG)
        m_old = m_ref[...]
        m_new = jnp.maximum(m_old, jnp.max(st, axis=0, keepdims=True))
        alpha = jnp.exp(m_old - m_new)
        p = jnp.exp(st - m_new)
        m_ref[...] = m_new
        l_ref[...] = alpha * l_ref[...] + jnp.sum(p, axis=0, keepdims=True)
        acc_ref[...] = alpha * acc_ref[...] + jnp.dot(vt_ref[j], p.astype(BF16),
                                                      preferred_element_type=F32)
        return carry

    lax.fori_loop(0, qi, body, 0)

    out_t = acc_ref[...] / l_ref[...]
    for r in range(MB_REP):
        o_ref[:, r * MB_HD:(r + 1) * MB_HD] = out_t[:, r * bs:(r + 1) * bs].T


def _moba(mq, mk, mvt, kmean):
    T = mq.shape[1]
    bs = MB_BLOCK
    nb = T // bs
    nq = MB_REP * bs
    return pl.pallas_call(
        _moba_kernel,
        grid=(MB_KV_HEADS, nb),
        in_specs=[
            pl.BlockSpec((MB_REP, bs, MB_HD), lambda g, i: (g, i, 0)),
            pl.BlockSpec((nb, None, bs, MB_HD), lambda g, i: (0, g, 0, 0)),
            pl.BlockSpec((nb, None, MB_HD, bs), lambda g, i: (0, g, 0, 0)),
            pl.BlockSpec((None, nb, MB_HD), lambda g, i: (g, 0, 0)),
        ],
        out_specs=pl.BlockSpec((bs, MB_REP * MB_HD), lambda g, i: (i, g)),
        out_shape=jax.ShapeDtypeStruct((T, MB_HEADS * MB_HD), F32),
        scratch_shapes=[pltpu.VMEM((nb, nq), F32), pltpu.VMEM((MB_HD, nq), F32),
                        pltpu.VMEM((1, nq), F32), pltpu.VMEM((1, nq), F32)],
        compiler_params=pltpu.CompilerParams(dimension_semantics=("arbitrary", "arbitrary"),
                                             vmem_limit_bytes=VMEM_LIMIT),
        name="moba",
    )(mq, mk, mvt, kmean)


def _post_kernel(x_ref, go_ref, ma_ref, wg_ref, wgp_ref, wmp_ref, wo_ref, g1_ref, b1_ref,
                 rwt_ref, rb_ref, sg_ref, su_ref, sd_ref,
                 h_ref, res_ref, eidx_ref, wts_ref):
    tm = x_ref.shape[0]
    x = x_ref[...]
    zg = jnp.dot(x.astype(BF16), wg_ref[...], preferred_element_type=F32)
    gate = zg[:, :D_MODEL]
    o = go_ref[...] * (gate * _sigmoid(gate))
    y_gdn = jnp.dot(o.astype(BF16), wgp_ref[...], preferred_element_type=F32)
    y_mb = jnp.dot(ma_ref[...].astype(BF16), wmp_ref[...], preferred_element_type=F32)
    m = _sigmoid(zg[:, D_MODEL:2 * D_MODEL]) * y_gdn + _sigmoid(zg[:, 2 * D_MODEL:]) * y_mb
    mix = jnp.dot(m.astype(BF16), wo_ref[...], preferred_element_type=F32)
    h = _layer_norm(DEEPNORM_ALPHA * x + mix, g1_ref[...], b1_ref[...])
    h_ref[...] = h
    hb = h.astype(BF16)

    hs = jnp.dot(hb, sg_ref[...], preferred_element_type=F32)
    hs = hs * _sigmoid(hs) * jnp.dot(hb, su_ref[...], preferred_element_type=F32)
    res_ref[...] = DEEPNORM_ALPHA * h + jnp.dot(hs.astype(BF16), sd_ref[...], preferred_element_type=F32)

    logits = lax.dot_general(rwt_ref[...], h, (((1,), (1,)), ((), ())),
                             precision=lax.Precision.HIGHEST, preferred_element_type=F32)
    scores = _sigmoid(logits)
    choice = scores + rb_ref[...]
    neg = -jnp.inf
    gi = lax.broadcasted_iota(jnp.int32, (GROUP_SIZE, tm), 0).astype(F32)
    gscore = []
    for g in range(N_GROUPS):
        cg = choice[g * GROUP_SIZE:(g + 1) * GROUP_SIZE, :]
        m1 = jnp.max(cg, axis=0, keepdims=True)
        i1 = jnp.min(jnp.where(cg == m1, gi, float(GROUP_SIZE)), axis=0, keepdims=True)
        m2 = jnp.max(jnp.where(gi == i1, neg, cg), axis=0, keepdims=True)
        gscore.append(m1 + m2)
    gs = jnp.concatenate(gscore, axis=0)
    gidx = lax.broadcasted_iota(jnp.int32, (N_GROUPS, tm), 0).astype(F32)
    gsel = jnp.zeros((N_GROUPS, tm), F32)
    for _ in range(TOPK_GROUPS):
        top = jnp.max(gs, axis=0, keepdims=True)
        first = jnp.min(jnp.where(gs == top, gidx, float(N_GROUPS)), axis=0, keepdims=True)
        hit = gidx == first
        gsel = jnp.where(hit, 1.0, gsel)
        gs = jnp.where(hit, neg, gs)
    masked = jnp.concatenate(
        [jnp.where(gsel[g:g + 1, :] > 0.0, choice[g * GROUP_SIZE:(g + 1) * GROUP_SIZE, :], neg)
         for g in range(N_GROUPS)], axis=0)
    ei = lax.broadcasted_iota(jnp.int32, (N_EXPERTS, tm), 0).astype(F32)
    idx_rows, w_rows = [], []
    for _ in range(TOP_K):
        top = jnp.max(masked, axis=0, keepdims=True)
        first = jnp.min(jnp.where(masked == top, ei, float(N_EXPERTS)), axis=0, keepdims=True)
        hit = ei == first
        idx_rows.append(first)
        w_rows.append(jnp.sum(jnp.where(hit, scores, 0.0), axis=0, keepdims=True))
        masked = jnp.where(hit, neg, masked)
    w = jnp.concatenate(w_rows, axis=0)
    w = w / (jnp.sum(w, axis=0, keepdims=True) + 1e-20) * ROUTED_SCALE
    eidx_ref[...] = jnp.concatenate(idx_rows, axis=0).astype(jnp.int32)
    wts_ref[...] = w


def _post(x2, gdn_o, moba_a, w_gates, w_gdn_proj, w_moba_proj, w_out, ln_g, ln_b,
          router_wt, router_b_col, sh_g, sh_u, sh_d):
    T = x2.shape[0]
    tm = TOK_TILE
    row_spec = pl.BlockSpec((tm, D_MODEL), lambda i: (i, 0))
    consts = [w_gates, w_gdn_proj, w_moba_proj, w_out, ln_g, ln_b, router_wt, router_b_col,
              sh_g, sh_u, sh_d]
    return pl.pallas_call(
        _post_kernel,
        grid=(T // tm,),
        in_specs=[row_spec, row_spec, row_spec] + [_const_spec(c.shape) for c in consts],
        out_specs=[row_spec, row_spec,
                   pl.BlockSpec((TOP_K, tm), lambda i: (0, i)),
                   pl.BlockSpec((TOP_K, tm), lambda i: (0, i))],
        out_shape=[jax.ShapeDtypeStruct((T, D_MODEL), F32),
                   jax.ShapeDtypeStruct((T, D_MODEL), F32),
                   jax.ShapeDtypeStruct((TOP_K, T), jnp.int32),
                   jax.ShapeDtypeStruct((TOP_K, T), F32)],
        compiler_params=pltpu.CompilerParams(dimension_semantics=("arbitrary",),
                                             vmem_limit_bytes=VMEM_LIMIT),
        name="post",
    )(x2, gdn_o, moba_a, *consts)


def _experts_kernel(blk_exp_ref, nvalid_ref, aid_ref, aidn_ref, h_ref, wg_ref, wu_ref, wd_ref,
                    out_ref, xbuf, ybuf, gsem, ssem, *, n_tok):
    b = pl.program_id(0)
    nblk = pl.num_programs(0)
    slot = b % 2
    nv = nvalid_ref[b]
    nv_next = jnp.where(b + 1 < nblk, nvalid_ref[jnp.minimum(b + 1, nblk - 1)], 0)
    nv_prev = jnp.where(b >= 1, nvalid_ref[jnp.maximum(b - 1, 0)], 0)
    nv_prev2 = jnp.where(b >= 2, nvalid_ref[jnp.maximum(b - 2, 0)], 0)

    def gather_copy(tok, i, s):
        return pltpu.make_async_copy(h_ref.at[pl.ds(tok, 1)], xbuf.at[s, pl.ds(i, 1)], gsem.at[s])

    def scatter_copy(dst, i, s):
        return pltpu.make_async_copy(ybuf.at[s, pl.ds(i, 1)], out_ref.at[pl.ds(dst, 1)], ssem.at[s])

    def start_gather(ids_ref, n, s):
        def issue(i, c):
            gather_copy(ids_ref[0, 0, i] // TOP_K, i, s).start()
            return c
        lax.fori_loop(0, n, issue, 0)

    def wait_rows(copy_fn, n, s):
        def w(i, c):
            copy_fn(0, i, s).wait()
            return c
        lax.fori_loop(0, n, w, 0)

    @pl.when(b == 0)
    def _():
        xbuf[...] = jnp.zeros_like(xbuf)
        start_gather(aid_ref, nv, 0)

    start_gather(aidn_ref, nv_next, 1 - slot)
    wait_rows(scatter_copy, nv_prev2, slot)

    @pl.when(nv > 0)
    def _():
        wait_rows(gather_copy, nv, slot)
        xb = xbuf[slot].astype(BF16)
        hg = jnp.dot(xb, wg_ref[...].astype(BF16), preferred_element_type=F32)
        hu = jnp.dot(xb, wu_ref[...].astype(BF16), preferred_element_type=F32)
        hb = (hg * _sigmoid(hg) * hu).astype(BF16)
        ybuf[slot] = jnp.dot(hb, wd_ref[...].astype(BF16), preferred_element_type=F32)

        def issue(i, c):
            aid = aid_ref[0, 0, i]
            scatter_copy((aid % TOP_K) * n_tok + aid // TOP_K, i, slot).start()
            return c
        lax.fori_loop(0, nv, issue, 0)

    @pl.when(b == nblk - 1)
    def _():
        wait_rows(scatter_copy, nv_prev, 1 - slot)
        wait_rows(scatter_copy, nv, slot)


def _experts(h, row_aid3, blk_exp, nvalid, wg, wu, wd):
    T = h.shape[0]
    nblk = row_aid3.shape[0]
    rows = MOE_BLOCK
    grid_spec = pltpu.PrefetchScalarGridSpec(
        num_scalar_prefetch=2,
        grid=(nblk,),
        in_specs=[
            pl.BlockSpec((1, 1, rows), lambda b, be, nu: (b, 0, 0), memory_space=pltpu.SMEM),
            pl.BlockSpec((1, 1, rows), lambda b, be, nu: (jnp.minimum(b + 1, nblk - 1), 0, 0),
                         memory_space=pltpu.SMEM),
            pl.BlockSpec(memory_space=pl.ANY),
            pl.BlockSpec((None, D_MODEL, EXPERT_DIM), lambda b, be, nu: (be[b], 0, 0)),
            pl.BlockSpec((None, D_MODEL, EXPERT_DIM), lambda b, be, nu: (be[b], 0, 0)),
            pl.BlockSpec((None, EXPERT_DIM, D_MODEL), lambda b, be, nu: (be[b], 0, 0)),
        ],
        out_specs=pl.BlockSpec(memory_space=pl.ANY),
        scratch_shapes=[pltpu.VMEM((2, rows, D_MODEL), F32), pltpu.VMEM((2, rows, D_MODEL), F32),
                        pltpu.SemaphoreType.DMA((2,)), pltpu.SemaphoreType.DMA((2,))],
    )
    return pl.pallas_call(
        functools.partial(_experts_kernel, n_tok=T),
        grid_spec=grid_spec,
        out_shape=jax.ShapeDtypeStruct((TOP_K * T, D_MODEL), F32),
        compiler_params=pltpu.CompilerParams(dimension_semantics=("arbitrary",),
                                             vmem_limit_bytes=VMEM_LIMIT),
        name="experts",
    )(blk_exp, nvalid, row_aid3, row_aid3, h, wg, wu, wd)


def _combine_kernel(res_ref, y_ref, w_ref, g_ref, b_ref, o_ref):
    acc = res_ref[...]
    w = w_ref[...]
    for k in range(TOP_K):
        acc = acc + w[:, k:k + 1] * y_ref[k]
    o_ref[...] = _layer_norm(acc, g_ref[...], b_ref[...])


def _combine(res, y3, wts, ln_g, ln_b):
    T = res.shape[0]
    tm = 128
    return pl.pallas_call(
        _combine_kernel,
        grid=(T // tm,),
        in_specs=[pl.BlockSpec((tm, D_MODEL), lambda i: (i, 0)),
                  pl.BlockSpec((TOP_K, tm, D_MODEL), lambda i: (0, i, 0)),
                  pl.BlockSpec((tm, TOP_K), lambda i: (i, 0)),
                  _const_spec((1, D_MODEL)), _const_spec((1, D_MODEL))],
        out_specs=pl.BlockSpec((tm, D_MODEL), lambda i: (i, 0)),
        out_shape=jax.ShapeDtypeStruct((T, D_MODEL), F32),
        compiler_params=pltpu.CompilerParams(dimension_semantics=("arbitrary",)),
        name="combine",
    )(res, y3, wts, ln_g, ln_b)


def _dispatch_plan(eidx_t, n_tok):
    nk = n_tok * TOP_K
    e_flat = eidx_t.T.reshape(nk)
    order = jnp.argsort(e_flat).astype(jnp.int32)
    e_s = e_flat[order]
    counts = jnp.bincount(e_flat, length=N_EXPERTS).astype(jnp.int32)
    padded = (counts + MOE_BLOCK - 1) // MOE_BLOCK * MOE_BLOCK
    start = jnp.cumsum(counts) - counts
    pend = jnp.cumsum(padded)
    pstart = pend - padded
    dest = pstart[e_s] + jnp.arange(nk, dtype=jnp.int32) - start[e_s]
    nblk = -(-nk // MOE_BLOCK) + N_EXPERTS
    row_aid = jnp.zeros((nblk * MOE_BLOCK,), jnp.int32).at[dest].set(order)
    blk_row0 = jnp.arange(nblk, dtype=jnp.int32) * MOE_BLOCK
    blk_exp = jnp.minimum(jnp.searchsorted(pend, blk_row0, side='right'), N_EXPERTS - 1).astype(jnp.int32)
    nvalid = jnp.clip(counts[blk_exp] - (blk_row0 - pstart[blk_exp]), 0, MOE_BLOCK)
    nvalid = jnp.where(blk_row0 < pend[-1], nvalid, 0).astype(jnp.int32)
    return row_aid.reshape(nblk, 1, MOE_BLOCK), blk_exp, nvalid


def kernel(x, w_in, conv_w, gdn_a_log, gdn_dt_bias, gdn_norm_w, w_gdn_proj, w_moba_proj, w_out,
           ln1_g, ln1_b, router_w, router_bias, exp_w_gate, exp_w_up, exp_w_down,
           sh_w_gate, sh_w_up, sh_w_down, ln2_g, ln2_b):
    B, T, D = x.shape
    assert B == 1 and D == D_MODEL and T % MB_BLOCK == 0
    x2 = x.reshape(T, D)

    o_gate = CONV_CH
    o_b = o_gate + GDN_HEADS * GDN_DV
    o_a = o_b + GDN_HEADS
    o_mq = o_a + GDN_HEADS
    o_gg = o_mq + (MB_HEADS + 2 * MB_KV_HEADS) * MB_HD
    w_qkv = w_in[:, :CONV_CH].astype(BF16)
    w_pack = jnp.concatenate([w_in[:, o_b:o_a]] + [w_in[:, o_a:o_mq]] * 4
                             + [jnp.zeros((D, LANES - 5 * GDN_HEADS), F32)], axis=1).astype(BF16)
    w_mb = w_in[:, o_mq:o_gg].astype(BF16)
    w_gates = jnp.concatenate([w_in[:, o_gate:o_b], w_in[:, o_gg:]], axis=1).astype(BF16)

    def lane_row(v):
        return jnp.zeros((1, LANES), F32).at[0, PK_GC:PK_GC + 4 * GDN_HEADS].set(jnp.tile(v.astype(F32), 4))

    q, k, v, pack, gct = _qkv_proj(x2, w_qkv, w_pack, conv_w.astype(F32),
                                   lane_row(gdn_a_log), lane_row(gdn_dt_bias))
    gct3 = gct.reshape(GDN_HEADS, T // GDN_CHUNK, GDN_CHUNK).transpose(1, 0, 2)
    gdn_o = _gdn(q, k, v, pack, gct3, gdn_norm_w.astype(F32).reshape(1, GDN_DV))

    half = ROT_DIM // 2
    inv = ROPE_THETA ** (-jnp.arange(half, dtype=F32) / half)
    ang = jnp.arange(T).astype(F32)[:, None] * inv[None, :]
    ones = jnp.ones((T, MB_HD - ROT_DIM), F32)
    cos_t = jnp.concatenate([jnp.cos(ang), jnp.cos(ang), ones], axis=1)
    sin_t = jnp.concatenate([-jnp.sin(ang), jnp.sin(ang), 0.0 * ones], axis=1)
    mq, mk, mvt, kmean = _moba_proj(x2, w_mb, cos_t, sin_t)
    moba_a = _moba(mq, mk, mvt, kmean.transpose(1, 0, 2))

    h, res, eidx_t, wts_t = _post(
        x2, gdn_o, moba_a, w_gates, w_gdn_proj.astype(BF16), w_moba_proj.astype(BF16),
        w_out.astype(BF16), ln1_g.reshape(1, D), ln1_b.reshape(1, D),
        router_w.T, router_bias.reshape(N_EXPERTS, 1),
        sh_w_gate.astype(BF16), sh_w_up.astype(BF16), sh_w_down.astype(BF16))

    row_aid3, blk_exp, nvalid = _dispatch_plan(eidx_t, T)
    y = _experts(h, row_aid3, blk_exp, nvalid, exp_w_gate, exp_w_up, exp_w_down)
    y3 = y.reshape(TOP_K, T, D)
    out = _combine(res, y3, wts_t.T, ln2_g.reshape(1, D), ln2_b.reshape(1, D))
    return out.reshape(B, T, D)
```

```python
import functools
import math

import jax
import jax.numpy as jnp
from jax import lax
from jax.experimental import pallas as pl
from jax.experimental.pallas import tpu as pltpu

F32 = jnp.float32
BF16 = jnp.bfloat16

D_MODEL = 1024
DEPTH = 1
GDN_HEADS = 8
GDN_DK = 128
GDN_DV = 128
GDN_CONV = 4
GDN_CHUNK = 64
MB_HEADS = 8
MB_KV_HEADS = 2
MB_REP = MB_HEADS // MB_KV_HEADS
MB_HD = 128
MB_BLOCK = 256
MB_TOPK = 3
ROT_DIM = MB_HD // 4
ROPE_THETA = 500000.0
N_EXPERTS = 256
TOP_K = 8
N_GROUPS = 8
GROUP_SIZE = N_EXPERTS // N_GROUPS
TOPK_GROUPS = 4
EXPERT_DIM = 256
SHARED_DIM = 256
ROUTED_SCALE = 2.5
MOE_BLOCK = 128
DEEPNORM_ALPHA = (2.0 * DEPTH) ** 0.25
LN_EPS = 1e-5
MASK_NEG = -1e30

QK_COLS = GDN_HEADS * GDN_DK
CONV_CH = 2 * QK_COLS + GDN_HEADS * GDN_DV
LANES = 128
SUBLANES = 8
VMEM_LIMIT = 56 * 1024 * 1024

TOK_TILE = 256

PK_BETA, PK_GC, PK_EG, PK_EGD, PK_EGL = 0, 8, 16, 24, 32


def _sigmoid(x):
    return 1.0 / (1.0 + jnp.exp(-x))


def _softplus(x):
    return jnp.maximum(x, 0.0) + jnp.log(1.0 + jnp.exp(-jnp.abs(x)))


def _bdot(a, b):
    return jnp.dot(a.astype(BF16), b.astype(BF16), preferred_element_type=F32)


def _bdot_nt(a, b):
    return lax.dot_general(a.astype(BF16), b.astype(BF16), (((1,), (1,)), ((), ())),
                           preferred_element_type=F32)


def _bdot_tn(a, b):
    return lax.dot_general(a.astype(BF16), b.astype(BF16), (((0,), (0,)), ((), ())),
                           preferred_element_type=F32)


def _layer_norm(x, g, b):
    mu = jnp.mean(x, axis=-1, keepdims=True)
    xc = x - mu
    var = jnp.mean(xc * xc, axis=-1, keepdims=True)
    return xc * lax.rsqrt(var + LN_EPS) * g + b


TOK_ROWS = D_MODEL // LANES


def _store_token_tiles(ref, x):
    n = x.shape[0]
    for j in range(TOK_ROWS):
        ref[pl.ds(j, n, stride=TOK_ROWS), :] = x[:, j * LANES:(j + 1) * LANES]


def _load_token_tiles(ref, n):
    return jnp.concatenate([ref[pl.ds(j, n, stride=TOK_ROWS), :] for j in range(TOK_ROWS)], axis=1)


def _const_spec(shape):
    nd = len(shape)
    return pl.BlockSpec(shape, lambda *_: (0,) * nd, pipeline_mode=pl.Buffered(1))


def _qkv_kernel(x_ref, w_ref, wp_ref, cw_ref, alog_ref, dtb_ref,
                q_ref, k_ref, v_ref, pack_ref, gct_ref, z_ref):
    tm = x_ref.shape[0]
    halo = SUBLANES

    @pl.when(pl.program_id(0) == 0)
    def _():
        z_ref[0:halo, :] = jnp.zeros((halo, CONV_CH), F32)

    xb = x_ref[...].astype(BF16)
    z_ref[halo:halo + tm, :] = jnp.dot(xb, w_ref[...], preferred_element_type=F32)

    for c in range(CONV_CH // LANES):
        cols = slice(c * LANES, (c + 1) * LANES)
        acc = z_ref[halo:halo + tm, cols] * cw_ref[GDN_CONV - 1:GDN_CONV, cols]
        for s in range(1, GDN_CONV):
            acc = acc + z_ref[halo - s:halo - s + tm, cols] * cw_ref[GDN_CONV - 1 - s:GDN_CONV - s, cols]
        y = acc * _sigmoid(acc)
        h = c % GDN_HEADS
        if c < GDN_HEADS:
            y = y * lax.rsqrt(jnp.sum(y * y, axis=-1, keepdims=True) + 1e-6) * (GDN_DK ** -0.5)
            q_ref[h] = y
        elif c < 2 * GDN_HEADS:
            y = y * lax.rsqrt(jnp.sum(y * y, axis=-1, keepdims=True) + 1e-6)
            k_ref[h] = y
        else:
            v_ref[h] = y

    z_ref[0:halo, :] = z_ref[tm:tm + halo, :]

    zp = jnp.dot(xb, wp_ref[...], preferred_element_type=F32)
    beta = _sigmoid(zp)
    g = -jnp.exp(alog_ref[...]) * _softplus(zp + dtb_ref[...])
    row = lax.broadcasted_iota(jnp.int32, (tm, LANES), 0) % GDN_CHUNK
    gc = g
    step = 1
    while step < GDN_CHUNK:
        gc = gc + jnp.where(row >= step, pltpu.roll(gc, step, axis=0), 0.0)
        step *= 2
    gl = gc.reshape(tm // GDN_CHUNK, GDN_CHUNK, LANES)[:, GDN_CHUNK - 1:GDN_CHUNK, :]
    gl = jnp.broadcast_to(gl, (tm // GDN_CHUNK, GDN_CHUNK, LANES)).reshape(tm, LANES)
    lane = lax.broadcasted_iota(jnp.int32, (tm, LANES), 1)
    pack = jnp.where(lane < PK_GC, beta,
           jnp.where(lane < PK_EG, gc,
           jnp.where(lane < PK_EGD, jnp.exp(gc),
           jnp.where(lane < PK_EGL, jnp.exp(gl - gc), jnp.exp(gl)))))
    pack_ref[...] = pack
    gct_ref[...] = pack.T[PK_GC:PK_GC + GDN_HEADS, :]


def _qkv_proj(x2, w_qkv, w_pack, conv_w, alog_row, dtb_row):
    T = x2.shape[0]
    tm = TOK_TILE
    hd_spec = pl.BlockSpec((GDN_HEADS, tm, LANES), lambda i: (0, i, 0))
    hd_shape = jax.ShapeDtypeStruct((GDN_HEADS, T, LANES), F32)
    return pl.pallas_call(
        _qkv_kernel,
        grid=(T // tm,),
        in_specs=[
            pl.BlockSpec((tm, D_MODEL), lambda i: (i, 0)),
            _const_spec((D_MODEL, CONV_CH)),
            _const_spec((D_MODEL, LANES)),
            _const_spec((GDN_CONV, CONV_CH)),
            _const_spec((1, LANES)),
            _const_spec((1, LANES)),
        ],
        out_specs=[hd_spec, hd_spec, hd_spec,
                   pl.BlockSpec((tm, LANES), lambda i: (i, 0)),
                   pl.BlockSpec((GDN_HEADS, tm), lambda i: (0, i))],
        out_shape=[hd_shape, hd_shape, hd_shape,
                   jax.ShapeDtypeStruct((T, LANES), F32),
                   jax.ShapeDtypeStruct((GDN_HEADS, T), F32)],
        scratch_shapes=[pltpu.VMEM((tm + SUBLANES, CONV_CH), F32)],
        compiler_params=pltpu.CompilerParams(dimension_semantics=("arbitrary",),
                                             vmem_limit_bytes=VMEM_LIMIT),
        name="qkv_proj",
    )(x2, w_qkv, w_pack, conv_w, alog_row, dtb_row)


def _rope(xh, cos_t, sin_t, lane):
    half = ROT_DIM // 2
    swapped = jnp.where(lane < half, pltpu.roll(xh, LANES - half, axis=1), pltpu.roll(xh, half, axis=1))
    return xh * cos_t + swapped * sin_t


def _moba_proj_kernel(x_ref, w_ref, cos_ref, sin_ref, q_ref, k_ref, vt_ref, km_ref):
    tm = x_ref.shape[0]
    z = jnp.dot(x_ref[...].astype(BF16), w_ref[...], preferred_element_type=F32)
    cos_t = cos_ref[...]
    sin_t = sin_ref[...]
    lane = lax.broadcasted_iota(jnp.int32, (tm, LANES), 1)
    for h in range(MB_HEADS):
        q_ref[h] = _rope(z[:, h * MB_HD:(h + 1) * MB_HD], cos_t, sin_t, lane)
    koff = MB_HEADS * MB_HD
    voff = koff + MB_KV_HEADS * MB_HD
    blk_onehot = jnp.where(lane == pl.program_id(0), 1.0, 0.0).astype(BF16)
    for g in range(MB_KV_HEADS):
        kr = _rope(z[:, koff + g * MB_HD:koff + (g + 1) * MB_HD], cos_t, sin_t, lane)
        k_ref[0, g] = jnp.concatenate([kr.astype(BF16), blk_onehot], axis=1)
        km_ref[0, g:g + 1, :] = jnp.mean(kr, axis=0, keepdims=True)
        vt_ref[0, g] = z[:, voff + g * MB_HD:voff + (g + 1) * MB_HD].T.astype(BF16)


def _moba_proj(x2, w_mb, cos_t, sin_t):
    T = x2.shape[0]
    tm = MB_BLOCK
    nb = T // tm
    return pl.pallas_call(
        _moba_proj_kernel,
        grid=(nb,),
        in_specs=[
            pl.BlockSpec((tm, D_MODEL), lambda i: (i, 0)),
            _const_spec(w_mb.shape),
            pl.BlockSpec((tm, LANES), lambda i: (i, 0)),
            pl.BlockSpec((tm, LANES), lambda i: (i, 0)),
        ],
        out_specs=[
            pl.BlockSpec((MB_HEADS, tm, MB_HD), lambda i: (0, i, 0)),
            pl.BlockSpec((1, MB_KV_HEADS, tm, 2 * MB_HD), lambda i: (i, 0, 0, 0)),
            pl.BlockSpec((1, MB_KV_HEADS, MB_HD, tm), lambda i: (i, 0, 0, 0)),
            pl.BlockSpec((1, MB_KV_HEADS, MB_HD), lambda i: (i, 0, 0)),
        ],
        out_shape=[
            jax.ShapeDtypeStruct((MB_HEADS, T, MB_HD), F32),
            jax.ShapeDtypeStruct((nb, MB_KV_HEADS, tm, 2 * MB_HD), BF16),
            jax.ShapeDtypeStruct((nb, MB_KV_HEADS, MB_HD, tm), BF16),
            jax.ShapeDtypeStruct((nb, MB_KV_HEADS, MB_HD), F32),
        ],
        compiler_params=pltpu.CompilerParams(dimension_semantics=("arbitrary",),
                                             vmem_limit_bytes=VMEM_LIMIT),
        name="moba_proj",
    )(x2, w_mb, cos_t, sin_t)


def _unit_lower_inverse(a_list):
    c = a_list[0].shape[0]
    rr = lax.broadcasted_iota(jnp.int32, (c, c), 0)
    cc = lax.broadcasted_iota(jnp.int32, (c, c), 1)
    eye = jnp.where(rr == cc, 1.0, 0.0).astype(F32)
    inv = [eye - a for a in a_list]
    p = [_bdot(a, a) for a in a_list]
    n = 2
    while True:
        inv = [x + _bdot(x, y) for x, y in zip(inv, p)]
        n *= 2
        if n >= c:
            break
        p = [_bdot(y, y) for y in p]
    return inv


def _gdn_kernel(q_ref, k_ref, v_ref, pack_ref, gct_ref, nw_ref, o_ref, s_ref):
    C = GDN_CHUNK
    H = range(GDN_HEADS)

    @pl.when(pl.program_id(0) == 0)
    def _():
        s_ref[...] = jnp.zeros_like(s_ref)

    rr = lax.broadcasted_iota(jnp.int32, (C, C), 0)
    cc = lax.broadcasted_iota(jnp.int32, (C, C), 1)
    tril = rr >= cc
    strict = rr > cc
    pack = pack_ref[...]
    nw = nw_ref[...]

    def col(base, h):
        return pack[:, base + h:base + h + 1]

    q = [q_ref[h] for h in H]
    k = [k_ref[h] for h in H]
    kb = [k[h] * col(PK_BETA, h) for h in H]
    decay = [jnp.where(tril, jnp.exp(jnp.where(tril, col(PK_GC, h) - gct_ref[0, h:h + 1, :], 0.0)), 0.0)
             for h in H]
    kk = [_bdot_nt(kb[h], k[h]) for h in H]
    qk = [_bdot_nt(q[h], k[h]) for h in H]
    a = [jnp.where(strict, kk[h] * decay[h], 0.0) for h in H]
    aqk = [qk[h] * decay[h] for h in H]
    tinv = _unit_lower_inverse(a)
    rhs = [jnp.concatenate([v_ref[h] * col(PK_BETA, h), kb[h] * col(PK_EG, h)], axis=1) for h in H]
    sol = [_bdot(tinv[h], rhs[h]) for h in H]
    s = [s_ref[h] for h in H]
    ws = [_bdot(sol[h][:, GDN_DV:], s[h]) for h in H]
    qs = [_bdot(q[h] * col(PK_EG, h), s[h]) for h in H]
    v_new = [sol[h][:, :GDN_DV] - ws[h] for h in H]
    o = [qs[h] + _bdot(aqk[h], v_new[h]) for h in H]
    ds = [_bdot_tn(k[h] * col(PK_EGD, h), v_new[h]) for h in H]
    for h in H:
        s_ref[h] = s[h] * pack[0:1, PK_EGL + h:PK_EGL + h + 1] + ds[h]
        on = o[h] * lax.rsqrt(jnp.mean(o[h] * o[h], axis=-1, keepdims=True) + 1e-6) * nw
        o_ref[:, h * GDN_DV:(h + 1) * GDN_DV] = on


def _gdn(q, k, v, pack, gct3, norm_w_row):
    T = q.shape[1]
    C = GDN_CHUNK
    hd_spec = pl.BlockSpec((GDN_HEADS, C, LANES), lambda i: (0, i, 0))
    return pl.pallas_call(
        _gdn_kernel,
        grid=(T // C,),
        in_specs=[hd_spec, hd_spec, hd_spec,
                  pl.BlockSpec((C, LANES), lambda i: (i, 0)),
                  pl.BlockSpec((1, GDN_HEADS, C), lambda i: (i, 0, 0)),
                  _const_spec((1, GDN_DV))],
        out_specs=pl.BlockSpec((C, GDN_HEADS * GDN_DV), lambda i: (i, 0)),
        out_shape=jax.ShapeDtypeStruct((T, GDN_HEADS * GDN_DV), F32),
        scratch_shapes=[pltpu.VMEM((GDN_HEADS, GDN_DK, GDN_DV), F32)],
        compiler_params=pltpu.CompilerParams(dimension_semantics=("arbitrary",)),
        name="gdn",
    )(q, k, v, pack, gct3, norm_w_row)


MB_QSPLIT = 2
MB_MAX_BLOCKS = MB_HD


def _moba_kernel(q_ref, k_ref, vt_ref, km_ref, o_ref, qx_ref, sa_ref, sb_ref, acc_ref, m_ref, l_ref):
    qi = pl.program_id(1)
    nb = k_ref.shape[0]
    bs = MB_BLOCK
    nq = MB_REP * bs
    nbp = MB_MAX_BLOCKS
    c = (MB_HD ** -0.5) * math.log2(math.e)

    qf = q_ref[...].reshape(nq, MB_HD)

    gate = lax.dot_general(km_ref[...], qf, (((1,), (1,)), ((), ())),
                           precision=lax.Precision.HIGHEST, preferred_element_type=F32)
    blk = lax.broadcasted_iota(jnp.int32, (nbp, nq), 0).astype(F32)
    qif = qi.astype(F32)
    gate = jnp.where(blk < qif, gate, -jnp.inf)
    sel = jnp.where(blk == qif, 1.0, 0.0)
    for _ in range(MB_TOPK):
        top = jnp.max(gate, axis=0, keepdims=True)
        first = jnp.min(jnp.where(gate == top, blk, float(nbp)), axis=0, keepdims=True)
        hit = (blk == first) & (top > -jnp.inf)
        sel = jnp.where(hit, 1.0, sel)
        gate = jnp.where(hit, -jnp.inf, gate)
    bias_t = jnp.where(sel > 0.0, 0.0, MASK_NEG)
    qc = qf * c
    qx_ref[0] = jnp.concatenate([qc, bias_t.T], axis=1).astype(BF16)
    qx_ref[1] = jnp.concatenate([qc, jnp.full((nq, nbp), MASK_NEG, F32)], axis=1).astype(BF16)

    w = nq // MB_QSPLIT
    groups = [slice(h * w, (h + 1) * w) for h in range(MB_QSPLIT)]

    def scores_into(dst_ref, t):
        kx = k_ref[jnp.minimum(t, nb - 1)]
        qset = jnp.where(t < qi, 0, 1)
        for lanes in groups:
            dst_ref[:, lanes] = lax.dot_general(kx, qx_ref[qset, lanes, :], (((1,), (1,)), ((), ())),
                                                preferred_element_type=F32)

    st = lax.dot_general(k_ref[qi], qx_ref[0], (((1,), (1,)), ((), ())), preferred_element_type=F32)
    kpos = lax.broadcasted_iota(jnp.int32, (bs, nq), 0)
    qpos = lax.broadcasted_iota(jnp.int32, (bs, nq), 1) % bs
    st = jnp.where(kpos <= qpos, st, MASK_NEG)
    m0 = jnp.max(st, axis=0, keepdims=True)
    p = jnp.exp2(st - m0)
    m_ref[...] = m0
    l_ref[...] = jnp.sum(p, axis=0, keepdims=True)
    acc_ref[...] = jnp.dot(vt_ref[qi], p.astype(BF16), preferred_element_type=F32)

    def absorb(src_ref, t):
        vt = vt_ref[jnp.minimum(t, nb - 1)]
        for lanes in groups:
            st = src_ref[:, lanes]
            m_old = m_ref[:, lanes]
            m_new = jnp.maximum(m_old, jnp.max(st, axis=0, keepdims=True))
            alpha = jnp.exp2(m_old - m_new)
            p = jnp.exp2(st - m_new)
            m_ref[:, lanes] = m_new
            l_ref[:, lanes] = alpha * l_ref[:, lanes] + jnp.sum(p, axis=0, keepdims=True)
            acc_ref[:, lanes] = alpha * acc_ref[:, lanes] + jnp.dot(vt, p.astype(BF16),
                                                                    preferred_element_type=F32)

    scores_into(sa_ref, 0)

    def body(i, carry):
        t = 2 * i
        scores_into(sb_ref, t + 1)
        absorb(sa_ref, t)
        scores_into(sa_ref, t + 2)
        absorb(sb_ref, t + 1)
        return carry

    lax.fori_loop(0, (qi + 1) // 2, body, 0)

    out_t = acc_ref[...] / l_ref[...]
    for r in range(MB_REP):
        o_ref[:, r * MB_HD:(r + 1) * MB_HD] = out_t[:, r * bs:(r + 1) * bs].T


def _moba(mq, mk, mvt, kmean):
    T = mq.shape[1]
    bs = MB_BLOCK
    nb = T // bs
    nq = MB_REP * bs
    assert nb <= MB_MAX_BLOCKS
    kmean = jnp.pad(kmean, ((0, 0), (0, MB_MAX_BLOCKS - nb), (0, 0)))
    return pl.pallas_call(
        _moba_kernel,
        grid=(MB_KV_HEADS, nb),
        in_specs=[
            pl.BlockSpec((MB_REP, bs, MB_HD), lambda g, i: (g, i, 0)),
            pl.BlockSpec((nb, None, bs, 2 * MB_HD), lambda g, i: (0, g, 0, 0)),
            pl.BlockSpec((nb, None, MB_HD, bs), lambda g, i: (0, g, 0, 0)),
            pl.BlockSpec((None, MB_MAX_BLOCKS, MB_HD), lambda g, i: (g, 0, 0)),
        ],
        out_specs=pl.BlockSpec((bs, MB_REP * MB_HD), lambda g, i: (i, g)),
        out_shape=jax.ShapeDtypeStruct((T, MB_HEADS * MB_HD), F32),
        scratch_shapes=[pltpu.VMEM((2, nq, 2 * MB_HD), BF16),
                        pltpu.VMEM((bs, nq), F32), pltpu.VMEM((bs, nq), F32),
                        pltpu.VMEM((MB_HD, nq), F32),
                        pltpu.VMEM((1, nq), F32), pltpu.VMEM((1, nq), F32)],
        compiler_params=pltpu.CompilerParams(dimension_semantics=("arbitrary", "arbitrary"),
                                             vmem_limit_bytes=VMEM_LIMIT),
        name="moba",
    )(mq, mk, mvt, kmean)


def _post_kernel(x_ref, go_ref, ma_ref, wg_ref, wgp_ref, wmp_ref, wo_ref, g1_ref, b1_ref,
                 rwt_ref, rb_ref, sg_ref, su_ref, sd_ref,
                 h_ref, res_ref, eidx_ref, wts_ref):
    tm = x_ref.shape[0]
    x = x_ref[...]
    zg = jnp.dot(x.astype(BF16), wg_ref[...], preferred_element_type=F32)
    gate = zg[:, :D_MODEL]
    o = go_ref[...] * (gate * _sigmoid(gate))
    y_gdn = jnp.dot(o.astype(BF16), wgp_ref[...], preferred_element_type=F32)
    y_mb = jnp.dot(ma_ref[...].astype(BF16), wmp_ref[...], preferred_element_type=F32)
    m = _sigmoid(zg[:, D_MODEL:2 * D_MODEL]) * y_gdn + _sigmoid(zg[:, 2 * D_MODEL:]) * y_mb
    mix = jnp.dot(m.astype(BF16), wo_ref[...], preferred_element_type=F32)
    h = _layer_norm(DEEPNORM_ALPHA * x + mix, g1_ref[...], b1_ref[...])
    _store_token_tiles(h_ref, h)
    hb = h.astype(BF16)

    hs = jnp.dot(hb, sg_ref[...], preferred_element_type=F32)
    hs = hs * _sigmoid(hs) * jnp.dot(hb, su_ref[...], preferred_element_type=F32)
    res_ref[...] = DEEPNORM_ALPHA * h + jnp.dot(hs.astype(BF16), sd_ref[...], preferred_element_type=F32)

    logits = lax.dot_general(rwt_ref[...], h, (((1,), (1,)), ((), ())),
                             precision=lax.Precision.HIGHEST, preferred_element_type=F32)
    scores = _sigmoid(logits)
    choice = scores + rb_ref[...]
    neg = -jnp.inf
    gi = lax.broadcasted_iota(jnp.int32, (GROUP_SIZE, tm), 0).astype(F32)
    gscore = []
    for g in range(N_GROUPS):
        cg = choice[g * GROUP_SIZE:(g + 1) * GROUP_SIZE, :]
        m1 = jnp.max(cg, axis=0, keepdims=True)
        i1 = jnp.min(jnp.where(cg == m1, gi, float(GROUP_SIZE)), axis=0, keepdims=True)
        m2 = jnp.max(jnp.where(gi == i1, neg, cg), axis=0, keepdims=True)
        gscore.append(m1 + m2)
    gs = jnp.concatenate(gscore, axis=0)
    gidx = lax.broadcasted_iota(jnp.int32, (N_GROUPS, tm), 0).astype(F32)
    gsel = jnp.zeros((N_GROUPS, tm), F32)
    for _ in range(TOPK_GROUPS):
        top = jnp.max(gs, axis=0, keepdims=True)
        first = jnp.min(jnp.where(gs == top, gidx, float(N_GROUPS)), axis=0, keepdims=True)
        hit = gidx == first
        gsel = jnp.where(hit, 1.0, gsel)
        gs = jnp.where(hit, neg, gs)
    masked = jnp.concatenate(
        [jnp.where(gsel[g:g + 1, :] > 0.0, choice[g * GROUP_SIZE:(g + 1) * GROUP_SIZE, :], neg)
         for g in range(N_GROUPS)], axis=0)
    ei = lax.broadcasted_iota(jnp.int32, (N_EXPERTS, tm), 0).astype(F32)
    idx_rows, w_rows = [], []
    for _ in range(TOP_K):
        top = jnp.max(masked, axis=0, keepdims=True)
        first = jnp.min(jnp.where(masked == top, ei, float(N_EXPERTS)), axis=0, keepdims=True)
        hit = ei == first
        idx_rows.append(first)
        w_rows.append(jnp.sum(jnp.where(hit, scores, 0.0), axis=0, keepdims=True))
        masked = jnp.where(hit, neg, masked)
    w = jnp.concatenate(w_rows, axis=0)
    w = w / (jnp.sum(w, axis=0, keepdims=True) + 1e-20) * ROUTED_SCALE
    eidx_ref[...] = jnp.concatenate(idx_rows, axis=0).astype(jnp.int32)
    wts_ref[...] = w


def _post(x2, gdn_o, moba_a, w_gates, w_gdn_proj, w_moba_proj, w_out, ln_g, ln_b,
          router_wt, router_b_col, sh_g, sh_u, sh_d):
    T = x2.shape[0]
    tm = TOK_TILE
    row_spec = pl.BlockSpec((tm, D_MODEL), lambda i: (i, 0))
    consts = [w_gates, w_gdn_proj, w_moba_proj, w_out, ln_g, ln_b, router_wt, router_b_col,
              sh_g, sh_u, sh_d]
    return pl.pallas_call(
        _post_kernel,
        grid=(T // tm,),
        in_specs=[row_spec, row_spec, row_spec] + [_const_spec(c.shape) for c in consts],
        out_specs=[pl.BlockSpec((tm * TOK_ROWS, LANES), lambda i: (i, 0)), row_spec,
                   pl.BlockSpec((TOP_K, tm), lambda i: (0, i)),
                   pl.BlockSpec((TOP_K, tm), lambda i: (0, i))],
        out_shape=[jax.ShapeDtypeStruct((T * TOK_ROWS, LANES), F32),
                   jax.ShapeDtypeStruct((T, D_MODEL), F32),
                   jax.ShapeDtypeStruct((TOP_K, T), jnp.int32),
                   jax.ShapeDtypeStruct((TOP_K, T), F32)],
        compiler_params=pltpu.CompilerParams(dimension_semantics=("arbitrary",),
                                             vmem_limit_bytes=VMEM_LIMIT),
        name="post",
    )(x2, gdn_o, moba_a, *consts)


ROW_UNROLL = 8


def _for_rows(n, fn):
    def group(g, c):
        for u in range(ROW_UNROLL):
            fn(g * ROW_UNROLL + u)
        return c

    def single(i, c):
        fn(i)
        return c
    full = n // ROW_UNROLL
    lax.fori_loop(0, full, group, 0)
    lax.fori_loop(full * ROW_UNROLL, n, single, 0)


def _experts_kernel(blk_exp_ref, nvalid_ref, src_ref, srcn_ref, dst_ref, h_ref, wg_ref, wu_ref, wd_ref,
                    out_ref, xbuf, ybuf, wgb, wub, wdb, gsem, ssem):
    b = pl.program_id(0)
    nblk = pl.num_programs(0)
    slot = b % 2
    nv = nvalid_ref[b]
    nv_next = jnp.where(b + 1 < nblk, nvalid_ref[jnp.minimum(b + 1, nblk - 1)], 0)
    nv_prev = jnp.where(b >= 1, nvalid_ref[jnp.maximum(b - 1, 0)], 0)
    nv_prev2 = jnp.where(b >= 2, nvalid_ref[jnp.maximum(b - 2, 0)], 0)

    def tile_at(ref, row0):
        return ref.at[pl.ds(pl.multiple_of(row0, TOK_ROWS), TOK_ROWS)]

    def start_gather(rows_ref, n, s):
        def issue(i):
            pltpu.make_async_copy(tile_at(h_ref, rows_ref[0, 0, i]), tile_at(xbuf.at[s], i * TOK_ROWS),
                                  gsem.at[s]).start()
        _for_rows(n, issue)

    def wait_tokens(n, src, dst, sem):
        @pl.when(n > 0)
        def _():
            rows = pl.multiple_of(n * TOK_ROWS, TOK_ROWS)
            pltpu.make_async_copy(src.at[pl.ds(0, rows)], dst.at[pl.ds(0, rows)], sem).wait()

    def wait_gather(n, s):
        wait_tokens(n, h_ref, xbuf.at[s], gsem.at[s])

    def wait_scatter(n, s):
        wait_tokens(n, ybuf.at[s], out_ref, ssem.at[s])

    @pl.when(b == 0)
    def _():
        xbuf[...] = jnp.zeros_like(xbuf)
        start_gather(src_ref, nv, 0)

    start_gather(srcn_ref, nv_next, 1 - slot)
    wait_scatter(nv_prev2, slot)

    @pl.when(nv > 0)
    def _():
        prev_exp = blk_exp_ref[jnp.maximum(b - 1, 0)]

        @pl.when((b == 0) | (blk_exp_ref[b] != prev_exp))
        def _():
            wgb[...] = wg_ref[...].astype(BF16)
            wub[...] = wu_ref[...].astype(BF16)
            wdb[...] = wd_ref[...].astype(BF16)

        wait_gather(nv, slot)
        xb = _load_token_tiles(xbuf.at[slot], MOE_BLOCK).astype(BF16)
        hg = jnp.dot(xb, wgb[...], preferred_element_type=F32)
        hu = jnp.dot(xb, wub[...], preferred_element_type=F32)
        hb = (hg * _sigmoid(hg) * hu).astype(BF16)
        _store_token_tiles(ybuf.at[slot], jnp.dot(hb, wdb[...], preferred_element_type=F32))

        def issue(i):
            pltpu.make_async_copy(tile_at(ybuf.at[slot], i * TOK_ROWS), tile_at(out_ref, dst_ref[0, 0, i]),
                                  ssem.at[slot]).start()
        _for_rows(nv, issue)

    @pl.when(b == nblk - 1)
    def _():
        wait_scatter(nv_prev, 1 - slot)
        wait_scatter(nv, slot)


def _experts(h, src_rows, dst_rows, blk_exp, nvalid, wg, wu, wd):
    T = h.shape[0] // TOK_ROWS
    nblk = src_rows.shape[0]
    rows = MOE_BLOCK
    cur_spec = pl.BlockSpec((1, 1, rows), lambda b, be, nu: (b, 0, 0), memory_space=pltpu.SMEM)
    grid_spec = pltpu.PrefetchScalarGridSpec(
        num_scalar_prefetch=2,
        grid=(nblk,),
        in_specs=[
            cur_spec,
            pl.BlockSpec((1, 1, rows), lambda b, be, nu: (jnp.minimum(b + 1, nblk - 1), 0, 0),
                         memory_space=pltpu.SMEM),
            cur_spec,
            pl.BlockSpec(memory_space=pl.ANY),
            pl.BlockSpec((None, D_MODEL, EXPERT_DIM), lambda b, be, nu: (be[b], 0, 0)),
            pl.BlockSpec((None, D_MODEL, EXPERT_DIM), lambda b, be, nu: (be[b], 0, 0)),
            pl.BlockSpec((None, EXPERT_DIM, D_MODEL), lambda b, be, nu: (be[b], 0, 0)),
        ],
        out_specs=pl.BlockSpec(memory_space=pl.ANY),
        scratch_shapes=[pltpu.VMEM((2, rows * TOK_ROWS, LANES), F32),
                        pltpu.VMEM((2, rows * TOK_ROWS, LANES), F32),
                        pltpu.VMEM((D_MODEL, EXPERT_DIM), BF16), pltpu.VMEM((D_MODEL, EXPERT_DIM), BF16),
                        pltpu.VMEM((EXPERT_DIM, D_MODEL), BF16),
                        pltpu.SemaphoreType.DMA((2,)), pltpu.SemaphoreType.DMA((2,))],
    )
    return pl.pallas_call(
        _experts_kernel,
        grid_spec=grid_spec,
        out_shape=jax.ShapeDtypeStruct((TOP_K * T * TOK_ROWS, LANES), F32),
        compiler_params=pltpu.CompilerParams(dimension_semantics=("arbitrary",),
                                             vmem_limit_bytes=VMEM_LIMIT,
                                             disable_bounds_checks=True),
        name="experts",
    )(blk_exp, nvalid, src_rows, src_rows, dst_rows, h, wg, wu, wd)


def _combine_kernel(res_ref, y_ref, w_ref, g_ref, b_ref, o_ref):
    tm = res_ref.shape[0]
    acc = res_ref[...]
    w = w_ref[...]
    for k in range(TOP_K):
        acc = acc + w[:, k:k + 1] * _load_token_tiles(y_ref.at[k], tm)
    o_ref[...] = _layer_norm(acc, g_ref[...], b_ref[...])


def _combine(res, y3, wts, ln_g, ln_b):
    T = res.shape[0]
    tm = 128
    return pl.pallas_call(
        _combine_kernel,
        grid=(T // tm,),
        in_specs=[pl.BlockSpec((tm, D_MODEL), lambda i: (i, 0)),
                  pl.BlockSpec((TOP_K, tm * TOK_ROWS, LANES), lambda i: (0, i, 0)),
                  pl.BlockSpec((tm, TOP_K), lambda i: (i, 0)),
                  _const_spec((1, D_MODEL)), _const_spec((1, D_MODEL))],
        out_specs=pl.BlockSpec((tm, D_MODEL), lambda i: (i, 0)),
        out_shape=jax.ShapeDtypeStruct((T, D_MODEL), F32),
        compiler_params=pltpu.CompilerParams(dimension_semantics=("arbitrary",)),
        name="combine",
    )(res, y3, wts, ln_g, ln_b)


def _dispatch_plan(eidx_t, n_tok):
    nk = n_tok * TOP_K
    e_flat = eidx_t.T.reshape(nk)
    _, order = lax.sort((e_flat, jnp.arange(nk, dtype=jnp.int32)), num_keys=1)
    experts = jnp.arange(N_EXPERTS, dtype=jnp.int32)
    counts = jnp.sum((e_flat[None, :] == experts[:, None]).astype(jnp.int32), axis=1)
    padded = (counts + MOE_BLOCK - 1) // MOE_BLOCK * MOE_BLOCK
    start = jnp.cumsum(counts) - counts
    pend = jnp.cumsum(padded)
    pstart = pend - padded
    nblk = -(-nk // MOE_BLOCK) + N_EXPERTS
    blk_row0 = jnp.arange(nblk, dtype=jnp.int32) * MOE_BLOCK
    blk_exp = jnp.minimum(jnp.sum((pend[None, :] <= blk_row0[:, None]).astype(jnp.int32), axis=1),
                          N_EXPERTS - 1)
    onehot = blk_exp[:, None] == experts[None, :]

    def lookup(table):
        return jnp.sum(jnp.where(onehot, table[None, :], 0), axis=1)

    in_exp = blk_row0 - lookup(pstart)
    nvalid = jnp.clip(lookup(counts) - in_exp, 0, MOE_BLOCK)
    nvalid = jnp.where(blk_row0 < pend[-1], nvalid, 0).astype(jnp.int32)
    src = jnp.clip(lookup(start) + in_exp, 0, nk)
    order_ext = jnp.concatenate([order, jnp.zeros((MOE_BLOCK,), jnp.int32)])
    aid = jax.vmap(lambda s: lax.dynamic_slice(order_ext, (s,), (MOE_BLOCK,)))(src)
    tok, slot_k = aid // TOP_K, aid % TOP_K
    src_rows = tok * TOK_ROWS
    dst_rows = (slot_k * n_tok + tok) * TOK_ROWS
    shape3 = (nblk, 1, MOE_BLOCK)
    return src_rows.reshape(shape3), dst_rows.reshape(shape3), blk_exp, nvalid


def kernel(x, w_in, conv_w, gdn_a_log, gdn_dt_bias, gdn_norm_w, w_gdn_proj, w_moba_proj, w_out,
           ln1_g, ln1_b, router_w, router_bias, exp_w_gate, exp_w_up, exp_w_down,
           sh_w_gate, sh_w_up, sh_w_down, ln2_g, ln2_b):
    B, T, D = x.shape
    assert B == 1 and D == D_MODEL and T % MB_BLOCK == 0
    x2 = x.reshape(T, D)

    o_gate = CONV_CH
    o_b = o_gate + GDN_HEADS * GDN_DV
    o_a = o_b + GDN_HEADS
    o_mq = o_a + GDN_HEADS
    o_gg = o_mq + (MB_HEADS + 2 * MB_KV_HEADS) * MB_HD
    w_qkv = w_in[:, :CONV_CH].astype(BF16)
    w_pack = jnp.concatenate([w_in[:, o_b:o_a]] + [w_in[:, o_a:o_mq]] * 4
                             + [jnp.zeros((D, LANES - 5 * GDN_HEADS), F32)], axis=1).astype(BF16)
    w_mb = w_in[:, o_mq:o_gg].astype(BF16)
    w_gates = jnp.concatenate([w_in[:, o_gate:o_b], w_in[:, o_gg:]], axis=1).astype(BF16)

    def lane_row(v):
        return jnp.zeros((1, LANES), F32).at[0, PK_GC:PK_GC + 4 * GDN_HEADS].set(jnp.tile(v.astype(F32), 4))

    q, k, v, pack, gct = _qkv_proj(x2, w_qkv, w_pack, conv_w.astype(F32),
                                   lane_row(gdn_a_log), lane_row(gdn_dt_bias))
    gct3 = gct.reshape(GDN_HEADS, T // GDN_CHUNK, GDN_CHUNK).transpose(1, 0, 2)
    gdn_o = _gdn(q, k, v, pack, gct3, gdn_norm_w.astype(F32).reshape(1, GDN_DV))

    half = ROT_DIM // 2
    inv = ROPE_THETA ** (-jnp.arange(half, dtype=F32) / half)
    ang = jnp.arange(T).astype(F32)[:, None] * inv[None, :]
    ones = jnp.ones((T, MB_HD - ROT_DIM), F32)
    cos_t = jnp.concatenate([jnp.cos(ang), jnp.cos(ang), ones], axis=1)
    sin_t = jnp.concatenate([-jnp.sin(ang), jnp.sin(ang), 0.0 * ones], axis=1)
    mq, mk, mvt, kmean = _moba_proj(x2, w_mb, cos_t, sin_t)
    moba_a = _moba(mq, mk, mvt, kmean.transpose(1, 0, 2))

    h, res, eidx_t, wts_t = _post(
        x2, gdn_o, moba_a, w_gates, w_gdn_proj.astype(BF16), w_moba_proj.astype(BF16),
        w_out.astype(BF16), ln1_g.reshape(1, D), ln1_b.reshape(1, D),
        router_w.T, router_bias.reshape(N_EXPERTS, 1),
        sh_w_gate.astype(BF16), sh_w_up.astype(BF16), sh_w_down.astype(BF16))

    src_rows, dst_rows, blk_exp, nvalid = _dispatch_plan(eidx_t, T)
    y = _experts(h, src_rows, dst_rows, blk_exp, nvalid, exp_w_gate, exp_w_up, exp_w_down)
    y3 = y.reshape(TOP_K, T * TOK_ROWS, LANES)
    out = _combine(res, y3, wts_t.T, ln2_g.reshape(1, D), ln2_b.reshape(1, D))
    return out.reshape(B, T, D)
```

```python
import functools
import math

import jax
import jax.numpy as jnp
from jax import lax
from jax.experimental import pallas as pl
from jax.experimental.pallas import tpu as pltpu

F32 = jnp.float32
BF16 = jnp.bfloat16

D_MODEL = 1024
DEPTH = 1
GDN_HEADS = 8
GDN_DK = 128
GDN_DV = 128
GDN_CONV = 4
GDN_CHUNK = 64
MB_HEADS = 8
MB_KV_HEADS = 2
MB_REP = MB_HEADS // MB_KV_HEADS
MB_HD = 128
MB_BLOCK = 256
MB_TOPK = 3
ROT_DIM = MB_HD // 4
ROPE_THETA = 500000.0
N_EXPERTS = 256
TOP_K = 8
N_GROUPS = 8
GROUP_SIZE = N_EXPERTS // N_GROUPS
TOPK_GROUPS = 4
EXPERT_DIM = 256
SHARED_DIM = 256
ROUTED_SCALE = 2.5
MOE_BLOCK = 128
DEEPNORM_ALPHA = (2.0 * DEPTH) ** 0.25
LN_EPS = 1e-5
MASK_NEG = -1e30

QK_COLS = GDN_HEADS * GDN_DK
CONV_CH = 2 * QK_COLS + GDN_HEADS * GDN_DV
LANES = 128
SUBLANES = 8
VMEM_LIMIT = 56 * 1024 * 1024

TOK_TILE = 256

PK_BETA, PK_GC, PK_EG, PK_EGD, PK_EGL = 0, 8, 16, 24, 32


def _sigmoid(x):
    return 1.0 / (1.0 + jnp.exp(-x))


def _softplus(x):
    return jnp.maximum(x, 0.0) + jnp.log(1.0 + jnp.exp(-jnp.abs(x)))


def _bdot(a, b):
    return jnp.dot(a.astype(BF16), b.astype(BF16), preferred_element_type=F32)


def _bdot_nt(a, b):
    return lax.dot_general(a.astype(BF16), b.astype(BF16), (((1,), (1,)), ((), ())),
                           preferred_element_type=F32)


def _bdot_tn(a, b):
    return lax.dot_general(a.astype(BF16), b.astype(BF16), (((0,), (0,)), ((), ())),
                           preferred_element_type=F32)


def _layer_norm(x, g, b):
    mu = jnp.mean(x, axis=-1, keepdims=True)
    xc = x - mu
    var = jnp.mean(xc * xc, axis=-1, keepdims=True)
    return xc * lax.rsqrt(var + LN_EPS) * g + b


TOK_ROWS = D_MODEL // LANES


def _store_token_tiles(ref, x):
    n = x.shape[0]
    for j in range(TOK_ROWS):
        ref[pl.ds(j, n, stride=TOK_ROWS), :] = x[:, j * LANES:(j + 1) * LANES]


def _load_token_tiles(ref, n):
    return jnp.concatenate([ref[pl.ds(j, n, stride=TOK_ROWS), :] for j in range(TOK_ROWS)], axis=1)


def _const_spec(shape):
    nd = len(shape)
    return pl.BlockSpec(shape, lambda *_: (0,) * nd, pipeline_mode=pl.Buffered(1))


def _qkv_kernel(x_ref, w_ref, wp_ref, cw_ref, alog_ref, dtb_ref,
                q_ref, k_ref, v_ref, pack_ref, gct_ref, z_ref):
    tm = x_ref.shape[0]
    halo = SUBLANES

    @pl.when(pl.program_id(0) == 0)
    def _():
        z_ref[0:halo, :] = jnp.zeros((halo, CONV_CH), F32)

    xb = x_ref[...].astype(BF16)
    z_ref[halo:halo + tm, :] = jnp.dot(xb, w_ref[...], preferred_element_type=F32)

    for c in range(CONV_CH // LANES):
        cols = slice(c * LANES, (c + 1) * LANES)
        acc = z_ref[halo:halo + tm, cols] * cw_ref[GDN_CONV - 1:GDN_CONV, cols]
        for s in range(1, GDN_CONV):
            acc = acc + z_ref[halo - s:halo - s + tm, cols] * cw_ref[GDN_CONV - 1 - s:GDN_CONV - s, cols]
        y = acc * _sigmoid(acc)
        h = c % GDN_HEADS
        if c < GDN_HEADS:
            y = y * lax.rsqrt(jnp.sum(y * y, axis=-1, keepdims=True) + 1e-6) * (GDN_DK ** -0.5)
            q_ref[h] = y
        elif c < 2 * GDN_HEADS:
            y = y * lax.rsqrt(jnp.sum(y * y, axis=-1, keepdims=True) + 1e-6)
            k_ref[h] = y
        else:
            v_ref[h] = y

    z_ref[0:halo, :] = z_ref[tm:tm + halo, :]

    zp = jnp.dot(xb, wp_ref[...], preferred_element_type=F32)
    beta = _sigmoid(zp)
    g = -jnp.exp(alog_ref[...]) * _softplus(zp + dtb_ref[...])
    row = lax.broadcasted_iota(jnp.int32, (tm, LANES), 0) % GDN_CHUNK
    gc = g
    step = 1
    while step < GDN_CHUNK:
        gc = gc + jnp.where(row >= step, pltpu.roll(gc, step, axis=0), 0.0)
        step *= 2
    gl = gc.reshape(tm // GDN_CHUNK, GDN_CHUNK, LANES)[:, GDN_CHUNK - 1:GDN_CHUNK, :]
    gl = jnp.broadcast_to(gl, (tm // GDN_CHUNK, GDN_CHUNK, LANES)).reshape(tm, LANES)
    lane = lax.broadcasted_iota(jnp.int32, (tm, LANES), 1)
    pack = jnp.where(lane < PK_GC, beta,
           jnp.where(lane < PK_EG, gc,
           jnp.where(lane < PK_EGD, jnp.exp(gc),
           jnp.where(lane < PK_EGL, jnp.exp(gl - gc), jnp.exp(gl)))))
    pack_ref[...] = pack
    gct_ref[...] = pack.T[PK_GC:PK_GC + GDN_HEADS, :]


def _qkv_proj(x2, w_qkv, w_pack, conv_w, alog_row, dtb_row):
    T = x2.shape[0]
    tm = TOK_TILE
    hd_spec = pl.BlockSpec((GDN_HEADS, tm, LANES), lambda i: (0, i, 0))
    hd_shape = jax.ShapeDtypeStruct((GDN_HEADS, T, LANES), F32)
    return pl.pallas_call(
        _qkv_kernel,
        grid=(T // tm,),
        in_specs=[
            pl.BlockSpec((tm, D_MODEL), lambda i: (i, 0)),
            _const_spec((D_MODEL, CONV_CH)),
            _const_spec((D_MODEL, LANES)),
            _const_spec((GDN_CONV, CONV_CH)),
            _const_spec((1, LANES)),
            _const_spec((1, LANES)),
        ],
        out_specs=[hd_spec, hd_spec, hd_spec,
                   pl.BlockSpec((tm, LANES), lambda i: (i, 0)),
                   pl.BlockSpec((GDN_HEADS, tm), lambda i: (0, i))],
        out_shape=[hd_shape, hd_shape, hd_shape,
                   jax.ShapeDtypeStruct((T, LANES), F32),
                   jax.ShapeDtypeStruct((GDN_HEADS, T), F32)],
        scratch_shapes=[pltpu.VMEM((tm + SUBLANES, CONV_CH), F32)],
        compiler_params=pltpu.CompilerParams(dimension_semantics=("arbitrary",),
                                             vmem_limit_bytes=VMEM_LIMIT),
        name="qkv_proj",
    )(x2, w_qkv, w_pack, conv_w, alog_row, dtb_row)


def _rope(xh, cos_t, sin_t, lane):
    half = ROT_DIM // 2
    swapped = jnp.where(lane < half, pltpu.roll(xh, LANES - half, axis=1), pltpu.roll(xh, half, axis=1))
    return xh * cos_t + swapped * sin_t


def _moba_proj_kernel(x_ref, w_ref, cos_ref, sin_ref, q_ref, k_ref, vt_ref, km_ref):
    tm = x_ref.shape[0]
    z = jnp.dot(x_ref[...].astype(BF16), w_ref[...], preferred_element_type=F32)
    cos_t = cos_ref[...]
    sin_t = sin_ref[...]
    lane = lax.broadcasted_iota(jnp.int32, (tm, LANES), 1)
    for h in range(MB_HEADS):
        q_ref[h] = _rope(z[:, h * MB_HD:(h + 1) * MB_HD], cos_t, sin_t, lane)
    koff = MB_HEADS * MB_HD
    voff = koff + MB_KV_HEADS * MB_HD
    blk_onehot = jnp.where(lane == pl.program_id(0), 1.0, 0.0).astype(BF16)
    for g in range(MB_KV_HEADS):
        kr = _rope(z[:, koff + g * MB_HD:koff + (g + 1) * MB_HD], cos_t, sin_t, lane)
        k_ref[0, g] = jnp.concatenate([kr.astype(BF16), blk_onehot], axis=1)
        km_ref[0, g:g + 1, :] = jnp.mean(kr, axis=0, keepdims=True)
        vt_ref[0, g] = z[:, voff + g * MB_HD:voff + (g + 1) * MB_HD].T.astype(BF16)


def _moba_proj(x2, w_mb, cos_t, sin_t):
    T = x2.shape[0]
    tm = MB_BLOCK
    nb = T // tm
    return pl.pallas_call(
        _moba_proj_kernel,
        grid=(nb,),
        in_specs=[
            pl.BlockSpec((tm, D_MODEL), lambda i: (i, 0)),
            _const_spec(w_mb.shape),
            pl.BlockSpec((tm, LANES), lambda i: (i, 0)),
            pl.BlockSpec((tm, LANES), lambda i: (i, 0)),
        ],
        out_specs=[
            pl.BlockSpec((MB_HEADS, tm, MB_HD), lambda i: (0, i, 0)),
            pl.BlockSpec((1, MB_KV_HEADS, tm, 2 * MB_HD), lambda i: (i, 0, 0, 0)),
            pl.BlockSpec((1, MB_KV_HEADS, MB_HD, tm), lambda i: (i, 0, 0, 0)),
            pl.BlockSpec((1, MB_KV_HEADS, MB_HD), lambda i: (i, 0, 0)),
        ],
        out_shape=[
            jax.ShapeDtypeStruct((MB_HEADS, T, MB_HD), F32),
            jax.ShapeDtypeStruct((nb, MB_KV_HEADS, tm, 2 * MB_HD), BF16),
            jax.ShapeDtypeStruct((nb, MB_KV_HEADS, MB_HD, tm), BF16),
            jax.ShapeDtypeStruct((nb, MB_KV_HEADS, MB_HD), F32),
        ],
        compiler_params=pltpu.CompilerParams(dimension_semantics=("arbitrary",),
                                             vmem_limit_bytes=VMEM_LIMIT),
        name="moba_proj",
    )(x2, w_mb, cos_t, sin_t)


def _unit_lower_inverse(a_list):
    c = a_list[0].shape[0]
    rr = lax.broadcasted_iota(jnp.int32, (c, c), 0)
    cc = lax.broadcasted_iota(jnp.int32, (c, c), 1)
    eye = jnp.where(rr == cc, 1.0, 0.0).astype(F32)
    inv = [eye - a for a in a_list]
    p = [_bdot(a, a) for a in a_list]
    n = 2
    while True:
        inv = [x + _bdot(x, y) for x, y in zip(inv, p)]
        n *= 2
        if n >= c:
            break
        p = [_bdot(y, y) for y in p]
    return inv


def _gdn_kernel(q_ref, k_ref, v_ref, pack_ref, gct_ref, nw_ref, o_ref, s_ref):
    C = GDN_CHUNK
    H = range(GDN_HEADS)

    @pl.when(pl.program_id(0) == 0)
    def _():
        s_ref[...] = jnp.zeros_like(s_ref)

    rr = lax.broadcasted_iota(jnp.int32, (C, C), 0)
    cc = lax.broadcasted_iota(jnp.int32, (C, C), 1)
    tril = rr >= cc
    strict = rr > cc
    pack = pack_ref[...]
    nw = nw_ref[...]

    def col(base, h):
        return pack[:, base + h:base + h + 1]

    q = [q_ref[h] for h in H]
    k = [k_ref[h] for h in H]
    kb = [k[h] * col(PK_BETA, h) for h in H]
    decay = [jnp.where(tril, jnp.exp(jnp.where(tril, col(PK_GC, h) - gct_ref[0, h:h + 1, :], 0.0)), 0.0)
             for h in H]
    kk = [_bdot_nt(kb[h], k[h]) for h in H]
    qk = [_bdot_nt(q[h], k[h]) for h in H]
    a = [jnp.where(strict, kk[h] * decay[h], 0.0) for h in H]
    aqk = [qk[h] * decay[h] for h in H]
    tinv = _unit_lower_inverse(a)
    rhs = [jnp.concatenate([v_ref[h] * col(PK_BETA, h), kb[h] * col(PK_EG, h)], axis=1) for h in H]
    sol = [_bdot(tinv[h], rhs[h]) for h in H]
    s = [s_ref[h] for h in H]
    ws = [_bdot(sol[h][:, GDN_DV:], s[h]) for h in H]
    qs = [_bdot(q[h] * col(PK_EG, h), s[h]) for h in H]
    v_new = [sol[h][:, :GDN_DV] - ws[h] for h in H]
    o = [qs[h] + _bdot(aqk[h], v_new[h]) for h in H]
    ds = [_bdot_tn(k[h] * col(PK_EGD, h), v_new[h]) for h in H]
    for h in H:
        s_ref[h] = s[h] * pack[0:1, PK_EGL + h:PK_EGL + h + 1] + ds[h]
        on = o[h] * lax.rsqrt(jnp.mean(o[h] * o[h], axis=-1, keepdims=True) + 1e-6) * nw
        o_ref[:, h * GDN_DV:(h + 1) * GDN_DV] = on


def _gdn(q, k, v, pack, gct3, norm_w_row):
    T = q.shape[1]
    C = GDN_CHUNK
    hd_spec = pl.BlockSpec((GDN_HEADS, C, LANES), lambda i: (0, i, 0))
    return pl.pallas_call(
        _gdn_kernel,
        grid=(T // C,),
        in_specs=[hd_spec, hd_spec, hd_spec,
                  pl.BlockSpec((C, LANES), lambda i: (i, 0)),
                  pl.BlockSpec((1, GDN_HEADS, C), lambda i: (i, 0, 0)),
                  _const_spec((1, GDN_DV))],
        out_specs=pl.BlockSpec((C, GDN_HEADS * GDN_DV), lambda i: (i, 0)),
        out_shape=jax.ShapeDtypeStruct((T, GDN_HEADS * GDN_DV), F32),
        scratch_shapes=[pltpu.VMEM((GDN_HEADS, GDN_DK, GDN_DV), F32)],
        compiler_params=pltpu.CompilerParams(dimension_semantics=("arbitrary",)),
        name="gdn",
    )(q, k, v, pack, gct3, norm_w_row)


MB_QSPLIT = 2
MB_MAX_BLOCKS = MB_HD


def _moba_kernel(q_ref, k_ref, vt_ref, km_ref, o_ref, qx_ref, sa_ref, sb_ref, acc_ref, m_ref, l_ref):
    qi = pl.program_id(1)
    nb = k_ref.shape[0]
    bs = MB_BLOCK
    nq = MB_REP * bs
    nbp = MB_MAX_BLOCKS
    c = (MB_HD ** -0.5) * math.log2(math.e)

    qf = q_ref[...].reshape(nq, MB_HD)

    gate = lax.dot_general(km_ref[...], qf, (((1,), (1,)), ((), ())),
                           precision=lax.Precision.HIGHEST, preferred_element_type=F32)
    blk = lax.broadcasted_iota(jnp.int32, (nbp, nq), 0).astype(F32)
    qif = qi.astype(F32)
    gate = jnp.where(blk < qif, gate, -jnp.inf)
    sel = jnp.where(blk == qif, 1.0, 0.0)
    for _ in range(MB_TOPK):
        top = jnp.max(gate, axis=0, keepdims=True)
        first = jnp.min(jnp.where(gate == top, blk, float(nbp)), axis=0, keepdims=True)
        hit = (blk == first) & (top > -jnp.inf)
        sel = jnp.where(hit, 1.0, sel)
        gate = jnp.where(hit, -jnp.inf, gate)
    bias_t = jnp.where(sel > 0.0, 0.0, MASK_NEG)
    qc = qf * c
    qx_ref[0] = jnp.concatenate([qc, bias_t.T], axis=1).astype(BF16)
    qx_ref[1] = jnp.concatenate([qc, jnp.full((nq, nbp), MASK_NEG, F32)], axis=1).astype(BF16)

    w = nq // MB_QSPLIT
    groups = [slice(h * w, (h + 1) * w) for h in range(MB_QSPLIT)]

    def scores_into(dst_ref, t):
        kx = k_ref[jnp.minimum(t, nb - 1)]
        qset = jnp.where(t < qi, 0, 1)
        for lanes in groups:
            dst_ref[:, lanes] = lax.dot_general(kx, qx_ref[qset, lanes, :], (((1,), (1,)), ((), ())),
                                                preferred_element_type=F32)

    st = lax.dot_general(k_ref[qi], qx_ref[0], (((1,), (1,)), ((), ())), preferred_element_type=F32)
    kpos = lax.broadcasted_iota(jnp.int32, (bs, nq), 0)
    qpos = lax.broadcasted_iota(jnp.int32, (bs, nq), 1) % bs
    st = jnp.where(kpos <= qpos, st, MASK_NEG)
    m0 = jnp.max(st, axis=0, keepdims=True)
    p = jnp.exp2(st - m0)
    m_ref[...] = m0
    l_ref[...] = jnp.sum(p, axis=0, keepdims=True)
    acc_ref[...] = jnp.dot(vt_ref[qi], p.astype(BF16), preferred_element_type=F32)

    def absorb(src_ref, t):
        vt = vt_ref[jnp.minimum(t, nb - 1)]
        for lanes in groups:
            st = src_ref[:, lanes]
            m_old = m_ref[:, lanes]
            m_new = jnp.maximum(m_old, jnp.max(st, axis=0, keepdims=True))
            alpha = jnp.exp2(m_old - m_new)
            p = jnp.exp2(st - m_new)
            m_ref[:, lanes] = m_new
            l_ref[:, lanes] = alpha * l_ref[:, lanes] + jnp.sum(p, axis=0, keepdims=True)
            acc_ref[:, lanes] = alpha * acc_ref[:, lanes] + jnp.dot(vt, p.astype(BF16),
                                                                    preferred_element_type=F32)

    scores_into(sa_ref, 0)

    def body(i, carry):
        t = 2 * i
        scores_into(sb_ref, t + 1)
        absorb(sa_ref, t)
        scores_into(sa_ref, t + 2)
        absorb(sb_ref, t + 1)
        return carry

    lax.fori_loop(0, (qi + 1) // 2, body, 0)

    out_t = acc_ref[...] / l_ref[...]
    for r in range(MB_REP):
        o_ref[:, r * MB_HD:(r + 1) * MB_HD] = out_t[:, r * bs:(r + 1) * bs].T


def _moba(mq, mk, mvt, kmean):
    T = mq.shape[1]
    bs = MB_BLOCK
    nb = T // bs
    nq = MB_REP * bs
    assert nb <= MB_MAX_BLOCKS
    kmean = jnp.pad(kmean, ((0, 0), (0, MB_MAX_BLOCKS - nb), (0, 0)))
    return pl.pallas_call(
        _moba_kernel,
        grid=(MB_KV_HEADS, nb),
        in_specs=[
            pl.BlockSpec((MB_REP, bs, MB_HD), lambda g, i: (g, i, 0)),
            pl.BlockSpec((nb, None, bs, 2 * MB_HD), lambda g, i: (0, g, 0, 0)),
            pl.BlockSpec((nb, None, MB_HD, bs), lambda g, i: (0, g, 0, 0)),
            pl.BlockSpec((None, MB_MAX_BLOCKS, MB_HD), lambda g, i: (g, 0, 0)),
        ],
        out_specs=pl.BlockSpec((bs, MB_REP * MB_HD), lambda g, i: (i, g)),
        out_shape=jax.ShapeDtypeStruct((T, MB_HEADS * MB_HD), F32),
        scratch_shapes=[pltpu.VMEM((2, nq, 2 * MB_HD), BF16),
                        pltpu.VMEM((bs, nq), F32), pltpu.VMEM((bs, nq), F32),
                        pltpu.VMEM((MB_HD, nq), F32),
                        pltpu.VMEM((1, nq), F32), pltpu.VMEM((1, nq), F32)],
        compiler_params=pltpu.CompilerParams(dimension_semantics=("arbitrary", "arbitrary"),
                                             vmem_limit_bytes=VMEM_LIMIT),
        name="moba",
    )(mq, mk, mvt, kmean)


def _post_kernel(x_ref, go_ref, ma_ref, wg_ref, wgp_ref, wmp_ref, wo_ref, g1_ref, b1_ref,
                 rwt_ref, rb_ref, sg_ref, su_ref, sd_ref,
                 h_ref, res_ref, eidx_ref, wts_ref):
    tm = x_ref.shape[0]
    x = x_ref[...]
    zg = jnp.dot(x.astype(BF16), wg_ref[...], preferred_element_type=F32)
    gate = zg[:, :D_MODEL]
    o = go_ref[...] * (gate * _sigmoid(gate))
    y_gdn = jnp.dot(o.astype(BF16), wgp_ref[...], preferred_element_type=F32)
    y_mb = jnp.dot(ma_ref[...].astype(BF16), wmp_ref[...], preferred_element_type=F32)
    m = _sigmoid(zg[:, D_MODEL:2 * D_MODEL]) * y_gdn + _sigmoid(zg[:, 2 * D_MODEL:]) * y_mb
    mix = jnp.dot(m.astype(BF16), wo_ref[...], preferred_element_type=F32)
    h = _layer_norm(DEEPNORM_ALPHA * x + mix, g1_ref[...], b1_ref[...])
    _store_token_tiles(h_ref, h)
    hb = h.astype(BF16)

    hs = jnp.dot(hb, sg_ref[...], preferred_element_type=F32)
    hs = hs * _sigmoid(hs) * jnp.dot(hb, su_ref[...], preferred_element_type=F32)
    res_ref[...] = DEEPNORM_ALPHA * h + jnp.dot(hs.astype(BF16), sd_ref[...], preferred_element_type=F32)

    logits = lax.dot_general(rwt_ref[...], h, (((1,), (1,)), ((), ())),
                             precision=lax.Precision.HIGHEST, preferred_element_type=F32)
    scores = _sigmoid(logits)
    choice = scores + rb_ref[...]
    neg = -jnp.inf
    gi = lax.broadcasted_iota(jnp.int32, (GROUP_SIZE, tm), 0).astype(F32)
    gscore = []
    for g in range(N_GROUPS):
        cg = choice[g * GROUP_SIZE:(g + 1) * GROUP_SIZE, :]
        m1 = jnp.max(cg, axis=0, keepdims=True)
        i1 = jnp.min(jnp.where(cg == m1, gi, float(GROUP_SIZE)), axis=0, keepdims=True)
        m2 = jnp.max(jnp.where(gi == i1, neg, cg), axis=0, keepdims=True)
        gscore.append(m1 + m2)
    gs = jnp.concatenate(gscore, axis=0)
    gidx = lax.broadcasted_iota(jnp.int32, (N_GROUPS, tm), 0).astype(F32)
    gsel = jnp.zeros((N_GROUPS, tm), F32)
    for _ in range(TOPK_GROUPS):
        top = jnp.max(gs, axis=0, keepdims=True)
        first = jnp.min(jnp.where(gs == top, gidx, float(N_GROUPS)), axis=0, keepdims=True)
        hit = gidx == first
        gsel = jnp.where(hit, 1.0, gsel)
        gs = jnp.where(hit, neg, gs)
    masked = jnp.concatenate(
        [jnp.where(gsel[g:g + 1, :] > 0.0, choice[g * GROUP_SIZE:(g + 1) * GROUP_SIZE, :], neg)
         for g in range(N_GROUPS)], axis=0)
    ei = lax.broadcasted_iota(jnp.int32, (N_EXPERTS, tm), 0).astype(F32)
    idx_rows, w_rows = [], []
    for _ in range(TOP_K):
        top = jnp.max(masked, axis=0, keepdims=True)
        first = jnp.min(jnp.where(masked == top, ei, float(N_EXPERTS)), axis=0, keepdims=True)
        hit = ei == first
        idx_rows.append(first)
        w_rows.append(jnp.sum(jnp.where(hit, scores, 0.0), axis=0, keepdims=True))
        masked = jnp.where(hit, neg, masked)
    w = jnp.concatenate(w_rows, axis=0)
    w = w / (jnp.sum(w, axis=0, keepdims=True) + 1e-20) * ROUTED_SCALE
    eidx_ref[...] = jnp.concatenate(idx_rows, axis=0).astype(jnp.int32)
    wts_ref[...] = w


def _post(x2, gdn_o, moba_a, w_gates, w_gdn_proj, w_moba_proj, w_out, ln_g, ln_b,
          router_wt, router_b_col, sh_g, sh_u, sh_d):
    T = x2.shape[0]
    tm = TOK_TILE
    row_spec = pl.BlockSpec((tm, D_MODEL), lambda i: (i, 0))
    consts = [w_gates, w_gdn_proj, w_moba_proj, w_out, ln_g, ln_b, router_wt, router_b_col,
              sh_g, sh_u, sh_d]
    return pl.pallas_call(
        _post_kernel,
        grid=(T // tm,),
        in_specs=[row_spec, row_spec, row_spec] + [_const_spec(c.shape) for c in consts],
        out_specs=[pl.BlockSpec((tm * TOK_ROWS, LANES), lambda i: (i, 0)), row_spec,
                   pl.BlockSpec((TOP_K, tm), lambda i: (0, i)),
                   pl.BlockSpec((TOP_K, tm), lambda i: (0, i))],
        out_shape=[jax.ShapeDtypeStruct((T * TOK_ROWS, LANES), F32),
                   jax.ShapeDtypeStruct((T, D_MODEL), F32),
                   jax.ShapeDtypeStruct((TOP_K, T), jnp.int32),
                   jax.ShapeDtypeStruct((TOP_K, T), F32)],
        compiler_params=pltpu.CompilerParams(dimension_semantics=("arbitrary",),
                                             vmem_limit_bytes=VMEM_LIMIT),
        name="post",
    )(x2, gdn_o, moba_a, *consts)


ROW_UNROLL = 8


def _experts_kernel(iblk_ref, iexp_ref, ilo_ref, ihi_ref, src_ref, srcn_ref, dst_ref,
                    h_ref, wg_ref, wu_ref, wd_ref, out_ref,
                    xbuf, ybuf, wgb, wub, wdb, cast_exp, gsem, ssem, *, n_blocks):
    p = pl.program_id(0)
    n_items = pl.num_programs(0)
    w = iblk_ref[p]
    slot = w % 2
    first = (p == 0) | (w != iblk_ref[jnp.maximum(p - 1, 0)])
    last = (p == n_items - 1) | (w != iblk_ref[jnp.minimum(p + 1, n_items - 1)])
    lo, hi = ilo_ref[p], ihi_ref[p]
    block_rows = MOE_BLOCK * TOK_ROWS

    def tile_at(ref, row0):
        return ref.at[pl.ds(pl.multiple_of(row0, TOK_ROWS), TOK_ROWS)]

    def for_block_rows(fn):
        def group(g, c):
            for u in range(ROW_UNROLL):
                fn(g * ROW_UNROLL + u)
            return c
        lax.fori_loop(0, MOE_BLOCK // ROW_UNROLL, group, 0)

    def start_gather(rows_ref, s):
        for_block_rows(lambda i: pltpu.make_async_copy(
            tile_at(h_ref, rows_ref[0, 0, i]), tile_at(xbuf.at[s], i * TOK_ROWS), gsem.at[s]).start())

    def start_scatter(s):
        for_block_rows(lambda i: pltpu.make_async_copy(
            tile_at(ybuf.at[s], i * TOK_ROWS), tile_at(out_ref, dst_ref[0, 0, i]), ssem.at[s]).start())

    def wait_gather(s):
        pltpu.make_async_copy(h_ref.at[pl.ds(0, block_rows)], xbuf.at[s], gsem.at[s]).wait()

    def wait_scatter(s):
        pltpu.make_async_copy(ybuf.at[s], out_ref.at[pl.ds(0, block_rows)], ssem.at[s]).wait()

    @pl.when(p == 0)
    def _():
        ybuf[...] = jnp.zeros_like(ybuf)
        cast_exp[0] = -1
        start_gather(src_ref, 0)

    @pl.when(first)
    def _():
        @pl.when(w + 1 < n_blocks)
        def _():
            start_gather(srcn_ref, 1 - slot)

        @pl.when(w >= 2)
        def _():
            wait_scatter(slot)
        wait_gather(slot)

    @pl.when(hi > lo)
    def _():
        @pl.when(iexp_ref[p] != cast_exp[0])
        def _():
            wgb[...] = wg_ref[...].astype(BF16)
            wub[...] = wu_ref[...].astype(BF16)
            wdb[...] = wd_ref[...].astype(BF16)
            cast_exp[0] = iexp_ref[p]

        xb = _load_token_tiles(xbuf.at[slot], MOE_BLOCK).astype(BF16)
        hg = jnp.dot(xb, wgb[...], preferred_element_type=F32)
        hu = jnp.dot(xb, wub[...], preferred_element_type=F32)
        hb = (hg * _sigmoid(hg) * hu).astype(BF16)
        y = jnp.dot(hb, wdb[...], preferred_element_type=F32)
        row = lax.broadcasted_iota(jnp.int32, (MOE_BLOCK, 1), 0)
        mine = (row >= lo) & (row < hi)
        _store_token_tiles(ybuf.at[slot], jnp.where(mine, y, _load_token_tiles(ybuf.at[slot], MOE_BLOCK)))

    @pl.when(last)
    def _():
        start_scatter(slot)

    @pl.when(p == n_items - 1)
    def _():
        @pl.when(w >= 1)
        def _():
            wait_scatter(1 - slot)
        wait_scatter(slot)


def _experts(h, src_rows, dst_rows, item_blk, item_exp, item_lo, item_hi, wg, wu, wd):
    T = h.shape[0] // TOK_ROWS
    n_blocks = src_rows.shape[0]
    n_items = item_blk.shape[0]
    rows = MOE_BLOCK
    cur_spec = pl.BlockSpec((1, 1, rows), lambda p, ib, ie, il, ih: (ib[p], 0, 0), memory_space=pltpu.SMEM)
    nxt_spec = pl.BlockSpec((1, 1, rows), lambda p, ib, ie, il, ih: (jnp.minimum(ib[p] + 1, n_blocks - 1), 0, 0),
                            memory_space=pltpu.SMEM)

    def w_spec(shape):
        return pl.BlockSpec((None,) + shape, lambda p, ib, ie, il, ih: (ie[p], 0, 0))

    grid_spec = pltpu.PrefetchScalarGridSpec(
        num_scalar_prefetch=4,
        grid=(n_items,),
        in_specs=[cur_spec, nxt_spec, cur_spec, pl.BlockSpec(memory_space=pl.ANY),
                  w_spec((D_MODEL, EXPERT_DIM)), w_spec((D_MODEL, EXPERT_DIM)), w_spec((EXPERT_DIM, D_MODEL))],
        out_specs=pl.BlockSpec(memory_space=pl.ANY),
        scratch_shapes=[pltpu.VMEM((2, rows * TOK_ROWS, LANES), F32),
                        pltpu.VMEM((2, rows * TOK_ROWS, LANES), F32),
                        pltpu.VMEM((D_MODEL, EXPERT_DIM), BF16), pltpu.VMEM((D_MODEL, EXPERT_DIM), BF16),
                        pltpu.VMEM((EXPERT_DIM, D_MODEL), BF16),
                        pltpu.SMEM((1,), jnp.int32),
                        pltpu.SemaphoreType.DMA((2,)), pltpu.SemaphoreType.DMA((2,))],
    )
    return pl.pallas_call(
        functools.partial(_experts_kernel, n_blocks=n_blocks),
        grid_spec=grid_spec,
        out_shape=jax.ShapeDtypeStruct((TOP_K * T * TOK_ROWS, LANES), F32),
        compiler_params=pltpu.CompilerParams(dimension_semantics=("arbitrary",),
                                             vmem_limit_bytes=VMEM_LIMIT,
                                             disable_bounds_checks=True),
        name="experts",
    )(item_blk, item_exp, item_lo, item_hi, src_rows, src_rows, dst_rows, h, wg, wu, wd)


def _combine_kernel(res_ref, y_ref, w_ref, g_ref, b_ref, o_ref):
    tm = res_ref.shape[0]
    acc = res_ref[...]
    w = w_ref[...]
    for k in range(TOP_K):
        acc = acc + w[:, k:k + 1] * _load_token_tiles(y_ref.at[k], tm)
    o_ref[...] = _layer_norm(acc, g_ref[...], b_ref[...])


def _combine(res, y3, wts, ln_g, ln_b):
    T = res.shape[0]
    tm = 128
    return pl.pallas_call(
        _combine_kernel,
        grid=(T // tm,),
        in_specs=[pl.BlockSpec((tm, D_MODEL), lambda i: (i, 0)),
                  pl.BlockSpec((TOP_K, tm * TOK_ROWS, LANES), lambda i: (0, i, 0)),
                  pl.BlockSpec((tm, TOP_K), lambda i: (i, 0)),
                  _const_spec((1, D_MODEL)), _const_spec((1, D_MODEL))],
        out_specs=pl.BlockSpec((tm, D_MODEL), lambda i: (i, 0)),
        out_shape=jax.ShapeDtypeStruct((T, D_MODEL), F32),
        compiler_params=pltpu.CompilerParams(dimension_semantics=("arbitrary",)),
        name="combine",
    )(res, y3, wts, ln_g, ln_b)


def _dispatch_plan(eidx_t, n_tok):
    nk = n_tok * TOP_K
    n_blocks = nk // MOE_BLOCK
    e_flat = eidx_t.T.reshape(nk)
    _, order = lax.sort((e_flat, jnp.arange(nk, dtype=jnp.int32)), num_keys=1)
    experts = jnp.arange(N_EXPERTS, dtype=jnp.int32)
    counts = jnp.sum((e_flat[None, :] == experts[:, None]).astype(jnp.int32), axis=1)
    start = jnp.cumsum(counts) - counts
    pos = jnp.sort(jnp.concatenate([jnp.arange(n_blocks, dtype=jnp.int32) * MOE_BLOCK, start]))
    nxt = jnp.concatenate([pos[1:], jnp.full((1,), nk, jnp.int32)])
    item_blk = jnp.minimum(pos // MOE_BLOCK, n_blocks - 1)
    item_exp = jnp.sum((start[None, :] <= pos[:, None]).astype(jnp.int32), axis=1) - 1
    item_lo = pos - item_blk * MOE_BLOCK
    item_hi = jnp.minimum(nxt, (item_blk + 1) * MOE_BLOCK) - item_blk * MOE_BLOCK
    tok, slot_k = order // TOP_K, order % TOP_K
    shape3 = (n_blocks, 1, MOE_BLOCK)
    src_rows = (tok * TOK_ROWS).reshape(shape3)
    dst_rows = ((slot_k * n_tok + tok) * TOK_ROWS).reshape(shape3)
    return src_rows, dst_rows, item_blk, item_exp, item_lo, item_hi


def kernel(x, w_in, conv_w, gdn_a_log, gdn_dt_bias, gdn_norm_w, w_gdn_proj, w_moba_proj, w_out,
           ln1_g, ln1_b, router_w, router_bias, exp_w_gate, exp_w_up, exp_w_down,
           sh_w_gate, sh_w_up, sh_w_down, ln2_g, ln2_b):
    B, T, D = x.shape
    assert B == 1 and D == D_MODEL and T % MB_BLOCK == 0
    x2 = x.reshape(T, D)

    o_gate = CONV_CH
    o_b = o_gate + GDN_HEADS * GDN_DV
    o_a = o_b + GDN_HEADS
    o_mq = o_a + GDN_HEADS
    o_gg = o_mq + (MB_HEADS + 2 * MB_KV_HEADS) * MB_HD
    w_qkv = w_in[:, :CONV_CH].astype(BF16)
    w_pack = jnp.concatenate([w_in[:, o_b:o_a]] + [w_in[:, o_a:o_mq]] * 4
                             + [jnp.zeros((D, LANES - 5 * GDN_HEADS), F32)], axis=1).astype(BF16)
    w_mb = w_in[:, o_mq:o_gg].astype(BF16)
    w_gates = jnp.concatenate([w_in[:, o_gate:o_b], w_in[:, o_gg:]], axis=1).astype(BF16)

    def lane_row(v):
        return jnp.zeros((1, LANES), F32).at[0, PK_GC:PK_GC + 4 * GDN_HEADS].set(jnp.tile(v.astype(F32), 4))

    q, k, v, pack, gct = _qkv_proj(x2, w_qkv, w_pack, conv_w.astype(F32),
                                   lane_row(gdn_a_log), lane_row(gdn_dt_bias))
    gct3 = gct.reshape(GDN_HEADS, T // GDN_CHUNK, GDN_CHUNK).transpose(1, 0, 2)
    gdn_o = _gdn(q, k, v, pack, gct3, gdn_norm_w.astype(F32).reshape(1, GDN_DV))

    half = ROT_DIM // 2
    inv = ROPE_THETA ** (-jnp.arange(half, dtype=F32) / half)
    ang = jnp.arange(T).astype(F32)[:, None] * inv[None, :]
    ones = jnp.ones((T, MB_HD - ROT_DIM), F32)
    cos_t = jnp.concatenate([jnp.cos(ang), jnp.cos(ang), ones], axis=1)
    sin_t = jnp.concatenate([-jnp.sin(ang), jnp.sin(ang), 0.0 * ones], axis=1)
    mq, mk, mvt, kmean = _moba_proj(x2, w_mb, cos_t, sin_t)
    moba_a = _moba(mq, mk, mvt, kmean.transpose(1, 0, 2))

    h, res, eidx_t, wts_t = _post(
        x2, gdn_o, moba_a, w_gates, w_gdn_proj.astype(BF16), w_moba_proj.astype(BF16),
        w_out.astype(BF16), ln1_g.reshape(1, D), ln1_b.reshape(1, D),
        router_w.T, router_bias.reshape(N_EXPERTS, 1),
        sh_w_gate.astype(BF16), sh_w_up.astype(BF16), sh_w_down.astype(BF16))

    plan = _dispatch_plan(eidx_t, T)
    y = _experts(h, *plan, exp_w_gate, exp_w_up, exp_w_down)
    y3 = y.reshape(TOP_K, T * TOK_ROWS, LANES)
    out = _combine(res, y3, wts_t.T, ln2_g.reshape(1, D), ln2_b.reshape(1, D))
    return out.reshape(B, T, D)
```

```python
import functools
import math

import jax
import jax.numpy as jnp
from jax import lax
from jax.experimental import pallas as pl
from jax.experimental.pallas import tpu as pltpu

F32 = jnp.float32
BF16 = jnp.bfloat16

D_MODEL = 1024
DEPTH = 1
GDN_HEADS = 8
GDN_DK = 128
GDN_DV = 128
GDN_CONV = 4
GDN_CHUNK = 64
MB_HEADS = 8
MB_KV_HEADS = 2
MB_REP = MB_HEADS // MB_KV_HEADS
MB_HD = 128
MB_BLOCK = 256
MB_TOPK = 3
ROT_DIM = MB_HD // 4
ROPE_THETA = 500000.0
N_EXPERTS = 256
TOP_K = 8
N_GROUPS = 8
GROUP_SIZE = N_EXPERTS // N_GROUPS
TOPK_GROUPS = 4
EXPERT_DIM = 256
SHARED_DIM = 256
ROUTED_SCALE = 2.5
MOE_BLOCK = 128
DEEPNORM_ALPHA = (2.0 * DEPTH) ** 0.25
LN_EPS = 1e-5
MASK_NEG = -1e30

QK_COLS = GDN_HEADS * GDN_DK
CONV_CH = 2 * QK_COLS + GDN_HEADS * GDN_DV
LANES = 128
SUBLANES = 8
VMEM_LIMIT = 56 * 1024 * 1024

TOK_TILE = 256

PK_BETA, PK_GC, PK_EG, PK_EGD, PK_EGL = 0, 8, 16, 24, 32


def _sigmoid(x):
    return 1.0 / (1.0 + jnp.exp(-x))


def _softplus(x):
    return jnp.maximum(x, 0.0) + jnp.log(1.0 + jnp.exp(-jnp.abs(x)))


def _bdot(a, b):
    return jnp.dot(a.astype(BF16), b.astype(BF16), preferred_element_type=F32)


def _bdot_nt(a, b):
    return lax.dot_general(a.astype(BF16), b.astype(BF16), (((1,), (1,)), ((), ())),
                           preferred_element_type=F32)


def _bdot_tn(a, b):
    return lax.dot_general(a.astype(BF16), b.astype(BF16), (((0,), (0,)), ((), ())),
                           preferred_element_type=F32)


def _layer_norm(x, g, b):
    mu = jnp.mean(x, axis=-1, keepdims=True)
    xc = x - mu
    var = jnp.mean(xc * xc, axis=-1, keepdims=True)
    return xc * lax.rsqrt(var + LN_EPS) * g + b


TOK_ROWS = D_MODEL // LANES


def _store_token_tiles(ref, x):
    n = x.shape[0]
    for j in range(TOK_ROWS):
        ref[pl.ds(j, n, stride=TOK_ROWS), :] = x[:, j * LANES:(j + 1) * LANES]


def _load_token_tiles(ref, n):
    return jnp.concatenate([ref[pl.ds(j, n, stride=TOK_ROWS), :] for j in range(TOK_ROWS)], axis=1)


def _const_spec(shape):
    nd = len(shape)
    return pl.BlockSpec(shape, lambda *_: (0,) * nd, pipeline_mode=pl.Buffered(1))


def _qkv_kernel(x_ref, w_ref, wp_ref, cw_ref, alog_ref, dtb_ref,
                q_ref, k_ref, v_ref, pack_ref, gct_ref, z_ref):
    tm = x_ref.shape[0]
    halo = SUBLANES

    @pl.when(pl.program_id(0) == 0)
    def _():
        z_ref[0:halo, :] = jnp.zeros((halo, CONV_CH), F32)

    xb = x_ref[...].astype(BF16)
    z_ref[halo:halo + tm, :] = jnp.dot(xb, w_ref[...], preferred_element_type=F32)

    for c in range(CONV_CH // LANES):
        cols = slice(c * LANES, (c + 1) * LANES)
        acc = z_ref[halo:halo + tm, cols] * cw_ref[GDN_CONV - 1:GDN_CONV, cols]
        for s in range(1, GDN_CONV):
            acc = acc + z_ref[halo - s:halo - s + tm, cols] * cw_ref[GDN_CONV - 1 - s:GDN_CONV - s, cols]
        y = acc * _sigmoid(acc)
        h = c % GDN_HEADS
        if c < GDN_HEADS:
            y = y * lax.rsqrt(jnp.sum(y * y, axis=-1, keepdims=True) + 1e-6) * (GDN_DK ** -0.5)
            q_ref[h] = y
        elif c < 2 * GDN_HEADS:
            y = y * lax.rsqrt(jnp.sum(y * y, axis=-1, keepdims=True) + 1e-6)
            k_ref[h] = y
        else:
            v_ref[h] = y

    z_ref[0:halo, :] = z_ref[tm:tm + halo, :]

    zp = jnp.dot(xb, wp_ref[...], preferred_element_type=F32)
    beta = _sigmoid(zp)
    g = -jnp.exp(alog_ref[...]) * _softplus(zp + dtb_ref[...])
    row = lax.broadcasted_iota(jnp.int32, (tm, LANES), 0) % GDN_CHUNK
    gc = g
    step = 1
    while step < GDN_CHUNK:
        gc = gc + jnp.where(row >= step, pltpu.roll(gc, step, axis=0), 0.0)
        step *= 2
    gl = gc.reshape(tm // GDN_CHUNK, GDN_CHUNK, LANES)[:, GDN_CHUNK - 1:GDN_CHUNK, :]
    gl = jnp.broadcast_to(gl, (tm // GDN_CHUNK, GDN_CHUNK, LANES)).reshape(tm, LANES)
    lane = lax.broadcasted_iota(jnp.int32, (tm, LANES), 1)
    pack = jnp.where(lane < PK_GC, beta,
           jnp.where(lane < PK_EG, gc,
           jnp.where(lane < PK_EGD, jnp.exp(gc),
           jnp.where(lane < PK_EGL, jnp.exp(gl - gc), jnp.exp(gl)))))
    pack_ref[...] = pack
    gct_ref[...] = pack.T[PK_GC:PK_GC + GDN_HEADS, :]


def _qkv_proj(x2, w_qkv, w_pack, conv_w, alog_row, dtb_row):
    T = x2.shape[0]
    tm = TOK_TILE
    hd_spec = pl.BlockSpec((GDN_HEADS, tm, LANES), lambda i: (0, i, 0))
    hd_shape = jax.ShapeDtypeStruct((GDN_HEADS, T, LANES), F32)
    return pl.pallas_call(
        _qkv_kernel,
        grid=(T // tm,),
        in_specs=[
            pl.BlockSpec((tm, D_MODEL), lambda i: (i, 0)),
            _const_spec((D_MODEL, CONV_CH)),
            _const_spec((D_MODEL, LANES)),
            _const_spec((GDN_CONV, CONV_CH)),
            _const_spec((1, LANES)),
            _const_spec((1, LANES)),
        ],
        out_specs=[hd_spec, hd_spec, hd_spec,
                   pl.BlockSpec((tm, LANES), lambda i: (i, 0)),
                   pl.BlockSpec((GDN_HEADS, tm), lambda i: (0, i))],
        out_shape=[hd_shape, hd_shape, hd_shape,
                   jax.ShapeDtypeStruct((T, LANES), F32),
                   jax.ShapeDtypeStruct((GDN_HEADS, T), F32)],
        scratch_shapes=[pltpu.VMEM((tm + SUBLANES, CONV_CH), F32)],
        compiler_params=pltpu.CompilerParams(dimension_semantics=("arbitrary",),
                                             vmem_limit_bytes=VMEM_LIMIT),
        name="qkv_proj",
    )(x2, w_qkv, w_pack, conv_w, alog_row, dtb_row)


MB_LROWS = 16
MB_VROWS = MB_HD + MB_LROWS


def _rope(xh, cos_t, sin_t, lane):
    half = ROT_DIM // 2
    swapped = jnp.where(lane < half, pltpu.roll(xh, LANES - half, axis=1), pltpu.roll(xh, half, axis=1))
    return xh * cos_t + swapped * sin_t


def _moba_proj_kernel(x_ref, w_ref, cos_ref, sin_ref, q_ref, k_ref, vt_ref, km_ref):
    tm = x_ref.shape[0]
    z = jnp.dot(x_ref[...].astype(BF16), w_ref[...], preferred_element_type=F32)
    cos_t = cos_ref[...]
    sin_t = sin_ref[...]
    lane = lax.broadcasted_iota(jnp.int32, (tm, LANES), 1)
    for h in range(MB_HEADS):
        q_ref[h] = _rope(z[:, h * MB_HD:(h + 1) * MB_HD], cos_t, sin_t, lane)
    koff = MB_HEADS * MB_HD
    voff = koff + MB_KV_HEADS * MB_HD
    blk_onehot = jnp.where(lane == pl.program_id(0), 1.0, 0.0).astype(BF16)
    for g in range(MB_KV_HEADS):
        kr = _rope(z[:, koff + g * MB_HD:koff + (g + 1) * MB_HD], cos_t, sin_t, lane)
        k_ref[0, g] = jnp.concatenate([kr.astype(BF16), blk_onehot], axis=1)
        km_ref[0, g:g + 1, :] = jnp.mean(kr, axis=0, keepdims=True)
        vt_ref[0, g] = jnp.concatenate([z[:, voff + g * MB_HD:voff + (g + 1) * MB_HD].T,
                                        jnp.ones((MB_LROWS, tm), F32)], axis=0).astype(BF16)


def _moba_proj(x2, w_mb, cos_t, sin_t):
    T = x2.shape[0]
    tm = MB_BLOCK
    nb = T // tm
    return pl.pallas_call(
        _moba_proj_kernel,
        grid=(nb,),
        in_specs=[
            pl.BlockSpec((tm, D_MODEL), lambda i: (i, 0)),
            _const_spec(w_mb.shape),
            pl.BlockSpec((tm, LANES), lambda i: (i, 0)),
            pl.BlockSpec((tm, LANES), lambda i: (i, 0)),
        ],
        out_specs=[
            pl.BlockSpec((MB_HEADS, tm, MB_HD), lambda i: (0, i, 0)),
            pl.BlockSpec((1, MB_KV_HEADS, tm, 2 * MB_HD), lambda i: (i, 0, 0, 0)),
            pl.BlockSpec((1, MB_KV_HEADS, MB_VROWS, tm), lambda i: (i, 0, 0, 0)),
            pl.BlockSpec((1, MB_KV_HEADS, MB_HD), lambda i: (i, 0, 0)),
        ],
        out_shape=[
            jax.ShapeDtypeStruct((MB_HEADS, T, MB_HD), F32),
            jax.ShapeDtypeStruct((nb, MB_KV_HEADS, tm, 2 * MB_HD), BF16),
            jax.ShapeDtypeStruct((nb, MB_KV_HEADS, MB_VROWS, tm), BF16),
            jax.ShapeDtypeStruct((nb, MB_KV_HEADS, MB_HD), F32),
        ],
        compiler_params=pltpu.CompilerParams(dimension_semantics=("arbitrary",),
                                             vmem_limit_bytes=VMEM_LIMIT),
        name="moba_proj",
    )(x2, w_mb, cos_t, sin_t)


def _unit_lower_inverse(a_list):
    c = a_list[0].shape[0]
    rr = lax.broadcasted_iota(jnp.int32, (c, c), 0)
    cc = lax.broadcasted_iota(jnp.int32, (c, c), 1)
    eye = jnp.where(rr == cc, 1.0, 0.0).astype(F32)
    inv = [eye - a for a in a_list]
    p = [_bdot(a, a) for a in a_list]
    n = 2
    while True:
        inv = [x + _bdot(x, y) for x, y in zip(inv, p)]
        n *= 2
        if n >= c:
            break
        p = [_bdot(y, y) for y in p]
    return inv


def _gdn_kernel(q_ref, k_ref, v_ref, pack_ref, gct_ref, nw_ref, o_ref, s_ref):
    C = GDN_CHUNK
    H = range(GDN_HEADS)

    @pl.when(pl.program_id(0) == 0)
    def _():
        s_ref[...] = jnp.zeros_like(s_ref)

    rr = lax.broadcasted_iota(jnp.int32, (C, C), 0)
    cc = lax.broadcasted_iota(jnp.int32, (C, C), 1)
    tril = rr >= cc
    strict = rr > cc
    pack = pack_ref[...]
    nw = nw_ref[...]

    def col(base, h):
        return pack[:, base + h:base + h + 1]

    q = [q_ref[h] for h in H]
    k = [k_ref[h] for h in H]
    kb = [k[h] * col(PK_BETA, h) for h in H]
    decay = [jnp.where(tril, jnp.exp(jnp.where(tril, col(PK_GC, h) - gct_ref[0, h:h + 1, :], 0.0)), 0.0)
             for h in H]
    kk = [_bdot_nt(kb[h], k[h]) for h in H]
    qk = [_bdot_nt(q[h], k[h]) for h in H]
    a = [jnp.where(strict, kk[h] * decay[h], 0.0) for h in H]
    aqk = [qk[h] * decay[h] for h in H]
    tinv = _unit_lower_inverse(a)
    rhs = [jnp.concatenate([v_ref[h] * col(PK_BETA, h), kb[h] * col(PK_EG, h)], axis=1) for h in H]
    sol = [_bdot(tinv[h], rhs[h]) for h in H]
    s = [s_ref[h] for h in H]
    ws = [_bdot(sol[h][:, GDN_DV:], s[h]) for h in H]
    qs = [_bdot(q[h] * col(PK_EG, h), s[h]) for h in H]
    v_new = [sol[h][:, :GDN_DV] - ws[h] for h in H]
    o = [qs[h] + _bdot(aqk[h], v_new[h]) for h in H]
    ds = [_bdot_tn(k[h] * col(PK_EGD, h), v_new[h]) for h in H]
    for h in H:
        s_ref[h] = s[h] * pack[0:1, PK_EGL + h:PK_EGL + h + 1] + ds[h]
        on = o[h] * lax.rsqrt(jnp.mean(o[h] * o[h], axis=-1, keepdims=True) + 1e-6) * nw
        o_ref[:, h * GDN_DV:(h + 1) * GDN_DV] = on


def _gdn(q, k, v, pack, gct3, norm_w_row):
    T = q.shape[1]
    C = GDN_CHUNK
    hd_spec = pl.BlockSpec((GDN_HEADS, C, LANES), lambda i: (0, i, 0))
    return pl.pallas_call(
        _gdn_kernel,
        grid=(T // C,),
        in_specs=[hd_spec, hd_spec, hd_spec,
                  pl.BlockSpec((C, LANES), lambda i: (i, 0)),
                  pl.BlockSpec((1, GDN_HEADS, C), lambda i: (i, 0, 0)),
                  _const_spec((1, GDN_DV))],
        out_specs=pl.BlockSpec((C, GDN_HEADS * GDN_DV), lambda i: (i, 0)),
        out_shape=jax.ShapeDtypeStruct((T, GDN_HEADS * GDN_DV), F32),
        scratch_shapes=[pltpu.VMEM((GDN_HEADS, GDN_DK, GDN_DV), F32)],
        compiler_params=pltpu.CompilerParams(dimension_semantics=("arbitrary",)),
        name="gdn",
    )(q, k, v, pack, gct3, norm_w_row)


MB_QSPLIT = 2
MB_TRIP = 4
MB_MAX_BLOCKS = MB_HD


def _moba_kernel(q_ref, k_ref, vt_ref, km_ref, o_ref, qx_ref, sa_ref, sb_ref, acc_ref, m_ref):
    qi = pl.program_id(1)
    nb = k_ref.shape[0]
    bs = MB_BLOCK
    nq = MB_REP * bs
    nbp = MB_MAX_BLOCKS
    c = (MB_HD ** -0.5) * math.log2(math.e)

    qf = q_ref[...].reshape(nq, MB_HD)

    gate = lax.dot_general(km_ref[...], qf, (((1,), (1,)), ((), ())),
                           precision=lax.Precision.HIGHEST, preferred_element_type=F32)
    blk = lax.broadcasted_iota(jnp.int32, (nbp, nq), 0).astype(F32)
    qif = qi.astype(F32)
    gate = jnp.where(blk < qif, gate, -jnp.inf)
    sel = jnp.where(blk == qif, 1.0, 0.0)
    for _ in range(MB_TOPK):
        top = jnp.max(gate, axis=0, keepdims=True)
        first = jnp.min(jnp.where(gate == top, blk, float(nbp)), axis=0, keepdims=True)
        hit = (blk == first) & (top > -jnp.inf)
        sel = jnp.where(hit, 1.0, sel)
        gate = jnp.where(hit, -jnp.inf, gate)
    bias_t = jnp.where(sel > 0.0, 0.0, MASK_NEG)
    qc = qf * c
    qx_ref[0] = jnp.concatenate([qc, bias_t.T], axis=1).astype(BF16)
    qx_ref[1] = jnp.concatenate([qc, jnp.full((nq, nbp), MASK_NEG, F32)], axis=1).astype(BF16)

    w = nq // MB_QSPLIT
    groups = [slice(h * w, (h + 1) * w) for h in range(MB_QSPLIT)]

    def scores_into(dst_ref, t):
        kx = k_ref[jnp.minimum(t, nb - 1)]
        qset = jnp.where(t < qi, 0, 1)
        for lanes in groups:
            dst_ref[:, lanes] = lax.dot_general(kx, qx_ref[qset, lanes, :], (((1,), (1,)), ((), ())),
                                                preferred_element_type=F32)

    st = lax.dot_general(k_ref[qi], qx_ref[0], (((1,), (1,)), ((), ())), preferred_element_type=F32)
    kpos = lax.broadcasted_iota(jnp.int32, (bs, nq), 0)
    qpos = lax.broadcasted_iota(jnp.int32, (bs, nq), 1) % bs
    st = jnp.where(kpos <= qpos, st, MASK_NEG)
    m0 = jnp.max(st, axis=0, keepdims=True)
    p = jnp.exp2(st - m0)
    m_ref[...] = m0
    acc_ref[...] = jnp.dot(vt_ref[qi], p.astype(BF16), preferred_element_type=F32)

    def absorb(src_ref, t):
        vt = vt_ref[jnp.minimum(t, nb - 1)]
        for lanes in groups:
            st = src_ref[:, lanes]
            m_old = m_ref[:, lanes]
            m_new = jnp.maximum(m_old, jnp.max(st, axis=0, keepdims=True))
            alpha = jnp.exp2(m_old - m_new)
            p = jnp.exp2(st - m_new)
            m_ref[:, lanes] = m_new
            acc_ref[:, lanes] = alpha * acc_ref[:, lanes] + jnp.dot(vt, p.astype(BF16),
                                                                    preferred_element_type=F32)

    scores_into(sa_ref, 0)

    def body(i, carry):
        t = MB_TRIP * i
        for u in range(0, MB_TRIP, 2):
            scores_into(sb_ref, t + u + 1)
            absorb(sa_ref, t + u)
            scores_into(sa_ref, t + u + 2)
            absorb(sb_ref, t + u + 1)
        return carry

    lax.fori_loop(0, (qi + MB_TRIP - 1) // MB_TRIP, body, 0)

    out_t = acc_ref[0:MB_HD, :] / acc_ref[MB_HD:MB_HD + 1, :]
    for r in range(MB_REP):
        o_ref[:, r * MB_HD:(r + 1) * MB_HD] = out_t[:, r * bs:(r + 1) * bs].T


def _moba(mq, mk, mvt, kmean):
    T = mq.shape[1]
    bs = MB_BLOCK
    nb = T // bs
    nq = MB_REP * bs
    assert nb <= MB_MAX_BLOCKS
    kmean = jnp.pad(kmean, ((0, 0), (0, MB_MAX_BLOCKS - nb), (0, 0)))
    return pl.pallas_call(
        _moba_kernel,
        grid=(MB_KV_HEADS, nb),
        in_specs=[
            pl.BlockSpec((MB_REP, bs, MB_HD), lambda g, i: (g, i, 0)),
            pl.BlockSpec((nb, None, bs, 2 * MB_HD), lambda g, i: (0, g, 0, 0)),
            pl.BlockSpec((nb, None, MB_VROWS, bs), lambda g, i: (0, g, 0, 0)),
            pl.BlockSpec((None, MB_MAX_BLOCKS, MB_HD), lambda g, i: (g, 0, 0)),
        ],
        out_specs=pl.BlockSpec((bs, MB_REP * MB_HD), lambda g, i: (i, g)),
        out_shape=jax.ShapeDtypeStruct((T, MB_HEADS * MB_HD), F32),
        scratch_shapes=[pltpu.VMEM((2, nq, 2 * MB_HD), BF16),
                        pltpu.VMEM((bs, nq), F32), pltpu.VMEM((bs, nq), F32),
                        pltpu.VMEM((MB_VROWS, nq), F32),
                        pltpu.VMEM((1, nq), F32)],
        compiler_params=pltpu.CompilerParams(dimension_semantics=("arbitrary", "arbitrary"),
                                             vmem_limit_bytes=VMEM_LIMIT),
        name="moba",
    )(mq, mk, mvt, kmean)


def _post_kernel(x_ref, go_ref, ma_ref, wg_ref, wgp_ref, wmp_ref, wo_ref, g1_ref, b1_ref,
                 rwt_ref, rb_ref, sg_ref, su_ref, sd_ref,
                 h_ref, res_ref, eidx_ref, wts_ref):
    tm = x_ref.shape[0]
    x = x_ref[...]
    zg = jnp.dot(x.astype(BF16), wg_ref[...], preferred_element_type=F32)
    gate = zg[:, :D_MODEL]
    o = go_ref[...] * (gate * _sigmoid(gate))
    y_gdn = jnp.dot(o.astype(BF16), wgp_ref[...], preferred_element_type=F32)
    y_mb = jnp.dot(ma_ref[...].astype(BF16), wmp_ref[...], preferred_element_type=F32)
    m = _sigmoid(zg[:, D_MODEL:2 * D_MODEL]) * y_gdn + _sigmoid(zg[:, 2 * D_MODEL:]) * y_mb
    mix = jnp.dot(m.astype(BF16), wo_ref[...], preferred_element_type=F32)
    h = _layer_norm(DEEPNORM_ALPHA * x + mix, g1_ref[...], b1_ref[...])
    _store_token_tiles(h_ref, h)
    hb = h.astype(BF16)

    hs = jnp.dot(hb, sg_ref[...], preferred_element_type=F32)
    hs = hs * _sigmoid(hs) * jnp.dot(hb, su_ref[...], preferred_element_type=F32)
    res_ref[...] = DEEPNORM_ALPHA * h + jnp.dot(hs.astype(BF16), sd_ref[...], preferred_element_type=F32)

    logits = lax.dot_general(rwt_ref[...], h, (((1,), (1,)), ((), ())),
                             precision=lax.Precision.HIGHEST, preferred_element_type=F32)
    scores = _sigmoid(logits)
    choice = scores + rb_ref[...]
    neg = -jnp.inf
    gi = lax.broadcasted_iota(jnp.int32, (GROUP_SIZE, tm), 0).astype(F32)
    gscore = []
    for g in range(N_GROUPS):
        cg = choice[g * GROUP_SIZE:(g + 1) * GROUP_SIZE, :]
        m1 = jnp.max(cg, axis=0, keepdims=True)
        i1 = jnp.min(jnp.where(cg == m1, gi, float(GROUP_SIZE)), axis=0, keepdims=True)
        m2 = jnp.max(jnp.where(gi == i1, neg, cg), axis=0, keepdims=True)
        gscore.append(m1 + m2)
    gs = jnp.concatenate(gscore, axis=0)
    gidx = lax.broadcasted_iota(jnp.int32, (N_GROUPS, tm), 0).astype(F32)
    gsel = jnp.zeros((N_GROUPS, tm), F32)
    for _ in range(TOPK_GROUPS):
        top = jnp.max(gs, axis=0, keepdims=True)
        first = jnp.min(jnp.where(gs == top, gidx, float(N_GROUPS)), axis=0, keepdims=True)
        hit = gidx == first
        gsel = jnp.where(hit, 1.0, gsel)
        gs = jnp.where(hit, neg, gs)
    masked = jnp.concatenate(
        [jnp.where(gsel[g:g + 1, :] > 0.0, choice[g * GROUP_SIZE:(g + 1) * GROUP_SIZE, :], neg)
         for g in range(N_GROUPS)], axis=0)
    ei = lax.broadcasted_iota(jnp.int32, (N_EXPERTS, tm), 0).astype(F32)
    idx_rows, w_rows = [], []
    for _ in range(TOP_K):
        top = jnp.max(masked, axis=0, keepdims=True)
        first = jnp.min(jnp.where(masked == top, ei, float(N_EXPERTS)), axis=0, keepdims=True)
        hit = ei == first
        idx_rows.append(first)
        w_rows.append(jnp.sum(jnp.where(hit, scores, 0.0), axis=0, keepdims=True))
        masked = jnp.where(hit, neg, masked)
    w = jnp.concatenate(w_rows, axis=0)
    w = w / (jnp.sum(w, axis=0, keepdims=True) + 1e-20) * ROUTED_SCALE
    eidx_ref[...] = jnp.concatenate(idx_rows, axis=0).astype(jnp.int32)
    wts_ref[...] = w


def _post(x2, gdn_o, moba_a, w_gates, w_gdn_proj, w_moba_proj, w_out, ln_g, ln_b,
          router_wt, router_b_col, sh_g, sh_u, sh_d):
    T = x2.shape[0]
    tm = TOK_TILE
    row_spec = pl.BlockSpec((tm, D_MODEL), lambda i: (i, 0))
    consts = [w_gates, w_gdn_proj, w_moba_proj, w_out, ln_g, ln_b, router_wt, router_b_col,
              sh_g, sh_u, sh_d]
    return pl.pallas_call(
        _post_kernel,
        grid=(T // tm,),
        in_specs=[row_spec, row_spec, row_spec] + [_const_spec(c.shape) for c in consts],
        out_specs=[pl.BlockSpec((tm * TOK_ROWS, LANES), lambda i: (i, 0)), row_spec,
                   pl.BlockSpec((TOP_K, tm), lambda i: (0, i)),
                   pl.BlockSpec((TOP_K, tm), lambda i: (0, i))],
        out_shape=[jax.ShapeDtypeStruct((T * TOK_ROWS, LANES), F32),
                   jax.ShapeDtypeStruct((T, D_MODEL), F32),
                   jax.ShapeDtypeStruct((TOP_K, T), jnp.int32),
                   jax.ShapeDtypeStruct((TOP_K, T), F32)],
        compiler_params=pltpu.CompilerParams(dimension_semantics=("arbitrary",),
                                             vmem_limit_bytes=VMEM_LIMIT),
        name="post",
    )(x2, gdn_o, moba_a, *consts)


ROW_UNROLL = 8


def _experts_kernel(iblk_ref, iexp_ref, ilo_ref, ihi_ref, src_ref, srcn_ref, dst_ref, dstp_ref,
                    h_ref, wg_ref, wu_ref, wd_ref, out_ref,
                    xbuf, ybuf, wgb, wub, wdb, cast_exp, gsem, ssem):
    p = pl.program_id(0)
    n_items = pl.num_programs(0)
    w = iblk_ref[p]
    slot = w % 2
    lo, hi = ilo_ref[p], ihi_ref[p]
    lead = (hi > lo) & (lo == 0)
    block_rows = MOE_BLOCK * TOK_ROWS

    def tile_at(ref, row0):
        return ref.at[pl.ds(pl.multiple_of(row0, TOK_ROWS), TOK_ROWS)]

    def gather_row(rows_ref, s, i):
        pltpu.make_async_copy(tile_at(h_ref, rows_ref[0, 0, i]), tile_at(xbuf.at[s], i * TOK_ROWS),
                              gsem.at[s]).start()

    def scatter_row(rows_ref, s, i):
        pltpu.make_async_copy(tile_at(ybuf.at[s], i * TOK_ROWS), tile_at(out_ref, rows_ref[0, 0, i]),
                              ssem.at[s]).start()

    def rolled(fn):
        def group(g, c):
            for u in range(ROW_UNROLL):
                fn(g * ROW_UNROLL + u)
            return c
        lax.fori_loop(0, MOE_BLOCK // ROW_UNROLL, group, 0)

    def wait_gather(s):
        pltpu.make_async_copy(h_ref.at[pl.ds(0, block_rows)], xbuf.at[s], gsem.at[s]).wait()

    def wait_scatter(s):
        pltpu.make_async_copy(ybuf.at[s], out_ref.at[pl.ds(0, block_rows)], ssem.at[s]).wait()

    def ffn_rows():
        @pl.when(iexp_ref[p] != cast_exp[0])
        def _():
            wgb[...] = wg_ref[...].astype(BF16)
            wub[...] = wu_ref[...].astype(BF16)
            wdb[...] = wd_ref[...].astype(BF16)
            cast_exp[0] = iexp_ref[p]

        xb = _load_token_tiles(xbuf.at[slot], MOE_BLOCK).astype(BF16)
        hg = jnp.dot(xb, wgb[...], preferred_element_type=F32)
        hu = jnp.dot(xb, wub[...], preferred_element_type=F32)
        hb = (hg * _sigmoid(hg) * hu).astype(BF16)
        y = jnp.dot(hb, wdb[...], preferred_element_type=F32)
        row = lax.broadcasted_iota(jnp.int32, (MOE_BLOCK, 1), 0)
        mine = (row >= lo) & (row < hi)
        _store_token_tiles(ybuf.at[slot], jnp.where(mine, y, _load_token_tiles(ybuf.at[slot], MOE_BLOCK)))

    @pl.when(p == 0)
    def _():
        ybuf[...] = jnp.zeros_like(ybuf)
        cast_exp[0] = -1
        rolled(lambda i: gather_row(src_ref, 0, i))

    @pl.when(lead)
    def _():
        @pl.when(w >= 2)
        def _():
            wait_scatter(slot)
        wait_gather(slot)

    @pl.when(lead & (w == 0))
    def _():
        ffn_rows()
        for i in range(MOE_BLOCK):
            gather_row(srcn_ref, 1 - slot, i)

    @pl.when(lead & (w > 0))
    def _():
        ffn_rows()
        for i in range(MOE_BLOCK):
            gather_row(srcn_ref, 1 - slot, i)
            scatter_row(dstp_ref, 1 - slot, i)

    @pl.when((hi > lo) & (lo > 0))
    def _():
        ffn_rows()

    @pl.when(p == n_items - 1)
    def _():
        rolled(lambda i: scatter_row(dst_ref, slot, i))
        wait_gather(1 - slot)
        @pl.when(w >= 1)
        def _():
            wait_scatter(1 - slot)
        wait_scatter(slot)


def _experts(h, src_rows, dst_rows, item_blk, item_exp, item_lo, item_hi, wg, wu, wd):
    T = h.shape[0] // TOK_ROWS
    n_blocks = src_rows.shape[0]
    n_items = item_blk.shape[0]
    rows = MOE_BLOCK

    def ids_spec(shift):
        return pl.BlockSpec((1, 1, rows),
                            lambda p, ib, ie, il, ih: (jnp.clip(ib[p] + shift, 0, n_blocks - 1), 0, 0),
                            memory_space=pltpu.SMEM)

    def w_spec(shape):
        return pl.BlockSpec((None,) + shape, lambda p, ib, ie, il, ih: (ie[p], 0, 0))

    grid_spec = pltpu.PrefetchScalarGridSpec(
        num_scalar_prefetch=4,
        grid=(n_items,),
        in_specs=[ids_spec(0), ids_spec(1), ids_spec(0), ids_spec(-1), pl.BlockSpec(memory_space=pl.ANY),
                  w_spec((D_MODEL, EXPERT_DIM)), w_spec((D_MODEL, EXPERT_DIM)), w_spec((EXPERT_DIM, D_MODEL))],
        out_specs=pl.BlockSpec(memory_space=pl.ANY),
        scratch_shapes=[pltpu.VMEM((2, rows * TOK_ROWS, LANES), F32),
                        pltpu.VMEM((2, rows * TOK_ROWS, LANES), F32),
                        pltpu.VMEM((D_MODEL, EXPERT_DIM), BF16), pltpu.VMEM((D_MODEL, EXPERT_DIM), BF16),
                        pltpu.VMEM((EXPERT_DIM, D_MODEL), BF16),
                        pltpu.SMEM((1,), jnp.int32),
                        pltpu.SemaphoreType.DMA((2,)), pltpu.SemaphoreType.DMA((2,))],
    )
    return pl.pallas_call(
        _experts_kernel,
        grid_spec=grid_spec,
        out_shape=jax.ShapeDtypeStruct((TOP_K * T * TOK_ROWS, LANES), F32),
        compiler_params=pltpu.CompilerParams(dimension_semantics=("arbitrary",),
                                             vmem_limit_bytes=VMEM_LIMIT,
                                             disable_bounds_checks=True),
        name="experts",
    )(item_blk, item_exp, item_lo, item_hi, src_rows, src_rows, dst_rows, dst_rows, h, wg, wu, wd)


def _combine_kernel(res_ref, y_ref, w_ref, g_ref, b_ref, o_ref):
    tm = res_ref.shape[0]
    acc = res_ref[...]
    w = w_ref[...]
    for k in range(TOP_K):
        acc = acc + w[:, k:k + 1] * _load_token_tiles(y_ref.at[k], tm)
    o_ref[...] = _layer_norm(acc, g_ref[...], b_ref[...])


def _combine(res, y3, wts, ln_g, ln_b):
    T = res.shape[0]
    tm = 128
    return pl.pallas_call(
        _combine_kernel,
        grid=(T // tm,),
        in_specs=[pl.BlockSpec((tm, D_MODEL), lambda i: (i, 0)),
                  pl.BlockSpec((TOP_K, tm * TOK_ROWS, LANES), lambda i: (0, i, 0)),
                  pl.BlockSpec((tm, TOP_K), lambda i: (i, 0)),
                  _const_spec((1, D_MODEL)), _const_spec((1, D_MODEL))],
        out_specs=pl.BlockSpec((tm, D_MODEL), lambda i: (i, 0)),
        out_shape=jax.ShapeDtypeStruct((T, D_MODEL), F32),
        compiler_params=pltpu.CompilerParams(dimension_semantics=("arbitrary",)),
        name="combine",
    )(res, y3, wts, ln_g, ln_b)


def _dispatch_plan(eidx_t, n_tok):
    nk = n_tok * TOP_K
    n_blocks = nk // MOE_BLOCK
    e_flat = eidx_t.T.reshape(nk)
    _, order = lax.sort((e_flat, jnp.arange(nk, dtype=jnp.int32)), num_keys=1)
    experts = jnp.arange(N_EXPERTS, dtype=jnp.int32)
    counts = jnp.sum((e_flat[None, :] == experts[:, None]).astype(jnp.int32), axis=1)
    start = jnp.cumsum(counts) - counts
    pos = jnp.sort(jnp.concatenate([jnp.arange(n_blocks, dtype=jnp.int32) * MOE_BLOCK, start]))
    nxt = jnp.concatenate([pos[1:], jnp.full((1,), nk, jnp.int32)])
    item_blk = jnp.minimum(pos // MOE_BLOCK, n_blocks - 1)
    item_exp = jnp.sum((start[None, :] <= pos[:, None]).astype(jnp.int32), axis=1) - 1
    item_lo = pos - item_blk * MOE_BLOCK
    item_hi = jnp.minimum(nxt, (item_blk + 1) * MOE_BLOCK) - item_blk * MOE_BLOCK
    tok, slot_k = order // TOP_K, order % TOP_K
    shape3 = (n_blocks, 1, MOE_BLOCK)
    src_rows = (tok * TOK_ROWS).reshape(shape3)
    dst_rows = ((slot_k * n_tok + tok) * TOK_ROWS).reshape(shape3)
    return src_rows, dst_rows, item_blk, item_exp, item_lo, item_hi


def kernel(x, w_in, conv_w, gdn_a_log, gdn_dt_bias, gdn_norm_w, w_gdn_proj, w_moba_proj, w_out,
           ln1_g, ln1_b, router_w, router_bias, exp_w_gate, exp_w_up, exp_w_down,
           sh_w_gate, sh_w_up, sh_w_down, ln2_g, ln2_b):
    B, T, D = x.shape
    assert B == 1 and D == D_MODEL and T % MB_BLOCK == 0
    x2 = x.reshape(T, D)

    o_gate = CONV_CH
    o_b = o_gate + GDN_HEADS * GDN_DV
    o_a = o_b + GDN_HEADS
    o_mq = o_a + GDN_HEADS
    o_gg = o_mq + (MB_HEADS + 2 * MB_KV_HEADS) * MB_HD
    w_qkv = w_in[:, :CONV_CH].astype(BF16)
    w_pack = jnp.concatenate([w_in[:, o_b:o_a]] + [w_in[:, o_a:o_mq]] * 4
                             + [jnp.zeros((D, LANES - 5 * GDN_HEADS), F32)], axis=1).astype(BF16)
    w_mb = w_in[:, o_mq:o_gg].astype(BF16)
    w_gates = jnp.concatenate([w_in[:, o_gate:o_b], w_in[:, o_gg:]], axis=1).astype(BF16)

    def lane_row(v):
        return jnp.zeros((1, LANES), F32).at[0, PK_GC:PK_GC + 4 * GDN_HEADS].set(jnp.tile(v.astype(F32), 4))

    q, k, v, pack, gct = _qkv_proj(x2, w_qkv, w_pack, conv_w.astype(F32),
                                   lane_row(gdn_a_log), lane_row(gdn_dt_bias))
    gct3 = gct.reshape(GDN_HEADS, T // GDN_CHUNK, GDN_CHUNK).transpose(1, 0, 2)
    gdn_o = _gdn(q, k, v, pack, gct3, gdn_norm_w.astype(F32).reshape(1, GDN_DV))

    half = ROT_DIM // 2
    inv = ROPE_THETA ** (-jnp.arange(half, dtype=F32) / half)
    ang = jnp.arange(T).astype(F32)[:, None] * inv[None, :]
    ones = jnp.ones((T, MB_HD - ROT_DIM), F32)
    cos_t = jnp.concatenate([jnp.cos(ang), jnp.cos(ang), ones], axis=1)
    sin_t = jnp.concatenate([-jnp.sin(ang), jnp.sin(ang), 0.0 * ones], axis=1)
    mq, mk, mvt, kmean = _moba_proj(x2, w_mb, cos_t, sin_t)
    moba_a = _moba(mq, mk, mvt, kmean.transpose(1, 0, 2))

    h, res, eidx_t, wts_t = _post(
        x2, gdn_o, moba_a, w_gates, w_gdn_proj.astype(BF16), w_moba_proj.astype(BF16),
        w_out.astype(BF16), ln1_g.reshape(1, D), ln1_b.reshape(1, D),
        router_w.T, router_bias.reshape(N_EXPERTS, 1),
        sh_w_gate.astype(BF16), sh_w_up.astype(BF16), sh_w_down.astype(BF16))

    plan = _dispatch_plan(eidx_t, T)
    y = _experts(h, *plan, exp_w_gate, exp_w_up, exp_w_down)
    y3 = y.reshape(TOP_K, T * TOK_ROWS, LANES)
    out = _combine(res, y3, wts_t.T, ln2_g.reshape(1, D), ln2_b.reshape(1, D))
    return out.reshape(B, T, D)
```

```python
import functools
import math

import jax
import jax.numpy as jnp
from jax import lax
from jax.experimental import pallas as pl
from jax.experimental.pallas import tpu as pltpu

F32 = jnp.float32
BF16 = jnp.bfloat16

D_MODEL = 1024
DEPTH = 1
GDN_HEADS = 8
GDN_DK = 128
GDN_DV = 128
GDN_CONV = 4
GDN_CHUNK = 64
MB_HEADS = 8
MB_KV_HEADS = 2
MB_REP = MB_HEADS // MB_KV_HEADS
MB_HD = 128
MB_BLOCK = 256
MB_TOPK = 3
ROT_DIM = MB_HD // 4
ROPE_THETA = 500000.0
N_EXPERTS = 256
TOP_K = 8
N_GROUPS = 8
GROUP_SIZE = N_EXPERTS // N_GROUPS
TOPK_GROUPS = 4
EXPERT_DIM = 256
SHARED_DIM = 256
ROUTED_SCALE = 2.5
MOE_BLOCK = 128
DEEPNORM_ALPHA = (2.0 * DEPTH) ** 0.25
LN_EPS = 1e-5
MASK_NEG = -1e30

QK_COLS = GDN_HEADS * GDN_DK
CONV_CH = 2 * QK_COLS + GDN_HEADS * GDN_DV
LANES = 128
SUBLANES = 8
VMEM_LIMIT = 56 * 1024 * 1024

TOK_TILE = 256

PK_BETA, PK_GC, PK_EG, PK_EGD, PK_EGL = 0, 8, 16, 24, 32


def _sigmoid(x):
    return 1.0 / (1.0 + jnp.exp(-x))


def _softplus(x):
    return jnp.maximum(x, 0.0) + jnp.log(1.0 + jnp.exp(-jnp.abs(x)))


def _bdot(a, b):
    return jnp.dot(a.astype(BF16), b.astype(BF16), preferred_element_type=F32)


def _bdot_nt(a, b):
    return lax.dot_general(a.astype(BF16), b.astype(BF16), (((1,), (1,)), ((), ())),
                           preferred_element_type=F32)


def _bdot_tn(a, b):
    return lax.dot_general(a.astype(BF16), b.astype(BF16), (((0,), (0,)), ((), ())),
                           preferred_element_type=F32)


def _layer_norm(x, g, b):
    mu = jnp.mean(x, axis=-1, keepdims=True)
    xc = x - mu
    var = jnp.mean(xc * xc, axis=-1, keepdims=True)
    return xc * lax.rsqrt(var + LN_EPS) * g + b


TOK_ROWS = D_MODEL // LANES


def _store_token_tiles(ref, x):
    n = x.shape[0]
    for j in range(TOK_ROWS):
        ref[pl.ds(j, n, stride=TOK_ROWS), :] = x[:, j * LANES:(j + 1) * LANES]


def _load_token_tiles(ref, n):
    return jnp.concatenate([ref[pl.ds(j, n, stride=TOK_ROWS), :] for j in range(TOK_ROWS)], axis=1)


def _const_spec(shape):
    nd = len(shape)
    return pl.BlockSpec(shape, lambda *_: (0,) * nd, pipeline_mode=pl.Buffered(1))


def _qkv_kernel(x_ref, w_ref, wp_ref, cw_ref, alog_ref, dtb_ref,
                q_ref, k_ref, v_ref, pack_ref, gct_ref, z_ref):
    tm = x_ref.shape[0]
    halo = SUBLANES

    @pl.when(pl.program_id(0) == 0)
    def _():
        z_ref[0:halo, :] = jnp.zeros((halo, CONV_CH), F32)

    xb = x_ref[...].astype(BF16)
    z_ref[halo:halo + tm, :] = jnp.dot(xb, w_ref[...], preferred_element_type=F32)

    for c in range(CONV_CH // LANES):
        cols = slice(c * LANES, (c + 1) * LANES)
        acc = z_ref[halo:halo + tm, cols] * cw_ref[GDN_CONV - 1:GDN_CONV, cols]
        for s in range(1, GDN_CONV):
            acc = acc + z_ref[halo - s:halo - s + tm, cols] * cw_ref[GDN_CONV - 1 - s:GDN_CONV - s, cols]
        y = acc * _sigmoid(acc)
        h = c % GDN_HEADS
        if c < GDN_HEADS:
            y = y * lax.rsqrt(jnp.sum(y * y, axis=-1, keepdims=True) + 1e-6) * (GDN_DK ** -0.5)
            q_ref[h] = y
        elif c < 2 * GDN_HEADS:
            y = y * lax.rsqrt(jnp.sum(y * y, axis=-1, keepdims=True) + 1e-6)
            k_ref[h] = y
        else:
            v_ref[h] = y

    z_ref[0:halo, :] = z_ref[tm:tm + halo, :]

    zp = jnp.dot(xb, wp_ref[...], preferred_element_type=F32)
    beta = _sigmoid(zp)
    g = -jnp.exp(alog_ref[...]) * _softplus(zp + dtb_ref[...])
    row = lax.broadcasted_iota(jnp.int32, (tm, LANES), 0) % GDN_CHUNK
    gc = g
    step = 1
    while step < GDN_CHUNK:
        gc = gc + jnp.where(row >= step, pltpu.roll(gc, step, axis=0), 0.0)
        step *= 2
    gl = gc.reshape(tm // GDN_CHUNK, GDN_CHUNK, LANES)[:, GDN_CHUNK - 1:GDN_CHUNK, :]
    gl = jnp.broadcast_to(gl, (tm // GDN_CHUNK, GDN_CHUNK, LANES)).reshape(tm, LANES)
    lane = lax.broadcasted_iota(jnp.int32, (tm, LANES), 1)
    pack = jnp.where(lane < PK_GC, beta,
           jnp.where(lane < PK_EG, gc,
           jnp.where(lane < PK_EGD, jnp.exp(gc),
           jnp.where(lane < PK_EGL, jnp.exp(gl - gc), jnp.exp(gl)))))
    pack_ref[...] = pack
    gct_ref[...] = pack.T[PK_GC:PK_GC + GDN_HEADS, :]


def _qkv_proj(x2, w_qkv, w_pack, conv_w, alog_row, dtb_row):
    T = x2.shape[0]
    tm = TOK_TILE
    hd_spec = pl.BlockSpec((GDN_HEADS, tm, LANES), lambda i: (0, i, 0))
    hd_shape = jax.ShapeDtypeStruct((GDN_HEADS, T, LANES), F32)
    return pl.pallas_call(
        _qkv_kernel,
        grid=(T // tm,),
        in_specs=[
            pl.BlockSpec((tm, D_MODEL), lambda i: (i, 0)),
            _const_spec((D_MODEL, CONV_CH)),
            _const_spec((D_MODEL, LANES)),
            _const_spec((GDN_CONV, CONV_CH)),
            _const_spec((1, LANES)),
            _const_spec((1, LANES)),
        ],
        out_specs=[hd_spec, hd_spec, hd_spec,
                   pl.BlockSpec((tm, LANES), lambda i: (i, 0)),
                   pl.BlockSpec((GDN_HEADS, tm), lambda i: (0, i))],
        out_shape=[hd_shape, hd_shape, hd_shape,
                   jax.ShapeDtypeStruct((T, LANES), F32),
                   jax.ShapeDtypeStruct((GDN_HEADS, T), F32)],
        scratch_shapes=[pltpu.VMEM((tm + SUBLANES, CONV_CH), F32)],
        compiler_params=pltpu.CompilerParams(dimension_semantics=("arbitrary",),
                                             vmem_limit_bytes=VMEM_LIMIT),
        name="qkv_proj",
    )(x2, w_qkv, w_pack, conv_w, alog_row, dtb_row)


MB_LROWS = 16
MB_VROWS = MB_HD + MB_LROWS


def _rope(xh, cos_t, sin_t, lane):
    half = ROT_DIM // 2
    swapped = jnp.where(lane < half, pltpu.roll(xh, LANES - half, axis=1), pltpu.roll(xh, half, axis=1))
    return xh * cos_t + swapped * sin_t


def _moba_proj_kernel(x_ref, w_ref, cos_ref, sin_ref, q_ref, k_ref, vt_ref, km_ref):
    tm = x_ref.shape[0]
    z = jnp.dot(x_ref[...].astype(BF16), w_ref[...], preferred_element_type=F32)
    cos_t = cos_ref[...]
    sin_t = sin_ref[...]
    lane = lax.broadcasted_iota(jnp.int32, (tm, LANES), 1)
    for h in range(MB_HEADS):
        q_ref[h] = _rope(z[:, h * MB_HD:(h + 1) * MB_HD], cos_t, sin_t, lane)
    koff = MB_HEADS * MB_HD
    voff = koff + MB_KV_HEADS * MB_HD
    blk_onehot = jnp.where(lane == pl.program_id(0), 1.0, 0.0).astype(BF16)
    for g in range(MB_KV_HEADS):
        kr = _rope(z[:, koff + g * MB_HD:koff + (g + 1) * MB_HD], cos_t, sin_t, lane)
        k_ref[0, g] = jnp.concatenate([kr.astype(BF16), blk_onehot], axis=1)
        km_ref[0, g:g + 1, :] = jnp.mean(kr, axis=0, keepdims=True)
        vt_ref[0, g] = jnp.concatenate([z[:, voff + g * MB_HD:voff + (g + 1) * MB_HD].T,
                                        jnp.ones((MB_LROWS, tm), F32)], axis=0).astype(BF16)


def _moba_proj(x2, w_mb, cos_t, sin_t):
    T = x2.shape[0]
    tm = MB_BLOCK
    nb = T // tm
    return pl.pallas_call(
        _moba_proj_kernel,
        grid=(nb,),
        in_specs=[
            pl.BlockSpec((tm, D_MODEL), lambda i: (i, 0)),
            _const_spec(w_mb.shape),
            pl.BlockSpec((tm, LANES), lambda i: (i, 0)),
            pl.BlockSpec((tm, LANES), lambda i: (i, 0)),
        ],
        out_specs=[
            pl.BlockSpec((MB_HEADS, tm, MB_HD), lambda i: (0, i, 0)),
            pl.BlockSpec((1, MB_KV_HEADS, tm, 2 * MB_HD), lambda i: (i, 0, 0, 0)),
            pl.BlockSpec((1, MB_KV_HEADS, MB_VROWS, tm), lambda i: (i, 0, 0, 0)),
            pl.BlockSpec((1, MB_KV_HEADS, MB_HD), lambda i: (i, 0, 0)),
        ],
        out_shape=[
            jax.ShapeDtypeStruct((MB_HEADS, T, MB_HD), F32),
            jax.ShapeDtypeStruct((nb, MB_KV_HEADS, tm, 2 * MB_HD), BF16),
            jax.ShapeDtypeStruct((nb, MB_KV_HEADS, MB_VROWS, tm), BF16),
            jax.ShapeDtypeStruct((nb, MB_KV_HEADS, MB_HD), F32),
        ],
        compiler_params=pltpu.CompilerParams(dimension_semantics=("arbitrary",),
                                             vmem_limit_bytes=VMEM_LIMIT),
        name="moba_proj",
    )(x2, w_mb, cos_t, sin_t)


def _unit_lower_inverse(a_list):
    c = a_list[0].shape[0]
    rr = lax.broadcasted_iota(jnp.int32, (c, c), 0)
    cc = lax.broadcasted_iota(jnp.int32, (c, c), 1)
    eye = jnp.where(rr == cc, 1.0, 0.0).astype(F32)
    inv = [eye - a for a in a_list]
    p = [_bdot(a, a) for a in a_list]
    n = 2
    while True:
        inv = [x + _bdot(x, y) for x, y in zip(inv, p)]
        n *= 2
        if n >= c:
            break
        p = [_bdot(y, y) for y in p]
    return inv


GDN_STEP_CHUNKS = 4


def _gdn_kernel(q_ref, k_ref, v_ref, pack_ref, gct_ref, nw_ref, o_ref, s_ref):
    C = GDN_CHUNK
    H = range(GDN_HEADS)
    P = [(c, h) for c in range(GDN_STEP_CHUNKS) for h in H]

    @pl.when(pl.program_id(0) == 0)
    def _():
        s_ref[...] = jnp.zeros_like(s_ref)

    rr = lax.broadcasted_iota(jnp.int32, (C, C), 0)
    cc = lax.broadcasted_iota(jnp.int32, (C, C), 1)
    tril = rr >= cc
    strict = rr > cc
    pack = pack_ref[...]
    nw = nw_ref[...]

    def rows(c):
        return slice(c * C, (c + 1) * C)

    def col(base, c, h):
        return pack[rows(c), base + h:base + h + 1]

    q = {(c, h): q_ref[h, rows(c), :] for c, h in P}
    k = {(c, h): k_ref[h, rows(c), :] for c, h in P}
    kb = {(c, h): k[c, h] * col(PK_BETA, c, h) for c, h in P}
    decay = {(c, h): jnp.where(tril, jnp.exp(jnp.where(tril, col(PK_GC, c, h) - gct_ref[c, h:h + 1, :], 0.0)), 0.0)
             for c, h in P}
    sc = {p: _bdot_nt(jnp.concatenate([kb[p], q[p]], axis=0), k[p]) for p in P}
    a = {p: jnp.where(strict, sc[p][:C] * decay[p], 0.0) for p in P}
    aqk = {p: sc[p][C:] * decay[p] for p in P}
    tinv = dict(zip(P, _unit_lower_inverse([a[p] for p in P])))
    rhs = {(c, h): jnp.concatenate([v_ref[h, rows(c), :] * col(PK_BETA, c, h), kb[c, h] * col(PK_EG, c, h)],
                                   axis=1) for c, h in P}
    sol = {p: _bdot(tinv[p], rhs[p]) for p in P}

    s = [s_ref[h] for h in H]
    for c in range(GDN_STEP_CHUNKS):
        wq = [_bdot(jnp.concatenate([sol[c, h][:, GDN_DV:], q[c, h] * col(PK_EG, c, h)], axis=0), s[h])
              for h in H]
        v_new = [sol[c, h][:, :GDN_DV] - wq[h][:C] for h in H]
        o = [wq[h][C:] + _bdot(aqk[c, h], v_new[h]) for h in H]
        ds = [_bdot_tn(k[c, h] * col(PK_EGD, c, h), v_new[h]) for h in H]
        for h in H:
            s[h] = s[h] * pack[c * C:c * C + 1, PK_EGL + h:PK_EGL + h + 1] + ds[h]
            on = o[h] * lax.rsqrt(jnp.mean(o[h] * o[h], axis=-1, keepdims=True) + 1e-6) * nw
            o_ref[rows(c), h * GDN_DV:(h + 1) * GDN_DV] = on
    for h in H:
        s_ref[h] = s[h]


def _gdn(q, k, v, pack, gct3, norm_w_row):
    T = q.shape[1]
    C = GDN_CHUNK * GDN_STEP_CHUNKS
    hd_spec = pl.BlockSpec((GDN_HEADS, C, LANES), lambda i: (0, i, 0))
    return pl.pallas_call(
        _gdn_kernel,
        grid=(T // C,),
        in_specs=[hd_spec, hd_spec, hd_spec,
                  pl.BlockSpec((C, LANES), lambda i: (i, 0)),
                  pl.BlockSpec((GDN_STEP_CHUNKS, GDN_HEADS, GDN_CHUNK), lambda i: (i, 0, 0)),
                  _const_spec((1, GDN_DV))],
        out_specs=pl.BlockSpec((C, GDN_HEADS * GDN_DV), lambda i: (i, 0)),
        out_shape=jax.ShapeDtypeStruct((T, GDN_HEADS * GDN_DV), F32),
        scratch_shapes=[pltpu.VMEM((GDN_HEADS, GDN_DK, GDN_DV), F32)],
        compiler_params=pltpu.CompilerParams(dimension_semantics=("arbitrary",)),
        name="gdn",
    )(q, k, v, pack, gct3, norm_w_row)


MB_QSPLIT = 2
MB_TRIP = 4
MB_MAX_BLOCKS = MB_HD


def _moba_kernel(q_ref, k_ref, vt_ref, km_ref, o_ref, qx_ref, sa_ref, sb_ref, acc_ref, m_ref):
    qi = pl.program_id(1)
    nb = k_ref.shape[0]
    bs = MB_BLOCK
    nq = MB_REP * bs
    nbp = MB_MAX_BLOCKS
    c = (MB_HD ** -0.5) * math.log2(math.e)

    qf = q_ref[...].reshape(nq, MB_HD)

    gate = lax.dot_general(km_ref[...], qf, (((1,), (1,)), ((), ())),
                           precision=lax.Precision.HIGHEST, preferred_element_type=F32)
    blk = lax.broadcasted_iota(jnp.int32, (nbp, nq), 0).astype(F32)
    qif = qi.astype(F32)
    gate = jnp.where(blk < qif, gate, -jnp.inf)
    sel = jnp.where(blk == qif, 1.0, 0.0)
    for _ in range(MB_TOPK):
        top = jnp.max(gate, axis=0, keepdims=True)
        first = jnp.min(jnp.where(gate == top, blk, float(nbp)), axis=0, keepdims=True)
        hit = (blk == first) & (top > -jnp.inf)
        sel = jnp.where(hit, 1.0, sel)
        gate = jnp.where(hit, -jnp.inf, gate)
    bias_t = jnp.where(sel > 0.0, 0.0, MASK_NEG)
    qc = qf * c
    qx_ref[0] = jnp.concatenate([qc, bias_t.T], axis=1).astype(BF16)
    qx_ref[1] = jnp.concatenate([qc, jnp.full((nq, nbp), MASK_NEG, F32)], axis=1).astype(BF16)

    w = nq // MB_QSPLIT
    groups = [slice(h * w, (h + 1) * w) for h in range(MB_QSPLIT)]

    def scores_into(dst_ref, t):
        kx = k_ref[jnp.minimum(t, nb - 1)]
        qset = jnp.where(t < qi, 0, 1)
        for lanes in groups:
            dst_ref[:, lanes] = lax.dot_general(kx, qx_ref[qset, lanes, :], (((1,), (1,)), ((), ())),
                                                preferred_element_type=F32)

    st = lax.dot_general(k_ref[qi], qx_ref[0], (((1,), (1,)), ((), ())), preferred_element_type=F32)
    kpos = lax.broadcasted_iota(jnp.int32, (bs, nq), 0)
    qpos = lax.broadcasted_iota(jnp.int32, (bs, nq), 1) % bs
    st = jnp.where(kpos <= qpos, st, MASK_NEG)
    m0 = jnp.max(st, axis=0, keepdims=True)
    p = jnp.exp2(st - m0)
    m_ref[...] = m0
    acc_ref[...] = jnp.dot(vt_ref[qi], p.astype(BF16), preferred_element_type=F32)

    def absorb(src_ref, t):
        vt = vt_ref[jnp.minimum(t, nb - 1)]
        for lanes in groups:
            st = src_ref[:, lanes]
            m_old = m_ref[:, lanes]
            m_new = jnp.maximum(m_old, jnp.max(st, axis=0, keepdims=True))
            alpha = jnp.exp2(m_old - m_new)
            p = jnp.exp2(st - m_new)
            m_ref[:, lanes] = m_new
            acc_ref[:, lanes] = alpha * acc_ref[:, lanes] + jnp.dot(vt, p.astype(BF16),
                                                                    preferred_element_type=F32)

    scores_into(sa_ref, 0)

    def body(i, carry):
        t = MB_TRIP * i
        for u in range(0, MB_TRIP, 2):
            scores_into(sb_ref, t + u + 1)
            absorb(sa_ref, t + u)
            scores_into(sa_ref, t + u + 2)
            absorb(sb_ref, t + u + 1)
        return carry

    lax.fori_loop(0, (qi + MB_TRIP - 1) // MB_TRIP, body, 0)

    out_t = acc_ref[0:MB_HD, :] / acc_ref[MB_HD:MB_HD + 1, :]
    for r in range(MB_REP):
        o_ref[:, r * MB_HD:(r + 1) * MB_HD] = out_t[:, r * bs:(r + 1) * bs].T


def _moba(mq, mk, mvt, kmean):
    T = mq.shape[1]
    bs = MB_BLOCK
    nb = T // bs
    nq = MB_REP * bs
    assert nb <= MB_MAX_BLOCKS
    kmean = jnp.pad(kmean, ((0, 0), (0, MB_MAX_BLOCKS - nb), (0, 0)))
    return pl.pallas_call(
        _moba_kernel,
        grid=(MB_KV_HEADS, nb),
        in_specs=[
            pl.BlockSpec((MB_REP, bs, MB_HD), lambda g, i: (g, i, 0)),
            pl.BlockSpec((nb, None, bs, 2 * MB_HD), lambda g, i: (0, g, 0, 0)),
            pl.BlockSpec((nb, None, MB_VROWS, bs), lambda g, i: (0, g, 0, 0)),
            pl.BlockSpec((None, MB_MAX_BLOCKS, MB_HD), lambda g, i: (g, 0, 0)),
        ],
        out_specs=pl.BlockSpec((bs, MB_REP * MB_HD), lambda g, i: (i, g)),
        out_shape=jax.ShapeDtypeStruct((T, MB_HEADS * MB_HD), F32),
        scratch_shapes=[pltpu.VMEM((2, nq, 2 * MB_HD), BF16),
                        pltpu.VMEM((bs, nq), F32), pltpu.VMEM((bs, nq), F32),
                        pltpu.VMEM((MB_VROWS, nq), F32),
                        pltpu.VMEM((1, nq), F32)],
        compiler_params=pltpu.CompilerParams(dimension_semantics=("arbitrary", "arbitrary"),
                                             vmem_limit_bytes=VMEM_LIMIT),
        name="moba",
    )(mq, mk, mvt, kmean)


def _post_kernel(x_ref, go_ref, ma_ref, wg_ref, wgp_ref, wmp_ref, wo_ref, g1_ref, b1_ref,
                 rwt_ref, rb_ref, sg_ref, su_ref, sd_ref,
                 h_ref, res_ref, eidx_ref, wts_ref):
    tm = x_ref.shape[0]
    x = x_ref[...]
    zg = jnp.dot(x.astype(BF16), wg_ref[...], preferred_element_type=F32)
    gate = zg[:, :D_MODEL]
    o = go_ref[...] * (gate * _sigmoid(gate))
    y_gdn = jnp.dot(o.astype(BF16), wgp_ref[...], preferred_element_type=F32)
    y_mb = jnp.dot(ma_ref[...].astype(BF16), wmp_ref[...], preferred_element_type=F32)
    m = _sigmoid(zg[:, D_MODEL:2 * D_MODEL]) * y_gdn + _sigmoid(zg[:, 2 * D_MODEL:]) * y_mb
    mix = jnp.dot(m.astype(BF16), wo_ref[...], preferred_element_type=F32)
    h = _layer_norm(DEEPNORM_ALPHA * x + mix, g1_ref[...], b1_ref[...])
    _store_token_tiles(h_ref, h)
    hb = h.astype(BF16)

    hs = jnp.dot(hb, sg_ref[...], preferred_element_type=F32)
    hs = hs * _sigmoid(hs) * jnp.dot(hb, su_ref[...], preferred_element_type=F32)
    res_ref[...] = DEEPNORM_ALPHA * h + jnp.dot(hs.astype(BF16), sd_ref[...], preferred_element_type=F32)

    logits = lax.dot_general(rwt_ref[...], h, (((1,), (1,)), ((), ())),
                             precision=lax.Precision.HIGHEST, preferred_element_type=F32)
    scores = _sigmoid(logits)
    choice = scores + rb_ref[...]
    neg = -jnp.inf
    gi = lax.broadcasted_iota(jnp.int32, (GROUP_SIZE, tm), 0).astype(F32)
    gscore = []
    for g in range(N_GROUPS):
        cg = choice[g * GROUP_SIZE:(g + 1) * GROUP_SIZE, :]
        m1 = jnp.max(cg, axis=0, keepdims=True)
        i1 = jnp.min(jnp.where(cg == m1, gi, float(GROUP_SIZE)), axis=0, keepdims=True)
        m2 = jnp.max(jnp.where(gi == i1, neg, cg), axis=0, keepdims=True)
        gscore.append(m1 + m2)
    gs = jnp.concatenate(gscore, axis=0)
    gidx = lax.broadcasted_iota(jnp.int32, (N_GROUPS, tm), 0).astype(F32)
    gsel = jnp.zeros((N_GROUPS, tm), F32)
    for _ in range(TOPK_GROUPS):
        top = jnp.max(gs, axis=0, keepdims=True)
        first = jnp.min(jnp.where(gs == top, gidx, float(N_GROUPS)), axis=0, keepdims=True)
        hit = gidx == first
        gsel = jnp.where(hit, 1.0, gsel)
        gs = jnp.where(hit, neg, gs)
    masked = jnp.concatenate(
        [jnp.where(gsel[g:g + 1, :] > 0.0, choice[g * GROUP_SIZE:(g + 1) * GROUP_SIZE, :], neg)
         for g in range(N_GROUPS)], axis=0)
    ei = lax.broadcasted_iota(jnp.int32, (N_EXPERTS, tm), 0).astype(F32)
    idx_rows, w_rows = [], []
    for _ in range(TOP_K):
        top = jnp.max(masked, axis=0, keepdims=True)
        first = jnp.min(jnp.where(masked == top, ei, float(N_EXPERTS)), axis=0, keepdims=True)
        hit = ei == first
        idx_rows.append(first)
        w_rows.append(jnp.sum(jnp.where(hit, scores, 0.0), axis=0, keepdims=True))
        masked = jnp.where(hit, neg, masked)
    w = jnp.concatenate(w_rows, axis=0)
    w = w / (jnp.sum(w, axis=0, keepdims=True) + 1e-20) * ROUTED_SCALE
    eidx_ref[...] = jnp.concatenate(idx_rows, axis=0).astype(jnp.int32)
    wts_ref[...] = w


def _post(x2, gdn_o, moba_a, w_gates, w_gdn_proj, w_moba_proj, w_out, ln_g, ln_b,
          router_wt, router_b_col, sh_g, sh_u, sh_d):
    T = x2.shape[0]
    tm = TOK_TILE
    row_spec = pl.BlockSpec((tm, D_MODEL), lambda i: (i, 0))
    consts = [w_gates, w_gdn_proj, w_moba_proj, w_out, ln_g, ln_b, router_wt, router_b_col,
              sh_g, sh_u, sh_d]
    return pl.pallas_call(
        _post_kernel,
        grid=(T // tm,),
        in_specs=[row_spec, row_spec, row_spec] + [_const_spec(c.shape) for c in consts],
        out_specs=[pl.BlockSpec((tm * TOK_ROWS, LANES), lambda i: (i, 0)), row_spec,
                   pl.BlockSpec((TOP_K, tm), lambda i: (0, i)),
                   pl.BlockSpec((TOP_K, tm), lambda i: (0, i))],
        out_shape=[jax.ShapeDtypeStruct((T * TOK_ROWS, LANES), F32),
                   jax.ShapeDtypeStruct((T, D_MODEL), F32),
                   jax.ShapeDtypeStruct((TOP_K, T), jnp.int32),
                   jax.ShapeDtypeStruct((TOP_K, T), F32)],
        compiler_params=pltpu.CompilerParams(dimension_semantics=("arbitrary",),
                                             vmem_limit_bytes=VMEM_LIMIT),
        name="post",
    )(x2, gdn_o, moba_a, *consts)


ROW_UNROLL = 8


def _experts_kernel(iblk_ref, iexp_ref, ilo_ref, ihi_ref, src_ref, srcn_ref, dst_ref,
                    h_ref, wg_ref, wu_ref, wd_ref, out_ref,
                    xbuf, ybuf, wgb, wub, wdb, cast_exp, gsem, ssem, *, n_blocks):
    p = pl.program_id(0)
    n_items = pl.num_programs(0)
    w = iblk_ref[p]
    slot = w % 2
    first = (p == 0) | (w != iblk_ref[jnp.maximum(p - 1, 0)])
    last = (p == n_items - 1) | (w != iblk_ref[jnp.minimum(p + 1, n_items - 1)])
    lo, hi = ilo_ref[p], ihi_ref[p]
    block_rows = MOE_BLOCK * TOK_ROWS

    def tile_at(ref, row0):
        return ref.at[pl.ds(pl.multiple_of(row0, TOK_ROWS), TOK_ROWS)]

    def for_block_rows(fn):
        def group(g, c):
            for u in range(ROW_UNROLL):
                fn(g * ROW_UNROLL + u)
            return c
        lax.fori_loop(0, MOE_BLOCK // ROW_UNROLL, group, 0)

    def start_gather(rows_ref, s):
        for_block_rows(lambda i: pltpu.make_async_copy(
            tile_at(h_ref, rows_ref[0, 0, i]), tile_at(xbuf.at[s], i * TOK_ROWS), gsem.at[s]).start())

    def start_scatter(s):
        for_block_rows(lambda i: pltpu.make_async_copy(
            tile_at(ybuf.at[s], i * TOK_ROWS), tile_at(out_ref, dst_ref[0, 0, i]), ssem.at[s]).start())

    def wait_gather(s):
        pltpu.make_async_copy(h_ref.at[pl.ds(0, block_rows)], xbuf.at[s], gsem.at[s]).wait()

    def wait_scatter(s):
        pltpu.make_async_copy(ybuf.at[s], out_ref.at[pl.ds(0, block_rows)], ssem.at[s]).wait()

    @pl.when(p == 0)
    def _():
        ybuf[...] = jnp.zeros_like(ybuf)
        cast_exp[0] = -1
        start_gather(src_ref, 0)

    @pl.when(first)
    def _():
        @pl.when(w + 1 < n_blocks)
        def _():
            start_gather(srcn_ref, 1 - slot)

        @pl.when(w >= 2)
        def _():
            wait_scatter(slot)
        wait_gather(slot)

    @pl.when(hi > lo)
    def _():
        @pl.when(iexp_ref[p] != cast_exp[0])
        def _():
            wgb[...] = wg_ref[...].astype(BF16)
            wub[...] = wu_ref[...].astype(BF16)
            wdb[...] = wd_ref[...].astype(BF16)
            cast_exp[0] = iexp_ref[p]

        xb = _load_token_tiles(xbuf.at[slot], MOE_BLOCK).astype(BF16)
        hg = jnp.dot(xb, wgb[...], preferred_element_type=F32)
        hu = jnp.dot(xb, wub[...], preferred_element_type=F32)
        hb = (hg * _sigmoid(hg) * hu).astype(BF16)
        y = jnp.dot(hb, wdb[...], preferred_element_type=F32)
        row = lax.broadcasted_iota(jnp.int32, (MOE_BLOCK, 1), 0)
        mine = (row >= lo) & (row < hi)
        _store_token_tiles(ybuf.at[slot], jnp.where(mine, y, _load_token_tiles(ybuf.at[slot], MOE_BLOCK)))

    @pl.when(last)
    def _():
        start_scatter(slot)

    @pl.when(p == n_items - 1)
    def _():
        @pl.when(w >= 1)
        def _():
            wait_scatter(1 - slot)
        wait_scatter(slot)


def _experts(h, src_rows, dst_rows, item_blk, item_exp, item_lo, item_hi, wg, wu, wd):
    T = h.shape[0] // TOK_ROWS
    n_blocks = src_rows.shape[0]
    n_items = item_blk.shape[0]
    rows = MOE_BLOCK
    cur_spec = pl.BlockSpec((1, 1, rows), lambda p, ib, ie, il, ih: (ib[p], 0, 0), memory_space=pltpu.SMEM)
    nxt_spec = pl.BlockSpec((1, 1, rows), lambda p, ib, ie, il, ih: (jnp.minimum(ib[p] + 1, n_blocks - 1), 0, 0),
                            memory_space=pltpu.SMEM)

    def w_spec(shape):
        return pl.BlockSpec((None,) + shape, lambda p, ib, ie, il, ih: (ie[p], 0, 0))

    grid_spec = pltpu.PrefetchScalarGridSpec(
        num_scalar_prefetch=4,
        grid=(n_items,),
        in_specs=[cur_spec, nxt_spec, cur_spec, pl.BlockSpec(memory_space=pl.ANY),
                  w_spec((D_MODEL, EXPERT_DIM)), w_spec((D_MODEL, EXPERT_DIM)), w_spec((EXPERT_DIM, D_MODEL))],
        out_specs=pl.BlockSpec(memory_space=pl.ANY),
        scratch_shapes=[pltpu.VMEM((2, rows * TOK_ROWS, LANES), F32),
                        pltpu.VMEM((2, rows * TOK_ROWS, LANES), F32),
                        pltpu.VMEM((D_MODEL, EXPERT_DIM), BF16), pltpu.VMEM((D_MODEL, EXPERT_DIM), BF16),
                        pltpu.VMEM((EXPERT_DIM, D_MODEL), BF16),
                        pltpu.SMEM((1,), jnp.int32),
                        pltpu.SemaphoreType.DMA((2,)), pltpu.SemaphoreType.DMA((2,))],
    )
    return pl.pallas_call(
        functools.partial(_experts_kernel, n_blocks=n_blocks),
        grid_spec=grid_spec,
        out_shape=jax.ShapeDtypeStruct((TOP_K * T * TOK_ROWS, LANES), F32),
        compiler_params=pltpu.CompilerParams(dimension_semantics=("arbitrary",),
                                             vmem_limit_bytes=VMEM_LIMIT,
                                             disable_bounds_checks=True),
        name="experts",
    )(item_blk, item_exp, item_lo, item_hi, src_rows, src_rows, dst_rows, h, wg, wu, wd)


def _combine_kernel(res_ref, y_ref, w_ref, g_ref, b_ref, o_ref):
    tm = res_ref.shape[0]
    acc = res_ref[...]
    w = w_ref[...]
    for k in range(TOP_K):
        acc = acc + w[:, k:k + 1] * _load_token_tiles(y_ref.at[k], tm)
    o_ref[...] = _layer_norm(acc, g_ref[...], b_ref[...])


def _combine(res, y3, wts, ln_g, ln_b):
    T = res.shape[0]
    tm = 128
    return pl.pallas_call(
        _combine_kernel,
        grid=(T // tm,),
        in_specs=[pl.BlockSpec((tm, D_MODEL), lambda i: (i, 0)),
                  pl.BlockSpec((TOP_K, tm * TOK_ROWS, LANES), lambda i: (0, i, 0)),
                  pl.BlockSpec((tm, TOP_K), lambda i: (i, 0)),
                  _const_spec((1, D_MODEL)), _const_spec((1, D_MODEL))],
        out_specs=pl.BlockSpec((tm, D_MODEL), lambda i: (i, 0)),
        out_shape=jax.ShapeDtypeStruct((T, D_MODEL), F32),
        compiler_params=pltpu.CompilerParams(dimension_semantics=("arbitrary",)),
        name="combine",
    )(res, y3, wts, ln_g, ln_b)


def _dispatch_plan(eidx_t, n_tok):
    nk = n_tok * TOP_K
    n_blocks = nk // MOE_BLOCK
    e_flat = eidx_t.T.reshape(nk)
    _, order = lax.sort((e_flat, jnp.arange(nk, dtype=jnp.int32)), num_keys=1)
    experts = jnp.arange(N_EXPERTS, dtype=jnp.int32)
    counts = jnp.sum((e_flat[None, :] == experts[:, None]).astype(jnp.int32), axis=1)
    start = jnp.cumsum(counts) - counts
    pos = jnp.sort(jnp.concatenate([jnp.arange(n_blocks, dtype=jnp.int32) * MOE_BLOCK, start]))
    nxt = jnp.concatenate([pos[1:], jnp.full((1,), nk, jnp.int32)])
    item_blk = jnp.minimum(pos // MOE_BLOCK, n_blocks - 1)
    item_exp = jnp.sum((start[None, :] <= pos[:, None]).astype(jnp.int32), axis=1) - 1
    item_lo = pos - item_blk * MOE_BLOCK
    item_hi = jnp.minimum(nxt, (item_blk + 1) * MOE_BLOCK) - item_blk * MOE_BLOCK
    tok, slot_k = order // TOP_K, order % TOP_K
    shape3 = (n_blocks, 1, MOE_BLOCK)
    src_rows = (tok * TOK_ROWS).reshape(shape3)
    dst_rows = ((slot_k * n_tok + tok) * TOK_ROWS).reshape(shape3)
    return src_rows, dst_rows, item_blk, item_exp, item_lo, item_hi


def kernel(x, w_in, conv_w, gdn_a_log, gdn_dt_bias, gdn_norm_w, w_gdn_proj, w_moba_proj, w_out,
           ln1_g, ln1_b, router_w, router_bias, exp_w_gate, exp_w_up, exp_w_down,
           sh_w_gate, sh_w_up, sh_w_down, ln2_g, ln2_b):
    B, T, D = x.shape
    assert B == 1 and D == D_MODEL and T % MB_BLOCK == 0
    x2 = x.reshape(T, D)

    o_gate = CONV_CH
    o_b = o_gate + GDN_HEADS * GDN_DV
    o_a = o_b + GDN_HEADS
    o_mq = o_a + GDN_HEADS
    o_gg = o_mq + (MB_HEADS + 2 * MB_KV_HEADS) * MB_HD
    w_qkv = w_in[:, :CONV_CH].astype(BF16)
    w_pack = jnp.concatenate([w_in[:, o_b:o_a]] + [w_in[:, o_a:o_mq]] * 4
                             + [jnp.zeros((D, LANES - 5 * GDN_HEADS), F32)], axis=1).astype(BF16)
    w_mb = w_in[:, o_mq:o_gg].astype(BF16)
    w_gates = jnp.concatenate([w_in[:, o_gate:o_b], w_in[:, o_gg:]], axis=1).astype(BF16)

    def lane_row(v):
        return jnp.zeros((1, LANES), F32).at[0, PK_GC:PK_GC + 4 * GDN_HEADS].set(jnp.tile(v.astype(F32), 4))

    q, k, v, pack, gct = _qkv_proj(x2, w_qkv, w_pack, conv_w.astype(F32),
                                   lane_row(gdn_a_log), lane_row(gdn_dt_bias))
    gct3 = gct.reshape(GDN_HEADS, T // GDN_CHUNK, GDN_CHUNK).transpose(1, 0, 2)
    gdn_o = _gdn(q, k, v, pack, gct3, gdn_norm_w.astype(F32).reshape(1, GDN_DV))

    half = ROT_DIM // 2
    inv = ROPE_THETA ** (-jnp.arange(half, dtype=F32) / half)
    ang = jnp.arange(T).astype(F32)[:, None] * inv[None, :]
    ones = jnp.ones((T, MB_HD - ROT_DIM), F32)
    cos_t = jnp.concatenate([jnp.cos(ang), jnp.cos(ang), ones], axis=1)
    sin_t = jnp.concatenate([-jnp.sin(ang), jnp.sin(ang), 0.0 * ones], axis=1)
    mq, mk, mvt, kmean = _moba_proj(x2, w_mb, cos_t, sin_t)
    moba_a = _moba(mq, mk, mvt, kmean.transpose(1, 0, 2))

    h, res, eidx_t, wts_t = _post(
        x2, gdn_o, moba_a, w_gates, w_gdn_proj.astype(BF16), w_moba_proj.astype(BF16),
        w_out.astype(BF16), ln1_g.reshape(1, D), ln1_b.reshape(1, D),
        router_w.T, router_bias.reshape(N_EXPERTS, 1),
        sh_w_gate.astype(BF16), sh_w_up.astype(BF16), sh_w_down.astype(BF16))

    plan = _dispatch_plan(eidx_t, T)
    y = _experts(h, *plan, exp_w_gate, exp_w_up, exp_w_down)
    y3 = y.reshape(TOP_K, T * TOK_ROWS, LANES)
    out = _combine(res, y3, wts_t.T, ln2_g.reshape(1, D), ln2_b.reshape(1, D))
    return out.reshape(B, T, D)
```

```python
import functools
import math

import jax
import jax.numpy as jnp
from jax import lax
from jax.experimental import pallas as pl
from jax.experimental.pallas import tpu as pltpu

F32 = jnp.float32
BF16 = jnp.bfloat16

D_MODEL = 1024
DEPTH = 1
GDN_HEADS = 8
GDN_DK = 128
GDN_DV = 128
GDN_CONV = 4
GDN_CHUNK = 64
MB_HEADS = 8
MB_KV_HEADS = 2
MB_REP = MB_HEADS // MB_KV_HEADS
MB_HD = 128
MB_BLOCK = 256
MB_TOPK = 3
ROT_DIM = MB_HD // 4
ROPE_THETA = 500000.0
N_EXPERTS = 256
TOP_K = 8
N_GROUPS = 8
GROUP_SIZE = N_EXPERTS // N_GROUPS
TOPK_GROUPS = 4
EXPERT_DIM = 256
SHARED_DIM = 256
ROUTED_SCALE = 2.5
MOE_BLOCK = 128
DEEPNORM_ALPHA = (2.0 * DEPTH) ** 0.25
LN_EPS = 1e-5
MASK_NEG = -1e30

QK_COLS = GDN_HEADS * GDN_DK
CONV_CH = 2 * QK_COLS + GDN_HEADS * GDN_DV
LANES = 128
SUBLANES = 8
VMEM_LIMIT = 56 * 1024 * 1024

TOK_TILE = 256

PK_BETA, PK_GC, PK_EG, PK_EGD, PK_EGL = 0, 8, 16, 24, 32


def _sigmoid(x):
    return 1.0 / (1.0 + jnp.exp(-x))


def _softplus(x):
    return jnp.maximum(x, 0.0) + jnp.log(1.0 + jnp.exp(-jnp.abs(x)))


def _bdot(a, b):
    return jnp.dot(a.astype(BF16), b.astype(BF16), preferred_element_type=F32)


def _bdot_nt(a, b):
    return lax.dot_general(a.astype(BF16), b.astype(BF16), (((1,), (1,)), ((), ())),
                           preferred_element_type=F32)


def _bdot_tn(a, b):
    return lax.dot_general(a.astype(BF16), b.astype(BF16), (((0,), (0,)), ((), ())),
                           preferred_element_type=F32)


def _layer_norm(x, g, b):
    mu = jnp.mean(x, axis=-1, keepdims=True)
    xc = x - mu
    var = jnp.mean(xc * xc, axis=-1, keepdims=True)
    return xc * lax.rsqrt(var + LN_EPS) * g + b


TOK_ROWS = D_MODEL // LANES


def _store_token_tiles(ref, x):
    n = x.shape[0]
    for j in range(TOK_ROWS):
        ref[pl.ds(j, n, stride=TOK_ROWS), :] = x[:, j * LANES:(j + 1) * LANES]


def _load_token_tiles(ref, n):
    return jnp.concatenate([ref[pl.ds(j, n, stride=TOK_ROWS), :] for j in range(TOK_ROWS)], axis=1)


def _const_spec(shape):
    nd = len(shape)
    return pl.BlockSpec(shape, lambda *_: (0,) * nd, pipeline_mode=pl.Buffered(1))


def _qkv_kernel(x_ref, w_ref, wp_ref, cw_ref, alog_ref, dtb_ref,
                q_ref, k_ref, v_ref, pack_ref, gct_ref, z_ref):
    tm = x_ref.shape[0]
    halo = SUBLANES

    @pl.when(pl.program_id(0) == 0)
    def _():
        z_ref[0:halo, :] = jnp.zeros((halo, CONV_CH), F32)

    xb = x_ref[...].astype(BF16)
    z_ref[halo:halo + tm, :] = jnp.dot(xb, w_ref[...], preferred_element_type=F32)

    for c in range(CONV_CH // LANES):
        cols = slice(c * LANES, (c + 1) * LANES)
        acc = z_ref[halo:halo + tm, cols] * cw_ref[GDN_CONV - 1:GDN_CONV, cols]
        for s in range(1, GDN_CONV):
            acc = acc + z_ref[halo - s:halo - s + tm, cols] * cw_ref[GDN_CONV - 1 - s:GDN_CONV - s, cols]
        y = acc * _sigmoid(acc)
        h = c % GDN_HEADS
        if c < GDN_HEADS:
            y = y * lax.rsqrt(jnp.sum(y * y, axis=-1, keepdims=True) + 1e-6) * (GDN_DK ** -0.5)
            q_ref[h] = y
        elif c < 2 * GDN_HEADS:
            y = y * lax.rsqrt(jnp.sum(y * y, axis=-1, keepdims=True) + 1e-6)
            k_ref[h] = y
        else:
            v_ref[h] = y

    z_ref[0:halo, :] = z_ref[tm:tm + halo, :]

    zp = jnp.dot(xb, wp_ref[...], preferred_element_type=F32)
    beta = _sigmoid(zp)
    g = -jnp.exp(alog_ref[...]) * _softplus(zp + dtb_ref[...])
    row = lax.broadcasted_iota(jnp.int32, (tm, LANES), 0) % GDN_CHUNK
    gc = g
    step = 1
    while step < GDN_CHUNK:
        gc = gc + jnp.where(row >= step, pltpu.roll(gc, step, axis=0), 0.0)
        step *= 2
    gl = gc.reshape(tm // GDN_CHUNK, GDN_CHUNK, LANES)[:, GDN_CHUNK - 1:GDN_CHUNK, :]
    gl = jnp.broadcast_to(gl, (tm // GDN_CHUNK, GDN_CHUNK, LANES)).reshape(tm, LANES)
    lane = lax.broadcasted_iota(jnp.int32, (tm, LANES), 1)
    pack = jnp.where(lane < PK_GC, beta,
           jnp.where(lane < PK_EG, gc,
           jnp.where(lane < PK_EGD, jnp.exp(gc),
           jnp.where(lane < PK_EGL, jnp.exp(gl - gc), jnp.exp(gl)))))
    pack_ref[...] = pack
    gct_ref[...] = pack.T[PK_GC:PK_GC + GDN_HEADS, :]


def _qkv_proj(x2, w_qkv, w_pack, conv_w, alog_row, dtb_row):
    T = x2.shape[0]
    tm = TOK_TILE
    hd_spec = pl.BlockSpec((GDN_HEADS, tm, LANES), lambda i: (0, i, 0))
    hd_shape = jax.ShapeDtypeStruct((GDN_HEADS, T, LANES), F32)
    return pl.pallas_call(
        _qkv_kernel,
        grid=(T // tm,),
        in_specs=[
            pl.BlockSpec((tm, D_MODEL), lambda i: (i, 0)),
            _const_spec((D_MODEL, CONV_CH)),
            _const_spec((D_MODEL, LANES)),
            _const_spec((GDN_CONV, CONV_CH)),
            _const_spec((1, LANES)),
            _const_spec((1, LANES)),
        ],
        out_specs=[hd_spec, hd_spec, hd_spec,
                   pl.BlockSpec((tm, LANES), lambda i: (i, 0)),
                   pl.BlockSpec((GDN_HEADS, tm), lambda i: (0, i))],
        out_shape=[hd_shape, hd_shape, hd_shape,
                   jax.ShapeDtypeStruct((T, LANES), F32),
                   jax.ShapeDtypeStruct((GDN_HEADS, T), F32)],
        scratch_shapes=[pltpu.VMEM((tm + SUBLANES, CONV_CH), F32)],
        compiler_params=pltpu.CompilerParams(dimension_semantics=("arbitrary",),
                                             vmem_limit_bytes=VMEM_LIMIT),
        name="qkv_proj",
    )(x2, w_qkv, w_pack, conv_w, alog_row, dtb_row)


MB_LROWS = 16
MB_VROWS = MB_HD + MB_LROWS


def _rope(xh, cos_t, sin_t, lane):
    half = ROT_DIM // 2
    swapped = jnp.where(lane < half, pltpu.roll(xh, LANES - half, axis=1), pltpu.roll(xh, half, axis=1))
    return xh * cos_t + swapped * sin_t


def _moba_proj_kernel(x_ref, w_ref, cos_ref, sin_ref, q_ref, k_ref, vt_ref, km_ref):
    tm = x_ref.shape[0]
    z = jnp.dot(x_ref[...].astype(BF16), w_ref[...], preferred_element_type=F32)
    cos_t = cos_ref[...]
    sin_t = sin_ref[...]
    lane = lax.broadcasted_iota(jnp.int32, (tm, LANES), 1)
    for h in range(MB_HEADS):
        q_ref[h] = _rope(z[:, h * MB_HD:(h + 1) * MB_HD], cos_t, sin_t, lane)
    koff = MB_HEADS * MB_HD
    voff = koff + MB_KV_HEADS * MB_HD
    blk_onehot = jnp.where(lane == pl.program_id(0), 1.0, 0.0).astype(BF16)
    for g in range(MB_KV_HEADS):
        kr = _rope(z[:, koff + g * MB_HD:koff + (g + 1) * MB_HD], cos_t, sin_t, lane)
        k_ref[0, g] = jnp.concatenate([kr.astype(BF16), blk_onehot], axis=1)
        km_ref[0, g:g + 1, :] = jnp.mean(kr, axis=0, keepdims=True)
        vt_ref[0, g] = jnp.concatenate([z[:, voff + g * MB_HD:voff + (g + 1) * MB_HD].T,
                                        jnp.ones((MB_LROWS, tm), F32)], axis=0).astype(BF16)


def _moba_proj(x2, w_mb, cos_t, sin_t):
    T = x2.shape[0]
    tm = MB_BLOCK
    nb = T // tm
    return pl.pallas_call(
        _moba_proj_kernel,
        grid=(nb,),
        in_specs=[
            pl.BlockSpec((tm, D_MODEL), lambda i: (i, 0)),
            _const_spec(w_mb.shape),
            pl.BlockSpec((tm, LANES), lambda i: (i, 0)),
            pl.BlockSpec((tm, LANES), lambda i: (i, 0)),
        ],
        out_specs=[
            pl.BlockSpec((MB_HEADS, tm, MB_HD), lambda i: (0, i, 0)),
            pl.BlockSpec((1, MB_KV_HEADS, tm, 2 * MB_HD), lambda i: (i, 0, 0, 0)),
            pl.BlockSpec((1, MB_KV_HEADS, MB_VROWS, tm), lambda i: (i, 0, 0, 0)),
            pl.BlockSpec((1, MB_KV_HEADS, MB_HD), lambda i: (i, 0, 0)),
        ],
        out_shape=[
            jax.ShapeDtypeStruct((MB_HEADS, T, MB_HD), F32),
            jax.ShapeDtypeStruct((nb, MB_KV_HEADS, tm, 2 * MB_HD), BF16),
            jax.ShapeDtypeStruct((nb, MB_KV_HEADS, MB_VROWS, tm), BF16),
            jax.ShapeDtypeStruct((nb, MB_KV_HEADS, MB_HD), F32),
        ],
        compiler_params=pltpu.CompilerParams(dimension_semantics=("arbitrary",),
                                             vmem_limit_bytes=VMEM_LIMIT),
        name="moba_proj",
    )(x2, w_mb, cos_t, sin_t)


def _unit_lower_inverse(a_list):
    c = a_list[0].shape[0]
    rr = lax.broadcasted_iota(jnp.int32, (c, c), 0)
    cc = lax.broadcasted_iota(jnp.int32, (c, c), 1)
    eye = jnp.where(rr == cc, 1.0, 0.0).astype(F32)
    inv = [eye - a for a in a_list]
    p = [_bdot(a, a) for a in a_list]
    n = 2
    while True:
        inv = [x + _bdot(x, y) for x, y in zip(inv, p)]
        n *= 2
        if n >= c:
            break
        p = [_bdot(y, y) for y in p]
    return inv


GDN_STEP_CHUNKS = 4


def _gdn_kernel(q_ref, k_ref, v_ref, pack_ref, gct_ref, nw_ref, o_ref, s_ref):
    C = GDN_CHUNK
    H = range(GDN_HEADS)
    P = [(c, h) for c in range(GDN_STEP_CHUNKS) for h in H]

    @pl.when(pl.program_id(0) == 0)
    def _():
        s_ref[...] = jnp.zeros_like(s_ref)

    rr = lax.broadcasted_iota(jnp.int32, (C, C), 0)
    cc = lax.broadcasted_iota(jnp.int32, (C, C), 1)
    tril = rr >= cc
    strict = rr > cc
    pack = pack_ref[...]
    nw = nw_ref[...]

    def rows(c):
        return slice(c * C, (c + 1) * C)

    def col(base, c, h):
        return pack[rows(c), base + h:base + h + 1]

    q = {(c, h): q_ref[h, rows(c), :] for c, h in P}
    k = {(c, h): k_ref[h, rows(c), :] for c, h in P}
    kb = {(c, h): k[c, h] * col(PK_BETA, c, h) for c, h in P}
    decay = {(c, h): jnp.where(tril, jnp.exp(jnp.where(tril, col(PK_GC, c, h) - gct_ref[c, h:h + 1, :], 0.0)), 0.0)
             for c, h in P}
    sc = {p: _bdot_nt(jnp.concatenate([kb[p], q[p]], axis=0), k[p]) for p in P}
    a = {p: jnp.where(strict, sc[p][:C] * decay[p], 0.0) for p in P}
    aqk = {p: sc[p][C:] * decay[p] for p in P}
    tinv = dict(zip(P, _unit_lower_inverse([a[p] for p in P])))
    rhs = {(c, h): jnp.concatenate([v_ref[h, rows(c), :] * col(PK_BETA, c, h), kb[c, h] * col(PK_EG, c, h)],
                                   axis=1) for c, h in P}
    sol = {p: _bdot(tinv[p], rhs[p]) for p in P}

    s = [s_ref[h] for h in H]
    for c in range(GDN_STEP_CHUNKS):
        wq = [_bdot(jnp.concatenate([sol[c, h][:, GDN_DV:], q[c, h] * col(PK_EG, c, h)], axis=0), s[h])
              for h in H]
        v_new = [sol[c, h][:, :GDN_DV] - wq[h][:C] for h in H]
        o = [wq[h][C:] + _bdot(aqk[c, h], v_new[h]) for h in H]
        ds = [_bdot_tn(k[c, h] * col(PK_EGD, c, h), v_new[h]) for h in H]
        for h in H:
            s[h] = s[h] * pack[c * C:c * C + 1, PK_EGL + h:PK_EGL + h + 1] + ds[h]
            on = o[h] * lax.rsqrt(jnp.mean(o[h] * o[h], axis=-1, keepdims=True) + 1e-6) * nw
            o_ref[rows(c), h * GDN_DV:(h + 1) * GDN_DV] = on
    for h in H:
        s_ref[h] = s[h]


def _gdn(q, k, v, pack, gct3, norm_w_row):
    T = q.shape[1]
    C = GDN_CHUNK * GDN_STEP_CHUNKS
    hd_spec = pl.BlockSpec((GDN_HEADS, C, LANES), lambda i: (0, i, 0))
    return pl.pallas_call(
        _gdn_kernel,
        grid=(T // C,),
        in_specs=[hd_spec, hd_spec, hd_spec,
                  pl.BlockSpec((C, LANES), lambda i: (i, 0)),
                  pl.BlockSpec((GDN_STEP_CHUNKS, GDN_HEADS, GDN_CHUNK), lambda i: (i, 0, 0)),
                  _const_spec((1, GDN_DV))],
        out_specs=pl.BlockSpec((C, GDN_HEADS * GDN_DV), lambda i: (i, 0)),
        out_shape=jax.ShapeDtypeStruct((T, GDN_HEADS * GDN_DV), F32),
        scratch_shapes=[pltpu.VMEM((GDN_HEADS, GDN_DK, GDN_DV), F32)],
        compiler_params=pltpu.CompilerParams(dimension_semantics=("arbitrary",)),
        name="gdn",
    )(q, k, v, pack, gct3, norm_w_row)


MB_QSPLIT = 2
MB_TRIP = 4
MB_MAX_BLOCKS = MB_HD


def _moba_kernel(q_ref, k_ref, vt_ref, km_ref, o_ref, qx_ref, sa_ref, sb_ref, acc_ref, m_ref):
    qi = pl.program_id(1)
    nb = k_ref.shape[0]
    bs = MB_BLOCK
    nq = MB_REP * bs
    nbp = MB_MAX_BLOCKS
    c = (MB_HD ** -0.5) * math.log2(math.e)

    qf = q_ref[...].reshape(nq, MB_HD)

    gate = lax.dot_general(km_ref[...], qf, (((1,), (1,)), ((), ())),
                           precision=lax.Precision.HIGHEST, preferred_element_type=F32)
    blk = lax.broadcasted_iota(jnp.int32, (nbp, nq), 0).astype(F32)
    qif = qi.astype(F32)
    gate = jnp.where(blk < qif, gate, -jnp.inf)
    sel = jnp.where(blk == qif, 1.0, 0.0)
    for _ in range(MB_TOPK):
        top = jnp.max(gate, axis=0, keepdims=True)
        first = jnp.min(jnp.where(gate == top, blk, float(nbp)), axis=0, keepdims=True)
        hit = (blk == first) & (top > -jnp.inf)
        sel = jnp.where(hit, 1.0, sel)
        gate = jnp.where(hit, -jnp.inf, gate)
    bias_t = jnp.where(sel > 0.0, 0.0, MASK_NEG)
    qc = qf * c
    qx_ref[0] = jnp.concatenate([qc, bias_t.T], axis=1).astype(BF16)
    qx_ref[1] = jnp.concatenate([qc, jnp.full((nq, nbp), MASK_NEG, F32)], axis=1).astype(BF16)

    w = nq // MB_QSPLIT
    groups = [slice(h * w, (h + 1) * w) for h in range(MB_QSPLIT)]

    def scores_into(dst_ref, t):
        kx = k_ref[jnp.minimum(t, nb - 1)]
        qset = jnp.where(t < qi, 0, 1)
        for lanes in groups:
            dst_ref[:, lanes] = lax.dot_general(kx, qx_ref[qset, lanes, :], (((1,), (1,)), ((), ())),
                                                preferred_element_type=F32)

    st = lax.dot_general(k_ref[qi], qx_ref[0], (((1,), (1,)), ((), ())), preferred_element_type=F32)
    kpos = lax.broadcasted_iota(jnp.int32, (bs, nq), 0)
    qpos = lax.broadcasted_iota(jnp.int32, (bs, nq), 1) % bs
    st = jnp.where(kpos <= qpos, st, MASK_NEG)
    m0 = jnp.max(st, axis=0, keepdims=True)
    p = jnp.exp2(st - m0)
    m_ref[...] = m0
    acc_ref[...] = jnp.dot(vt_ref[qi], p.astype(BF16), preferred_element_type=F32)

    def absorb(src_ref, t):
        vt = vt_ref[jnp.minimum(t, nb - 1)]
        for lanes in groups:
            st = src_ref[:, lanes]
            m_old = m_ref[:, lanes]
            m_new = jnp.maximum(m_old, jnp.max(st, axis=0, keepdims=True))
            alpha = jnp.exp2(m_old - m_new)
            p = jnp.exp2(st - m_new)
            m_ref[:, lanes] = m_new
            acc_ref[:, lanes] = alpha * acc_ref[:, lanes] + jnp.dot(vt, p.astype(BF16),
                                                                    preferred_element_type=F32)

    scores_into(sa_ref, 0)

    def body(i, carry):
        t = MB_TRIP * i
        for u in range(0, MB_TRIP, 2):
            scores_into(sb_ref, t + u + 1)
            absorb(sa_ref, t + u)
            scores_into(sa_ref, t + u + 2)
            absorb(sb_ref, t + u + 1)
        return carry

    lax.fori_loop(0, (qi + MB_TRIP - 1) // MB_TRIP, body, 0)

    out_t = acc_ref[0:MB_HD, :] / acc_ref[MB_HD:MB_HD + 1, :]
    for r in range(MB_REP):
        o_ref[:, r * MB_HD:(r + 1) * MB_HD] = out_t[:, r * bs:(r + 1) * bs].T


def _moba(mq, mk, mvt, kmean):
    T = mq.shape[1]
    bs = MB_BLOCK
    nb = T // bs
    nq = MB_REP * bs
    assert nb <= MB_MAX_BLOCKS
    kmean = jnp.pad(kmean, ((0, 0), (0, MB_MAX_BLOCKS - nb), (0, 0)))
    return pl.pallas_call(
        _moba_kernel,
        grid=(MB_KV_HEADS, nb),
        in_specs=[
            pl.BlockSpec((MB_REP, bs, MB_HD), lambda g, i: (g, i, 0)),
            pl.BlockSpec((nb, None, bs, 2 * MB_HD), lambda g, i: (0, g, 0, 0)),
            pl.BlockSpec((nb, None, MB_VROWS, bs), lambda g, i: (0, g, 0, 0)),
            pl.BlockSpec((None, MB_MAX_BLOCKS, MB_HD), lambda g, i: (g, 0, 0)),
        ],
        out_specs=pl.BlockSpec((bs, MB_REP * MB_HD), lambda g, i: (i, g)),
        out_shape=jax.ShapeDtypeStruct((T, MB_HEADS * MB_HD), F32),
        scratch_shapes=[pltpu.VMEM((2, nq, 2 * MB_HD), BF16),
                        pltpu.VMEM((bs, nq), F32), pltpu.VMEM((bs, nq), F32),
                        pltpu.VMEM((MB_VROWS, nq), F32),
                        pltpu.VMEM((1, nq), F32)],
        compiler_params=pltpu.CompilerParams(dimension_semantics=("arbitrary", "arbitrary"),
                                             vmem_limit_bytes=VMEM_LIMIT),
        name="moba",
    )(mq, mk, mvt, kmean)


def _post_kernel(x_ref, go_ref, ma_ref, wg_ref, wgp_ref, wmp_ref, wo_ref, g1_ref, b1_ref,
                 rwt_ref, rb_ref, sg_ref, su_ref, sd_ref,
                 h_ref, res_ref, eidx_ref, wts_ref):
    tm = x_ref.shape[0]
    x = x_ref[...]
    zg = jnp.dot(x.astype(BF16), wg_ref[...], preferred_element_type=F32)
    gate = zg[:, :D_MODEL]
    o = go_ref[...] * (gate * _sigmoid(gate))
    y_gdn = jnp.dot(o.astype(BF16), wgp_ref[...], preferred_element_type=F32)
    y_mb = jnp.dot(ma_ref[...].astype(BF16), wmp_ref[...], preferred_element_type=F32)
    m = _sigmoid(zg[:, D_MODEL:2 * D_MODEL]) * y_gdn + _sigmoid(zg[:, 2 * D_MODEL:]) * y_mb
    mix = jnp.dot(m.astype(BF16), wo_ref[...], preferred_element_type=F32)
    h = _layer_norm(DEEPNORM_ALPHA * x + mix, g1_ref[...], b1_ref[...])
    _store_token_tiles(h_ref, h)
    hb = h.astype(BF16)

    hs = jnp.dot(hb, sg_ref[...], preferred_element_type=F32)
    hs = hs * _sigmoid(hs) * jnp.dot(hb, su_ref[...], preferred_element_type=F32)
    res_ref[...] = DEEPNORM_ALPHA * h + jnp.dot(hs.astype(BF16), sd_ref[...], preferred_element_type=F32)

    logits = lax.dot_general(rwt_ref[...], h, (((1,), (1,)), ((), ())),
                             precision=lax.Precision.HIGHEST, preferred_element_type=F32)
    scores = _sigmoid(logits)
    choice = scores + rb_ref[...]
    neg = -jnp.inf
    gi = lax.broadcasted_iota(jnp.int32, (GROUP_SIZE, tm), 0).astype(F32)
    gscore = []
    for g in range(N_GROUPS):
        cg = choice[g * GROUP_SIZE:(g + 1) * GROUP_SIZE, :]
        m1 = jnp.max(cg, axis=0, keepdims=True)
        i1 = jnp.min(jnp.where(cg == m1, gi, float(GROUP_SIZE)), axis=0, keepdims=True)
        m2 = jnp.max(jnp.where(gi == i1, neg, cg), axis=0, keepdims=True)
        gscore.append(m1 + m2)
    gs = jnp.concatenate(gscore, axis=0)
    gidx = lax.broadcasted_iota(jnp.int32, (N_GROUPS, tm), 0).astype(F32)
    gsel = jnp.zeros((N_GROUPS, tm), F32)
    for _ in range(TOPK_GROUPS):
        top = jnp.max(gs, axis=0, keepdims=True)
        first = jnp.min(jnp.where(gs == top, gidx, float(N_GROUPS)), axis=0, keepdims=True)
        hit = gidx == first
        gsel = jnp.where(hit, 1.0, gsel)
        gs = jnp.where(hit, neg, gs)
    masked = jnp.concatenate(
        [jnp.where(gsel[g:g + 1, :] > 0.0, choice[g * GROUP_SIZE:(g + 1) * GROUP_SIZE, :], neg)
         for g in range(N_GROUPS)], axis=0)
    ei = lax.broadcasted_iota(jnp.int32, (N_EXPERTS, tm), 0).astype(F32)
    idx_rows, w_rows = [], []
    for _ in range(TOP_K):
        top = jnp.max(masked, axis=0, keepdims=True)
        first = jnp.min(jnp.where(masked == top, ei, float(N_EXPERTS)), axis=0, keepdims=True)
        hit = ei == first
        idx_rows.append(first)
        w_rows.append(jnp.sum(jnp.where(hit, scores, 0.0), axis=0, keepdims=True))
        masked = jnp.where(hit, neg, masked)
    w = jnp.concatenate(w_rows, axis=0)
    w = w / (jnp.sum(w, axis=0, keepdims=True) + 1e-20) * ROUTED_SCALE
    eidx_ref[...] = jnp.concatenate(idx_rows, axis=0).astype(jnp.int32)
    wts_ref[...] = w


def _post(x2, gdn_o, moba_a, w_gates, w_gdn_proj, w_moba_proj, w_out, ln_g, ln_b,
          router_wt, router_b_col, sh_g, sh_u, sh_d):
    T = x2.shape[0]
    tm = TOK_TILE
    row_spec = pl.BlockSpec((tm, D_MODEL), lambda i: (i, 0))
    consts = [w_gates, w_gdn_proj, w_moba_proj, w_out, ln_g, ln_b, router_wt, router_b_col,
              sh_g, sh_u, sh_d]
    return pl.pallas_call(
        _post_kernel,
        grid=(T // tm,),
        in_specs=[row_spec, row_spec, row_spec] + [_const_spec(c.shape) for c in consts],
        out_specs=[pl.BlockSpec((tm * TOK_ROWS, LANES), lambda i: (i, 0)), row_spec,
                   pl.BlockSpec((TOP_K, tm), lambda i: (0, i)),
                   pl.BlockSpec((TOP_K, tm), lambda i: (0, i))],
        out_shape=[jax.ShapeDtypeStruct((T * TOK_ROWS, LANES), F32),
                   jax.ShapeDtypeStruct((T, D_MODEL), F32),
                   jax.ShapeDtypeStruct((TOP_K, T), jnp.int32),
                   jax.ShapeDtypeStruct((TOP_K, T), F32)],
        compiler_params=pltpu.CompilerParams(dimension_semantics=("arbitrary",),
                                             vmem_limit_bytes=VMEM_LIMIT),
        name="post",
    )(x2, gdn_o, moba_a, *consts)


ROW_UNROLL = 8


def _experts_kernel(iblk_ref, iexp_ref, ilo_ref, ihi_ref, src_ref, srcn_ref, dst_ref,
                    h_ref, wg_ref, wu_ref, wd_ref, out_ref,
                    xbuf, ybuf, wgb, wub, wdb, cast_exp, gsem, ssem, *, n_blocks):
    p = pl.program_id(0)
    n_items = pl.num_programs(0)
    w = iblk_ref[p]
    slot = w % 2
    first = (p == 0) | (w != iblk_ref[jnp.maximum(p - 1, 0)])
    last = (p == n_items - 1) | (w != iblk_ref[jnp.minimum(p + 1, n_items - 1)])
    lo, hi = ilo_ref[p], ihi_ref[p]
    block_rows = MOE_BLOCK * TOK_ROWS

    def tile_at(ref, row0):
        return ref.at[pl.ds(pl.multiple_of(row0, TOK_ROWS), TOK_ROWS)]

    def for_block_rows(fn):
        def group(g, c):
            for u in range(ROW_UNROLL):
                fn(g * ROW_UNROLL + u, u % 2)
            return c
        lax.fori_loop(0, MOE_BLOCK // ROW_UNROLL, group, 0)

    def start_gather(rows_ref, s):
        for_block_rows(lambda i, prio: pltpu.make_async_copy(
            tile_at(h_ref, rows_ref[0, 0, i]), tile_at(xbuf.at[s], i * TOK_ROWS), gsem.at[s]).start(priority=prio))

    def start_scatter(s):
        for_block_rows(lambda i, prio: pltpu.make_async_copy(
            tile_at(ybuf.at[s], i * TOK_ROWS), tile_at(out_ref, dst_ref[0, 0, i]), ssem.at[s]).start(priority=prio))

    def wait_gather(s):
        pltpu.make_async_copy(h_ref.at[pl.ds(0, block_rows)], xbuf.at[s], gsem.at[s]).wait()

    def wait_scatter(s):
        pltpu.make_async_copy(ybuf.at[s], out_ref.at[pl.ds(0, block_rows)], ssem.at[s]).wait()

    @pl.when(p == 0)
    def _():
        ybuf[...] = jnp.zeros_like(ybuf)
        cast_exp[0] = -1
        start_gather(src_ref, 0)

    @pl.when(first)
    def _():
        @pl.when(w + 1 < n_blocks)
        def _():
            start_gather(srcn_ref, 1 - slot)

        @pl.when(w >= 2)
        def _():
            wait_scatter(slot)
        wait_gather(slot)

    @pl.when(hi > lo)
    def _():
        @pl.when(iexp_ref[p] != cast_exp[0])
        def _():
            wgb[...] = wg_ref[...].astype(BF16)
            wub[...] = wu_ref[...].astype(BF16)
            wdb[...] = wd_ref[...].astype(BF16)
            cast_exp[0] = iexp_ref[p]

        xb = _load_token_tiles(xbuf.at[slot], MOE_BLOCK).astype(BF16)
        hg = jnp.dot(xb, wgb[...], preferred_element_type=F32)
        hu = jnp.dot(xb, wub[...], preferred_element_type=F32)
        hb = (hg * _sigmoid(hg) * hu).astype(BF16)
        y = jnp.dot(hb, wdb[...], preferred_element_type=F32)
        row = lax.broadcasted_iota(jnp.int32, (MOE_BLOCK, 1), 0)
        mine = (row >= lo) & (row < hi)
        _store_token_tiles(ybuf.at[slot], jnp.where(mine, y, _load_token_tiles(ybuf.at[slot], MOE_BLOCK)))

    @pl.when(last)
    def _():
        start_scatter(slot)

    @pl.when(p == n_items - 1)
    def _():
        @pl.when(w >= 1)
        def _():
            wait_scatter(1 - slot)
        wait_scatter(slot)


def _experts(h, src_rows, dst_rows, item_blk, item_exp, item_lo, item_hi, wg, wu, wd):
    T = h.shape[0] // TOK_ROWS
    n_blocks = src_rows.shape[0]
    n_items = item_blk.shape[0]
    rows = MOE_BLOCK
    cur_spec = pl.BlockSpec((1, 1, rows), lambda p, ib, ie, il, ih: (ib[p], 0, 0), memory_space=pltpu.SMEM)
    nxt_spec = pl.BlockSpec((1, 1, rows), lambda p, ib, ie, il, ih: (jnp.minimum(ib[p] + 1, n_blocks - 1), 0, 0),
                            memory_space=pltpu.SMEM)

    def w_spec(shape):
        return pl.BlockSpec((None,) + shape, lambda p, ib, ie, il, ih: (ie[p], 0, 0))

    grid_spec = pltpu.PrefetchScalarGridSpec(
        num_scalar_prefetch=4,
        grid=(n_items,),
        in_specs=[cur_spec, nxt_spec, cur_spec, pl.BlockSpec(memory_space=pl.ANY),
                  w_spec((D_MODEL, EXPERT_DIM)), w_spec((D_MODEL, EXPERT_DIM)), w_spec((EXPERT_DIM, D_MODEL))],
        out_specs=pl.BlockSpec(memory_space=pl.ANY),
        scratch_shapes=[pltpu.VMEM((2, rows * TOK_ROWS, LANES), F32),
                        pltpu.VMEM((2, rows * TOK_ROWS, LANES), F32),
                        pltpu.VMEM((D_MODEL, EXPERT_DIM), BF16), pltpu.VMEM((D_MODEL, EXPERT_DIM), BF16),
                        pltpu.VMEM((EXPERT_DIM, D_MODEL), BF16),
                        pltpu.SMEM((1,), jnp.int32),
                        pltpu.SemaphoreType.DMA((2,)), pltpu.SemaphoreType.DMA((2,))],
    )
    return pl.pallas_call(
        functools.partial(_experts_kernel, n_blocks=n_blocks),
        grid_spec=grid_spec,
        out_shape=jax.ShapeDtypeStruct((TOP_K * T * TOK_ROWS, LANES), F32),
        compiler_params=pltpu.CompilerParams(dimension_semantics=("arbitrary",),
                                             vmem_limit_bytes=VMEM_LIMIT,
                                             disable_bounds_checks=True),
        name="experts",
    )(item_blk, item_exp, item_lo, item_hi, src_rows, src_rows, dst_rows, h, wg, wu, wd)


def _combine_kernel(res_ref, y_ref, w_ref, g_ref, b_ref, o_ref):
    tm = res_ref.shape[0]
    acc = res_ref[...]
    w = w_ref[...]
    for k in range(TOP_K):
        acc = acc + w[:, k:k + 1] * _load_token_tiles(y_ref.at[k], tm)
    o_ref[...] = _layer_norm(acc, g_ref[...], b_ref[...])


def _combine(res, y3, wts, ln_g, ln_b):
    T = res.shape[0]
    tm = 128
    return pl.pallas_call(
        _combine_kernel,
        grid=(T // tm,),
        in_specs=[pl.BlockSpec((tm, D_MODEL), lambda i: (i, 0)),
                  pl.BlockSpec((TOP_K, tm * TOK_ROWS, LANES), lambda i: (0, i, 0)),
                  pl.BlockSpec((tm, TOP_K), lambda i: (i, 0)),
                  _const_spec((1, D_MODEL)), _const_spec((1, D_MODEL))],
        out_specs=pl.BlockSpec((tm, D_MODEL), lambda i: (i, 0)),
        out_shape=jax.ShapeDtypeStruct((T, D_MODEL), F32),
        compiler_params=pltpu.CompilerParams(dimension_semantics=("arbitrary",)),
        name="combine",
    )(res, y3, wts, ln_g, ln_b)


def _dispatch_plan(eidx_t, n_tok):
    nk = n_tok * TOP_K
    n_blocks = nk // MOE_BLOCK
    e_flat = eidx_t.T.reshape(nk)
    _, order = lax.sort((e_flat, jnp.arange(nk, dtype=jnp.int32)), num_keys=1)
    experts = jnp.arange(N_EXPERTS, dtype=jnp.int32)
    counts = jnp.sum((e_flat[None, :] == experts[:, None]).astype(jnp.int32), axis=1)
    start = jnp.cumsum(counts) - counts
    pos = jnp.sort(jnp.concatenate([jnp.arange(n_blocks, dtype=jnp.int32) * MOE_BLOCK, start]))
    nxt = jnp.concatenate([pos[1:], jnp.full((1,), nk, jnp.int32)])
    item_blk = jnp.minimum(pos // MOE_BLOCK, n_blocks - 1)
    item_exp = jnp.sum((start[None, :] <= pos[:, None]).astype(jnp.int32), axis=1) - 1
    item_lo = pos - item_blk * MOE_BLOCK
    item_hi = jnp.minimum(nxt, (item_blk + 1) * MOE_BLOCK) - item_blk * MOE_BLOCK
    tok, slot_k = order // TOP_K, order % TOP_K
    shape3 = (n_blocks, 1, MOE_BLOCK)
    src_rows = (tok * TOK_ROWS).reshape(shape3)
    dst_rows = ((slot_k * n_tok + tok) * TOK_ROWS).reshape(shape3)
    return src_rows, dst_rows, item_blk, item_exp, item_lo, item_hi


def kernel(x, w_in, conv_w, gdn_a_log, gdn_dt_bias, gdn_norm_w, w_gdn_proj, w_moba_proj, w_out,
           ln1_g, ln1_b, router_w, router_bias, exp_w_gate, exp_w_up, exp_w_down,
           sh_w_gate, sh_w_up, sh_w_down, ln2_g, ln2_b):
    B, T, D = x.shape
    assert B == 1 and D == D_MODEL and T % MB_BLOCK == 0
    x2 = x.reshape(T, D)

    o_gate = CONV_CH
    o_b = o_gate + GDN_HEADS * GDN_DV
    o_a = o_b + GDN_HEADS
    o_mq = o_a + GDN_HEADS
    o_gg = o_mq + (MB_HEADS + 2 * MB_KV_HEADS) * MB_HD
    w_qkv = w_in[:, :CONV_CH].astype(BF16)
    w_pack = jnp.concatenate([w_in[:, o_b:o_a]] + [w_in[:, o_a:o_mq]] * 4
                             + [jnp.zeros((D, LANES - 5 * GDN_HEADS), F32)], axis=1).astype(BF16)
    w_mb = w_in[:, o_mq:o_gg].astype(BF16)
    w_gates = jnp.concatenate([w_in[:, o_gate:o_b], w_in[:, o_gg:]], axis=1).astype(BF16)

    def lane_row(v):
        return jnp.zeros((1, LANES), F32).at[0, PK_GC:PK_GC + 4 * GDN_HEADS].set(jnp.tile(v.astype(F32), 4))

    q, k, v, pack, gct = _qkv_proj(x2, w_qkv, w_pack, conv_w.astype(F32),
                                   lane_row(gdn_a_log), lane_row(gdn_dt_bias))
    gct3 = gct.reshape(GDN_HEADS, T // GDN_CHUNK, GDN_CHUNK).transpose(1, 0, 2)
    gdn_o = _gdn(q, k, v, pack, gct3, gdn_norm_w.astype(F32).reshape(1, GDN_DV))

    half = ROT_DIM // 2
    inv = ROPE_THETA ** (-jnp.arange(half, dtype=F32) / half)
    ang = jnp.arange(T).astype(F32)[:, None] * inv[None, :]
    ones = jnp.ones((T, MB_HD - ROT_DIM), F32)
    cos_t = jnp.concatenate([jnp.cos(ang), jnp.cos(ang), ones], axis=1)
    sin_t = jnp.concatenate([-jnp.sin(ang), jnp.sin(ang), 0.0 * ones], axis=1)
    mq, mk, mvt, kmean = _moba_proj(x2, w_mb, cos_t, sin_t)
    moba_a = _moba(mq, mk, mvt, kmean.transpose(1, 0, 2))

    h, res, eidx_t, wts_t = _post(
        x2, gdn_o, moba_a, w_gates, w_gdn_proj.astype(BF16), w_moba_proj.astype(BF16),
        w_out.astype(BF16), ln1_g.reshape(1, D), ln1_b.reshape(1, D),
        router_w.T, router_bias.reshape(N_EXPERTS, 1),
        sh_w_gate.astype(BF16), sh_w_up.astype(BF16), sh_w_down.astype(BF16))

    plan = _dispatch_plan(eidx_t, T)
    y = _experts(h, *plan, exp_w_gate, exp_w_up, exp_w_down)
    y3 = y.reshape(TOP_K, T * TOK_ROWS, LANES)
    out = _combine(res, y3, wts_t.T, ln2_g.reshape(1, D), ln2_b.reshape(1, D))
    return out.reshape(B, T, D)
```

```python
import functools
import math

import jax
import jax.numpy as jnp
from jax import lax
from jax.experimental import pallas as pl
from jax.experimental.pallas import tpu as pltpu

F32 = jnp.float32
BF16 = jnp.bfloat16

D_MODEL = 1024
DEPTH = 1
GDN_HEADS = 8
GDN_DK = 128
GDN_DV = 128
GDN_CONV = 4
GDN_CHUNK = 64
MB_HEADS = 8
MB_KV_HEADS = 2
MB_REP = MB_HEADS // MB_KV_HEADS
MB_HD = 128
MB_BLOCK = 256
MB_TOPK = 3
ROT_DIM = MB_HD // 4
ROPE_THETA = 500000.0
N_EXPERTS = 256
TOP_K = 8
N_GROUPS = 8
GROUP_SIZE = N_EXPERTS // N_GROUPS
TOPK_GROUPS = 4
EXPERT_DIM = 256
SHARED_DIM = 256
ROUTED_SCALE = 2.5
MOE_BLOCK = 256
DEEPNORM_ALPHA = (2.0 * DEPTH) ** 0.25
LN_EPS = 1e-5
MASK_NEG = -1e30

QK_COLS = GDN_HEADS * GDN_DK
CONV_CH = 2 * QK_COLS + GDN_HEADS * GDN_DV
LANES = 128
SUBLANES = 8
VMEM_LIMIT = 56 * 1024 * 1024

TOK_TILE = 256

PK_BETA, PK_GC, PK_EG, PK_EGD, PK_EGL = 0, 8, 16, 24, 32


def _sigmoid(x):
    return 1.0 / (1.0 + jnp.exp(-x))


def _softplus(x):
    return jnp.maximum(x, 0.0) + jnp.log(1.0 + jnp.exp(-jnp.abs(x)))


def _bdot(a, b):
    return jnp.dot(a.astype(BF16), b.astype(BF16), preferred_element_type=F32)


def _bdot_nt(a, b):
    return lax.dot_general(a.astype(BF16), b.astype(BF16), (((1,), (1,)), ((), ())),
                           preferred_element_type=F32)


def _bdot_tn(a, b):
    return lax.dot_general(a.astype(BF16), b.astype(BF16), (((0,), (0,)), ((), ())),
                           preferred_element_type=F32)


def _layer_norm(x, g, b):
    mu = jnp.mean(x, axis=-1, keepdims=True)
    xc = x - mu
    var = jnp.mean(xc * xc, axis=-1, keepdims=True)
    return xc * lax.rsqrt(var + LN_EPS) * g + b


TOK_ROWS = D_MODEL // LANES


def _store_token_tiles(ref, x):
    n = x.shape[0]
    for j in range(TOK_ROWS):
        ref[pl.ds(j, n, stride=TOK_ROWS), :] = x[:, j * LANES:(j + 1) * LANES]


def _load_token_tiles(ref, n):
    return jnp.concatenate([ref[pl.ds(j, n, stride=TOK_ROWS), :] for j in range(TOK_ROWS)], axis=1)


def _const_spec(shape):
    nd = len(shape)
    return pl.BlockSpec(shape, lambda *_: (0,) * nd, pipeline_mode=pl.Buffered(1))


def _qkv_kernel(x_ref, w_ref, wp_ref, cw_ref, alog_ref, dtb_ref,
                q_ref, k_ref, v_ref, pack_ref, gct_ref, z_ref):
    tm = x_ref.shape[0]
    halo = SUBLANES

    @pl.when(pl.program_id(0) == 0)
    def _():
        z_ref[0:halo, :] = jnp.zeros((halo, CONV_CH), F32)

    xb = x_ref[...].astype(BF16)
    z_ref[halo:halo + tm, :] = jnp.dot(xb, w_ref[...], preferred_element_type=F32)

    for c in range(CONV_CH // LANES):
        cols = slice(c * LANES, (c + 1) * LANES)
        acc = z_ref[halo:halo + tm, cols] * cw_ref[GDN_CONV - 1:GDN_CONV, cols]
        for s in range(1, GDN_CONV):
            acc = acc + z_ref[halo - s:halo - s + tm, cols] * cw_ref[GDN_CONV - 1 - s:GDN_CONV - s, cols]
        y = acc * _sigmoid(acc)
        h = c % GDN_HEADS
        if c < GDN_HEADS:
            y = y * lax.rsqrt(jnp.sum(y * y, axis=-1, keepdims=True) + 1e-6) * (GDN_DK ** -0.5)
            q_ref[h] = y
        elif c < 2 * GDN_HEADS:
            y = y * lax.rsqrt(jnp.sum(y * y, axis=-1, keepdims=True) + 1e-6)
            k_ref[h] = y
        else:
            v_ref[h] = y

    z_ref[0:halo, :] = z_ref[tm:tm + halo, :]

    zp = jnp.dot(xb, wp_ref[...], preferred_element_type=F32)
    beta = _sigmoid(zp)
    g = -jnp.exp(alog_ref[...]) * _softplus(zp + dtb_ref[...])
    row = lax.broadcasted_iota(jnp.int32, (tm, LANES), 0) % GDN_CHUNK
    gc = g
    step = 1
    while step < GDN_CHUNK:
        gc = gc + jnp.where(row >= step, pltpu.roll(gc, step, axis=0), 0.0)
        step *= 2
    gl = gc.reshape(tm // GDN_CHUNK, GDN_CHUNK, LANES)[:, GDN_CHUNK - 1:GDN_CHUNK, :]
    gl = jnp.broadcast_to(gl, (tm // GDN_CHUNK, GDN_CHUNK, LANES)).reshape(tm, LANES)
    lane = lax.broadcasted_iota(jnp.int32, (tm, LANES), 1)
    pack = jnp.where(lane < PK_GC, beta,
           jnp.where(lane < PK_EG, gc,
           jnp.where(lane < PK_EGD, jnp.exp(gc),
           jnp.where(lane < PK_EGL, jnp.exp(gl - gc), jnp.exp(gl)))))
    pack_ref[...] = pack
    gct_ref[...] = pack.T[PK_GC:PK_GC + GDN_HEADS, :]


def _qkv_proj(x2, w_qkv, w_pack, conv_w, alog_row, dtb_row):
    T = x2.shape[0]
    tm = TOK_TILE
    hd_spec = pl.BlockSpec((GDN_HEADS, tm, LANES), lambda i: (0, i, 0))
    hd_shape = jax.ShapeDtypeStruct((GDN_HEADS, T, LANES), F32)
    return pl.pallas_call(
        _qkv_kernel,
        grid=(T // tm,),
        in_specs=[
            pl.BlockSpec((tm, D_MODEL), lambda i: (i, 0)),
            _const_spec((D_MODEL, CONV_CH)),
            _const_spec((D_MODEL, LANES)),
            _const_spec((GDN_CONV, CONV_CH)),
            _const_spec((1, LANES)),
            _const_spec((1, LANES)),
        ],
        out_specs=[hd_spec, hd_spec, hd_spec,
                   pl.BlockSpec((tm, LANES), lambda i: (i, 0)),
                   pl.BlockSpec((GDN_HEADS, tm), lambda i: (0, i))],
        out_shape=[hd_shape, hd_shape, hd_shape,
                   jax.ShapeDtypeStruct((T, LANES), F32),
                   jax.ShapeDtypeStruct((GDN_HEADS, T), F32)],
        scratch_shapes=[pltpu.VMEM((tm + SUBLANES, CONV_CH), F32)],
        compiler_params=pltpu.CompilerParams(dimension_semantics=("arbitrary",),
                                             vmem_limit_bytes=VMEM_LIMIT),
        name="qkv_proj",
    )(x2, w_qkv, w_pack, conv_w, alog_row, dtb_row)


MB_LROWS = 16
MB_VROWS = MB_HD + MB_LROWS


def _rope(xh, cos_t, sin_t, lane):
    half = ROT_DIM // 2
    swapped = jnp.where(lane < half, pltpu.roll(xh, LANES - half, axis=1), pltpu.roll(xh, half, axis=1))
    return xh * cos_t + swapped * sin_t


def _moba_proj_kernel(x_ref, w_ref, cos_ref, sin_ref, q_ref, k_ref, vt_ref, km_ref):
    tm = x_ref.shape[0]
    z = jnp.dot(x_ref[...].astype(BF16), w_ref[...], preferred_element_type=F32)
    cos_t = cos_ref[...]
    sin_t = sin_ref[...]
    lane = lax.broadcasted_iota(jnp.int32, (tm, LANES), 1)
    for h in range(MB_HEADS):
        q_ref[h] = _rope(z[:, h * MB_HD:(h + 1) * MB_HD], cos_t, sin_t, lane)
    koff = MB_HEADS * MB_HD
    voff = koff + MB_KV_HEADS * MB_HD
    blk_onehot = jnp.where(lane == pl.program_id(0), 1.0, 0.0).astype(BF16)
    for g in range(MB_KV_HEADS):
        kr = _rope(z[:, koff + g * MB_HD:koff + (g + 1) * MB_HD], cos_t, sin_t, lane)
        k_ref[0, g] = jnp.concatenate([kr.astype(BF16), blk_onehot], axis=1)
        km_ref[0, g:g + 1, :] = jnp.mean(kr, axis=0, keepdims=True)
        vt_ref[0, g] = jnp.concatenate([z[:, voff + g * MB_HD:voff + (g + 1) * MB_HD].T,
                                        jnp.ones((MB_LROWS, tm), F32)], axis=0).astype(BF16)


def _moba_proj(x2, w_mb, cos_t, sin_t):
    T = x2.shape[0]
    tm = MB_BLOCK
    nb = T // tm
    return pl.pallas_call(
        _moba_proj_kernel,
        grid=(nb,),
        in_specs=[
            pl.BlockSpec((tm, D_MODEL), lambda i: (i, 0)),
            _const_spec(w_mb.shape),
            pl.BlockSpec((tm, LANES), lambda i: (i, 0)),
            pl.BlockSpec((tm, LANES), lambda i: (i, 0)),
        ],
        out_specs=[
            pl.BlockSpec((MB_HEADS, tm, MB_HD), lambda i: (0, i, 0)),
            pl.BlockSpec((1, MB_KV_HEADS, tm, 2 * MB_HD), lambda i: (i, 0, 0, 0)),
            pl.BlockSpec((1, MB_KV_HEADS, MB_VROWS, tm), lambda i: (i, 0, 0, 0)),
            pl.BlockSpec((1, MB_KV_HEADS, MB_HD), lambda i: (i, 0, 0)),
        ],
        out_shape=[
            jax.ShapeDtypeStruct((MB_HEADS, T, MB_HD), F32),
            jax.ShapeDtypeStruct((nb, MB_KV_HEADS, tm, 2 * MB_HD), BF16),
            jax.ShapeDtypeStruct((nb, MB_KV_HEADS, MB_VROWS, tm), BF16),
            jax.ShapeDtypeStruct((nb, MB_KV_HEADS, MB_HD), F32),
        ],
        compiler_params=pltpu.CompilerParams(dimension_semantics=("arbitrary",),
                                             vmem_limit_bytes=VMEM_LIMIT),
        name="moba_proj",
    )(x2, w_mb, cos_t, sin_t)


def _unit_lower_inverse(a_list):
    c = a_list[0].shape[0]
    rr = lax.broadcasted_iota(jnp.int32, (c, c), 0)
    cc = lax.broadcasted_iota(jnp.int32, (c, c), 1)
    eye = jnp.where(rr == cc, 1.0, 0.0).astype(F32)
    inv = [eye - a for a in a_list]
    p = [_bdot(a, a) for a in a_list]
    n = 2
    while True:
        inv = [x + _bdot(x, y) for x, y in zip(inv, p)]
        n *= 2
        if n >= c:
            break
        p = [_bdot(y, y) for y in p]
    return inv


GDN_STEP_CHUNKS = 4


def _gdn_kernel(q_ref, k_ref, v_ref, pack_ref, gct_ref, nw_ref, o_ref, s_ref):
    C = GDN_CHUNK
    H = range(GDN_HEADS)
    P = [(c, h) for c in range(GDN_STEP_CHUNKS) for h in H]

    @pl.when(pl.program_id(0) == 0)
    def _():
        s_ref[...] = jnp.zeros_like(s_ref)

    rr = lax.broadcasted_iota(jnp.int32, (C, C), 0)
    cc = lax.broadcasted_iota(jnp.int32, (C, C), 1)
    tril = rr >= cc
    strict = rr > cc
    pack = pack_ref[...]
    nw = nw_ref[...]

    def rows(c):
        return slice(c * C, (c + 1) * C)

    def col(base, c, h):
        return pack[rows(c), base + h:base + h + 1]

    q = {(c, h): q_ref[h, rows(c), :] for c, h in P}
    k = {(c, h): k_ref[h, rows(c), :] for c, h in P}
    kb = {(c, h): k[c, h] * col(PK_BETA, c, h) for c, h in P}
    decay = {(c, h): jnp.where(tril, jnp.exp(jnp.where(tril, col(PK_GC, c, h) - gct_ref[c, h:h + 1, :], 0.0)), 0.0)
             for c, h in P}
    sc = {p: _bdot_nt(jnp.concatenate([kb[p], q[p]], axis=0), k[p]) for p in P}
    a = {p: jnp.where(strict, sc[p][:C] * decay[p], 0.0) for p in P}
    aqk = {p: sc[p][C:] * decay[p] for p in P}
    tinv = dict(zip(P, _unit_lower_inverse([a[p] for p in P])))
    rhs = {(c, h): jnp.concatenate([v_ref[h, rows(c), :] * col(PK_BETA, c, h), kb[c, h] * col(PK_EG, c, h)],
                                   axis=1) for c, h in P}
    sol = {p: _bdot(tinv[p], rhs[p]) for p in P}

    s = [s_ref[h] for h in H]
    for c in range(GDN_STEP_CHUNKS):
        wq = [_bdot(jnp.concatenate([sol[c, h][:, GDN_DV:], q[c, h] * col(PK_EG, c, h)], axis=0), s[h])
              for h in H]
        v_new = [sol[c, h][:, :GDN_DV] - wq[h][:C] for h in H]
        o = [wq[h][C:] + _bdot(aqk[c, h], v_new[h]) for h in H]
        ds = [_bdot_tn(k[c, h] * col(PK_EGD, c, h), v_new[h]) for h in H]
        for h in H:
            s[h] = s[h] * pack[c * C:c * C + 1, PK_EGL + h:PK_EGL + h + 1] + ds[h]
            on = o[h] * lax.rsqrt(jnp.mean(o[h] * o[h], axis=-1, keepdims=True) + 1e-6) * nw
            o_ref[rows(c), h * GDN_DV:(h + 1) * GDN_DV] = on
    for h in H:
        s_ref[h] = s[h]


def _gdn(q, k, v, pack, gct3, norm_w_row):
    T = q.shape[1]
    C = GDN_CHUNK * GDN_STEP_CHUNKS
    hd_spec = pl.BlockSpec((GDN_HEADS, C, LANES), lambda i: (0, i, 0))
    return pl.pallas_call(
        _gdn_kernel,
        grid=(T // C,),
        in_specs=[hd_spec, hd_spec, hd_spec,
                  pl.BlockSpec((C, LANES), lambda i: (i, 0)),
                  pl.BlockSpec((GDN_STEP_CHUNKS, GDN_HEADS, GDN_CHUNK), lambda i: (i, 0, 0)),
                  _const_spec((1, GDN_DV))],
        out_specs=pl.BlockSpec((C, GDN_HEADS * GDN_DV), lambda i: (i, 0)),
        out_shape=jax.ShapeDtypeStruct((T, GDN_HEADS * GDN_DV), F32),
        scratch_shapes=[pltpu.VMEM((GDN_HEADS, GDN_DK, GDN_DV), F32)],
        compiler_params=pltpu.CompilerParams(dimension_semantics=("arbitrary",)),
        name="gdn",
    )(q, k, v, pack, gct3, norm_w_row)


MB_QSPLIT = 2
MB_TRIP = 4
MB_MAX_BLOCKS = MB_HD


def _moba_kernel(q_ref, k_ref, vt_ref, km_ref, o_ref, qx_ref, sa_ref, sb_ref, acc_ref, m_ref):
    qi = pl.program_id(1)
    nb = k_ref.shape[0]
    bs = MB_BLOCK
    nq = MB_REP * bs
    nbp = MB_MAX_BLOCKS
    c = (MB_HD ** -0.5) * math.log2(math.e)

    qf = q_ref[...].reshape(nq, MB_HD)

    gate = lax.dot_general(km_ref[...], qf, (((1,), (1,)), ((), ())),
                           precision=lax.Precision.HIGHEST, preferred_element_type=F32)
    blk = lax.broadcasted_iota(jnp.int32, (nbp, nq), 0).astype(F32)
    qif = qi.astype(F32)
    gate = jnp.where(blk < qif, gate, -jnp.inf)
    sel = jnp.where(blk == qif, 1.0, 0.0)
    for _ in range(MB_TOPK):
        top = jnp.max(gate, axis=0, keepdims=True)
        first = jnp.min(jnp.where(gate == top, blk, float(nbp)), axis=0, keepdims=True)
        hit = (blk == first) & (top > -jnp.inf)
        sel = jnp.where(hit, 1.0, sel)
        gate = jnp.where(hit, -jnp.inf, gate)
    bias_t = jnp.where(sel > 0.0, 0.0, MASK_NEG)
    qc = qf * c
    qx_ref[0] = jnp.concatenate([qc, bias_t.T], axis=1).astype(BF16)
    qx_ref[1] = jnp.concatenate([qc, jnp.full((nq, nbp), MASK_NEG, F32)], axis=1).astype(BF16)

    w = nq // MB_QSPLIT
    groups = [slice(h * w, (h + 1) * w) for h in range(MB_QSPLIT)]

    def scores_into(dst_ref, t):
        kx = k_ref[jnp.minimum(t, nb - 1)]
        qset = jnp.where(t < qi, 0, 1)
        for lanes in groups:
            dst_ref[:, lanes] = lax.dot_general(kx, qx_ref[qset, lanes, :], (((1,), (1,)), ((), ())),
                                                preferred_element_type=F32)

    st = lax.dot_general(k_ref[qi], qx_ref[0], (((1,), (1,)), ((), ())), preferred_element_type=F32)
    kpos = lax.broadcasted_iota(jnp.int32, (bs, nq), 0)
    qpos = lax.broadcasted_iota(jnp.int32, (bs, nq), 1) % bs
    st = jnp.where(kpos <= qpos, st, MASK_NEG)
    m0 = jnp.max(st, axis=0, keepdims=True)
    p = jnp.exp2(st - m0)
    m_ref[...] = m0
    acc_ref[...] = jnp.dot(vt_ref[qi], p.astype(BF16), preferred_element_type=F32)

    def absorb(src_ref, t):
        vt = vt_ref[jnp.minimum(t, nb - 1)]
        for lanes in groups:
            st = src_ref[:, lanes]
            m_old = m_ref[:, lanes]
            m_new = jnp.maximum(m_old, jnp.max(st, axis=0, keepdims=True))
            alpha = jnp.exp2(m_old - m_new)
            p = jnp.exp2(st - m_new)
            m_ref[:, lanes] = m_new
            acc_ref[:, lanes] = alpha * acc_ref[:, lanes] + jnp.dot(vt, p.astype(BF16),
                                                                    preferred_element_type=F32)

    scores_into(sa_ref, 0)

    def body(i, carry):
        t = MB_TRIP * i
        for u in range(0, MB_TRIP, 2):
            scores_into(sb_ref, t + u + 1)
            absorb(sa_ref, t + u)
            scores_into(sa_ref, t + u + 2)
            absorb(sb_ref, t + u + 1)
        return carry

    lax.fori_loop(0, (qi + MB_TRIP - 1) // MB_TRIP, body, 0)

    out_t = acc_ref[0:MB_HD, :] / acc_ref[MB_HD:MB_HD + 1, :]
    for r in range(MB_REP):
        o_ref[:, r * MB_HD:(r + 1) * MB_HD] = out_t[:, r * bs:(r + 1) * bs].T


def _moba(mq, mk, mvt, kmean):
    T = mq.shape[1]
    bs = MB_BLOCK
    nb = T // bs
    nq = MB_REP * bs
    assert nb <= MB_MAX_BLOCKS
    kmean = jnp.pad(kmean, ((0, 0), (0, MB_MAX_BLOCKS - nb), (0, 0)))
    return pl.pallas_call(
        _moba_kernel,
        grid=(MB_KV_HEADS, nb),
        in_specs=[
            pl.BlockSpec((MB_REP, bs, MB_HD), lambda g, i: (g, i, 0)),
            pl.BlockSpec((nb, None, bs, 2 * MB_HD), lambda g, i: (0, g, 0, 0)),
            pl.BlockSpec((nb, None, MB_VROWS, bs), lambda g, i: (0, g, 0, 0)),
            pl.BlockSpec((None, MB_MAX_BLOCKS, MB_HD), lambda g, i: (g, 0, 0)),
        ],
        out_specs=pl.BlockSpec((bs, MB_REP * MB_HD), lambda g, i: (i, g)),
        out_shape=jax.ShapeDtypeStruct((T, MB_HEADS * MB_HD), F32),
        scratch_shapes=[pltpu.VMEM((2, nq, 2 * MB_HD), BF16),
                        pltpu.VMEM((bs, nq), F32), pltpu.VMEM((bs, nq), F32),
                        pltpu.VMEM((MB_VROWS, nq), F32),
                        pltpu.VMEM((1, nq), F32)],
        compiler_params=pltpu.CompilerParams(dimension_semantics=("arbitrary", "arbitrary"),
                                             vmem_limit_bytes=VMEM_LIMIT),
        name="moba",
    )(mq, mk, mvt, kmean)


def _post_kernel(x_ref, go_ref, ma_ref, wg_ref, wgp_ref, wmp_ref, wo_ref, g1_ref, b1_ref,
                 rwt_ref, rb_ref, sg_ref, su_ref, sd_ref,
                 h_ref, res_ref, eidx_ref, wts_ref):
    tm = x_ref.shape[0]
    x = x_ref[...]
    zg = jnp.dot(x.astype(BF16), wg_ref[...], preferred_element_type=F32)
    gate = zg[:, :D_MODEL]
    o = go_ref[...] * (gate * _sigmoid(gate))
    y_gdn = jnp.dot(o.astype(BF16), wgp_ref[...], preferred_element_type=F32)
    y_mb = jnp.dot(ma_ref[...].astype(BF16), wmp_ref[...], preferred_element_type=F32)
    m = _sigmoid(zg[:, D_MODEL:2 * D_MODEL]) * y_gdn + _sigmoid(zg[:, 2 * D_MODEL:]) * y_mb
    mix = jnp.dot(m.astype(BF16), wo_ref[...], preferred_element_type=F32)
    h = _layer_norm(DEEPNORM_ALPHA * x + mix, g1_ref[...], b1_ref[...])
    _store_token_tiles(h_ref, h)
    hb = h.astype(BF16)

    hs = jnp.dot(hb, sg_ref[...], preferred_element_type=F32)
    hs = hs * _sigmoid(hs) * jnp.dot(hb, su_ref[...], preferred_element_type=F32)
    res_ref[...] = DEEPNORM_ALPHA * h + jnp.dot(hs.astype(BF16), sd_ref[...], preferred_element_type=F32)

    logits = lax.dot_general(rwt_ref[...], h, (((1,), (1,)), ((), ())),
                             precision=lax.Precision.HIGHEST, preferred_element_type=F32)
    scores = _sigmoid(logits)
    choice = scores + rb_ref[...]
    neg = -jnp.inf
    gi = lax.broadcasted_iota(jnp.int32, (GROUP_SIZE, tm), 0).astype(F32)
    gscore = []
    for g in range(N_GROUPS):
        cg = choice[g * GROUP_SIZE:(g + 1) * GROUP_SIZE, :]
        m1 = jnp.max(cg, axis=0, keepdims=True)
        i1 = jnp.min(jnp.where(cg == m1, gi, float(GROUP_SIZE)), axis=0, keepdims=True)
        m2 = jnp.max(jnp.where(gi == i1, neg, cg), axis=0, keepdims=True)
        gscore.append(m1 + m2)
    gs = jnp.concatenate(gscore, axis=0)
    gidx = lax.broadcasted_iota(jnp.int32, (N_GROUPS, tm), 0).astype(F32)
    gsel = jnp.zeros((N_GROUPS, tm), F32)
    for _ in range(TOPK_GROUPS):
        top = jnp.max(gs, axis=0, keepdims=True)
        first = jnp.min(jnp.where(gs == top, gidx, float(N_GROUPS)), axis=0, keepdims=True)
        hit = gidx == first
        gsel = jnp.where(hit, 1.0, gsel)
        gs = jnp.where(hit, neg, gs)
    masked = jnp.concatenate(
        [jnp.where(gsel[g:g + 1, :] > 0.0, choice[g * GROUP_SIZE:(g + 1) * GROUP_SIZE, :], neg)
         for g in range(N_GROUPS)], axis=0)
    ei = lax.broadcasted_iota(jnp.int32, (N_EXPERTS, tm), 0).astype(F32)
    idx_rows, w_rows = [], []
    for _ in range(TOP_K):
        top = jnp.max(masked, axis=0, keepdims=True)
        first = jnp.min(jnp.where(masked == top, ei, float(N_EXPERTS)), axis=0, keepdims=True)
        hit = ei == first
        idx_rows.append(first)
        w_rows.append(jnp.sum(jnp.where(hit, scores, 0.0), axis=0, keepdims=True))
        masked = jnp.where(hit, neg, masked)
    w = jnp.concatenate(w_rows, axis=0)
    w = w / (jnp.sum(w, axis=0, keepdims=True) + 1e-20) * ROUTED_SCALE
    eidx_ref[...] = jnp.concatenate(idx_rows, axis=0).astype(jnp.int32)
    wts_ref[...] = w


def _post(x2, gdn_o, moba_a, w_gates, w_gdn_proj, w_moba_proj, w_out, ln_g, ln_b,
          router_wt, router_b_col, sh_g, sh_u, sh_d):
    T = x2.shape[0]
    tm = TOK_TILE
    row_spec = pl.BlockSpec((tm, D_MODEL), lambda i: (i, 0))
    consts = [w_gates, w_gdn_proj, w_moba_proj, w_out, ln_g, ln_b, router_wt, router_b_col,
              sh_g, sh_u, sh_d]
    return pl.pallas_call(
        _post_kernel,
        grid=(T // tm,),
        in_specs=[row_spec, row_spec, row_spec] + [_const_spec(c.shape) for c in consts],
        out_specs=[pl.BlockSpec((tm * TOK_ROWS, LANES), lambda i: (i, 0)), row_spec,
                   pl.BlockSpec((TOP_K, tm), lambda i: (0, i)),
                   pl.BlockSpec((TOP_K, tm), lambda i: (0, i))],
        out_shape=[jax.ShapeDtypeStruct((T * TOK_ROWS, LANES), F32),
                   jax.ShapeDtypeStruct((T, D_MODEL), F32),
                   jax.ShapeDtypeStruct((TOP_K, T), jnp.int32),
                   jax.ShapeDtypeStruct((TOP_K, T), F32)],
        compiler_params=pltpu.CompilerParams(dimension_semantics=("arbitrary",),
                                             vmem_limit_bytes=VMEM_LIMIT),
        name="post",
    )(x2, gdn_o, moba_a, *consts)


ROW_UNROLL = 8


def _experts_kernel(iblk_ref, iexp_ref, ilo_ref, ihi_ref, src_ref, srcn_ref, dst_ref,
                    h_ref, wg_ref, wu_ref, wd_ref, out_ref,
                    xbuf, ybuf, wgb, wub, wdb, cast_exp, gsem, ssem, *, n_blocks):
    p = pl.program_id(0)
    n_items = pl.num_programs(0)
    w = iblk_ref[p]
    slot = w % 2
    first = (p == 0) | (w != iblk_ref[jnp.maximum(p - 1, 0)])
    last = (p == n_items - 1) | (w != iblk_ref[jnp.minimum(p + 1, n_items - 1)])
    lo, hi = ilo_ref[p], ihi_ref[p]
    block_rows = MOE_BLOCK * TOK_ROWS

    def tile_at(ref, row0):
        return ref.at[pl.ds(pl.multiple_of(row0, TOK_ROWS), TOK_ROWS)]

    def for_block_rows(fn):
        def group(g, c):
            for u in range(ROW_UNROLL):
                fn(g * ROW_UNROLL + u, u % 2)
            return c
        lax.fori_loop(0, MOE_BLOCK // ROW_UNROLL, group, 0)

    def start_gather(rows_ref, s):
        for_block_rows(lambda i, prio: pltpu.make_async_copy(
            tile_at(h_ref, rows_ref[0, 0, i]), tile_at(xbuf.at[s], i * TOK_ROWS), gsem.at[s]).start(priority=prio))

    def start_scatter(s):
        for_block_rows(lambda i, prio: pltpu.make_async_copy(
            tile_at(ybuf.at[s], i * TOK_ROWS), tile_at(out_ref, dst_ref[0, 0, i]), ssem.at[s]).start(priority=prio))

    def wait_gather(s):
        pltpu.make_async_copy(h_ref.at[pl.ds(0, block_rows)], xbuf.at[s], gsem.at[s]).wait()

    def wait_scatter(s):
        pltpu.make_async_copy(ybuf.at[s], out_ref.at[pl.ds(0, block_rows)], ssem.at[s]).wait()

    @pl.when(p == 0)
    def _():
        ybuf[...] = jnp.zeros_like(ybuf)
        cast_exp[0] = -1
        start_gather(src_ref, 0)

    @pl.when(first)
    def _():
        @pl.when(w + 1 < n_blocks)
        def _():
            start_gather(srcn_ref, 1 - slot)

        @pl.when(w >= 2)
        def _():
            wait_scatter(slot)
        wait_gather(slot)

    @pl.when(hi > lo)
    def _():
        @pl.when(iexp_ref[p] != cast_exp[0])
        def _():
            wgb[...] = wg_ref[...].astype(BF16)
            wub[...] = wu_ref[...].astype(BF16)
            wdb[...] = wd_ref[...].astype(BF16)
            cast_exp[0] = iexp_ref[p]

        xb = _load_token_tiles(xbuf.at[slot], MOE_BLOCK).astype(BF16)
        hg = jnp.dot(xb, wgb[...], preferred_element_type=F32)
        hu = jnp.dot(xb, wub[...], preferred_element_type=F32)
        hb = (hg * _sigmoid(hg) * hu).astype(BF16)
        y = jnp.dot(hb, wdb[...], preferred_element_type=F32)
        row = lax.broadcasted_iota(jnp.int32, (MOE_BLOCK, 1), 0)
        mine = (row >= lo) & (row < hi)
        _store_token_tiles(ybuf.at[slot], jnp.where(mine, y, _load_token_tiles(ybuf.at[slot], MOE_BLOCK)))

    @pl.when(last)
    def _():
        start_scatter(slot)

    @pl.when(p == n_items - 1)
    def _():
        @pl.when(w >= 1)
        def _():
            wait_scatter(1 - slot)
        wait_scatter(slot)


def _experts(h, src_rows, dst_rows, item_blk, item_exp, item_lo, item_hi, wg, wu, wd):
    T = h.shape[0] // TOK_ROWS
    n_blocks = src_rows.shape[0]
    n_items = item_blk.shape[0]
    rows = MOE_BLOCK
    cur_spec = pl.BlockSpec((1, 1, rows), lambda p, ib, ie, il, ih: (ib[p], 0, 0), memory_space=pltpu.SMEM)
    nxt_spec = pl.BlockSpec((1, 1, rows), lambda p, ib, ie, il, ih: (jnp.minimum(ib[p] + 1, n_blocks - 1), 0, 0),
                            memory_space=pltpu.SMEM)

    def w_spec(shape):
        return pl.BlockSpec((None,) + shape, lambda p, ib, ie, il, ih: (ie[p], 0, 0))

    grid_spec = pltpu.PrefetchScalarGridSpec(
        num_scalar_prefetch=4,
        grid=(n_items,),
        in_specs=[cur_spec, nxt_spec, cur_spec, pl.BlockSpec(memory_space=pl.ANY),
                  w_spec((D_MODEL, EXPERT_DIM)), w_spec((D_MODEL, EXPERT_DIM)), w_spec((EXPERT_DIM, D_MODEL))],
        out_specs=pl.BlockSpec(memory_space=pl.ANY),
        scratch_shapes=[pltpu.VMEM((2, rows * TOK_ROWS, LANES), F32),
                        pltpu.VMEM((2, rows * TOK_ROWS, LANES), F32),
                        pltpu.VMEM((D_MODEL, EXPERT_DIM), BF16), pltpu.VMEM((D_MODEL, EXPERT_DIM), BF16),
                        pltpu.VMEM((EXPERT_DIM, D_MODEL), BF16),
                        pltpu.SMEM((1,), jnp.int32),
                        pltpu.SemaphoreType.DMA((2,)), pltpu.SemaphoreType.DMA((2,))],
    )
    return pl.pallas_call(
        functools.partial(_experts_kernel, n_blocks=n_blocks),
        grid_spec=grid_spec,
        out_shape=jax.ShapeDtypeStruct((TOP_K * T * TOK_ROWS, LANES), F32),
        compiler_params=pltpu.CompilerParams(dimension_semantics=("arbitrary",),
                                             vmem_limit_bytes=VMEM_LIMIT,
                                             disable_bounds_checks=True),
        name="experts",
    )(item_blk, item_exp, item_lo, item_hi, src_rows, src_rows, dst_rows, h, wg, wu, wd)


def _combine_kernel(res_ref, y_ref, w_ref, g_ref, b_ref, o_ref):
    tm = res_ref.shape[0]
    acc = res_ref[...]
    w = w_ref[...]
    for k in range(TOP_K):
        acc = acc + w[:, k:k + 1] * _load_token_tiles(y_ref.at[k], tm)
    o_ref[...] = _layer_norm(acc, g_ref[...], b_ref[...])


def _combine(res, y3, wts, ln_g, ln_b):
    T = res.shape[0]
    tm = 128
    return pl.pallas_call(
        _combine_kernel,
        grid=(T // tm,),
        in_specs=[pl.BlockSpec((tm, D_MODEL), lambda i: (i, 0)),
                  pl.BlockSpec((TOP_K, tm * TOK_ROWS, LANES), lambda i: (0, i, 0)),
                  pl.BlockSpec((tm, TOP_K), lambda i: (i, 0)),
                  _const_spec((1, D_MODEL)), _const_spec((1, D_MODEL))],
        out_specs=pl.BlockSpec((tm, D_MODEL), lambda i: (i, 0)),
        out_shape=jax.ShapeDtypeStruct((T, D_MODEL), F32),
        compiler_params=pltpu.CompilerParams(dimension_semantics=("arbitrary",)),
        name="combine",
    )(res, y3, wts, ln_g, ln_b)


def _dispatch_plan(eidx_t, n_tok):
    nk = n_tok * TOP_K
    n_blocks = nk // MOE_BLOCK
    e_flat = eidx_t.T.reshape(nk)
    _, order = lax.sort((e_flat, jnp.arange(nk, dtype=jnp.int32)), num_keys=1)
    experts = jnp.arange(N_EXPERTS, dtype=jnp.int32)
    counts = jnp.sum((e_flat[None, :] == experts[:, None]).astype(jnp.int32), axis=1)
    start = jnp.cumsum(counts) - counts
    pos = jnp.sort(jnp.concatenate([jnp.arange(n_blocks, dtype=jnp.int32) * MOE_BLOCK, start]))
    nxt = jnp.concatenate([pos[1:], jnp.full((1,), nk, jnp.int32)])
    item_blk = jnp.minimum(pos // MOE_BLOCK, n_blocks - 1)
    item_exp = jnp.sum((start[None, :] <= pos[:, None]).astype(jnp.int32), axis=1) - 1
    item_lo = pos - item_blk * MOE_BLOCK
    item_hi = jnp.minimum(nxt, (item_blk + 1) * MOE_BLOCK) - item_blk * MOE_BLOCK
    tok, slot_k = order // TOP_K, order % TOP_K
    shape3 = (n_blocks, 1, MOE_BLOCK)
    src_rows = (tok * TOK_ROWS).reshape(shape3)
    dst_rows = ((slot_k * n_tok + tok) * TOK_ROWS).reshape(shape3)
    return src_rows, dst_rows, item_blk, item_exp, item_lo, item_hi


def kernel(x, w_in, conv_w, gdn_a_log, gdn_dt_bias, gdn_norm_w, w_gdn_proj, w_moba_proj, w_out,
           ln1_g, ln1_b, router_w, router_bias, exp_w_gate, exp_w_up, exp_w_down,
           sh_w_gate, sh_w_up, sh_w_down, ln2_g, ln2_b):
    B, T, D = x.shape
    assert B == 1 and D == D_MODEL and T % MB_BLOCK == 0
    x2 = x.reshape(T, D)

    o_gate = CONV_CH
    o_b = o_gate + GDN_HEADS * GDN_DV
    o_a = o_b + GDN_HEADS
    o_mq = o_a + GDN_HEADS
    o_gg = o_mq + (MB_HEADS + 2 * MB_KV_HEADS) * MB_HD
    w_qkv = w_in[:, :CONV_CH].astype(BF16)
    w_pack = jnp.concatenate([w_in[:, o_b:o_a]] + [w_in[:, o_a:o_mq]] * 4
                             + [jnp.zeros((D, LANES - 5 * GDN_HEADS), F32)], axis=1).astype(BF16)
    w_mb = w_in[:, o_mq:o_gg].astype(BF16)
    w_gates = jnp.concatenate([w_in[:, o_gate:o_b], w_in[:, o_gg:]], axis=1).astype(BF16)

    def lane_row(v):
        return jnp.zeros((1, LANES), F32).at[0, PK_GC:PK_GC + 4 * GDN_HEADS].set(jnp.tile(v.astype(F32), 4))

    q, k, v, pack, gct = _qkv_proj(x2, w_qkv, w_pack, conv_w.astype(F32),
                                   lane_row(gdn_a_log), lane_row(gdn_dt_bias))
    gct3 = gct.reshape(GDN_HEADS, T // GDN_CHUNK, GDN_CHUNK).transpose(1, 0, 2)
    gdn_o = _gdn(q, k, v, pack, gct3, gdn_norm_w.astype(F32).reshape(1, GDN_DV))

    half = ROT_DIM // 2
    inv = ROPE_THETA ** (-jnp.arange(half, dtype=F32) / half)
    ang = jnp.arange(T).astype(F32)[:, None] * inv[None, :]
    ones = jnp.ones((T, MB_HD - ROT_DIM), F32)
    cos_t = jnp.concatenate([jnp.cos(ang), jnp.cos(ang), ones], axis=1)
    sin_t = jnp.concatenate([-jnp.sin(ang), jnp.sin(ang), 0.0 * ones], axis=1)
    mq, mk, mvt, kmean = _moba_proj(x2, w_mb, cos_t, sin_t)
    moba_a = _moba(mq, mk, mvt, kmean.transpose(1, 0, 2))

    h, res, eidx_t, wts_t = _post(
        x2, gdn_o, moba_a, w_gates, w_gdn_proj.astype(BF16), w_moba_proj.astype(BF16),
        w_out.astype(BF16), ln1_g.reshape(1, D), ln1_b.reshape(1, D),
        router_w.T, router_bias.reshape(N_EXPERTS, 1),
        sh_w_gate.astype(BF16), sh_w_up.astype(BF16), sh_w_down.astype(BF16))

    plan = _dispatch_plan(eidx_t, T)
    y = _experts(h, *plan, exp_w_gate, exp_w_up, exp_w_down)
    y3 = y.reshape(TOP_K, T * TOK_ROWS, LANES)
    out = _combine(res, y3, wts_t.T, ln2_g.reshape(1, D), ln2_b.reshape(1, D))
    return out.reshape(B, T, D)
```

```python
import functools
import math

import jax
import jax.numpy as jnp
from jax import lax
from jax.experimental import pallas as pl
from jax.experimental.pallas import tpu as pltpu

F32 = jnp.float32
BF16 = jnp.bfloat16

D_MODEL = 1024
DEPTH = 1
GDN_HEADS = 8
GDN_DK = 128
GDN_DV = 128
GDN_CONV = 4
GDN_CHUNK = 64
MB_HEADS = 8
MB_KV_HEADS = 2
MB_REP = MB_HEADS // MB_KV_HEADS
MB_HD = 128
MB_BLOCK = 256
MB_TOPK = 3
ROT_DIM = MB_HD // 4
ROPE_THETA = 500000.0
N_EXPERTS = 256
TOP_K = 8
N_GROUPS = 8
GROUP_SIZE = N_EXPERTS // N_GROUPS
TOPK_GROUPS = 4
EXPERT_DIM = 256
SHARED_DIM = 256
ROUTED_SCALE = 2.5
MOE_BLOCK = 256
DEEPNORM_ALPHA = (2.0 * DEPTH) ** 0.25
LN_EPS = 1e-5
MASK_NEG = -1e30

QK_COLS = GDN_HEADS * GDN_DK
CONV_CH = 2 * QK_COLS + GDN_HEADS * GDN_DV
LANES = 128
SUBLANES = 8
VMEM_LIMIT = 56 * 1024 * 1024

TOK_TILE = 256

PK_BETA, PK_GC, PK_EG, PK_EGD, PK_EGL = 0, 8, 16, 24, 32


def _sigmoid(x):
    return 1.0 / (1.0 + jnp.exp(-x))


def _softplus(x):
    return jnp.maximum(x, 0.0) + jnp.log(1.0 + jnp.exp(-jnp.abs(x)))


def _bdot(a, b):
    return jnp.dot(a.astype(BF16), b.astype(BF16), preferred_element_type=F32)


def _bdot_nt(a, b):
    return lax.dot_general(a.astype(BF16), b.astype(BF16), (((1,), (1,)), ((), ())),
                           preferred_element_type=F32)


def _bdot_tn(a, b):
    return lax.dot_general(a.astype(BF16), b.astype(BF16), (((0,), (0,)), ((), ())),
                           preferred_element_type=F32)


def _layer_norm(x, g, b):
    mu = jnp.mean(x, axis=-1, keepdims=True)
    xc = x - mu
    var = jnp.mean(xc * xc, axis=-1, keepdims=True)
    return xc * lax.rsqrt(var + LN_EPS) * g + b


TOK_ROWS = D_MODEL // LANES


def _store_token_tiles(ref, x):
    n = x.shape[0]
    for j in range(TOK_ROWS):
        ref[pl.ds(j, n, stride=TOK_ROWS), :] = x[:, j * LANES:(j + 1) * LANES]


def _load_token_tiles(ref, n):
    return jnp.concatenate([ref[pl.ds(j, n, stride=TOK_ROWS), :] for j in range(TOK_ROWS)], axis=1)


def _const_spec(shape):
    nd = len(shape)
    return pl.BlockSpec(shape, lambda *_: (0,) * nd, pipeline_mode=pl.Buffered(1))


def _qkv_kernel(x_ref, w_ref, wp_ref, cw_ref, alog_ref, dtb_ref,
                q_ref, k_ref, v_ref, pack_ref, gct_ref, z_ref):
    tm = x_ref.shape[0]
    halo = SUBLANES

    @pl.when(pl.program_id(0) == 0)
    def _():
        z_ref[0:halo, :] = jnp.zeros((halo, CONV_CH), F32)

    xb = x_ref[...].astype(BF16)
    z_ref[halo:halo + tm, :] = jnp.dot(xb, w_ref[...], preferred_element_type=F32)

    for c in range(CONV_CH // LANES):
        cols = slice(c * LANES, (c + 1) * LANES)
        acc = z_ref[halo:halo + tm, cols] * cw_ref[GDN_CONV - 1:GDN_CONV, cols]
        for s in range(1, GDN_CONV):
            acc = acc + z_ref[halo - s:halo - s + tm, cols] * cw_ref[GDN_CONV - 1 - s:GDN_CONV - s, cols]
        y = acc * _sigmoid(acc)
        h = c % GDN_HEADS
        if c < GDN_HEADS:
            y = y * lax.rsqrt(jnp.sum(y * y, axis=-1, keepdims=True) + 1e-6) * (GDN_DK ** -0.5)
            q_ref[h] = y
        elif c < 2 * GDN_HEADS:
            y = y * lax.rsqrt(jnp.sum(y * y, axis=-1, keepdims=True) + 1e-6)
            k_ref[h] = y
        else:
            v_ref[h] = y

    z_ref[0:halo, :] = z_ref[tm:tm + halo, :]

    zp = jnp.dot(xb, wp_ref[...], preferred_element_type=F32)
    beta = _sigmoid(zp)
    g = -jnp.exp(alog_ref[...]) * _softplus(zp + dtb_ref[...])
    row = lax.broadcasted_iota(jnp.int32, (tm, LANES), 0) % GDN_CHUNK
    gc = g
    step = 1
    while step < GDN_CHUNK:
        gc = gc + jnp.where(row >= step, pltpu.roll(gc, step, axis=0), 0.0)
        step *= 2
    gl = gc.reshape(tm // GDN_CHUNK, GDN_CHUNK, LANES)[:, GDN_CHUNK - 1:GDN_CHUNK, :]
    gl = jnp.broadcast_to(gl, (tm // GDN_CHUNK, GDN_CHUNK, LANES)).reshape(tm, LANES)
    lane = lax.broadcasted_iota(jnp.int32, (tm, LANES), 1)
    pack = jnp.where(lane < PK_GC, beta,
           jnp.where(lane < PK_EG, gc,
           jnp.where(lane < PK_EGD, jnp.exp(gc),
           jnp.where(lane < PK_EGL, jnp.exp(gl - gc), jnp.exp(gl)))))
    pack_ref[...] = pack
    gct_ref[...] = pack.T[PK_GC:PK_GC + GDN_HEADS, :]


def _qkv_proj(x2, w_qkv, w_pack, conv_w, alog_row, dtb_row):
    T = x2.shape[0]
    tm = TOK_TILE
    hd_spec = pl.BlockSpec((GDN_HEADS, tm, LANES), lambda i: (0, i, 0))
    hd_shape = jax.ShapeDtypeStruct((GDN_HEADS, T, LANES), F32)
    return pl.pallas_call(
        _qkv_kernel,
        grid=(T // tm,),
        in_specs=[
            pl.BlockSpec((tm, D_MODEL), lambda i: (i, 0)),
            _const_spec((D_MODEL, CONV_CH)),
            _const_spec((D_MODEL, LANES)),
            _const_spec((GDN_CONV, CONV_CH)),
            _const_spec((1, LANES)),
            _const_spec((1, LANES)),
        ],
        out_specs=[hd_spec, hd_spec, hd_spec,
                   pl.BlockSpec((tm, LANES), lambda i: (i, 0)),
                   pl.BlockSpec((GDN_HEADS, tm), lambda i: (0, i))],
        out_shape=[hd_shape, hd_shape, hd_shape,
                   jax.ShapeDtypeStruct((T, LANES), F32),
                   jax.ShapeDtypeStruct((GDN_HEADS, T), F32)],
        scratch_shapes=[pltpu.VMEM((tm + SUBLANES, CONV_CH), F32)],
        compiler_params=pltpu.CompilerParams(dimension_semantics=("arbitrary",),
                                             vmem_limit_bytes=VMEM_LIMIT),
        name="qkv_proj",
    )(x2, w_qkv, w_pack, conv_w, alog_row, dtb_row)


MB_LROWS = 16
MB_VROWS = MB_HD + MB_LROWS


def _rope(xh, cos_t, sin_t, lane):
    half = ROT_DIM // 2
    swapped = jnp.where(lane < half, pltpu.roll(xh, LANES - half, axis=1), pltpu.roll(xh, half, axis=1))
    return xh * cos_t + swapped * sin_t


def _moba_proj_kernel(x_ref, w_ref, cos_ref, sin_ref, q_ref, k_ref, vt_ref, km_ref):
    tm = x_ref.shape[0]
    z = jnp.dot(x_ref[...].astype(BF16), w_ref[...], preferred_element_type=F32)
    cos_t = cos_ref[...]
    sin_t = sin_ref[...]
    lane = lax.broadcasted_iota(jnp.int32, (tm, LANES), 1)
    for h in range(MB_HEADS):
        q_ref[h] = _rope(z[:, h * MB_HD:(h + 1) * MB_HD], cos_t, sin_t, lane)
    koff = MB_HEADS * MB_HD
    voff = koff + MB_KV_HEADS * MB_HD
    blk_onehot = jnp.where(lane == pl.program_id(0), 1.0, 0.0).astype(BF16)
    for g in range(MB_KV_HEADS):
        kr = _rope(z[:, koff + g * MB_HD:koff + (g + 1) * MB_HD], cos_t, sin_t, lane)
        k_ref[0, g] = jnp.concatenate([kr.astype(BF16), blk_onehot], axis=1)
        km_ref[0, g:g + 1, :] = jnp.mean(kr, axis=0, keepdims=True)
        vt_ref[0, g] = jnp.concatenate([z[:, voff + g * MB_HD:voff + (g + 1) * MB_HD].T,
                                        jnp.ones((MB_LROWS, tm), F32)], axis=0).astype(BF16)


def _moba_proj(x2, w_mb, cos_t, sin_t):
    T = x2.shape[0]
    tm = MB_BLOCK
    nb = T // tm
    return pl.pallas_call(
        _moba_proj_kernel,
        grid=(nb,),
        in_specs=[
            pl.BlockSpec((tm, D_MODEL), lambda i: (i, 0)),
            _const_spec(w_mb.shape),
            pl.BlockSpec((tm, LANES), lambda i: (i, 0)),
            pl.BlockSpec((tm, LANES), lambda i: (i, 0)),
        ],
        out_specs=[
            pl.BlockSpec((MB_HEADS, tm, MB_HD), lambda i: (0, i, 0)),
            pl.BlockSpec((1, MB_KV_HEADS, tm, 2 * MB_HD), lambda i: (i, 0, 0, 0)),
            pl.BlockSpec((1, MB_KV_HEADS, MB_VROWS, tm), lambda i: (i, 0, 0, 0)),
            pl.BlockSpec((1, MB_KV_HEADS, MB_HD), lambda i: (i, 0, 0)),
        ],
        out_shape=[
            jax.ShapeDtypeStruct((MB_HEADS, T, MB_HD), F32),
            jax.ShapeDtypeStruct((nb, MB_KV_HEADS, tm, 2 * MB_HD), BF16),
            jax.ShapeDtypeStruct((nb, MB_KV_HEADS, MB_VROWS, tm), BF16),
            jax.ShapeDtypeStruct((nb, MB_KV_HEADS, MB_HD), F32),
        ],
        compiler_params=pltpu.CompilerParams(dimension_semantics=("arbitrary",),
                                             vmem_limit_bytes=VMEM_LIMIT),
        name="moba_proj",
    )(x2, w_mb, cos_t, sin_t)


def _unit_lower_inverse(a_list):
    c = a_list[0].shape[0]
    rr = lax.broadcasted_iota(jnp.int32, (c, c), 0)
    cc = lax.broadcasted_iota(jnp.int32, (c, c), 1)
    eye = jnp.where(rr == cc, 1.0, 0.0).astype(F32)
    inv = [eye - a for a in a_list]
    p = [_bdot(a, a) for a in a_list]
    n = 2
    while True:
        inv = [x + _bdot(x, y) for x, y in zip(inv, p)]
        n *= 2
        if n >= c:
            break
        p = [_bdot(y, y) for y in p]
    return inv


GDN_STEP_CHUNKS = 4


def _gdn_kernel(q_ref, k_ref, v_ref, pack_ref, gct_ref, nw_ref, o_ref, s_ref):
    C = GDN_CHUNK
    H = range(GDN_HEADS)
    P = [(c, h) for c in range(GDN_STEP_CHUNKS) for h in H]

    @pl.when(pl.program_id(0) == 0)
    def _():
        s_ref[...] = jnp.zeros_like(s_ref)

    rr = lax.broadcasted_iota(jnp.int32, (C, C), 0)
    cc = lax.broadcasted_iota(jnp.int32, (C, C), 1)
    tril = rr >= cc
    strict = rr > cc
    pack = pack_ref[...]
    nw = nw_ref[...]

    def rows(c):
        return slice(c * C, (c + 1) * C)

    def col(base, c, h):
        return pack[rows(c), base + h:base + h + 1]

    q = {(c, h): q_ref[h, rows(c), :] for c, h in P}
    k = {(c, h): k_ref[h, rows(c), :] for c, h in P}
    kb = {(c, h): k[c, h] * col(PK_BETA, c, h) for c, h in P}
    decay = {(c, h): jnp.where(tril, jnp.exp(jnp.where(tril, col(PK_GC, c, h) - gct_ref[c, h:h + 1, :], 0.0)), 0.0)
             for c, h in P}
    sc = {p: _bdot_nt(jnp.concatenate([kb[p], q[p]], axis=0), k[p]) for p in P}
    a = {p: jnp.where(strict, sc[p][:C] * decay[p], 0.0) for p in P}
    aqk = {p: sc[p][C:] * decay[p] for p in P}
    tinv = dict(zip(P, _unit_lower_inverse([a[p] for p in P])))
    rhs = {(c, h): jnp.concatenate([v_ref[h, rows(c), :] * col(PK_BETA, c, h), kb[c, h] * col(PK_EG, c, h)],
                                   axis=1) for c, h in P}
    sol = {p: _bdot(tinv[p], rhs[p]) for p in P}

    s = [s_ref[h] for h in H]
    for c in range(GDN_STEP_CHUNKS):
        wq = [_bdot(jnp.concatenate([sol[c, h][:, GDN_DV:], q[c, h] * col(PK_EG, c, h)], axis=0), s[h])
              for h in H]
        v_new = [sol[c, h][:, :GDN_DV] - wq[h][:C] for h in H]
        o = [wq[h][C:] + _bdot(aqk[c, h], v_new[h]) for h in H]
        ds = [_bdot_tn(k[c, h] * col(PK_EGD, c, h), v_new[h]) for h in H]
        for h in H:
            s[h] = s[h] * pack[c * C:c * C + 1, PK_EGL + h:PK_EGL + h + 1] + ds[h]
            on = o[h] * lax.rsqrt(jnp.mean(o[h] * o[h], axis=-1, keepdims=True) + 1e-6) * nw
            o_ref[rows(c), h * GDN_DV:(h + 1) * GDN_DV] = on
    for h in H:
        s_ref[h] = s[h]


def _gdn(q, k, v, pack, gct3, norm_w_row):
    T = q.shape[1]
    C = GDN_CHUNK * GDN_STEP_CHUNKS
    hd_spec = pl.BlockSpec((GDN_HEADS, C, LANES), lambda i: (0, i, 0))
    return pl.pallas_call(
        _gdn_kernel,
        grid=(T // C,),
        in_specs=[hd_spec, hd_spec, hd_spec,
                  pl.BlockSpec((C, LANES), lambda i: (i, 0)),
                  pl.BlockSpec((GDN_STEP_CHUNKS, GDN_HEADS, GDN_CHUNK), lambda i: (i, 0, 0)),
                  _const_spec((1, GDN_DV))],
        out_specs=pl.BlockSpec((C, GDN_HEADS * GDN_DV), lambda i: (i, 0)),
        out_shape=jax.ShapeDtypeStruct((T, GDN_HEADS * GDN_DV), F32),
        scratch_shapes=[pltpu.VMEM((GDN_HEADS, GDN_DK, GDN_DV), F32)],
        compiler_params=pltpu.CompilerParams(dimension_semantics=("arbitrary",)),
        name="gdn",
    )(q, k, v, pack, gct3, norm_w_row)


MB_QSPLIT = 2
MB_TRIP = 4
MB_MAX_BLOCKS = MB_HD


def _moba_kernel(q_ref, k_ref, vt_ref, km_ref, o_ref, qx_ref, sa_ref, sb_ref, acc_ref, m_ref):
    qi = pl.program_id(1)
    nb = k_ref.shape[0]
    bs = MB_BLOCK
    nq = MB_REP * bs
    nbp = MB_MAX_BLOCKS
    c = (MB_HD ** -0.5) * math.log2(math.e)

    qf = q_ref[...].reshape(nq, MB_HD)

    gate = lax.dot_general(km_ref[...], qf, (((1,), (1,)), ((), ())),
                           precision=lax.Precision.HIGHEST, preferred_element_type=F32)
    blk = lax.broadcasted_iota(jnp.int32, (nbp, nq), 0).astype(F32)
    qif = qi.astype(F32)
    gate = jnp.where(blk < qif, gate, -jnp.inf)
    sel = jnp.where(blk == qif, 1.0, 0.0)
    for _ in range(MB_TOPK):
        top = jnp.max(gate, axis=0, keepdims=True)
        first = jnp.min(jnp.where(gate == top, blk, float(nbp)), axis=0, keepdims=True)
        hit = (blk == first) & (top > -jnp.inf)
        sel = jnp.where(hit, 1.0, sel)
        gate = jnp.where(hit, -jnp.inf, gate)
    bias_t = jnp.where(sel > 0.0, 0.0, MASK_NEG)
    qc = qf * c
    qx_ref[0] = jnp.concatenate([qc, bias_t.T], axis=1).astype(BF16)
    qx_ref[1] = jnp.concatenate([qc, jnp.full((nq, nbp), MASK_NEG, F32)], axis=1).astype(BF16)

    w = nq // MB_QSPLIT
    groups = [slice(h * w, (h + 1) * w) for h in range(MB_QSPLIT)]

    def scores_into(dst_ref, t):
        kx = k_ref[jnp.minimum(t, nb - 1)]
        qset = jnp.where(t < qi, 0, 1)
        for lanes in groups:
            dst_ref[:, lanes] = lax.dot_general(kx, qx_ref[qset, lanes, :], (((1,), (1,)), ((), ())),
                                                preferred_element_type=F32)

    st = lax.dot_general(k_ref[qi], qx_ref[0], (((1,), (1,)), ((), ())), preferred_element_type=F32)
    kpos = lax.broadcasted_iota(jnp.int32, (bs, nq), 0)
    qpos = lax.broadcasted_iota(jnp.int32, (bs, nq), 1) % bs
    st = jnp.where(kpos <= qpos, st, MASK_NEG)
    m0 = jnp.max(st, axis=0, keepdims=True)
    p = jnp.exp2(st - m0)
    m_ref[...] = m0
    acc_ref[...] = jnp.dot(vt_ref[qi], p.astype(BF16), preferred_element_type=F32)

    def absorb(src_ref, t):
        vt = vt_ref[jnp.minimum(t, nb - 1)]
        for lanes in groups:
            st = src_ref[:, lanes]
            m_old = m_ref[:, lanes]
            m_new = jnp.maximum(m_old, jnp.max(st, axis=0, keepdims=True))
            alpha = jnp.exp2(m_old - m_new)
            p = jnp.exp2(st - m_new)
            m_ref[:, lanes] = m_new
            acc_ref[:, lanes] = alpha * acc_ref[:, lanes] + jnp.dot(vt, p.astype(BF16),
                                                                    preferred_element_type=F32)

    scores_into(sa_ref, 0)

    def body(i, carry):
        t = MB_TRIP * i
        for u in range(0, MB_TRIP, 2):
            scores_into(sb_ref, t + u + 1)
            absorb(sa_ref, t + u)
            scores_into(sa_ref, t + u + 2)
            absorb(sb_ref, t + u + 1)
        return carry

    lax.fori_loop(0, (qi + MB_TRIP - 1) // MB_TRIP, body, 0)

    out_t = acc_ref[0:MB_HD, :] / acc_ref[MB_HD:MB_HD + 1, :]
    for r in range(MB_REP):
        o_ref[:, r * MB_HD:(r + 1) * MB_HD] = out_t[:, r * bs:(r + 1) * bs].T


def _moba(mq, mk, mvt, kmean):
    T = mq.shape[1]
    bs = MB_BLOCK
    nb = T // bs
    nq = MB_REP * bs
    assert nb <= MB_MAX_BLOCKS
    kmean = jnp.pad(kmean, ((0, 0), (0, MB_MAX_BLOCKS - nb), (0, 0)))
    return pl.pallas_call(
        _moba_kernel,
        grid=(MB_KV_HEADS, nb),
        in_specs=[
            pl.BlockSpec((MB_REP, bs, MB_HD), lambda g, i: (g, i, 0)),
            pl.BlockSpec((nb, None, bs, 2 * MB_HD), lambda g, i: (0, g, 0, 0)),
            pl.BlockSpec((nb, None, MB_VROWS, bs), lambda g, i: (0, g, 0, 0)),
            pl.BlockSpec((None, MB_MAX_BLOCKS, MB_HD), lambda g, i: (g, 0, 0)),
        ],
        out_specs=pl.BlockSpec((bs, MB_REP * MB_HD), lambda g, i: (i, g)),
        out_shape=jax.ShapeDtypeStruct((T, MB_HEADS * MB_HD), F32),
        scratch_shapes=[pltpu.VMEM((2, nq, 2 * MB_HD), BF16),
                        pltpu.VMEM((bs, nq), F32), pltpu.VMEM((bs, nq), F32),
                        pltpu.VMEM((MB_VROWS, nq), F32),
                        pltpu.VMEM((1, nq), F32)],
        compiler_params=pltpu.CompilerParams(dimension_semantics=("arbitrary", "arbitrary"),
                                             vmem_limit_bytes=VMEM_LIMIT),
        name="moba",
    )(mq, mk, mvt, kmean)


def _post_kernel(x_ref, go_ref, ma_ref, wg_ref, wgp_ref, wmp_ref, wo_ref, g1_ref, b1_ref,
                 rwt_ref, rb_ref, sg_ref, su_ref, sd_ref,
                 h_ref, res_ref, eidx_ref, wts_ref):
    tm = x_ref.shape[0]
    x = x_ref[...]
    zg = jnp.dot(x.astype(BF16), wg_ref[...], preferred_element_type=F32)
    gate = zg[:, :D_MODEL]
    o = go_ref[...] * (gate * _sigmoid(gate))
    y_gdn = jnp.dot(o.astype(BF16), wgp_ref[...], preferred_element_type=F32)
    y_mb = jnp.dot(ma_ref[...].astype(BF16), wmp_ref[...], preferred_element_type=F32)
    m = _sigmoid(zg[:, D_MODEL:2 * D_MODEL]) * y_gdn + _sigmoid(zg[:, 2 * D_MODEL:]) * y_mb
    mix = jnp.dot(m.astype(BF16), wo_ref[...], preferred_element_type=F32)
    h = _layer_norm(DEEPNORM_ALPHA * x + mix, g1_ref[...], b1_ref[...])
    _store_token_tiles(h_ref, h)
    hb = h.astype(BF16)

    hs = jnp.dot(hb, sg_ref[...], preferred_element_type=F32)
    hs = hs * _sigmoid(hs) * jnp.dot(hb, su_ref[...], preferred_element_type=F32)
    res_ref[...] = DEEPNORM_ALPHA * h + jnp.dot(hs.astype(BF16), sd_ref[...], preferred_element_type=F32)

    logits = lax.dot_general(rwt_ref[...], h, (((1,), (1,)), ((), ())),
                             precision=lax.Precision.HIGHEST, preferred_element_type=F32)
    scores = _sigmoid(logits)
    choice = scores + rb_ref[...]
    neg = -jnp.inf
    gi = lax.broadcasted_iota(jnp.int32, (GROUP_SIZE, tm), 0).astype(F32)
    gscore = []
    for g in range(N_GROUPS):
        cg = choice[g * GROUP_SIZE:(g + 1) * GROUP_SIZE, :]
        m1 = jnp.max(cg, axis=0, keepdims=True)
        i1 = jnp.min(jnp.where(cg == m1, gi, float(GROUP_SIZE)), axis=0, keepdims=True)
        m2 = jnp.max(jnp.where(gi == i1, neg, cg), axis=0, keepdims=True)
        gscore.append(m1 + m2)
    gs = jnp.concatenate(gscore, axis=0)
    gidx = lax.broadcasted_iota(jnp.int32, (N_GROUPS, tm), 0).astype(F32)
    gsel = jnp.zeros((N_GROUPS, tm), F32)
    for _ in range(TOPK_GROUPS):
        top = jnp.max(gs, axis=0, keepdims=True)
        first = jnp.min(jnp.where(gs == top, gidx, float(N_GROUPS)), axis=0, keepdims=True)
        hit = gidx == first
        gsel = jnp.where(hit, 1.0, gsel)
        gs = jnp.where(hit, neg, gs)
    masked = jnp.concatenate(
        [jnp.where(gsel[g:g + 1, :] > 0.0, choice[g * GROUP_SIZE:(g + 1) * GROUP_SIZE, :], neg)
         for g in range(N_GROUPS)], axis=0)
    ei = lax.broadcasted_iota(jnp.int32, (N_EXPERTS, tm), 0).astype(F32)
    idx_rows, w_rows = [], []
    for _ in range(TOP_K):
        top = jnp.max(masked, axis=0, keepdims=True)
        first = jnp.min(jnp.where(masked == top, ei, float(N_EXPERTS)), axis=0, keepdims=True)
        hit = ei == first
        idx_rows.append(first)
        w_rows.append(jnp.sum(jnp.where(hit, scores, 0.0), axis=0, keepdims=True))
        masked = jnp.where(hit, neg, masked)
    w = jnp.concatenate(w_rows, axis=0)
    w = w / (jnp.sum(w, axis=0, keepdims=True) + 1e-20) * ROUTED_SCALE
    eidx_ref[...] = jnp.concatenate(idx_rows, axis=0).astype(jnp.int32)
    wts_ref[...] = w


def _post(x2, gdn_o, moba_a, w_gates, w_gdn_proj, w_moba_proj, w_out, ln_g, ln_b,
          router_wt, router_b_col, sh_g, sh_u, sh_d):
    T = x2.shape[0]
    tm = TOK_TILE
    row_spec = pl.BlockSpec((tm, D_MODEL), lambda i: (i, 0))
    consts = [w_gates, w_gdn_proj, w_moba_proj, w_out, ln_g, ln_b, router_wt, router_b_col,
              sh_g, sh_u, sh_d]
    return pl.pallas_call(
        _post_kernel,
        grid=(T // tm,),
        in_specs=[row_spec, row_spec, row_spec] + [_const_spec(c.shape) for c in consts],
        out_specs=[pl.BlockSpec((tm * TOK_ROWS, LANES), lambda i: (i, 0)), row_spec,
                   pl.BlockSpec((TOP_K, tm), lambda i: (0, i)),
                   pl.BlockSpec((TOP_K, tm), lambda i: (0, i))],
        out_shape=[jax.ShapeDtypeStruct((T * TOK_ROWS, LANES), F32),
                   jax.ShapeDtypeStruct((T, D_MODEL), F32),
                   jax.ShapeDtypeStruct((TOP_K, T), jnp.int32),
                   jax.ShapeDtypeStruct((TOP_K, T), F32)],
        compiler_params=pltpu.CompilerParams(dimension_semantics=("arbitrary",),
                                             vmem_limit_bytes=VMEM_LIMIT),
        name="post",
    )(x2, gdn_o, moba_a, *consts)


ROW_UNROLL = 8
MOE_PARTS = 2


def _experts_kernel(iblk_ref, iexp_ref, ilo_ref, ihi_ref, src_ref, srcn_ref, dst_ref,
                    h_ref, wg_ref, wu_ref, wd_ref, out_ref,
                    xbuf, ybuf, wgub, wdb, cast_exp, gsem, ssem, *, n_blocks):
    p = pl.program_id(0)
    n_items = pl.num_programs(0)
    w = iblk_ref[p]
    slot = w % 2
    first = (p == 0) | (w != iblk_ref[jnp.maximum(p - 1, 0)])
    last = (p == n_items - 1) | (w != iblk_ref[jnp.minimum(p + 1, n_items - 1)])
    lo, hi = ilo_ref[p], ihi_ref[p]
    block_rows = MOE_BLOCK * TOK_ROWS

    def tile_at(ref, row0):
        return ref.at[pl.ds(pl.multiple_of(row0, TOK_ROWS), TOK_ROWS)]

    def for_block_rows(fn):
        def group(g, c):
            for u in range(ROW_UNROLL):
                fn(g * ROW_UNROLL + u, u % 2)
            return c
        lax.fori_loop(0, MOE_BLOCK // ROW_UNROLL, group, 0)

    def start_gather(rows_ref, s):
        for_block_rows(lambda i, prio: pltpu.make_async_copy(
            tile_at(h_ref, rows_ref[0, 0, i]), tile_at(xbuf.at[s], i * TOK_ROWS), gsem.at[s]).start(priority=prio))

    def start_scatter(s):
        for_block_rows(lambda i, prio: pltpu.make_async_copy(
            tile_at(ybuf.at[s], i * TOK_ROWS), tile_at(out_ref, dst_ref[0, 0, i]), ssem.at[s]).start(priority=prio))

    def wait_gather(s):
        pltpu.make_async_copy(h_ref.at[pl.ds(0, block_rows)], xbuf.at[s], gsem.at[s]).wait()

    def wait_scatter(s):
        pltpu.make_async_copy(ybuf.at[s], out_ref.at[pl.ds(0, block_rows)], ssem.at[s]).wait()

    @pl.when(p == 0)
    def _():
        ybuf[...] = jnp.zeros_like(ybuf)
        cast_exp[0] = -1
        start_gather(src_ref, 0)

    @pl.when(first)
    def _():
        @pl.when(w + 1 < n_blocks)
        def _():
            start_gather(srcn_ref, 1 - slot)

        @pl.when(w >= 2)
        def _():
            wait_scatter(slot)
        wait_gather(slot)

    @pl.when(hi > lo)
    def _():
        @pl.when(iexp_ref[p] != cast_exp[0])
        def _():
            wgub[:, :EXPERT_DIM] = wg_ref[...].astype(BF16)
            wgub[:, EXPERT_DIM:] = wu_ref[...].astype(BF16)
            wdb[...] = wd_ref[...].astype(BF16)
            cast_exp[0] = iexp_ref[p]

        part_rows = MOE_BLOCK // MOE_PARTS
        parts = range(MOE_PARTS)

        def part(buf, i):
            return buf.at[slot, pl.ds(i * part_rows * TOK_ROWS, part_rows * TOK_ROWS)]

        xb = [_load_token_tiles(part(xbuf, i), part_rows).astype(BF16) for i in parts]
        hgu = [jnp.dot(xb[i], wgub[...], preferred_element_type=F32) for i in parts]
        hb = [(hgu[i][:, :EXPERT_DIM] * _sigmoid(hgu[i][:, :EXPERT_DIM]) * hgu[i][:, EXPERT_DIM:]).astype(BF16)
              for i in parts]
        y = [jnp.dot(hb[i], wdb[...], preferred_element_type=F32) for i in parts]
        whole = (lo == 0) & (hi == MOE_BLOCK)

        @pl.when(whole)
        def _():
            for i in parts:
                _store_token_tiles(part(ybuf, i), y[i])

        @pl.when(jnp.logical_not(whole))
        def _():
            row = lax.broadcasted_iota(jnp.int32, (part_rows, 1), 0)
            for i in parts:
                mine = (row >= lo - i * part_rows) & (row < hi - i * part_rows)
                _store_token_tiles(part(ybuf, i),
                                   jnp.where(mine, y[i], _load_token_tiles(part(ybuf, i), part_rows)))

    @pl.when(last)
    def _():
        start_scatter(slot)

    @pl.when(p == n_items - 1)
    def _():
        @pl.when(w >= 1)
        def _():
            wait_scatter(1 - slot)
        wait_scatter(slot)


def _experts(h, src_rows, dst_rows, item_blk, item_exp, item_lo, item_hi, wg, wu, wd):
    T = h.shape[0] // TOK_ROWS
    n_blocks = src_rows.shape[0]
    n_items = item_blk.shape[0]
    rows = MOE_BLOCK
    cur_spec = pl.BlockSpec((1, 1, rows), lambda p, ib, ie, il, ih: (ib[p], 0, 0), memory_space=pltpu.SMEM)
    nxt_spec = pl.BlockSpec((1, 1, rows), lambda p, ib, ie, il, ih: (jnp.minimum(ib[p] + 1, n_blocks - 1), 0, 0),
                            memory_space=pltpu.SMEM)

    def w_spec(shape):
        return pl.BlockSpec((None,) + shape, lambda p, ib, ie, il, ih: (ie[p], 0, 0))

    grid_spec = pltpu.PrefetchScalarGridSpec(
        num_scalar_prefetch=4,
        grid=(n_items,),
        in_specs=[cur_spec, nxt_spec, cur_spec, pl.BlockSpec(memory_space=pl.ANY),
                  w_spec((D_MODEL, EXPERT_DIM)), w_spec((D_MODEL, EXPERT_DIM)), w_spec((EXPERT_DIM, D_MODEL))],
        out_specs=pl.BlockSpec(memory_space=pl.ANY),
        scratch_shapes=[pltpu.VMEM((2, rows * TOK_ROWS, LANES), F32),
                        pltpu.VMEM((2, rows * TOK_ROWS, LANES), F32),
                        pltpu.VMEM((D_MODEL, 2 * EXPERT_DIM), BF16),
                        pltpu.VMEM((EXPERT_DIM, D_MODEL), BF16),
                        pltpu.SMEM((1,), jnp.int32),
                        pltpu.SemaphoreType.DMA((2,)), pltpu.SemaphoreType.DMA((2,))],
    )
    return pl.pallas_call(
        functools.partial(_experts_kernel, n_blocks=n_blocks),
        grid_spec=grid_spec,
        out_shape=jax.ShapeDtypeStruct((TOP_K * T * TOK_ROWS, LANES), F32),
        compiler_params=pltpu.CompilerParams(dimension_semantics=("arbitrary",),
                                             vmem_limit_bytes=VMEM_LIMIT,
                                             disable_bounds_checks=True),
        name="experts",
    )(item_blk, item_exp, item_lo, item_hi, src_rows, src_rows, dst_rows, h, wg, wu, wd)


def _combine_kernel(res_ref, y_ref, w_ref, g_ref, b_ref, o_ref):
    tm = res_ref.shape[0]
    acc = res_ref[...]
    w = w_ref[...]
    for k in range(TOP_K):
        acc = acc + w[:, k:k + 1] * _load_token_tiles(y_ref.at[k], tm)
    o_ref[...] = _layer_norm(acc, g_ref[...], b_ref[...])


def _combine(res, y3, wts, ln_g, ln_b):
    T = res.shape[0]
    tm = 128
    return pl.pallas_call(
        _combine_kernel,
        grid=(T // tm,),
        in_specs=[pl.BlockSpec((tm, D_MODEL), lambda i: (i, 0)),
                  pl.BlockSpec((TOP_K, tm * TOK_ROWS, LANES), lambda i: (0, i, 0)),
                  pl.BlockSpec((tm, TOP_K), lambda i: (i, 0)),
                  _const_spec((1, D_MODEL)), _const_spec((1, D_MODEL))],
        out_specs=pl.BlockSpec((tm, D_MODEL), lambda i: (i, 0)),
        out_shape=jax.ShapeDtypeStruct((T, D_MODEL), F32),
        compiler_params=pltpu.CompilerParams(dimension_semantics=("arbitrary",)),
        name="combine",
    )(res, y3, wts, ln_g, ln_b)


def _dispatch_plan(eidx_t, n_tok):
    nk = n_tok * TOP_K
    n_blocks = nk // MOE_BLOCK
    e_flat = eidx_t.T.reshape(nk)
    _, order = lax.sort((e_flat, jnp.arange(nk, dtype=jnp.int32)), num_keys=1)
    experts = jnp.arange(N_EXPERTS, dtype=jnp.int32)
    counts = jnp.sum((e_flat[None, :] == experts[:, None]).astype(jnp.int32), axis=1)
    start = jnp.cumsum(counts) - counts
    pos = jnp.sort(jnp.concatenate([jnp.arange(n_blocks, dtype=jnp.int32) * MOE_BLOCK, start]))
    nxt = jnp.concatenate([pos[1:], jnp.full((1,), nk, jnp.int32)])
    item_blk = jnp.minimum(pos // MOE_BLOCK, n_blocks - 1)
    item_exp = jnp.sum((start[None, :] <= pos[:, None]).astype(jnp.int32), axis=1) - 1
    item_lo = pos - item_blk * MOE_BLOCK
    item_hi = jnp.minimum(nxt, (item_blk + 1) * MOE_BLOCK) - item_blk * MOE_BLOCK
    tok, slot_k = order // TOP_K, order % TOP_K
    shape3 = (n_blocks, 1, MOE_BLOCK)
    src_rows = (tok * TOK_ROWS).reshape(shape3)
    dst_rows = ((slot_k * n_tok + tok) * TOK_ROWS).reshape(shape3)
    return src_rows, dst_rows, item_blk, item_exp, item_lo, item_hi


def kernel(x, w_in, conv_w, gdn_a_log, gdn_dt_bias, gdn_norm_w, w_gdn_proj, w_moba_proj, w_out,
           ln1_g, ln1_b, router_w, router_bias, exp_w_gate, exp_w_up, exp_w_down,
           sh_w_gate, sh_w_up, sh_w_down, ln2_g, ln2_b):
    B, T, D = x.shape
    assert B == 1 and D == D_MODEL and T % MB_BLOCK == 0
    x2 = x.reshape(T, D)

    o_gate = CONV_CH
    o_b = o_gate + GDN_HEADS * GDN_DV
    o_a = o_b + GDN_HEADS
    o_mq = o_a + GDN_HEADS
    o_gg = o_mq + (MB_HEADS + 2 * MB_KV_HEADS) * MB_HD
    w_qkv = w_in[:, :CONV_CH].astype(BF16)
    w_pack = jnp.concatenate([w_in[:, o_b:o_a]] + [w_in[:, o_a:o_mq]] * 4
                             + [jnp.zeros((D, LANES - 5 * GDN_HEADS), F32)], axis=1).astype(BF16)
    w_mb = w_in[:, o_mq:o_gg].astype(BF16)
    w_gates = jnp.concatenate([w_in[:, o_gate:o_b], w_in[:, o_gg:]], axis=1).astype(BF16)

    def lane_row(v):
        return jnp.zeros((1, LANES), F32).at[0, PK_GC:PK_GC + 4 * GDN_HEADS].set(jnp.tile(v.astype(F32), 4))

    q, k, v, pack, gct = _qkv_proj(x2, w_qkv, w_pack, conv_w.astype(F32),
                                   lane_row(gdn_a_log), lane_row(gdn_dt_bias))
    gct3 = gct.reshape(GDN_HEADS, T // GDN_CHUNK, GDN_CHUNK).transpose(1, 0, 2)
    gdn_o = _gdn(q, k, v, pack, gct3, gdn_norm_w.astype(F32).reshape(1, GDN_DV))

    half = ROT_DIM // 2
    inv = ROPE_THETA ** (-jnp.arange(half, dtype=F32) / half)
    ang = jnp.arange(T).astype(F32)[:, None] * inv[None, :]
    ones = jnp.ones((T, MB_HD - ROT_DIM), F32)
    cos_t = jnp.concatenate([jnp.cos(ang), jnp.cos(ang), ones], axis=1)
    sin_t = jnp.concatenate([-jnp.sin(ang), jnp.sin(ang), 0.0 * ones], axis=1)
    mq, mk, mvt, kmean = _moba_proj(x2, w_mb, cos_t, sin_t)
    moba_a = _moba(mq, mk, mvt, kmean.transpose(1, 0, 2))

    h, res, eidx_t, wts_t = _post(
        x2, gdn_o, moba_a, w_gates, w_gdn_proj.astype(BF16), w_moba_proj.astype(BF16),
        w_out.astype(BF16), ln1_g.reshape(1, D), ln1_b.reshape(1, D),
        router_w.T, router_bias.reshape(N_EXPERTS, 1),
        sh_w_gate.astype(BF16), sh_w_up.astype(BF16), sh_w_down.astype(BF16))

    plan = _dispatch_plan(eidx_t, T)
    y = _experts(h, *plan, exp_w_gate, exp_w_up, exp_w_down)
    y3 = y.reshape(TOP_K, T * TOK_ROWS, LANES)
    out = _combine(res, y3, wts_t.T, ln2_g.reshape(1, D), ln2_b.reshape(1, D))
    return out.reshape(B, T, D)
```

```python
import functools
import math

import jax
import jax.numpy as jnp
from jax import lax
from jax.experimental import pallas as pl
from jax.experimental.pallas import tpu as pltpu

F32 = jnp.float32
BF16 = jnp.bfloat16

D_MODEL = 1024
DEPTH = 1
GDN_HEADS = 8
GDN_DK = 128
GDN_DV = 128
GDN_CONV = 4
GDN_CHUNK = 64
MB_HEADS = 8
MB_KV_HEADS = 2
MB_REP = MB_HEADS // MB_KV_HEADS
MB_HD = 128
MB_BLOCK = 256
MB_TOPK = 3
ROT_DIM = MB_HD // 4
ROPE_THETA = 500000.0
N_EXPERTS = 256
TOP_K = 8
N_GROUPS = 8
GROUP_SIZE = N_EXPERTS // N_GROUPS
TOPK_GROUPS = 4
EXPERT_DIM = 256
SHARED_DIM = 256
ROUTED_SCALE = 2.5
MOE_BLOCK = 256
DEEPNORM_ALPHA = (2.0 * DEPTH) ** 0.25
LN_EPS = 1e-5
MASK_NEG = -1e30

QK_COLS = GDN_HEADS * GDN_DK
CONV_CH = 2 * QK_COLS + GDN_HEADS * GDN_DV
LANES = 128
SUBLANES = 8
VMEM_LIMIT = 56 * 1024 * 1024

TOK_TILE = 256

PK_BETA, PK_GC, PK_EG, PK_EGD, PK_EGL = 0, 8, 16, 24, 32


def _sigmoid(x):
    return 1.0 / (1.0 + jnp.exp(-x))


def _softplus(x):
    return jnp.maximum(x, 0.0) + jnp.log(1.0 + jnp.exp(-jnp.abs(x)))


def _bdot(a, b):
    return jnp.dot(a.astype(BF16), b.astype(BF16), preferred_element_type=F32)


def _bdot_nt(a, b):
    return lax.dot_general(a.astype(BF16), b.astype(BF16), (((1,), (1,)), ((), ())),
                           preferred_element_type=F32)


def _bdot_tn(a, b):
    return lax.dot_general(a.astype(BF16), b.astype(BF16), (((0,), (0,)), ((), ())),
                           preferred_element_type=F32)


def _layer_norm(x, g, b):
    mu = jnp.mean(x, axis=-1, keepdims=True)
    xc = x - mu
    var = jnp.mean(xc * xc, axis=-1, keepdims=True)
    return xc * lax.rsqrt(var + LN_EPS) * g + b


TOK_ROWS = D_MODEL // LANES


HP_ROWS = TOK_ROWS // 2


def _store_token_tiles(ref, x):
    n, r = x.shape[0], x.shape[1] // LANES
    for j in range(r):
        ref[pl.ds(j, n, stride=r), :] = x[:, j * LANES:(j + 1) * LANES]


def _load_token_tiles(ref, n):
    r = ref.shape[0] // n
    return jnp.concatenate([ref[pl.ds(j, n, stride=r), :] for j in range(r)], axis=1)


def _pack_bf16_pairs(x):
    half = x.shape[1] // 2
    bits = pltpu.bitcast(x.astype(BF16).astype(F32), jnp.uint32)
    return (bits[:, :half] >> 16) | (bits[:, half:] & jnp.uint32(0xFFFF0000))


def _unpack_bf16_pairs(w):
    lo = pltpu.bitcast(w << 16, F32)
    hi = pltpu.bitcast(w & jnp.uint32(0xFFFF0000), F32)
    return jnp.concatenate([lo, hi], axis=1)


def _const_spec(shape):
    nd = len(shape)
    return pl.BlockSpec(shape, lambda *_: (0,) * nd, pipeline_mode=pl.Buffered(1))


def _qkv_kernel(x_ref, w_ref, wp_ref, cw_ref, alog_ref, dtb_ref,
                q_ref, k_ref, v_ref, pack_ref, gct_ref, z_ref):
    tm = x_ref.shape[0]
    halo = SUBLANES

    @pl.when(pl.program_id(0) == 0)
    def _():
        z_ref[0:halo, :] = jnp.zeros((halo, CONV_CH), F32)

    xb = x_ref[...].astype(BF16)
    z_ref[halo:halo + tm, :] = jnp.dot(xb, w_ref[...], preferred_element_type=F32)

    for c in range(CONV_CH // LANES):
        cols = slice(c * LANES, (c + 1) * LANES)
        acc = z_ref[halo:halo + tm, cols] * cw_ref[GDN_CONV - 1:GDN_CONV, cols]
        for s in range(1, GDN_CONV):
            acc = acc + z_ref[halo - s:halo - s + tm, cols] * cw_ref[GDN_CONV - 1 - s:GDN_CONV - s, cols]
        y = acc * _sigmoid(acc)
        h = c % GDN_HEADS
        if c < GDN_HEADS:
            y = y * lax.rsqrt(jnp.sum(y * y, axis=-1, keepdims=True) + 1e-6) * (GDN_DK ** -0.5)
            q_ref[h] = y
        elif c < 2 * GDN_HEADS:
            y = y * lax.rsqrt(jnp.sum(y * y, axis=-1, keepdims=True) + 1e-6)
            k_ref[h] = y
        else:
            v_ref[h] = y

    z_ref[0:halo, :] = z_ref[tm:tm + halo, :]

    zp = jnp.dot(xb, wp_ref[...], preferred_element_type=F32)
    beta = _sigmoid(zp)
    g = -jnp.exp(alog_ref[...]) * _softplus(zp + dtb_ref[...])
    row = lax.broadcasted_iota(jnp.int32, (tm, LANES), 0) % GDN_CHUNK
    gc = g
    step = 1
    while step < GDN_CHUNK:
        gc = gc + jnp.where(row >= step, pltpu.roll(gc, step, axis=0), 0.0)
        step *= 2
    gl = gc.reshape(tm // GDN_CHUNK, GDN_CHUNK, LANES)[:, GDN_CHUNK - 1:GDN_CHUNK, :]
    gl = jnp.broadcast_to(gl, (tm // GDN_CHUNK, GDN_CHUNK, LANES)).reshape(tm, LANES)
    lane = lax.broadcasted_iota(jnp.int32, (tm, LANES), 1)
    pack = jnp.where(lane < PK_GC, beta,
           jnp.where(lane < PK_EG, gc,
           jnp.where(lane < PK_EGD, jnp.exp(gc),
           jnp.where(lane < PK_EGL, jnp.exp(gl - gc), jnp.exp(gl)))))
    pack_ref[...] = pack
    gct_ref[...] = pack.T[PK_GC:PK_GC + GDN_HEADS, :]


def _qkv_proj(x2, w_qkv, w_pack, conv_w, alog_row, dtb_row):
    T = x2.shape[0]
    tm = TOK_TILE
    hd_spec = pl.BlockSpec((GDN_HEADS, tm, LANES), lambda i: (0, i, 0))
    hd_shape = jax.ShapeDtypeStruct((GDN_HEADS, T, LANES), F32)
    return pl.pallas_call(
        _qkv_kernel,
        grid=(T // tm,),
        in_specs=[
            pl.BlockSpec((tm, D_MODEL), lambda i: (i, 0)),
            _const_spec((D_MODEL, CONV_CH)),
            _const_spec((D_MODEL, LANES)),
            _const_spec((GDN_CONV, CONV_CH)),
            _const_spec((1, LANES)),
            _const_spec((1, LANES)),
        ],
        out_specs=[hd_spec, hd_spec, hd_spec,
                   pl.BlockSpec((tm, LANES), lambda i: (i, 0)),
                   pl.BlockSpec((GDN_HEADS, tm), lambda i: (0, i))],
        out_shape=[hd_shape, hd_shape, hd_shape,
                   jax.ShapeDtypeStruct((T, LANES), F32),
                   jax.ShapeDtypeStruct((GDN_HEADS, T), F32)],
        scratch_shapes=[pltpu.VMEM((tm + SUBLANES, CONV_CH), F32)],
        compiler_params=pltpu.CompilerParams(dimension_semantics=("arbitrary",),
                                             vmem_limit_bytes=VMEM_LIMIT),
        name="qkv_proj",
    )(x2, w_qkv, w_pack, conv_w, alog_row, dtb_row)


MB_LROWS = 16
MB_VROWS = MB_HD + MB_LROWS


def _rope(xh, cos_t, sin_t, lane):
    half = ROT_DIM // 2
    swapped = jnp.where(lane < half, pltpu.roll(xh, LANES - half, axis=1), pltpu.roll(xh, half, axis=1))
    return xh * cos_t + swapped * sin_t


def _moba_proj_kernel(x_ref, w_ref, cos_ref, sin_ref, q_ref, k_ref, vt_ref, km_ref):
    tm = x_ref.shape[0]
    z = jnp.dot(x_ref[...].astype(BF16), w_ref[...], preferred_element_type=F32)
    cos_t = cos_ref[...]
    sin_t = sin_ref[...]
    lane = lax.broadcasted_iota(jnp.int32, (tm, LANES), 1)
    for h in range(MB_HEADS):
        q_ref[h] = _rope(z[:, h * MB_HD:(h + 1) * MB_HD], cos_t, sin_t, lane)
    koff = MB_HEADS * MB_HD
    voff = koff + MB_KV_HEADS * MB_HD
    blk_onehot = jnp.where(lane == pl.program_id(0), 1.0, 0.0).astype(BF16)
    for g in range(MB_KV_HEADS):
        kr = _rope(z[:, koff + g * MB_HD:koff + (g + 1) * MB_HD], cos_t, sin_t, lane)
        k_ref[0, g] = jnp.concatenate([kr.astype(BF16), blk_onehot], axis=1)
        km_ref[0, g:g + 1, :] = jnp.mean(kr, axis=0, keepdims=True)
        vt_ref[0, g] = jnp.concatenate([z[:, voff + g * MB_HD:voff + (g + 1) * MB_HD].T,
                                        jnp.ones((MB_LROWS, tm), F32)], axis=0).astype(BF16)


def _moba_proj(x2, w_mb, cos_t, sin_t):
    T = x2.shape[0]
    tm = MB_BLOCK
    nb = T // tm
    return pl.pallas_call(
        _moba_proj_kernel,
        grid=(nb,),
        in_specs=[
            pl.BlockSpec((tm, D_MODEL), lambda i: (i, 0)),
            _const_spec(w_mb.shape),
            pl.BlockSpec((tm, LANES), lambda i: (i, 0)),
            pl.BlockSpec((tm, LANES), lambda i: (i, 0)),
        ],
        out_specs=[
            pl.BlockSpec((MB_HEADS, tm, MB_HD), lambda i: (0, i, 0)),
            pl.BlockSpec((1, MB_KV_HEADS, tm, 2 * MB_HD), lambda i: (i, 0, 0, 0)),
            pl.BlockSpec((1, MB_KV_HEADS, MB_VROWS, tm), lambda i: (i, 0, 0, 0)),
            pl.BlockSpec((1, MB_KV_HEADS, MB_HD), lambda i: (i, 0, 0)),
        ],
        out_shape=[
            jax.ShapeDtypeStruct((MB_HEADS, T, MB_HD), F32),
            jax.ShapeDtypeStruct((nb, MB_KV_HEADS, tm, 2 * MB_HD), BF16),
            jax.ShapeDtypeStruct((nb, MB_KV_HEADS, MB_VROWS, tm), BF16),
            jax.ShapeDtypeStruct((nb, MB_KV_HEADS, MB_HD), F32),
        ],
        compiler_params=pltpu.CompilerParams(dimension_semantics=("arbitrary",),
                                             vmem_limit_bytes=VMEM_LIMIT),
        name="moba_proj",
    )(x2, w_mb, cos_t, sin_t)


def _unit_lower_inverse(a_list):
    c = a_list[0].shape[0]
    rr = lax.broadcasted_iota(jnp.int32, (c, c), 0)
    cc = lax.broadcasted_iota(jnp.int32, (c, c), 1)
    eye = jnp.where(rr == cc, 1.0, 0.0).astype(F32)
    inv = [eye - a for a in a_list]
    p = [_bdot(a, a) for a in a_list]
    n = 2
    while True:
        inv = [x + _bdot(x, y) for x, y in zip(inv, p)]
        n *= 2
        if n >= c:
            break
        p = [_bdot(y, y) for y in p]
    return inv


GDN_STEP_CHUNKS = 4


def _gdn_kernel(q_ref, k_ref, v_ref, pack_ref, gct_ref, nw_ref, o_ref, s_ref):
    C = GDN_CHUNK
    H = range(GDN_HEADS)
    P = [(c, h) for c in range(GDN_STEP_CHUNKS) for h in H]

    @pl.when(pl.program_id(0) == 0)
    def _():
        s_ref[...] = jnp.zeros_like(s_ref)

    rr = lax.broadcasted_iota(jnp.int32, (C, C), 0)
    cc = lax.broadcasted_iota(jnp.int32, (C, C), 1)
    tril = rr >= cc
    strict = rr > cc
    pack = pack_ref[...]
    nw = nw_ref[...]

    def rows(c):
        return slice(c * C, (c + 1) * C)

    def col(base, c, h):
        return pack[rows(c), base + h:base + h + 1]

    q = {(c, h): q_ref[h, rows(c), :] for c, h in P}
    k = {(c, h): k_ref[h, rows(c), :] for c, h in P}
    kb = {(c, h): k[c, h] * col(PK_BETA, c, h) for c, h in P}
    decay = {(c, h): jnp.where(tril, jnp.exp(jnp.where(tril, col(PK_GC, c, h) - gct_ref[c, h:h + 1, :], 0.0)), 0.0)
             for c, h in P}
    sc = {p: _bdot_nt(jnp.concatenate([kb[p], q[p]], axis=0), k[p]) for p in P}
    a = {p: jnp.where(strict, sc[p][:C] * decay[p], 0.0) for p in P}
    aqk = {p: sc[p][C:] * decay[p] for p in P}
    tinv = dict(zip(P, _unit_lower_inverse([a[p] for p in P])))
    rhs = {(c, h): jnp.concatenate([v_ref[h, rows(c), :] * col(PK_BETA, c, h), kb[c, h] * col(PK_EG, c, h)],
                                   axis=1) for c, h in P}
    sol = {p: _bdot(tinv[p], rhs[p]) for p in P}

    s = [s_ref[h] for h in H]
    for c in range(GDN_STEP_CHUNKS):
        wq = [_bdot(jnp.concatenate([sol[c, h][:, GDN_DV:], q[c, h] * col(PK_EG, c, h)], axis=0), s[h])
              for h in H]
        v_new = [sol[c, h][:, :GDN_DV] - wq[h][:C] for h in H]
        o = [wq[h][C:] + _bdot(aqk[c, h], v_new[h]) for h in H]
        ds = [_bdot_tn(k[c, h] * col(PK_EGD, c, h), v_new[h]) for h in H]
        for h in H:
            s[h] = s[h] * pack[c * C:c * C + 1, PK_EGL + h:PK_EGL + h + 1] + ds[h]
            on = o[h] * lax.rsqrt(jnp.mean(o[h] * o[h], axis=-1, keepdims=True) + 1e-6) * nw
            o_ref[rows(c), h * GDN_DV:(h + 1) * GDN_DV] = on
    for h in H:
        s_ref[h] = s[h]


def _gdn(q, k, v, pack, gct3, norm_w_row):
    T = q.shape[1]
    C = GDN_CHUNK * GDN_STEP_CHUNKS
    hd_spec = pl.BlockSpec((GDN_HEADS, C, LANES), lambda i: (0, i, 0))
    return pl.pallas_call(
        _gdn_kernel,
        grid=(T // C,),
        in_specs=[hd_spec, hd_spec, hd_spec,
                  pl.BlockSpec((C, LANES), lambda i: (i, 0)),
                  pl.BlockSpec((GDN_STEP_CHUNKS, GDN_HEADS, GDN_CHUNK), lambda i: (i, 0, 0)),
                  _const_spec((1, GDN_DV))],
        out_specs=pl.BlockSpec((C, GDN_HEADS * GDN_DV), lambda i: (i, 0)),
        out_shape=jax.ShapeDtypeStruct((T, GDN_HEADS * GDN_DV), F32),
        scratch_shapes=[pltpu.VMEM((GDN_HEADS, GDN_DK, GDN_DV), F32)],
        compiler_params=pltpu.CompilerParams(dimension_semantics=("arbitrary",)),
        name="gdn",
    )(q, k, v, pack, gct3, norm_w_row)


MB_QSPLIT = 2
MB_TRIP = 4
MB_MAX_BLOCKS = MB_HD


def _moba_kernel(q_ref, k_ref, vt_ref, km_ref, o_ref, qx_ref, sa_ref, sb_ref, acc_ref, m_ref):
    qi = pl.program_id(1)
    nb = k_ref.shape[0]
    bs = MB_BLOCK
    nq = MB_REP * bs
    nbp = MB_MAX_BLOCKS
    c = (MB_HD ** -0.5) * math.log2(math.e)

    qf = q_ref[...].reshape(nq, MB_HD)

    gate = lax.dot_general(km_ref[...], qf, (((1,), (1,)), ((), ())),
                           precision=lax.Precision.HIGHEST, preferred_element_type=F32)
    blk = lax.broadcasted_iota(jnp.int32, (nbp, nq), 0).astype(F32)
    qif = qi.astype(F32)
    gate = jnp.where(blk < qif, gate, -jnp.inf)
    sel = jnp.where(blk == qif, 1.0, 0.0)
    for _ in range(MB_TOPK):
        top = jnp.max(gate, axis=0, keepdims=True)
        first = jnp.min(jnp.where(gate == top, blk, float(nbp)), axis=0, keepdims=True)
        hit = (blk == first) & (top > -jnp.inf)
        sel = jnp.where(hit, 1.0, sel)
        gate = jnp.where(hit, -jnp.inf, gate)
    bias_t = jnp.where(sel > 0.0, 0.0, MASK_NEG)
    qc = qf * c
    qx_ref[0] = jnp.concatenate([qc, bias_t.T], axis=1).astype(BF16)
    qx_ref[1] = jnp.concatenate([qc, jnp.full((nq, nbp), MASK_NEG, F32)], axis=1).astype(BF16)

    w = nq // MB_QSPLIT
    groups = [slice(h * w, (h + 1) * w) for h in range(MB_QSPLIT)]

    def scores_into(dst_ref, t):
        kx = k_ref[jnp.minimum(t, nb - 1)]
        qset = jnp.where(t < qi, 0, 1)
        for lanes in groups:
            dst_ref[:, lanes] = lax.dot_general(kx, qx_ref[qset, lanes, :], (((1,), (1,)), ((), ())),
                                                preferred_element_type=F32)

    st = lax.dot_general(k_ref[qi], qx_ref[0], (((1,), (1,)), ((), ())), preferred_element_type=F32)
    kpos = lax.broadcasted_iota(jnp.int32, (bs, nq), 0)
    qpos = lax.broadcasted_iota(jnp.int32, (bs, nq), 1) % bs
    st = jnp.where(kpos <= qpos, st, MASK_NEG)
    m0 = jnp.max(st, axis=0, keepdims=True)
    p = jnp.exp2(st - m0)
    m_ref[...] = m0
    acc_ref[...] = jnp.dot(vt_ref[qi], p.astype(BF16), preferred_element_type=F32)

    def absorb(src_ref, t):
        vt = vt_ref[jnp.minimum(t, nb - 1)]
        for lanes in groups:
            st = src_ref[:, lanes]
            m_old = m_ref[:, lanes]
            m_new = jnp.maximum(m_old, jnp.max(st, axis=0, keepdims=True))
            alpha = jnp.exp2(m_old - m_new)
            p = jnp.exp2(st - m_new)
            m_ref[:, lanes] = m_new
            acc_ref[:, lanes] = alpha * acc_ref[:, lanes] + jnp.dot(vt, p.astype(BF16),
                                                                    preferred_element_type=F32)

    scores_into(sa_ref, 0)

    def body(i, carry):
        t = MB_TRIP * i
        for u in range(0, MB_TRIP, 2):
            scores_into(sb_ref, t + u + 1)
            absorb(sa_ref, t + u)
            scores_into(sa_ref, t + u + 2)
            absorb(sb_ref, t + u + 1)
        return carry

    lax.fori_loop(0, (qi + MB_TRIP - 1) // MB_TRIP, body, 0)

    out_t = acc_ref[0:MB_HD, :] / acc_ref[MB_HD:MB_HD + 1, :]
    for r in range(MB_REP):
        o_ref[:, r * MB_HD:(r + 1) * MB_HD] = out_t[:, r * bs:(r + 1) * bs].T


def _moba(mq, mk, mvt, kmean):
    T = mq.shape[1]
    bs = MB_BLOCK
    nb = T // bs
    nq = MB_REP * bs
    assert nb <= MB_MAX_BLOCKS
    kmean = jnp.pad(kmean, ((0, 0), (0, MB_MAX_BLOCKS - nb), (0, 0)))
    return pl.pallas_call(
        _moba_kernel,
        grid=(MB_KV_HEADS, nb),
        in_specs=[
            pl.BlockSpec((MB_REP, bs, MB_HD), lambda g, i: (g, i, 0)),
            pl.BlockSpec((nb, None, bs, 2 * MB_HD), lambda g, i: (0, g, 0, 0)),
            pl.BlockSpec((nb, None, MB_VROWS, bs), lambda g, i: (0, g, 0, 0)),
            pl.BlockSpec((None, MB_MAX_BLOCKS, MB_HD), lambda g, i: (g, 0, 0)),
        ],
        out_specs=pl.BlockSpec((bs, MB_REP * MB_HD), lambda g, i: (i, g)),
        out_shape=jax.ShapeDtypeStruct((T, MB_HEADS * MB_HD), F32),
        scratch_shapes=[pltpu.VMEM((2, nq, 2 * MB_HD), BF16),
                        pltpu.VMEM((bs, nq), F32), pltpu.VMEM((bs, nq), F32),
                        pltpu.VMEM((MB_VROWS, nq), F32),
                        pltpu.VMEM((1, nq), F32)],
        compiler_params=pltpu.CompilerParams(dimension_semantics=("arbitrary", "arbitrary"),
                                             vmem_limit_bytes=VMEM_LIMIT),
        name="moba",
    )(mq, mk, mvt, kmean)


def _post_kernel(x_ref, go_ref, ma_ref, wg_ref, wgp_ref, wmp_ref, wo_ref, g1_ref, b1_ref,
                 rwt_ref, rb_ref, sg_ref, su_ref, sd_ref,
                 hp_ref, res_ref, eidx_ref, wts_ref):
    tm = x_ref.shape[0]
    x = x_ref[...]
    zg = jnp.dot(x.astype(BF16), wg_ref[...], preferred_element_type=F32)
    gate = zg[:, :D_MODEL]
    o = go_ref[...] * (gate * _sigmoid(gate))
    y_gdn = jnp.dot(o.astype(BF16), wgp_ref[...], preferred_element_type=F32)
    y_mb = jnp.dot(ma_ref[...].astype(BF16), wmp_ref[...], preferred_element_type=F32)
    m = _sigmoid(zg[:, D_MODEL:2 * D_MODEL]) * y_gdn + _sigmoid(zg[:, 2 * D_MODEL:]) * y_mb
    mix = jnp.dot(m.astype(BF16), wo_ref[...], preferred_element_type=F32)
    h = _layer_norm(DEEPNORM_ALPHA * x + mix, g1_ref[...], b1_ref[...])
    hb = h.astype(BF16)
    _store_token_tiles(hp_ref, _pack_bf16_pairs(h))

    hs = jnp.dot(hb, sg_ref[...], preferred_element_type=F32)
    hs = hs * _sigmoid(hs) * jnp.dot(hb, su_ref[...], preferred_element_type=F32)
    res_ref[...] = DEEPNORM_ALPHA * h + jnp.dot(hs.astype(BF16), sd_ref[...], preferred_element_type=F32)

    logits = lax.dot_general(rwt_ref[...], h, (((1,), (1,)), ((), ())),
                             precision=lax.Precision.HIGHEST, preferred_element_type=F32)
    scores = _sigmoid(logits)
    choice = scores + rb_ref[...]
    neg = -jnp.inf
    gi = lax.broadcasted_iota(jnp.int32, (GROUP_SIZE, tm), 0).astype(F32)
    gscore = []
    for g in range(N_GROUPS):
        cg = choice[g * GROUP_SIZE:(g + 1) * GROUP_SIZE, :]
        m1 = jnp.max(cg, axis=0, keepdims=True)
        i1 = jnp.min(jnp.where(cg == m1, gi, float(GROUP_SIZE)), axis=0, keepdims=True)
        m2 = jnp.max(jnp.where(gi == i1, neg, cg), axis=0, keepdims=True)
        gscore.append(m1 + m2)
    gs = jnp.concatenate(gscore, axis=0)
    gidx = lax.broadcasted_iota(jnp.int32, (N_GROUPS, tm), 0).astype(F32)
    gsel = jnp.zeros((N_GROUPS, tm), F32)
    for _ in range(TOPK_GROUPS):
        top = jnp.max(gs, axis=0, keepdims=True)
        first = jnp.min(jnp.where(gs == top, gidx, float(N_GROUPS)), axis=0, keepdims=True)
        hit = gidx == first
        gsel = jnp.where(hit, 1.0, gsel)
        gs = jnp.where(hit, neg, gs)
    masked = jnp.concatenate(
        [jnp.where(gsel[g:g + 1, :] > 0.0, choice[g * GROUP_SIZE:(g + 1) * GROUP_SIZE, :], neg)
         for g in range(N_GROUPS)], axis=0)
    ei = lax.broadcasted_iota(jnp.int32, (N_EXPERTS, tm), 0).astype(F32)
    idx_rows, w_rows = [], []
    for _ in range(TOP_K):
        top = jnp.max(masked, axis=0, keepdims=True)
        first = jnp.min(jnp.where(masked == top, ei, float(N_EXPERTS)), axis=0, keepdims=True)
        hit = ei == first
        idx_rows.append(first)
        w_rows.append(jnp.sum(jnp.where(hit, scores, 0.0), axis=0, keepdims=True))
        masked = jnp.where(hit, neg, masked)
    w = jnp.concatenate(w_rows, axis=0)
    w = w / (jnp.sum(w, axis=0, keepdims=True) + 1e-20) * ROUTED_SCALE
    eidx_ref[...] = jnp.concatenate(idx_rows, axis=0).astype(jnp.int32)
    wts_ref[...] = w


def _post(x2, gdn_o, moba_a, w_gates, w_gdn_proj, w_moba_proj, w_out, ln_g, ln_b,
          router_wt, router_b_col, sh_g, sh_u, sh_d):
    T = x2.shape[0]
    tm = TOK_TILE
    row_spec = pl.BlockSpec((tm, D_MODEL), lambda i: (i, 0))
    consts = [w_gates, w_gdn_proj, w_moba_proj, w_out, ln_g, ln_b, router_wt, router_b_col,
              sh_g, sh_u, sh_d]
    return pl.pallas_call(
        _post_kernel,
        grid=(T // tm,),
        in_specs=[row_spec, row_spec, row_spec] + [_const_spec(c.shape) for c in consts],
        out_specs=[pl.BlockSpec((tm * HP_ROWS, LANES), lambda i: (i, 0)), row_spec,
                   pl.BlockSpec((TOP_K, tm), lambda i: (0, i)),
                   pl.BlockSpec((TOP_K, tm), lambda i: (0, i))],
        out_shape=[jax.ShapeDtypeStruct((T * HP_ROWS, LANES), jnp.uint32),
                   jax.ShapeDtypeStruct((T, D_MODEL), F32),
                   jax.ShapeDtypeStruct((TOP_K, T), jnp.int32),
                   jax.ShapeDtypeStruct((TOP_K, T), F32)],
        compiler_params=pltpu.CompilerParams(dimension_semantics=("arbitrary",),
                                             vmem_limit_bytes=VMEM_LIMIT),
        name="post",
    )(x2, gdn_o, moba_a, *consts)


ROW_UNROLL = 8


def _experts_kernel(iblk_ref, iexp_ref, ilo_ref, ihi_ref, src_ref, dst_ref,
                    hp_ref, wg_ref, wu_ref, wd_ref, out_ref,
                    xbuf, ybuf, wgb, wub, wdb, cast_exp, ssem):
    p = pl.program_id(0)
    n_items = pl.num_programs(0)
    w = iblk_ref[p]
    slot = w % 2
    first = (p == 0) | (w != iblk_ref[jnp.maximum(p - 1, 0)])
    last = (p == n_items - 1) | (w != iblk_ref[jnp.minimum(p + 1, n_items - 1)])
    lo, hi = ilo_ref[p], ihi_ref[p]
    block_rows = MOE_BLOCK * TOK_ROWS

    def tile_at(ref, row0, rows):
        return ref.at[pl.ds(pl.multiple_of(row0, rows), rows)]

    def for_block_rows(fn):
        def group(g, c):
            for u in range(ROW_UNROLL):
                fn(g * ROW_UNROLL + u)
            return c
        lax.fori_loop(0, MOE_BLOCK // ROW_UNROLL, group, 0)

    def gather_block():
        def one(i):
            xbuf[pl.ds(pl.multiple_of(i * HP_ROWS, HP_ROWS), HP_ROWS), :] = (
                hp_ref[pl.ds(pl.multiple_of(src_ref[0, 0, i], HP_ROWS), HP_ROWS), :])
        for_block_rows(one)

    def start_scatter(s):
        for_block_rows(lambda i: pltpu.make_async_copy(
            tile_at(ybuf.at[s], i * TOK_ROWS, TOK_ROWS), tile_at(out_ref, dst_ref[0, 0, i], TOK_ROWS),
            ssem.at[s]).start())

    def wait_scatter(s):
        pltpu.make_async_copy(ybuf.at[s], out_ref.at[pl.ds(0, block_rows)], ssem.at[s]).wait()

    @pl.when(p == 0)
    def _():
        ybuf[...] = jnp.zeros_like(ybuf)
        cast_exp[0] = -1

    @pl.when(first)
    def _():
        @pl.when(w >= 2)
        def _():
            wait_scatter(slot)
        gather_block()

    @pl.when(hi > lo)
    def _():
        @pl.when(iexp_ref[p] != cast_exp[0])
        def _():
            wgb[...] = wg_ref[...].astype(BF16)
            wub[...] = wu_ref[...].astype(BF16)
            wdb[...] = wd_ref[...].astype(BF16)
            cast_exp[0] = iexp_ref[p]

        xb = _unpack_bf16_pairs(_load_token_tiles(xbuf, MOE_BLOCK)).astype(BF16)
        hg = jnp.dot(xb, wgb[...], preferred_element_type=F32)
        hu = jnp.dot(xb, wub[...], preferred_element_type=F32)
        hb = (hg * _sigmoid(hg) * hu).astype(BF16)
        y = jnp.dot(hb, wdb[...], preferred_element_type=F32)
        row = lax.broadcasted_iota(jnp.int32, (MOE_BLOCK, 1), 0)
        mine = (row >= lo) & (row < hi)
        _store_token_tiles(ybuf.at[slot], jnp.where(mine, y, _load_token_tiles(ybuf.at[slot], MOE_BLOCK)))

    @pl.when(last)
    def _():
        start_scatter(slot)

    @pl.when(p == n_items - 1)
    def _():
        @pl.when(w >= 1)
        def _():
            wait_scatter(1 - slot)
        wait_scatter(slot)


def _experts(hp, src_rows, dst_rows, item_blk, item_exp, item_lo, item_hi, wg, wu, wd):
    T = hp.shape[0] // HP_ROWS
    n_items = item_blk.shape[0]
    rows = MOE_BLOCK
    ids_spec = pl.BlockSpec((1, 1, rows), lambda p, ib, ie, il, ih: (ib[p], 0, 0), memory_space=pltpu.SMEM)

    def w_spec(shape):
        return pl.BlockSpec((None,) + shape, lambda p, ib, ie, il, ih: (ie[p], 0, 0))

    grid_spec = pltpu.PrefetchScalarGridSpec(
        num_scalar_prefetch=4,
        grid=(n_items,),
        in_specs=[ids_spec, ids_spec, _const_spec(hp.shape),
                  w_spec((D_MODEL, EXPERT_DIM)), w_spec((D_MODEL, EXPERT_DIM)), w_spec((EXPERT_DIM, D_MODEL))],
        out_specs=pl.BlockSpec(memory_space=pl.ANY),
        scratch_shapes=[pltpu.VMEM((rows * HP_ROWS, LANES), jnp.uint32),
                        pltpu.VMEM((2, rows * TOK_ROWS, LANES), F32),
                        pltpu.VMEM((D_MODEL, EXPERT_DIM), BF16), pltpu.VMEM((D_MODEL, EXPERT_DIM), BF16),
                        pltpu.VMEM((EXPERT_DIM, D_MODEL), BF16),
                        pltpu.SMEM((1,), jnp.int32),
                        pltpu.SemaphoreType.DMA((2,))],
    )
    return pl.pallas_call(
        _experts_kernel,
        grid_spec=grid_spec,
        out_shape=jax.ShapeDtypeStruct((TOP_K * T * TOK_ROWS, LANES), F32),
        compiler_params=pltpu.CompilerParams(dimension_semantics=("arbitrary",),
                                             vmem_limit_bytes=VMEM_LIMIT,
                                             disable_bounds_checks=True),
        name="experts",
    )(item_blk, item_exp, item_lo, item_hi, src_rows, dst_rows, hp, wg, wu, wd)


def _combine_kernel(res_ref, y_ref, w_ref, g_ref, b_ref, o_ref):
    tm = res_ref.shape[0]
    acc = res_ref[...]
    w = w_ref[...]
    for k in range(TOP_K):
        acc = acc + w[:, k:k + 1] * _load_token_tiles(y_ref.at[k], tm)
    o_ref[...] = _layer_norm(acc, g_ref[...], b_ref[...])


def _combine(res, y3, wts, ln_g, ln_b):
    T = res.shape[0]
    tm = 128
    return pl.pallas_call(
        _combine_kernel,
        grid=(T // tm,),
        in_specs=[pl.BlockSpec((tm, D_MODEL), lambda i: (i, 0)),
                  pl.BlockSpec((TOP_K, tm * TOK_ROWS, LANES), lambda i: (0, i, 0)),
                  pl.BlockSpec((tm, TOP_K), lambda i: (i, 0)),
                  _const_spec((1, D_MODEL)), _const_spec((1, D_MODEL))],
        out_specs=pl.BlockSpec((tm, D_MODEL), lambda i: (i, 0)),
        out_shape=jax.ShapeDtypeStruct((T, D_MODEL), F32),
        compiler_params=pltpu.CompilerParams(dimension_semantics=("arbitrary",)),
        name="combine",
    )(res, y3, wts, ln_g, ln_b)


def _dispatch_plan(eidx_t, n_tok):
    nk = n_tok * TOP_K
    n_blocks = nk // MOE_BLOCK
    e_flat = eidx_t.T.reshape(nk)
    _, order = lax.sort((e_flat, jnp.arange(nk, dtype=jnp.int32)), num_keys=1)
    experts = jnp.arange(N_EXPERTS, dtype=jnp.int32)
    counts = jnp.sum((e_flat[None, :] == experts[:, None]).astype(jnp.int32), axis=1)
    start = jnp.cumsum(counts) - counts
    pos = jnp.sort(jnp.concatenate([jnp.arange(n_blocks, dtype=jnp.int32) * MOE_BLOCK, start]))
    nxt = jnp.concatenate([pos[1:], jnp.full((1,), nk, jnp.int32)])
    item_blk = jnp.minimum(pos // MOE_BLOCK, n_blocks - 1)
    item_exp = jnp.sum((start[None, :] <= pos[:, None]).astype(jnp.int32), axis=1) - 1
    item_lo = pos - item_blk * MOE_BLOCK
    item_hi = jnp.minimum(nxt, (item_blk + 1) * MOE_BLOCK) - item_blk * MOE_BLOCK
    tok, slot_k = order // TOP_K, order % TOP_K
    shape3 = (n_blocks, 1, MOE_BLOCK)
    src_rows = (tok * HP_ROWS).reshape(shape3)
    dst_rows = ((slot_k * n_tok + tok) * TOK_ROWS).reshape(shape3)
    return src_rows, dst_rows, item_blk, item_exp, item_lo, item_hi


def kernel(x, w_in, conv_w, gdn_a_log, gdn_dt_bias, gdn_norm_w, w_gdn_proj, w_moba_proj, w_out,
           ln1_g, ln1_b, router_w, router_bias, exp_w_gate, exp_w_up, exp_w_down,
           sh_w_gate, sh_w_up, sh_w_down, ln2_g, ln2_b):
    B, T, D = x.shape
    assert B == 1 and D == D_MODEL and T % MB_BLOCK == 0
    x2 = x.reshape(T, D)

    o_gate = CONV_CH
    o_b = o_gate + GDN_HEADS * GDN_DV
    o_a = o_b + GDN_HEADS
    o_mq = o_a + GDN_HEADS
    o_gg = o_mq + (MB_HEADS + 2 * MB_KV_HEADS) * MB_HD
    w_qkv = w_in[:, :CONV_CH].astype(BF16)
    w_pack = jnp.concatenate([w_in[:, o_b:o_a]] + [w_in[:, o_a:o_mq]] * 4
                             + [jnp.zeros((D, LANES - 5 * GDN_HEADS), F32)], axis=1).astype(BF16)
    w_mb = w_in[:, o_mq:o_gg].astype(BF16)
    w_gates = jnp.concatenate([w_in[:, o_gate:o_b], w_in[:, o_gg:]], axis=1).astype(BF16)

    def lane_row(v):
        return jnp.zeros((1, LANES), F32).at[0, PK_GC:PK_GC + 4 * GDN_HEADS].set(jnp.tile(v.astype(F32), 4))

    q, k, v, pack, gct = _qkv_proj(x2, w_qkv, w_pack, conv_w.astype(F32),
                                   lane_row(gdn_a_log), lane_row(gdn_dt_bias))
    gct3 = gct.reshape(GDN_HEADS, T // GDN_CHUNK, GDN_CHUNK).transpose(1, 0, 2)
    gdn_o = _gdn(q, k, v, pack, gct3, gdn_norm_w.astype(F32).reshape(1, GDN_DV))

    half = ROT_DIM // 2
    inv = ROPE_THETA ** (-jnp.arange(half, dtype=F32) / half)
    ang = jnp.arange(T).astype(F32)[:, None] * inv[None, :]
    ones = jnp.ones((T, MB_HD - ROT_DIM), F32)
    cos_t = jnp.concatenate([jnp.cos(ang), jnp.cos(ang), ones], axis=1)
    sin_t = jnp.concatenate([-jnp.sin(ang), jnp.sin(ang), 0.0 * ones], axis=1)
    mq, mk, mvt, kmean = _moba_proj(x2, w_mb, cos_t, sin_t)
    moba_a = _moba(mq, mk, mvt, kmean.transpose(1, 0, 2))

    hp, res, eidx_t, wts_t = _post(
        x2, gdn_o, moba_a, w_gates, w_gdn_proj.astype(BF16), w_moba_proj.astype(BF16),
        w_out.astype(BF16), ln1_g.reshape(1, D), ln1_b.reshape(1, D),
        router_w.T, router_bias.reshape(N_EXPERTS, 1),
        sh_w_gate.astype(BF16), sh_w_up.astype(BF16), sh_w_down.astype(BF16))

    plan = _dispatch_plan(eidx_t, T)
    y = _experts(hp, *plan, exp_w_gate, exp_w_up, exp_w_down)
    y3 = y.reshape(TOP_K, T * TOK_ROWS, LANES)
    out = _combine(res, y3, wts_t.T, ln2_g.reshape(1, D), ln2_b.reshape(1, D))
    return out.reshape(B, T, D)
```

```python
import functools
import math

import jax
import jax.numpy as jnp
from jax import lax
from jax.experimental import pallas as pl
from jax.experimental.pallas import tpu as pltpu

F32 = jnp.float32
BF16 = jnp.bfloat16

D_MODEL = 1024
DEPTH = 1
GDN_HEADS = 8
GDN_DK = 128
GDN_DV = 128
GDN_CONV = 4
GDN_CHUNK = 64
MB_HEADS = 8
MB_KV_HEADS = 2
MB_REP = MB_HEADS // MB_KV_HEADS
MB_HD = 128
MB_BLOCK = 256
MB_TOPK = 3
ROT_DIM = MB_HD // 4
ROPE_THETA = 500000.0
N_EXPERTS = 256
TOP_K = 8
N_GROUPS = 8
GROUP_SIZE = N_EXPERTS // N_GROUPS
TOPK_GROUPS = 4
EXPERT_DIM = 256
SHARED_DIM = 256
ROUTED_SCALE = 2.5
MOE_BLOCK = 256
DEEPNORM_ALPHA = (2.0 * DEPTH) ** 0.25
LN_EPS = 1e-5
MASK_NEG = -1e30

QK_COLS = GDN_HEADS * GDN_DK
CONV_CH = 2 * QK_COLS + GDN_HEADS * GDN_DV
LANES = 128
SUBLANES = 8
VMEM_LIMIT = 56 * 1024 * 1024

TOK_TILE = 256
POST_TILE = 512
COMBINE_TILE = 256

PK_BETA, PK_GC, PK_EG, PK_EGD, PK_EGL = 0, 8, 16, 24, 32


def _sigmoid(x):
    return 1.0 / (1.0 + jnp.exp(-x))


def _softplus(x):
    return jnp.maximum(x, 0.0) + jnp.log(1.0 + jnp.exp(-jnp.abs(x)))


def _bdot(a, b):
    return jnp.dot(a.astype(BF16), b.astype(BF16), preferred_element_type=F32)


def _bdot_nt(a, b):
    return lax.dot_general(a.astype(BF16), b.astype(BF16), (((1,), (1,)), ((), ())),
                           preferred_element_type=F32)


def _bdot_tn(a, b):
    return lax.dot_general(a.astype(BF16), b.astype(BF16), (((0,), (0,)), ((), ())),
                           preferred_element_type=F32)


def _layer_norm(x, g, b):
    mu = jnp.mean(x, axis=-1, keepdims=True)
    xc = x - mu
    var = jnp.mean(xc * xc, axis=-1, keepdims=True)
    return xc * lax.rsqrt(var + LN_EPS) * g + b


TOK_ROWS = D_MODEL // LANES


HP_ROWS = TOK_ROWS // 2


def _store_token_tiles(ref, x):
    n, r = x.shape[0], x.shape[1] // LANES
    for j in range(r):
        ref[pl.ds(j, n, stride=r), :] = x[:, j * LANES:(j + 1) * LANES]


def _load_token_tiles(ref, n):
    r = ref.shape[0] // n
    return jnp.concatenate([ref[pl.ds(j, n, stride=r), :] for j in range(r)], axis=1)


def _pack_bf16_pairs(x):
    half = x.shape[1] // 2
    bits = pltpu.bitcast(x.astype(BF16).astype(F32), jnp.uint32)
    return (bits[:, :half] >> 16) | (bits[:, half:] & jnp.uint32(0xFFFF0000))


def _unpack_bf16_pairs(w):
    lo = pltpu.bitcast(w << 16, F32)
    hi = pltpu.bitcast(w & jnp.uint32(0xFFFF0000), F32)
    return jnp.concatenate([lo, hi], axis=1)


def _const_spec(shape):
    nd = len(shape)
    return pl.BlockSpec(shape, lambda *_: (0,) * nd, pipeline_mode=pl.Buffered(1))


def _qkv_kernel(x_ref, w_ref, wp_ref, cw_ref, alog_ref, dtb_ref,
                q_ref, k_ref, v_ref, pack_ref, gct_ref, z_ref):
    tm = x_ref.shape[0]
    halo = SUBLANES

    @pl.when(pl.program_id(0) == 0)
    def _():
        z_ref[0:halo, :] = jnp.zeros((halo, CONV_CH), F32)

    xb = x_ref[...].astype(BF16)
    z_ref[halo:halo + tm, :] = jnp.dot(xb, w_ref[...], preferred_element_type=F32)

    for c in range(CONV_CH // LANES):
        cols = slice(c * LANES, (c + 1) * LANES)
        acc = z_ref[halo:halo + tm, cols] * cw_ref[GDN_CONV - 1:GDN_CONV, cols]
        for s in range(1, GDN_CONV):
            acc = acc + z_ref[halo - s:halo - s + tm, cols] * cw_ref[GDN_CONV - 1 - s:GDN_CONV - s, cols]
        y = acc * _sigmoid(acc)
        h = c % GDN_HEADS
        if c < GDN_HEADS:
            y = y * lax.rsqrt(jnp.sum(y * y, axis=-1, keepdims=True) + 1e-6) * (GDN_DK ** -0.5)
            q_ref[h] = y
        elif c < 2 * GDN_HEADS:
            y = y * lax.rsqrt(jnp.sum(y * y, axis=-1, keepdims=True) + 1e-6)
            k_ref[h] = y
        else:
            v_ref[h] = y

    z_ref[0:halo, :] = z_ref[tm:tm + halo, :]

    zp = jnp.dot(xb, wp_ref[...], preferred_element_type=F32)
    beta = _sigmoid(zp)
    g = -jnp.exp(alog_ref[...]) * _softplus(zp + dtb_ref[...])
    row = lax.broadcasted_iota(jnp.int32, (tm, LANES), 0) % GDN_CHUNK
    gc = g
    step = 1
    while step < GDN_CHUNK:
        gc = gc + jnp.where(row >= step, pltpu.roll(gc, step, axis=0), 0.0)
        step *= 2
    gl = gc.reshape(tm // GDN_CHUNK, GDN_CHUNK, LANES)[:, GDN_CHUNK - 1:GDN_CHUNK, :]
    gl = jnp.broadcast_to(gl, (tm // GDN_CHUNK, GDN_CHUNK, LANES)).reshape(tm, LANES)
    lane = lax.broadcasted_iota(jnp.int32, (tm, LANES), 1)
    pack = jnp.where(lane < PK_GC, beta,
           jnp.where(lane < PK_EG, gc,
           jnp.where(lane < PK_EGD, jnp.exp(gc),
           jnp.where(lane < PK_EGL, jnp.exp(gl - gc), jnp.exp(gl)))))
    pack_ref[...] = pack
    gct_ref[...] = pack.T[PK_GC:PK_GC + GDN_HEADS, :]


def _qkv_proj(x2, w_qkv, w_pack, conv_w, alog_row, dtb_row):
    T = x2.shape[0]
    tm = TOK_TILE
    hd_spec = pl.BlockSpec((GDN_HEADS, tm, LANES), lambda i: (0, i, 0))
    hd_shape = jax.ShapeDtypeStruct((GDN_HEADS, T, LANES), F32)
    return pl.pallas_call(
        _qkv_kernel,
        grid=(T // tm,),
        in_specs=[
            pl.BlockSpec((tm, D_MODEL), lambda i: (i, 0)),
            _const_spec((D_MODEL, CONV_CH)),
            _const_spec((D_MODEL, LANES)),
            _const_spec((GDN_CONV, CONV_CH)),
            _const_spec((1, LANES)),
            _const_spec((1, LANES)),
        ],
        out_specs=[hd_spec, hd_spec, hd_spec,
                   pl.BlockSpec((tm, LANES), lambda i: (i, 0)),
                   pl.BlockSpec((GDN_HEADS, tm), lambda i: (0, i))],
        out_shape=[hd_shape, hd_shape, hd_shape,
                   jax.ShapeDtypeStruct((T, LANES), F32),
                   jax.ShapeDtypeStruct((GDN_HEADS, T), F32)],
        scratch_shapes=[pltpu.VMEM((tm + SUBLANES, CONV_CH), F32)],
        compiler_params=pltpu.CompilerParams(dimension_semantics=("arbitrary",),
                                             vmem_limit_bytes=VMEM_LIMIT),
        name="qkv_proj",
    )(x2, w_qkv, w_pack, conv_w, alog_row, dtb_row)


MB_LROWS = 16
MB_VROWS = MB_HD + MB_LROWS


def _rope(xh, cos_t, sin_t, lane):
    half = ROT_DIM // 2
    swapped = jnp.where(lane < half, pltpu.roll(xh, LANES - half, axis=1), pltpu.roll(xh, half, axis=1))
    return xh * cos_t + swapped * sin_t


def _moba_proj_kernel(x_ref, w_ref, cos_ref, sin_ref, q_ref, k_ref, vt_ref, km_ref):
    tm = x_ref.shape[0]
    z = jnp.dot(x_ref[...].astype(BF16), w_ref[...], preferred_element_type=F32)
    cos_t = cos_ref[...]
    sin_t = sin_ref[...]
    lane = lax.broadcasted_iota(jnp.int32, (tm, LANES), 1)
    for h in range(MB_HEADS):
        q_ref[h] = _rope(z[:, h * MB_HD:(h + 1) * MB_HD], cos_t, sin_t, lane)
    koff = MB_HEADS * MB_HD
    voff = koff + MB_KV_HEADS * MB_HD
    blk_onehot = jnp.where(lane == pl.program_id(0), 1.0, 0.0).astype(BF16)
    for g in range(MB_KV_HEADS):
        kr = _rope(z[:, koff + g * MB_HD:koff + (g + 1) * MB_HD], cos_t, sin_t, lane)
        k_ref[0, g] = jnp.concatenate([kr.astype(BF16), blk_onehot], axis=1)
        km_ref[0, g:g + 1, :] = jnp.mean(kr, axis=0, keepdims=True)
        vt_ref[0, g] = jnp.concatenate([z[:, voff + g * MB_HD:voff + (g + 1) * MB_HD].T,
                                        jnp.ones((MB_LROWS, tm), F32)], axis=0).astype(BF16)


def _moba_proj(x2, w_mb, cos_t, sin_t):
    T = x2.shape[0]
    tm = MB_BLOCK
    nb = T // tm
    return pl.pallas_call(
        _moba_proj_kernel,
        grid=(nb,),
        in_specs=[
            pl.BlockSpec((tm, D_MODEL), lambda i: (i, 0)),
            _const_spec(w_mb.shape),
            pl.BlockSpec((tm, LANES), lambda i: (i, 0)),
            pl.BlockSpec((tm, LANES), lambda i: (i, 0)),
        ],
        out_specs=[
            pl.BlockSpec((MB_HEADS, tm, MB_HD), lambda i: (0, i, 0)),
            pl.BlockSpec((1, MB_KV_HEADS, tm, 2 * MB_HD), lambda i: (i, 0, 0, 0)),
            pl.BlockSpec((1, MB_KV_HEADS, MB_VROWS, tm), lambda i: (i, 0, 0, 0)),
            pl.BlockSpec((1, MB_KV_HEADS, MB_HD), lambda i: (i, 0, 0)),
        ],
        out_shape=[
            jax.ShapeDtypeStruct((MB_HEADS, T, MB_HD), F32),
            jax.ShapeDtypeStruct((nb, MB_KV_HEADS, tm, 2 * MB_HD), BF16),
            jax.ShapeDtypeStruct((nb, MB_KV_HEADS, MB_VROWS, tm), BF16),
            jax.ShapeDtypeStruct((nb, MB_KV_HEADS, MB_HD), F32),
        ],
        compiler_params=pltpu.CompilerParams(dimension_semantics=("arbitrary",),
                                             vmem_limit_bytes=VMEM_LIMIT),
        name="moba_proj",
    )(x2, w_mb, cos_t, sin_t)


def _unit_lower_inverse(a_list):
    c = a_list[0].shape[0]
    rr = lax.broadcasted_iota(jnp.int32, (c, c), 0)
    cc = lax.broadcasted_iota(jnp.int32, (c, c), 1)
    eye = jnp.where(rr == cc, 1.0, 0.0).astype(F32)
    inv = [eye - a for a in a_list]
    p = [_bdot(a, a) for a in a_list]
    n = 2
    while True:
        inv = [x + _bdot(x, y) for x, y in zip(inv, p)]
        n *= 2
        if n >= c:
            break
        p = [_bdot(y, y) for y in p]
    return inv


GDN_STEP_CHUNKS = 4


def _gdn_kernel(q_ref, k_ref, v_ref, pack_ref, gct_ref, nw_ref, o_ref, s_ref):
    C = GDN_CHUNK
    H = range(GDN_HEADS)
    P = [(c, h) for c in range(GDN_STEP_CHUNKS) for h in H]

    @pl.when(pl.program_id(0) == 0)
    def _():
        s_ref[...] = jnp.zeros_like(s_ref)

    rr = lax.broadcasted_iota(jnp.int32, (C, C), 0)
    cc = lax.broadcasted_iota(jnp.int32, (C, C), 1)
    tril = rr >= cc
    strict = rr > cc
    pack = pack_ref[...]
    nw = nw_ref[...]

    def rows(c):
        return slice(c * C, (c + 1) * C)

    def col(base, c, h):
        return pack[rows(c), base + h:base + h + 1]

    q = {(c, h): q_ref[h, rows(c), :] for c, h in P}
    k = {(c, h): k_ref[h, rows(c), :] for c, h in P}
    kb = {(c, h): k[c, h] * col(PK_BETA, c, h) for c, h in P}
    decay = {(c, h): jnp.where(tril, jnp.exp(jnp.where(tril, col(PK_GC, c, h) - gct_ref[c, h:h + 1, :], 0.0)), 0.0)
             for c, h in P}
    sc = {p: _bdot_nt(jnp.concatenate([kb[p], q[p]], axis=0), k[p]) for p in P}
    a = {p: jnp.where(strict, sc[p][:C] * decay[p], 0.0) for p in P}
    aqk = {p: sc[p][C:] * decay[p] for p in P}
    tinv = dict(zip(P, _unit_lower_inverse([a[p] for p in P])))
    rhs = {(c, h): jnp.concatenate([v_ref[h, rows(c), :] * col(PK_BETA, c, h), kb[c, h] * col(PK_EG, c, h)],
                                   axis=1) for c, h in P}
    sol = {p: _bdot(tinv[p], rhs[p]) for p in P}

    s = [s_ref[h] for h in H]
    for c in range(GDN_STEP_CHUNKS):
        wq = [_bdot(jnp.concatenate([sol[c, h][:, GDN_DV:], q[c, h] * col(PK_EG, c, h)], axis=0), s[h])
              for h in H]
        v_new = [sol[c, h][:, :GDN_DV] - wq[h][:C] for h in H]
        o = [wq[h][C:] + _bdot(aqk[c, h], v_new[h]) for h in H]
        ds = [_bdot_tn(k[c, h] * col(PK_EGD, c, h), v_new[h]) for h in H]
        for h in H:
            s[h] = s[h] * pack[c * C:c * C + 1, PK_EGL + h:PK_EGL + h + 1] + ds[h]
            on = o[h] * lax.rsqrt(jnp.mean(o[h] * o[h], axis=-1, keepdims=True) + 1e-6) * nw
            o_ref[rows(c), h * GDN_DV:(h + 1) * GDN_DV] = on
    for h in H:
        s_ref[h] = s[h]


def _gdn(q, k, v, pack, gct3, norm_w_row):
    T = q.shape[1]
    C = GDN_CHUNK * GDN_STEP_CHUNKS
    hd_spec = pl.BlockSpec((GDN_HEADS, C, LANES), lambda i: (0, i, 0))
    return pl.pallas_call(
        _gdn_kernel,
        grid=(T // C,),
        in_specs=[hd_spec, hd_spec, hd_spec,
                  pl.BlockSpec((C, LANES), lambda i: (i, 0)),
                  pl.BlockSpec((GDN_STEP_CHUNKS, GDN_HEADS, GDN_CHUNK), lambda i: (i, 0, 0)),
                  _const_spec((1, GDN_DV))],
        out_specs=pl.BlockSpec((C, GDN_HEADS * GDN_DV), lambda i: (i, 0)),
        out_shape=jax.ShapeDtypeStruct((T, GDN_HEADS * GDN_DV), F32),
        scratch_shapes=[pltpu.VMEM((GDN_HEADS, GDN_DK, GDN_DV), F32)],
        compiler_params=pltpu.CompilerParams(dimension_semantics=("arbitrary",)),
        name="gdn",
    )(q, k, v, pack, gct3, norm_w_row)


MB_QSPLIT = 2
MB_TRIP = 4
MB_MAX_BLOCKS = MB_HD


def _moba_kernel(q_ref, k_ref, vt_ref, km_ref, o_ref, qx_ref, sa_ref, sb_ref, acc_ref, m_ref):
    qi = pl.program_id(1)
    nb = k_ref.shape[0]
    bs = MB_BLOCK
    nq = MB_REP * bs
    nbp = MB_MAX_BLOCKS
    c = (MB_HD ** -0.5) * math.log2(math.e)

    qf = q_ref[...].reshape(nq, MB_HD)

    gate = lax.dot_general(km_ref[...], qf, (((1,), (1,)), ((), ())),
                           precision=lax.Precision.HIGHEST, preferred_element_type=F32)
    blk = lax.broadcasted_iota(jnp.int32, (nbp, nq), 0).astype(F32)
    qif = qi.astype(F32)
    gate = jnp.where(blk < qif, gate, -jnp.inf)
    sel = jnp.where(blk == qif, 1.0, 0.0)
    for _ in range(MB_TOPK):
        top = jnp.max(gate, axis=0, keepdims=True)
        first = jnp.min(jnp.where(gate == top, blk, float(nbp)), axis=0, keepdims=True)
        hit = (blk == first) & (top > -jnp.inf)
        sel = jnp.where(hit, 1.0, sel)
        gate = jnp.where(hit, -jnp.inf, gate)
    bias_t = jnp.where(sel > 0.0, 0.0, MASK_NEG)
    qc = qf * c
    qx_ref[0] = jnp.concatenate([qc, bias_t.T], axis=1).astype(BF16)
    qx_ref[1] = jnp.concatenate([qc, jnp.full((nq, nbp), MASK_NEG, F32)], axis=1).astype(BF16)

    w = nq // MB_QSPLIT
    groups = [slice(h * w, (h + 1) * w) for h in range(MB_QSPLIT)]

    def scores_into(dst_ref, t):
        kx = k_ref[jnp.minimum(t, nb - 1)]
        qset = jnp.where(t < qi, 0, 1)
        for lanes in groups:
            dst_ref[:, lanes] = lax.dot_general(kx, qx_ref[qset, lanes, :], (((1,), (1,)), ((), ())),
                                                preferred_element_type=F32)

    st = lax.dot_general(k_ref[qi], qx_ref[0], (((1,), (1,)), ((), ())), preferred_element_type=F32)
    kpos = lax.broadcasted_iota(jnp.int32, (bs, nq), 0)
    qpos = lax.broadcasted_iota(jnp.int32, (bs, nq), 1) % bs
    st = jnp.where(kpos <= qpos, st, MASK_NEG)
    m0 = jnp.max(st, axis=0, keepdims=True)
    p = jnp.exp2(st - m0)
    m_ref[...] = m0
    acc_ref[...] = jnp.dot(vt_ref[qi], p.astype(BF16), preferred_element_type=F32)

    def absorb(src_ref, t):
        vt = vt_ref[jnp.minimum(t, nb - 1)]
        for lanes in groups:
            st = src_ref[:, lanes]
            m_old = m_ref[:, lanes]
            m_new = jnp.maximum(m_old, jnp.max(st, axis=0, keepdims=True))
            alpha = jnp.exp2(m_old - m_new)
            p = jnp.exp2(st - m_new)
            m_ref[:, lanes] = m_new
            acc_ref[:, lanes] = alpha * acc_ref[:, lanes] + jnp.dot(vt, p.astype(BF16),
                                                                    preferred_element_type=F32)

    scores_into(sa_ref, 0)

    def body(i, carry):
        t = MB_TRIP * i
        for u in range(0, MB_TRIP, 2):
            scores_into(sb_ref, t + u + 1)
            absorb(sa_ref, t + u)
            scores_into(sa_ref, t + u + 2)
            absorb(sb_ref, t + u + 1)
        return carry

    lax.fori_loop(0, (qi + MB_TRIP - 1) // MB_TRIP, body, 0)

    out_t = acc_ref[0:MB_HD, :] / acc_ref[MB_HD:MB_HD + 1, :]
    for r in range(MB_REP):
        o_ref[:, r * MB_HD:(r + 1) * MB_HD] = out_t[:, r * bs:(r + 1) * bs].T


def _moba(mq, mk, mvt, kmean):
    T = mq.shape[1]
    bs = MB_BLOCK
    nb = T // bs
    nq = MB_REP * bs
    assert nb <= MB_MAX_BLOCKS
    kmean = jnp.pad(kmean, ((0, 0), (0, MB_MAX_BLOCKS - nb), (0, 0)))
    return pl.pallas_call(
        _moba_kernel,
        grid=(MB_KV_HEADS, nb),
        in_specs=[
            pl.BlockSpec((MB_REP, bs, MB_HD), lambda g, i: (g, i, 0)),
            pl.BlockSpec((nb, None, bs, 2 * MB_HD), lambda g, i: (0, g, 0, 0)),
            pl.BlockSpec((nb, None, MB_VROWS, bs), lambda g, i: (0, g, 0, 0)),
            pl.BlockSpec((None, MB_MAX_BLOCKS, MB_HD), lambda g, i: (g, 0, 0)),
        ],
        out_specs=pl.BlockSpec((bs, MB_REP * MB_HD), lambda g, i: (i, g)),
        out_shape=jax.ShapeDtypeStruct((T, MB_HEADS * MB_HD), F32),
        scratch_shapes=[pltpu.VMEM((2, nq, 2 * MB_HD), BF16),
                        pltpu.VMEM((bs, nq), F32), pltpu.VMEM((bs, nq), F32),
                        pltpu.VMEM((MB_VROWS, nq), F32),
                        pltpu.VMEM((1, nq), F32)],
        compiler_params=pltpu.CompilerParams(dimension_semantics=("arbitrary", "arbitrary"),
                                             vmem_limit_bytes=VMEM_LIMIT),
        name="moba",
    )(mq, mk, mvt, kmean)


def _post_kernel(x_ref, go_ref, ma_ref, wg_ref, wgp_ref, wmp_ref, wo_ref, g1_ref, b1_ref,
                 rwt_ref, rb_ref, sg_ref, su_ref, sd_ref,
                 hp_ref, res_ref, eidx_ref, wts_ref):
    tm = x_ref.shape[0]
    x = x_ref[...]
    zg = jnp.dot(x.astype(BF16), wg_ref[...], preferred_element_type=F32)
    gate = zg[:, :D_MODEL]
    o = go_ref[...] * (gate * _sigmoid(gate))
    y_gdn = jnp.dot(o.astype(BF16), wgp_ref[...], preferred_element_type=F32)
    y_mb = jnp.dot(ma_ref[...].astype(BF16), wmp_ref[...], preferred_element_type=F32)
    m = _sigmoid(zg[:, D_MODEL:2 * D_MODEL]) * y_gdn + _sigmoid(zg[:, 2 * D_MODEL:]) * y_mb
    mix = jnp.dot(m.astype(BF16), wo_ref[...], preferred_element_type=F32)
    h = _layer_norm(DEEPNORM_ALPHA * x + mix, g1_ref[...], b1_ref[...])
    hb = h.astype(BF16)
    _store_token_tiles(hp_ref, _pack_bf16_pairs(h))

    hs = jnp.dot(hb, sg_ref[...], preferred_element_type=F32)
    hs = hs * _sigmoid(hs) * jnp.dot(hb, su_ref[...], preferred_element_type=F32)
    res_ref[...] = DEEPNORM_ALPHA * h + jnp.dot(hs.astype(BF16), sd_ref[...], preferred_element_type=F32)

    logits = lax.dot_general(rwt_ref[...], h, (((1,), (1,)), ((), ())),
                             precision=lax.Precision.HIGHEST, preferred_element_type=F32)
    scores = _sigmoid(logits)
    choice = scores + rb_ref[...]
    neg = -jnp.inf
    gi = lax.broadcasted_iota(jnp.int32, (GROUP_SIZE, tm), 0).astype(F32)
    gscore = []
    for g in range(N_GROUPS):
        cg = choice[g * GROUP_SIZE:(g + 1) * GROUP_SIZE, :]
        m1 = jnp.max(cg, axis=0, keepdims=True)
        i1 = jnp.min(jnp.where(cg == m1, gi, float(GROUP_SIZE)), axis=0, keepdims=True)
        m2 = jnp.max(jnp.where(gi == i1, neg, cg), axis=0, keepdims=True)
        gscore.append(m1 + m2)
    gs = jnp.concatenate(gscore, axis=0)
    gidx = lax.broadcasted_iota(jnp.int32, (N_GROUPS, tm), 0).astype(F32)
    gsel = jnp.zeros((N_GROUPS, tm), F32)
    for _ in range(TOPK_GROUPS):
        top = jnp.max(gs, axis=0, keepdims=True)
        first = jnp.min(jnp.where(gs == top, gidx, float(N_GROUPS)), axis=0, keepdims=True)
        hit = gidx == first
        gsel = jnp.where(hit, 1.0, gsel)
        gs = jnp.where(hit, neg, gs)
    masked = jnp.concatenate(
        [jnp.where(gsel[g:g + 1, :] > 0.0, choice[g * GROUP_SIZE:(g + 1) * GROUP_SIZE, :], neg)
         for g in range(N_GROUPS)], axis=0)
    ei = lax.broadcasted_iota(jnp.int32, (N_EXPERTS, tm), 0).astype(F32)
    idx_rows, w_rows = [], []
    for _ in range(TOP_K):
        top = jnp.max(masked, axis=0, keepdims=True)
        first = jnp.min(jnp.where(masked == top, ei, float(N_EXPERTS)), axis=0, keepdims=True)
        hit = ei == first
        idx_rows.append(first)
        w_rows.append(jnp.sum(jnp.where(hit, scores, 0.0), axis=0, keepdims=True))
        masked = jnp.where(hit, neg, masked)
    w = jnp.concatenate(w_rows, axis=0)
    w = w / (jnp.sum(w, axis=0, keepdims=True) + 1e-20) * ROUTED_SCALE
    eidx_ref[...] = jnp.concatenate(idx_rows, axis=0).astype(jnp.int32)
    wts_ref[...] = w


def _post(x2, gdn_o, moba_a, w_gates, w_gdn_proj, w_moba_proj, w_out, ln_g, ln_b,
          router_wt, router_b_col, sh_g, sh_u, sh_d):
    T = x2.shape[0]
    tm = POST_TILE
    row_spec = pl.BlockSpec((tm, D_MODEL), lambda i: (i, 0))
    consts = [w_gates, w_gdn_proj, w_moba_proj, w_out, ln_g, ln_b, router_wt, router_b_col,
              sh_g, sh_u, sh_d]
    return pl.pallas_call(
        _post_kernel,
        grid=(T // tm,),
        in_specs=[row_spec, row_spec, row_spec] + [_const_spec(c.shape) for c in consts],
        out_specs=[pl.BlockSpec((tm * HP_ROWS, LANES), lambda i: (i, 0)), row_spec,
                   pl.BlockSpec((TOP_K, tm), lambda i: (0, i)),
                   pl.BlockSpec((TOP_K, tm), lambda i: (0, i))],
        out_shape=[jax.ShapeDtypeStruct((T * HP_ROWS, LANES), jnp.uint32),
                   jax.ShapeDtypeStruct((T, D_MODEL), F32),
                   jax.ShapeDtypeStruct((TOP_K, T), jnp.int32),
                   jax.ShapeDtypeStruct((TOP_K, T), F32)],
        compiler_params=pltpu.CompilerParams(dimension_semantics=("arbitrary",),
                                             vmem_limit_bytes=VMEM_LIMIT),
        name="post",
    )(x2, gdn_o, moba_a, *consts)


ROW_UNROLL = 8


def _experts_kernel(iblk_ref, iexp_ref, ilo_ref, ihi_ref, src_ref, dst_ref,
                    hp_ref, wg_ref, wu_ref, wd_ref, out_ref,
                    xbuf, ybuf, wgb, wub, wdb, cast_exp, ssem):
    p = pl.program_id(0)
    n_items = pl.num_programs(0)
    w = iblk_ref[p]
    slot = w % 2
    first = (p == 0) | (w != iblk_ref[jnp.maximum(p - 1, 0)])
    last = (p == n_items - 1) | (w != iblk_ref[jnp.minimum(p + 1, n_items - 1)])
    lo, hi = ilo_ref[p], ihi_ref[p]
    block_rows = MOE_BLOCK * TOK_ROWS

    def tile_at(ref, row0, rows):
        return ref.at[pl.ds(pl.multiple_of(row0, rows), rows)]

    def for_block_rows(fn):
        def group(g, c):
            for u in range(ROW_UNROLL):
                fn(g * ROW_UNROLL + u)
            return c
        lax.fori_loop(0, MOE_BLOCK // ROW_UNROLL, group, 0)

    def gather_block():
        def one(i):
            xbuf[pl.ds(pl.multiple_of(i * HP_ROWS, HP_ROWS), HP_ROWS), :] = (
                hp_ref[pl.ds(pl.multiple_of(src_ref[0, 0, i], HP_ROWS), HP_ROWS), :])
        for_block_rows(one)

    def start_scatter(s):
        for_block_rows(lambda i: pltpu.make_async_copy(
            tile_at(ybuf.at[s], i * TOK_ROWS, TOK_ROWS), tile_at(out_ref, dst_ref[0, 0, i], TOK_ROWS),
            ssem.at[s]).start())

    def wait_scatter(s):
        pltpu.make_async_copy(ybuf.at[s], out_ref.at[pl.ds(0, block_rows)], ssem.at[s]).wait()

    @pl.when(p == 0)
    def _():
        ybuf[...] = jnp.zeros_like(ybuf)
        cast_exp[0] = -1

    @pl.when(first)
    def _():
        @pl.when(w >= 2)
        def _():
            wait_scatter(slot)
        gather_block()

    @pl.when(hi > lo)
    def _():
        @pl.when(iexp_ref[p] != cast_exp[0])
        def _():
            wgb[...] = wg_ref[...].astype(BF16)
            wub[...] = wu_ref[...].astype(BF16)
            wdb[...] = wd_ref[...].astype(BF16)
            cast_exp[0] = iexp_ref[p]

        xb = _unpack_bf16_pairs(_load_token_tiles(xbuf, MOE_BLOCK)).astype(BF16)
        hg = jnp.dot(xb, wgb[...], preferred_element_type=F32)
        hu = jnp.dot(xb, wub[...], preferred_element_type=F32)
        hb = (hg * _sigmoid(hg) * hu).astype(BF16)
        y = jnp.dot(hb, wdb[...], preferred_element_type=F32)
        row = lax.broadcasted_iota(jnp.int32, (MOE_BLOCK, 1), 0)
        mine = (row >= lo) & (row < hi)
        _store_token_tiles(ybuf.at[slot], jnp.where(mine, y, _load_token_tiles(ybuf.at[slot], MOE_BLOCK)))

    @pl.when(last)
    def _():
        start_scatter(slot)

    @pl.when(p == n_items - 1)
    def _():
        @pl.when(w >= 1)
        def _():
            wait_scatter(1 - slot)
        wait_scatter(slot)


def _experts(hp, src_rows, dst_rows, item_blk, item_exp, item_lo, item_hi, wg, wu, wd):
    T = hp.shape[0] // HP_ROWS
    n_items = item_blk.shape[0]
    rows = MOE_BLOCK
    ids_spec = pl.BlockSpec((1, 1, rows), lambda p, ib, ie, il, ih: (ib[p], 0, 0), memory_space=pltpu.SMEM)

    def w_spec(shape):
        return pl.BlockSpec((None,) + shape, lambda p, ib, ie, il, ih: (ie[p], 0, 0))

    grid_spec = pltpu.PrefetchScalarGridSpec(
        num_scalar_prefetch=4,
        grid=(n_items,),
        in_specs=[ids_spec, ids_spec, _const_spec(hp.shape),
                  w_spec((D_MODEL, EXPERT_DIM)), w_spec((D_MODEL, EXPERT_DIM)), w_spec((EXPERT_DIM, D_MODEL))],
        out_specs=pl.BlockSpec(memory_space=pl.ANY),
        scratch_shapes=[pltpu.VMEM((rows * HP_ROWS, LANES), jnp.uint32),
                        pltpu.VMEM((2, rows * TOK_ROWS, LANES), F32),
                        pltpu.VMEM((D_MODEL, EXPERT_DIM), BF16), pltpu.VMEM((D_MODEL, EXPERT_DIM), BF16),
                        pltpu.VMEM((EXPERT_DIM, D_MODEL), BF16),
                        pltpu.SMEM((1,), jnp.int32),
                        pltpu.SemaphoreType.DMA((2,))],
    )
    return pl.pallas_call(
        _experts_kernel,
        grid_spec=grid_spec,
        out_shape=jax.ShapeDtypeStruct((TOP_K * T * TOK_ROWS, LANES), F32),
        compiler_params=pltpu.CompilerParams(dimension_semantics=("arbitrary",),
                                             vmem_limit_bytes=VMEM_LIMIT,
                                             disable_bounds_checks=True),
        name="experts",
    )(item_blk, item_exp, item_lo, item_hi, src_rows, dst_rows, hp, wg, wu, wd)


def _combine_kernel(res_ref, y_ref, w_ref, g_ref, b_ref, o_ref):
    tm = res_ref.shape[0]
    acc = res_ref[...]
    w = w_ref[...]
    for k in range(TOP_K):
        acc = acc + w[:, k:k + 1] * _load_token_tiles(y_ref.at[k], tm)
    o_ref[...] = _layer_norm(acc, g_ref[...], b_ref[...])


def _combine(res, y3, wts, ln_g, ln_b):
    T = res.shape[0]
    tm = COMBINE_TILE
    return pl.pallas_call(
        _combine_kernel,
        grid=(T // tm,),
        in_specs=[pl.BlockSpec((tm, D_MODEL), lambda i: (i, 0)),
                  pl.BlockSpec((TOP_K, tm * TOK_ROWS, LANES), lambda i: (0, i, 0)),
                  pl.BlockSpec((tm, TOP_K), lambda i: (i, 0)),
                  _const_spec((1, D_MODEL)), _const_spec((1, D_MODEL))],
        out_specs=pl.BlockSpec((tm, D_MODEL), lambda i: (i, 0)),
        out_shape=jax.ShapeDtypeStruct((T, D_MODEL), F32),
        compiler_params=pltpu.CompilerParams(dimension_semantics=("arbitrary",),
                                             vmem_limit_bytes=VMEM_LIMIT),
        name="combine",
    )(res, y3, wts, ln_g, ln_b)


def _dispatch_plan(eidx_t, n_tok):
    nk = n_tok * TOP_K
    n_blocks = nk // MOE_BLOCK
    e_flat = eidx_t.T.reshape(nk)
    _, order = lax.sort((e_flat, jnp.arange(nk, dtype=jnp.int32)), num_keys=1)
    experts = jnp.arange(N_EXPERTS, dtype=jnp.int32)
    counts = jnp.sum((e_flat[None, :] == experts[:, None]).astype(jnp.int32), axis=1)
    start = jnp.cumsum(counts) - counts
    pos = jnp.sort(jnp.concatenate([jnp.arange(n_blocks, dtype=jnp.int32) * MOE_BLOCK, start]))
    nxt = jnp.concatenate([pos[1:], jnp.full((1,), nk, jnp.int32)])
    item_blk = jnp.minimum(pos // MOE_BLOCK, n_blocks - 1)
    item_exp = jnp.sum((start[None, :] <= pos[:, None]).astype(jnp.int32), axis=1) - 1
    item_lo = pos - item_blk * MOE_BLOCK
    item_hi = jnp.minimum(nxt, (item_blk + 1) * MOE_BLOCK) - item_blk * MOE_BLOCK
    tok, slot_k = order // TOP_K, order % TOP_K
    shape3 = (n_blocks, 1, MOE_BLOCK)
    src_rows = (tok * HP_ROWS).reshape(shape3)
    dst_rows = ((slot_k * n_tok + tok) * TOK_ROWS).reshape(shape3)
    return src_rows, dst_rows, item_blk, item_exp, item_lo, item_hi


def kernel(x, w_in, conv_w, gdn_a_log, gdn_dt_bias, gdn_norm_w, w_gdn_proj, w_moba_proj, w_out,
           ln1_g, ln1_b, router_w, router_bias, exp_w_gate, exp_w_up, exp_w_down,
           sh_w_gate, sh_w_up, sh_w_down, ln2_g, ln2_b):
    B, T, D = x.shape
    assert B == 1 and D == D_MODEL and T % MB_BLOCK == 0
    x2 = x.reshape(T, D)

    o_gate = CONV_CH
    o_b = o_gate + GDN_HEADS * GDN_DV
    o_a = o_b + GDN_HEADS
    o_mq = o_a + GDN_HEADS
    o_gg = o_mq + (MB_HEADS + 2 * MB_KV_HEADS) * MB_HD
    w_qkv = w_in[:, :CONV_CH].astype(BF16)
    w_pack = jnp.concatenate([w_in[:, o_b:o_a]] + [w_in[:, o_a:o_mq]] * 4
                             + [jnp.zeros((D, LANES - 5 * GDN_HEADS), F32)], axis=1).astype(BF16)
    w_mb = w_in[:, o_mq:o_gg].astype(BF16)
    w_gates = jnp.concatenate([w_in[:, o_gate:o_b], w_in[:, o_gg:]], axis=1).astype(BF16)

    def lane_row(v):
        return jnp.zeros((1, LANES), F32).at[0, PK_GC:PK_GC + 4 * GDN_HEADS].set(jnp.tile(v.astype(F32), 4))

    q, k, v, pack, gct = _qkv_proj(x2, w_qkv, w_pack, conv_w.astype(F32),
                                   lane_row(gdn_a_log), lane_row(gdn_dt_bias))
    gct3 = gct.reshape(GDN_HEADS, T // GDN_CHUNK, GDN_CHUNK).transpose(1, 0, 2)
    gdn_o = _gdn(q, k, v, pack, gct3, gdn_norm_w.astype(F32).reshape(1, GDN_DV))

    half = ROT_DIM // 2
    inv = ROPE_THETA ** (-jnp.arange(half, dtype=F32) / half)
    ang = jnp.arange(T).astype(F32)[:, None] * inv[None, :]
    ones = jnp.ones((T, MB_HD - ROT_DIM), F32)
    cos_t = jnp.concatenate([jnp.cos(ang), jnp.cos(ang), ones], axis=1)
    sin_t = jnp.concatenate([-jnp.sin(ang), jnp.sin(ang), 0.0 * ones], axis=1)
    mq, mk, mvt, kmean = _moba_proj(x2, w_mb, cos_t, sin_t)
    moba_a = _moba(mq, mk, mvt, kmean.transpose(1, 0, 2))

    hp, res, eidx_t, wts_t = _post(
        x2, gdn_o, moba_a, w_gates, w_gdn_proj.astype(BF16), w_moba_proj.astype(BF16),
        w_out.astype(BF16), ln1_g.reshape(1, D), ln1_b.reshape(1, D),
        router_w.T, router_bias.reshape(N_EXPERTS, 1),
        sh_w_gate.astype(BF16), sh_w_up.astype(BF16), sh_w_down.astype(BF16))

    plan = _dispatch_plan(eidx_t, T)
    y = _experts(hp, *plan, exp_w_gate, exp_w_up, exp_w_down)
    y3 = y.reshape(TOP_K, T * TOK_ROWS, LANES)
    out = _combine(res, y3, wts_t.T, ln2_g.reshape(1, D), ln2_b.reshape(1, D))
    return out.reshape(B, T, D)
```

```python
import functools
import math

import jax
import jax.numpy as jnp
from jax import lax
from jax.experimental import pallas as pl
from jax.experimental.pallas import tpu as pltpu

F32 = jnp.float32
BF16 = jnp.bfloat16

D_MODEL = 1024
DEPTH = 1
GDN_HEADS = 8
GDN_DK = 128
GDN_DV = 128
GDN_CONV = 4
GDN_CHUNK = 64
MB_HEADS = 8
MB_KV_HEADS = 2
MB_REP = MB_HEADS // MB_KV_HEADS
MB_HD = 128
MB_BLOCK = 256
MB_TOPK = 3
ROT_DIM = MB_HD // 4
ROPE_THETA = 500000.0
N_EXPERTS = 256
TOP_K = 8
N_GROUPS = 8
GROUP_SIZE = N_EXPERTS // N_GROUPS
TOPK_GROUPS = 4
EXPERT_DIM = 256
SHARED_DIM = 256
ROUTED_SCALE = 2.5
MOE_BLOCK = 256
DEEPNORM_ALPHA = (2.0 * DEPTH) ** 0.25
LN_EPS = 1e-5
MASK_NEG = -1e30

QK_COLS = GDN_HEADS * GDN_DK
CONV_CH = 2 * QK_COLS + GDN_HEADS * GDN_DV
LANES = 128
SUBLANES = 8
VMEM_LIMIT = 56 * 1024 * 1024

TOK_TILE = 256
POST_TILE = 512
COMBINE_TILE = 256

PK_BETA, PK_GC, PK_EG, PK_EGD, PK_EGL = 0, 8, 16, 24, 32


def _sigmoid(x):
    return 1.0 / (1.0 + jnp.exp(-x))


def _softplus(x):
    return jnp.maximum(x, 0.0) + jnp.log(1.0 + jnp.exp(-jnp.abs(x)))


def _bdot(a, b):
    return jnp.dot(a.astype(BF16), b.astype(BF16), preferred_element_type=F32)


def _bdot_nt(a, b):
    return lax.dot_general(a.astype(BF16), b.astype(BF16), (((1,), (1,)), ((), ())),
                           preferred_element_type=F32)


def _bdot_tn(a, b):
    return lax.dot_general(a.astype(BF16), b.astype(BF16), (((0,), (0,)), ((), ())),
                           preferred_element_type=F32)


def _layer_norm(x, g, b):
    mu = jnp.mean(x, axis=-1, keepdims=True)
    xc = x - mu
    var = jnp.mean(xc * xc, axis=-1, keepdims=True)
    return xc * lax.rsqrt(var + LN_EPS) * g + b


TOK_ROWS = D_MODEL // LANES


HP_ROWS = TOK_ROWS // 2


def _store_token_tiles(ref, x):
    n, r = x.shape[0], x.shape[1] // LANES
    for j in range(r):
        ref[pl.ds(j, n, stride=r), :] = x[:, j * LANES:(j + 1) * LANES]


def _load_token_tiles(ref, n):
    r = ref.shape[0] // n
    return jnp.concatenate([ref[pl.ds(j, n, stride=r), :] for j in range(r)], axis=1)


def _pack_bf16_pairs(x):
    half = x.shape[1] // 2
    bits = pltpu.bitcast(x.astype(BF16).astype(F32), jnp.uint32)
    return (bits[:, :half] >> 16) | (bits[:, half:] & jnp.uint32(0xFFFF0000))


def _unpack_bf16_pairs(w):
    lo = pltpu.bitcast(w << 16, F32)
    hi = pltpu.bitcast(w & jnp.uint32(0xFFFF0000), F32)
    return jnp.concatenate([lo, hi], axis=1)


def _const_spec(shape):
    nd = len(shape)
    return pl.BlockSpec(shape, lambda *_: (0,) * nd, pipeline_mode=pl.Buffered(1))


def _qkv_kernel(x_ref, w_ref, wp_ref, cw_ref, alog_ref, dtb_ref,
                q_ref, k_ref, v_ref, pack_ref, gct_ref, z_ref):
    tm = x_ref.shape[0]
    halo = SUBLANES

    @pl.when(pl.program_id(0) == 0)
    def _():
        z_ref[0:halo, :] = jnp.zeros((halo, CONV_CH), F32)

    xb = x_ref[...].astype(BF16)
    z_ref[halo:halo + tm, :] = jnp.dot(xb, w_ref[...], preferred_element_type=F32)

    for c in range(CONV_CH // LANES):
        cols = slice(c * LANES, (c + 1) * LANES)
        acc = z_ref[halo:halo + tm, cols] * cw_ref[GDN_CONV - 1:GDN_CONV, cols]
        for s in range(1, GDN_CONV):
            acc = acc + z_ref[halo - s:halo - s + tm, cols] * cw_ref[GDN_CONV - 1 - s:GDN_CONV - s, cols]
        y = acc * _sigmoid(acc)
        h = c % GDN_HEADS
        if c < GDN_HEADS:
            y = y * lax.rsqrt(jnp.sum(y * y, axis=-1, keepdims=True) + 1e-6) * (GDN_DK ** -0.5)
            q_ref[h] = y
        elif c < 2 * GDN_HEADS:
            y = y * lax.rsqrt(jnp.sum(y * y, axis=-1, keepdims=True) + 1e-6)
            k_ref[h] = y
        else:
            v_ref[h] = y

    z_ref[0:halo, :] = z_ref[tm:tm + halo, :]

    zp = jnp.dot(xb, wp_ref[...], preferred_element_type=F32)
    beta = _sigmoid(zp)
    g = -jnp.exp(alog_ref[...]) * _softplus(zp + dtb_ref[...])
    row = lax.broadcasted_iota(jnp.int32, (tm, LANES), 0) % GDN_CHUNK
    gc = g
    step = 1
    while step < GDN_CHUNK:
        gc = gc + jnp.where(row >= step, pltpu.roll(gc, step, axis=0), 0.0)
        step *= 2
    gl = gc.reshape(tm // GDN_CHUNK, GDN_CHUNK, LANES)[:, GDN_CHUNK - 1:GDN_CHUNK, :]
    gl = jnp.broadcast_to(gl, (tm // GDN_CHUNK, GDN_CHUNK, LANES)).reshape(tm, LANES)
    lane = lax.broadcasted_iota(jnp.int32, (tm, LANES), 1)
    pack = jnp.where(lane < PK_GC, beta,
           jnp.where(lane < PK_EG, gc,
           jnp.where(lane < PK_EGD, jnp.exp(gc),
           jnp.where(lane < PK_EGL, jnp.exp(gl - gc), jnp.exp(gl)))))
    pack_ref[...] = pack
    gct_ref[...] = pack.T[PK_GC:PK_GC + GDN_HEADS, :]


def _qkv_proj(x2, w_qkv, w_pack, conv_w, alog_row, dtb_row):
    T = x2.shape[0]
    tm = TOK_TILE
    hd_spec = pl.BlockSpec((GDN_HEADS, tm, LANES), lambda i: (0, i, 0))
    hd_shape = jax.ShapeDtypeStruct((GDN_HEADS, T, LANES), F32)
    return pl.pallas_call(
        _qkv_kernel,
        grid=(T // tm,),
        in_specs=[
            pl.BlockSpec((tm, D_MODEL), lambda i: (i, 0)),
            _const_spec((D_MODEL, CONV_CH)),
            _const_spec((D_MODEL, LANES)),
            _const_spec((GDN_CONV, CONV_CH)),
            _const_spec((1, LANES)),
            _const_spec((1, LANES)),
        ],
        out_specs=[hd_spec, hd_spec, hd_spec,
                   pl.BlockSpec((tm, LANES), lambda i: (i, 0)),
                   pl.BlockSpec((GDN_HEADS, tm), lambda i: (0, i))],
        out_shape=[hd_shape, hd_shape, hd_shape,
                   jax.ShapeDtypeStruct((T, LANES), F32),
                   jax.ShapeDtypeStruct((GDN_HEADS, T), F32)],
        scratch_shapes=[pltpu.VMEM((tm + SUBLANES, CONV_CH), F32)],
        compiler_params=pltpu.CompilerParams(dimension_semantics=("arbitrary",),
                                             vmem_limit_bytes=VMEM_LIMIT),
        name="qkv_proj",
    )(x2, w_qkv, w_pack, conv_w, alog_row, dtb_row)


MB_LROWS = 16
MB_VROWS = MB_HD + MB_LROWS


def _rope(xh, cos_t, sin_t, lane):
    half = ROT_DIM // 2
    swapped = jnp.where(lane < half, pltpu.roll(xh, LANES - half, axis=1), pltpu.roll(xh, half, axis=1))
    return xh * cos_t + swapped * sin_t


def _moba_proj_kernel(x_ref, w_ref, cos_ref, sin_ref, q_ref, k_ref, vt_ref, km_ref):
    tm = x_ref.shape[0]
    z = jnp.dot(x_ref[...].astype(BF16), w_ref[...], preferred_element_type=F32)
    cos_t = cos_ref[...]
    sin_t = sin_ref[...]
    lane = lax.broadcasted_iota(jnp.int32, (tm, LANES), 1)
    for h in range(MB_HEADS):
        q_ref[h] = _rope(z[:, h * MB_HD:(h + 1) * MB_HD], cos_t, sin_t, lane)
    koff = MB_HEADS * MB_HD
    voff = koff + MB_KV_HEADS * MB_HD
    blk_onehot = jnp.where(lane == pl.program_id(0), 1.0, 0.0).astype(BF16)
    for g in range(MB_KV_HEADS):
        kr = _rope(z[:, koff + g * MB_HD:koff + (g + 1) * MB_HD], cos_t, sin_t, lane)
        k_ref[0, g] = jnp.concatenate([kr.astype(BF16), blk_onehot], axis=1)
        km_ref[0, g:g + 1, :] = jnp.mean(kr, axis=0, keepdims=True)
        vt_ref[0, g] = jnp.concatenate([z[:, voff + g * MB_HD:voff + (g + 1) * MB_HD].T,
                                        jnp.ones((MB_LROWS, tm), F32)], axis=0).astype(BF16)


def _moba_proj(x2, w_mb, cos_t, sin_t):
    T = x2.shape[0]
    tm = MB_BLOCK
    nb = T // tm
    return pl.pallas_call(
        _moba_proj_kernel,
        grid=(nb,),
        in_specs=[
            pl.BlockSpec((tm, D_MODEL), lambda i: (i, 0)),
            _const_spec(w_mb.shape),
            pl.BlockSpec((tm, LANES), lambda i: (i, 0)),
            pl.BlockSpec((tm, LANES), lambda i: (i, 0)),
        ],
        out_specs=[
            pl.BlockSpec((MB_HEADS, tm, MB_HD), lambda i: (0, i, 0)),
            pl.BlockSpec((1, MB_KV_HEADS, tm, 2 * MB_HD), lambda i: (i, 0, 0, 0)),
            pl.BlockSpec((1, MB_KV_HEADS, MB_VROWS, tm), lambda i: (i, 0, 0, 0)),
            pl.BlockSpec((1, MB_KV_HEADS, MB_HD), lambda i: (i, 0, 0)),
        ],
        out_shape=[
            jax.ShapeDtypeStruct((MB_HEADS, T, MB_HD), F32),
            jax.ShapeDtypeStruct((nb, MB_KV_HEADS, tm, 2 * MB_HD), BF16),
            jax.ShapeDtypeStruct((nb, MB_KV_HEADS, MB_VROWS, tm), BF16),
            jax.ShapeDtypeStruct((nb, MB_KV_HEADS, MB_HD), F32),
        ],
        compiler_params=pltpu.CompilerParams(dimension_semantics=("arbitrary",),
                                             vmem_limit_bytes=VMEM_LIMIT),
        name="moba_proj",
    )(x2, w_mb, cos_t, sin_t)


def _unit_lower_inverse(a_list):
    c = a_list[0].shape[0]
    rr = lax.broadcasted_iota(jnp.int32, (c, c), 0)
    cc = lax.broadcasted_iota(jnp.int32, (c, c), 1)
    eye = jnp.where(rr == cc, 1.0, 0.0).astype(F32)
    inv = [eye - a for a in a_list]
    p = [_bdot(a, a) for a in a_list]
    n = 2
    while True:
        inv = [x + _bdot(x, y) for x, y in zip(inv, p)]
        n *= 2
        if n >= c:
            break
        p = [_bdot(y, y) for y in p]
    return inv


GDN_STEP_CHUNKS = 4


def _gdn_kernel(q_ref, k_ref, v_ref, pack_ref, gct_ref, nw_ref, o_ref, s_ref):
    C = GDN_CHUNK
    H = range(GDN_HEADS)
    P = [(c, h) for c in range(GDN_STEP_CHUNKS) for h in H]

    @pl.when(pl.program_id(0) == 0)
    def _():
        s_ref[...] = jnp.zeros_like(s_ref)

    rr = lax.broadcasted_iota(jnp.int32, (C, C), 0)
    cc = lax.broadcasted_iota(jnp.int32, (C, C), 1)
    tril = rr >= cc
    strict = rr > cc
    pack = pack_ref[...]
    nw = nw_ref[...]

    def rows(c):
        return slice(c * C, (c + 1) * C)

    def col(base, c, h):
        return pack[rows(c), base + h:base + h + 1]

    q = {(c, h): q_ref[h, rows(c), :] for c, h in P}
    k = {(c, h): k_ref[h, rows(c), :] for c, h in P}
    kb = {(c, h): k[c, h] * col(PK_BETA, c, h) for c, h in P}
    decay = {(c, h): jnp.where(tril, jnp.exp(jnp.where(tril, col(PK_GC, c, h) - gct_ref[c, h:h + 1, :], 0.0)), 0.0)
             for c, h in P}
    sc = {p: _bdot_nt(jnp.concatenate([kb[p], q[p]], axis=0), k[p]) for p in P}
    a = {p: jnp.where(strict, sc[p][:C] * decay[p], 0.0) for p in P}
    aqk = {p: sc[p][C:] * decay[p] for p in P}
    tinv = dict(zip(P, _unit_lower_inverse([a[p] for p in P])))
    rhs = {(c, h): jnp.concatenate([v_ref[h, rows(c), :] * col(PK_BETA, c, h), kb[c, h] * col(PK_EG, c, h)],
                                   axis=1) for c, h in P}
    sol = {p: _bdot(tinv[p], rhs[p]) for p in P}

    s = [s_ref[h] for h in H]
    for c in range(GDN_STEP_CHUNKS):
        wq = [_bdot(jnp.concatenate([sol[c, h][:, GDN_DV:], q[c, h] * col(PK_EG, c, h)], axis=0), s[h])
              for h in H]
        v_new = [sol[c, h][:, :GDN_DV] - wq[h][:C] for h in H]
        o = [wq[h][C:] + _bdot(aqk[c, h], v_new[h]) for h in H]
        ds = [_bdot_tn(k[c, h] * col(PK_EGD, c, h), v_new[h]) for h in H]
        for h in H:
            s[h] = s[h] * pack[c * C:c * C + 1, PK_EGL + h:PK_EGL + h + 1] + ds[h]
            on = o[h] * lax.rsqrt(jnp.mean(o[h] * o[h], axis=-1, keepdims=True) + 1e-6) * nw
            o_ref[rows(c), h * GDN_DV:(h + 1) * GDN_DV] = on
    for h in H:
        s_ref[h] = s[h]


def _gdn(q, k, v, pack, gct3, norm_w_row):
    T = q.shape[1]
    C = GDN_CHUNK * GDN_STEP_CHUNKS
    hd_spec = pl.BlockSpec((GDN_HEADS, C, LANES), lambda i: (0, i, 0))
    return pl.pallas_call(
        _gdn_kernel,
        grid=(T // C,),
        in_specs=[hd_spec, hd_spec, hd_spec,
                  pl.BlockSpec((C, LANES), lambda i: (i, 0)),
                  pl.BlockSpec((GDN_STEP_CHUNKS, GDN_HEADS, GDN_CHUNK), lambda i: (i, 0, 0)),
                  _const_spec((1, GDN_DV))],
        out_specs=pl.BlockSpec((C, GDN_HEADS * GDN_DV), lambda i: (i, 0)),
        out_shape=jax.ShapeDtypeStruct((T, GDN_HEADS * GDN_DV), F32),
        scratch_shapes=[pltpu.VMEM((GDN_HEADS, GDN_DK, GDN_DV), F32)],
        compiler_params=pltpu.CompilerParams(dimension_semantics=("arbitrary",)),
        name="gdn",
    )(q, k, v, pack, gct3, norm_w_row)


MB_QSPLIT = 2
MB_TRIP = 4
MB_MAX_BLOCKS = MB_HD


def _moba_kernel(q_ref, k_ref, vt_ref, km_ref, o_ref, qx_ref, sa_ref, sb_ref, acc_ref, m_ref):
    qi = pl.program_id(1)
    nb = k_ref.shape[0]
    bs = MB_BLOCK
    nq = MB_REP * bs
    nbp = MB_MAX_BLOCKS
    c = (MB_HD ** -0.5) * math.log2(math.e)

    qf = q_ref[...].reshape(nq, MB_HD)

    gate = lax.dot_general(km_ref[...], qf, (((1,), (1,)), ((), ())),
                           precision=lax.Precision.HIGHEST, preferred_element_type=F32)
    blk = lax.broadcasted_iota(jnp.int32, (nbp, nq), 0).astype(F32)
    qif = qi.astype(F32)
    gate = jnp.where(blk < qif, gate, -jnp.inf)
    sel = jnp.where(blk == qif, 1.0, 0.0)
    for _ in range(MB_TOPK):
        top = jnp.max(gate, axis=0, keepdims=True)
        first = jnp.min(jnp.where(gate == top, blk, float(nbp)), axis=0, keepdims=True)
        hit = (blk == first) & (top > -jnp.inf)
        sel = jnp.where(hit, 1.0, sel)
        gate = jnp.where(hit, -jnp.inf, gate)
    bias_t = jnp.where(sel > 0.0, 0.0, MASK_NEG)
    qc = qf * c
    qx_ref[0] = jnp.concatenate([qc, bias_t.T], axis=1).astype(BF16)
    qx_ref[1] = jnp.concatenate([qc, jnp.full((nq, nbp), MASK_NEG, F32)], axis=1).astype(BF16)

    w = nq // MB_QSPLIT
    groups = [slice(h * w, (h + 1) * w) for h in range(MB_QSPLIT)]

    def scores_into(dst_ref, t):
        kx = k_ref[jnp.minimum(t, nb - 1)]
        qset = jnp.where(t < qi, 0, 1)
        for lanes in groups:
            dst_ref[:, lanes] = lax.dot_general(kx, qx_ref[qset, lanes, :], (((1,), (1,)), ((), ())),
                                                preferred_element_type=F32)

    st = lax.dot_general(k_ref[qi], qx_ref[0], (((1,), (1,)), ((), ())), preferred_element_type=F32)
    kpos = lax.broadcasted_iota(jnp.int32, (bs, nq), 0)
    qpos = lax.broadcasted_iota(jnp.int32, (bs, nq), 1) % bs
    st = jnp.where(kpos <= qpos, st, MASK_NEG)
    m0 = jnp.max(st, axis=0, keepdims=True)
    p = jnp.exp2(st - m0)
    m_ref[...] = m0
    acc_ref[...] = jnp.dot(vt_ref[qi], p.astype(BF16), preferred_element_type=F32)

    def absorb(src_ref, t):
        vt = vt_ref[jnp.minimum(t, nb - 1)]
        for lanes in groups:
            st = src_ref[:, lanes]
            m_old = m_ref[:, lanes]
            m_new = jnp.maximum(m_old, jnp.max(st, axis=0, keepdims=True))
            alpha = jnp.exp2(m_old - m_new)
            p = jnp.exp2(st - m_new)
            m_ref[:, lanes] = m_new
            acc_ref[:, lanes] = alpha * acc_ref[:, lanes] + jnp.dot(vt, p.astype(BF16),
                                                                    preferred_element_type=F32)

    scores_into(sa_ref, 0)

    def body(i, carry):
        t = MB_TRIP * i
        for u in range(0, MB_TRIP, 2):
            scores_into(sb_ref, t + u + 1)
            absorb(sa_ref, t + u)
            scores_into(sa_ref, t + u + 2)
            absorb(sb_ref, t + u + 1)
        return carry

    lax.fori_loop(0, (qi + MB_TRIP - 1) // MB_TRIP, body, 0)

    out_t = acc_ref[0:MB_HD, :] / acc_ref[MB_HD:MB_HD + 1, :]
    for r in range(MB_REP):
        o_ref[:, r * MB_HD:(r + 1) * MB_HD] = out_t[:, r * bs:(r + 1) * bs].T


def _moba(mq, mk, mvt, kmean):
    T = mq.shape[1]
    bs = MB_BLOCK
    nb = T // bs
    nq = MB_REP * bs
    assert nb <= MB_MAX_BLOCKS
    kmean = jnp.pad(kmean, ((0, 0), (0, MB_MAX_BLOCKS - nb), (0, 0)))
    return pl.pallas_call(
        _moba_kernel,
        grid=(MB_KV_HEADS, nb),
        in_specs=[
            pl.BlockSpec((MB_REP, bs, MB_HD), lambda g, i: (g, i, 0)),
            pl.BlockSpec((nb, None, bs, 2 * MB_HD), lambda g, i: (0, g, 0, 0)),
            pl.BlockSpec((nb, None, MB_VROWS, bs), lambda g, i: (0, g, 0, 0)),
            pl.BlockSpec((None, MB_MAX_BLOCKS, MB_HD), lambda g, i: (g, 0, 0)),
        ],
        out_specs=pl.BlockSpec((bs, MB_REP * MB_HD), lambda g, i: (i, g)),
        out_shape=jax.ShapeDtypeStruct((T, MB_HEADS * MB_HD), F32),
        scratch_shapes=[pltpu.VMEM((2, nq, 2 * MB_HD), BF16),
                        pltpu.VMEM((bs, nq), F32), pltpu.VMEM((bs, nq), F32),
                        pltpu.VMEM((MB_VROWS, nq), F32),
                        pltpu.VMEM((1, nq), F32)],
        compiler_params=pltpu.CompilerParams(dimension_semantics=("arbitrary", "arbitrary"),
                                             vmem_limit_bytes=VMEM_LIMIT),
        name="moba",
    )(mq, mk, mvt, kmean)


def _post_kernel(x_ref, go_ref, ma_ref, wg_ref, wgp_ref, wmp_ref, wo_ref, g1_ref, b1_ref,
                 rwt_ref, rb_ref, sg_ref, su_ref, sd_ref,
                 hp_ref, res_ref, eidx_ref, wts_ref):
    tm = x_ref.shape[0]
    x = x_ref[...]
    zg = jnp.dot(x.astype(BF16), wg_ref[...], preferred_element_type=F32)
    gate = zg[:, :D_MODEL]
    o = go_ref[...] * (gate * _sigmoid(gate))
    y_gdn = jnp.dot(o.astype(BF16), wgp_ref[...], preferred_element_type=F32)
    y_mb = jnp.dot(ma_ref[...].astype(BF16), wmp_ref[...], preferred_element_type=F32)
    m = _sigmoid(zg[:, D_MODEL:2 * D_MODEL]) * y_gdn + _sigmoid(zg[:, 2 * D_MODEL:]) * y_mb
    mix = jnp.dot(m.astype(BF16), wo_ref[...], preferred_element_type=F32)
    h = _layer_norm(DEEPNORM_ALPHA * x + mix, g1_ref[...], b1_ref[...])
    hb = h.astype(BF16)
    _store_token_tiles(hp_ref, _pack_bf16_pairs(h))

    hs = jnp.dot(hb, sg_ref[...], preferred_element_type=F32)
    hs = hs * _sigmoid(hs) * jnp.dot(hb, su_ref[...], preferred_element_type=F32)
    res_ref[...] = DEEPNORM_ALPHA * h + jnp.dot(hs.astype(BF16), sd_ref[...], preferred_element_type=F32)

    logits = lax.dot_general(rwt_ref[...], h, (((1,), (1,)), ((), ())),
                             precision=lax.Precision.HIGHEST, preferred_element_type=F32)
    scores = _sigmoid(logits)
    choice = scores + rb_ref[...]
    neg = -jnp.inf
    gi = lax.broadcasted_iota(jnp.int32, (GROUP_SIZE, tm), 0).astype(F32)
    gscore = []
    for g in range(N_GROUPS):
        cg = choice[g * GROUP_SIZE:(g + 1) * GROUP_SIZE, :]
        m1 = jnp.max(cg, axis=0, keepdims=True)
        i1 = jnp.min(jnp.where(cg == m1, gi, float(GROUP_SIZE)), axis=0, keepdims=True)
        m2 = jnp.max(jnp.where(gi == i1, neg, cg), axis=0, keepdims=True)
        gscore.append(m1 + m2)
    gs = jnp.concatenate(gscore, axis=0)
    gidx = lax.broadcasted_iota(jnp.int32, (N_GROUPS, tm), 0).astype(F32)
    gsel = jnp.zeros((N_GROUPS, tm), F32)
    for _ in range(TOPK_GROUPS):
        top = jnp.max(gs, axis=0, keepdims=True)
        first = jnp.min(jnp.where(gs == top, gidx, float(N_GROUPS)), axis=0, keepdims=True)
        hit = gidx == first
        gsel = jnp.where(hit, 1.0, gsel)
        gs = jnp.where(hit, neg, gs)
    masked = jnp.concatenate(
        [jnp.where(gsel[g:g + 1, :] > 0.0, choice[g * GROUP_SIZE:(g + 1) * GROUP_SIZE, :], neg)
         for g in range(N_GROUPS)], axis=0)
    ei = lax.broadcasted_iota(jnp.int32, (N_EXPERTS, tm), 0).astype(F32)
    idx_rows, w_rows = [], []
    for _ in range(TOP_K):
        top = jnp.max(masked, axis=0, keepdims=True)
        first = jnp.min(jnp.where(masked == top, ei, float(N_EXPERTS)), axis=0, keepdims=True)
        hit = ei == first
        idx_rows.append(first)
        w_rows.append(jnp.sum(jnp.where(hit, scores, 0.0), axis=0, keepdims=True))
        masked = jnp.where(hit, neg, masked)
    w = jnp.concatenate(w_rows, axis=0)
    w = w / (jnp.sum(w, axis=0, keepdims=True) + 1e-20) * ROUTED_SCALE
    eidx_ref[...] = jnp.concatenate(idx_rows, axis=0).astype(jnp.int32)
    wts_ref[...] = w


def _post(x2, gdn_o, moba_a, w_gates, w_gdn_proj, w_moba_proj, w_out, ln_g, ln_b,
          router_wt, router_b_col, sh_g, sh_u, sh_d):
    T = x2.shape[0]
    tm = POST_TILE
    row_spec = pl.BlockSpec((tm, D_MODEL), lambda i: (i, 0))
    consts = [w_gates, w_gdn_proj, w_moba_proj, w_out, ln_g, ln_b, router_wt, router_b_col,
              sh_g, sh_u, sh_d]
    return pl.pallas_call(
        _post_kernel,
        grid=(T // tm,),
        in_specs=[row_spec, row_spec, row_spec] + [_const_spec(c.shape) for c in consts],
        out_specs=[pl.BlockSpec((tm * HP_ROWS, LANES), lambda i: (i, 0)), row_spec,
                   pl.BlockSpec((TOP_K, tm), lambda i: (0, i)),
                   pl.BlockSpec((TOP_K, tm), lambda i: (0, i))],
        out_shape=[jax.ShapeDtypeStruct((T * HP_ROWS, LANES), jnp.uint32),
                   jax.ShapeDtypeStruct((T, D_MODEL), F32),
                   jax.ShapeDtypeStruct((TOP_K, T), jnp.int32),
                   jax.ShapeDtypeStruct((TOP_K, T), F32)],
        compiler_params=pltpu.CompilerParams(dimension_semantics=("arbitrary",),
                                             vmem_limit_bytes=VMEM_LIMIT),
        name="post",
    )(x2, gdn_o, moba_a, *consts)


ROW_UNROLL = 8


def _experts_kernel(iblk_ref, iexp_ref, ilo_ref, ihi_ref, src_ref, dst_ref,
                    hp_ref, wg_ref, wu_ref, wd_ref, out_ref,
                    xbuf, ybuf, wgb, wub, wdb, cast_exp, ssem):
    p = pl.program_id(0)
    n_items = pl.num_programs(0)
    w = iblk_ref[p]
    slot = w % 2
    first = (p == 0) | (w != iblk_ref[jnp.maximum(p - 1, 0)])
    last = (p == n_items - 1) | (w != iblk_ref[jnp.minimum(p + 1, n_items - 1)])
    lo, hi = ilo_ref[p], ihi_ref[p]
    block_rows = MOE_BLOCK * TOK_ROWS

    def tile_at(ref, row0, rows):
        return ref.at[pl.ds(pl.multiple_of(row0, rows), rows)]

    def for_block_rows(fn):
        def group(g, c):
            for u in range(ROW_UNROLL):
                fn(g * ROW_UNROLL + u)
            return c
        lax.fori_loop(0, MOE_BLOCK // ROW_UNROLL, group, 0)

    def gather_block():
        def one(i):
            xbuf[pl.ds(pl.multiple_of(i * HP_ROWS, HP_ROWS), HP_ROWS), :] = (
                hp_ref[pl.ds(pl.multiple_of(src_ref[0, 0, i], HP_ROWS), HP_ROWS), :])
        for_block_rows(one)

    def start_scatter(s):
        for_block_rows(lambda i: pltpu.make_async_copy(
            tile_at(ybuf.at[s], i * TOK_ROWS, TOK_ROWS), tile_at(out_ref, dst_ref[0, 0, i], TOK_ROWS),
            ssem.at[s]).start())

    def wait_scatter(s):
        pltpu.make_async_copy(ybuf.at[s], out_ref.at[pl.ds(0, block_rows)], ssem.at[s]).wait()

    @pl.when(p == 0)
    def _():
        ybuf[...] = jnp.zeros_like(ybuf)
        cast_exp[0] = -1

    @pl.when(first)
    def _():
        @pl.when(w >= 2)
        def _():
            wait_scatter(slot)
        gather_block()

    @pl.when(hi > lo)
    def _():
        @pl.when(iexp_ref[p] != cast_exp[0])
        def _():
            wgb[...] = wg_ref[...].astype(BF16)
            wub[...] = wu_ref[...].astype(BF16)
            wdb[...] = wd_ref[...].astype(BF16)
            cast_exp[0] = iexp_ref[p]

        xb = _unpack_bf16_pairs(_load_token_tiles(xbuf, MOE_BLOCK)).astype(BF16)
        hg = jnp.dot(xb, wgb[...], preferred_element_type=F32)
        hu = jnp.dot(xb, wub[...], preferred_element_type=F32)
        hb = (hg * _sigmoid(hg) * hu).astype(BF16)
        y = jnp.dot(hb, wdb[...], preferred_element_type=F32)
        row = lax.broadcasted_iota(jnp.int32, (MOE_BLOCK, 1), 0)
        mine = (row >= lo) & (row < hi)
        _store_token_tiles(ybuf.at[slot], jnp.where(mine, y, _load_token_tiles(ybuf.at[slot], MOE_BLOCK)))

    @pl.when(last)
    def _():
        start_scatter(slot)

    @pl.when(p == n_items - 1)
    def _():
        @pl.when(w >= 1)
        def _():
            wait_scatter(1 - slot)
        wait_scatter(slot)


def _experts(hp, src_rows, dst_rows, item_blk, item_exp, item_lo, item_hi, wg, wu, wd):
    T = hp.shape[0] // HP_ROWS
    n_items = item_blk.shape[0]
    rows = MOE_BLOCK
    ids_spec = pl.BlockSpec((1, 1, rows), lambda p, ib, ie, il, ih: (ib[p], 0, 0), memory_space=pltpu.SMEM)

    def w_spec(shape):
        return pl.BlockSpec((None,) + shape, lambda p, ib, ie, il, ih: (ie[p], 0, 0))

    grid_spec = pltpu.PrefetchScalarGridSpec(
        num_scalar_prefetch=4,
        grid=(n_items,),
        in_specs=[ids_spec, ids_spec, _const_spec(hp.shape),
                  w_spec((D_MODEL, EXPERT_DIM)), w_spec((D_MODEL, EXPERT_DIM)), w_spec((EXPERT_DIM, D_MODEL))],
        out_specs=pl.BlockSpec(memory_space=pl.ANY),
        scratch_shapes=[pltpu.VMEM((rows * HP_ROWS, LANES), jnp.uint32),
                        pltpu.VMEM((2, rows * TOK_ROWS, LANES), F32),
                        pltpu.VMEM((D_MODEL, EXPERT_DIM), BF16), pltpu.VMEM((D_MODEL, EXPERT_DIM), BF16),
                        pltpu.VMEM((EXPERT_DIM, D_MODEL), BF16),
                        pltpu.SMEM((1,), jnp.int32),
                        pltpu.SemaphoreType.DMA((2,))],
    )
    return pl.pallas_call(
        _experts_kernel,
        grid_spec=grid_spec,
        out_shape=jax.ShapeDtypeStruct((TOP_K * T * TOK_ROWS, LANES), F32),
        compiler_params=pltpu.CompilerParams(dimension_semantics=("arbitrary",),
                                             vmem_limit_bytes=VMEM_LIMIT,
                                             disable_bounds_checks=True),
        name="experts",
    )(item_blk, item_exp, item_lo, item_hi, src_rows, dst_rows, hp, wg, wu, wd)


def _combine_kernel(res_ref, y_ref, w_ref, g_ref, b_ref, o_ref):
    tm = res_ref.shape[0]
    acc = res_ref[...]
    w = w_ref[...]
    for k in range(TOP_K):
        acc = acc + w[:, k:k + 1] * _load_token_tiles(y_ref.at[k], tm)
    o_ref[...] = _layer_norm(acc, g_ref[...], b_ref[...])


def _combine(res, y3, wts, ln_g, ln_b):
    T = res.shape[0]
    tm = COMBINE_TILE
    return pl.pallas_call(
        _combine_kernel,
        grid=(T // tm,),
        in_specs=[pl.BlockSpec((tm, D_MODEL), lambda i: (i, 0)),
                  pl.BlockSpec((TOP_K, tm * TOK_ROWS, LANES), lambda i: (0, i, 0)),
                  pl.BlockSpec((tm, TOP_K), lambda i: (i, 0)),
                  _const_spec((1, D_MODEL)), _const_spec((1, D_MODEL))],
        out_specs=pl.BlockSpec((tm, D_MODEL), lambda i: (i, 0)),
        out_shape=jax.ShapeDtypeStruct((T, D_MODEL), F32),
        compiler_params=pltpu.CompilerParams(dimension_semantics=("arbitrary",),
                                             vmem_limit_bytes=VMEM_LIMIT),
        name="combine",
    )(res, y3, wts, ln_g, ln_b)


def _dispatch_plan(eidx_t, n_tok):
    nk = n_tok * TOP_K
    n_blocks = nk // MOE_BLOCK
    e_flat = eidx_t.T.reshape(nk)
    assert N_EXPERTS * nk <= 2 ** 31
    order = jnp.sort(e_flat * nk + jnp.arange(nk, dtype=jnp.int32)) % nk
    experts = jnp.arange(N_EXPERTS, dtype=jnp.int32)
    counts = jnp.sum((e_flat[None, :] == experts[:, None]).astype(jnp.int32), axis=1)
    start = jnp.cumsum(counts) - counts
    pos = jnp.sort(jnp.concatenate([jnp.arange(n_blocks, dtype=jnp.int32) * MOE_BLOCK, start]))
    nxt = jnp.concatenate([pos[1:], jnp.full((1,), nk, jnp.int32)])
    item_blk = jnp.minimum(pos // MOE_BLOCK, n_blocks - 1)
    item_exp = jnp.sum((start[None, :] <= pos[:, None]).astype(jnp.int32), axis=1) - 1
    item_lo = pos - item_blk * MOE_BLOCK
    item_hi = jnp.minimum(nxt, (item_blk + 1) * MOE_BLOCK) - item_blk * MOE_BLOCK
    tok, slot_k = order // TOP_K, order % TOP_K
    shape3 = (n_blocks, 1, MOE_BLOCK)
    src_rows = (tok * HP_ROWS).reshape(shape3)
    dst_rows = ((slot_k * n_tok + tok) * TOK_ROWS).reshape(shape3)
    return src_rows, dst_rows, item_blk, item_exp, item_lo, item_hi


def kernel(x, w_in, conv_w, gdn_a_log, gdn_dt_bias, gdn_norm_w, w_gdn_proj, w_moba_proj, w_out,
           ln1_g, ln1_b, router_w, router_bias, exp_w_gate, exp_w_up, exp_w_down,
           sh_w_gate, sh_w_up, sh_w_down, ln2_g, ln2_b):
    B, T, D = x.shape
    assert B == 1 and D == D_MODEL and T % MB_BLOCK == 0
    x2 = x.reshape(T, D)

    o_gate = CONV_CH
    o_b = o_gate + GDN_HEADS * GDN_DV
    o_a = o_b + GDN_HEADS
    o_mq = o_a + GDN_HEADS
    o_gg = o_mq + (MB_HEADS + 2 * MB_KV_HEADS) * MB_HD
    w_qkv = w_in[:, :CONV_CH].astype(BF16)
    w_pack = jnp.concatenate([w_in[:, o_b:o_a]] + [w_in[:, o_a:o_mq]] * 4
                             + [jnp.zeros((D, LANES - 5 * GDN_HEADS), F32)], axis=1).astype(BF16)
    w_mb = w_in[:, o_mq:o_gg].astype(BF16)
    w_gates = jnp.concatenate([w_in[:, o_gate:o_b], w_in[:, o_gg:]], axis=1).astype(BF16)

    def lane_row(v):
        return jnp.zeros((1, LANES), F32).at[0, PK_GC:PK_GC + 4 * GDN_HEADS].set(jnp.tile(v.astype(F32), 4))

    q, k, v, pack, gct = _qkv_proj(x2, w_qkv, w_pack, conv_w.astype(F32),
                                   lane_row(gdn_a_log), lane_row(gdn_dt_bias))
    gct3 = gct.reshape(GDN_HEADS, T // GDN_CHUNK, GDN_CHUNK).transpose(1, 0, 2)
    gdn_o = _gdn(q, k, v, pack, gct3, gdn_norm_w.astype(F32).reshape(1, GDN_DV))

    half = ROT_DIM // 2
    inv = ROPE_THETA ** (-jnp.arange(half, dtype=F32) / half)
    ang = jnp.arange(T).astype(F32)[:, None] * inv[None, :]
    ones = jnp.ones((T, MB_HD - ROT_DIM), F32)
    cos_t = jnp.concatenate([jnp.cos(ang), jnp.cos(ang), ones], axis=1)
    sin_t = jnp.concatenate([-jnp.sin(ang), jnp.sin(ang), 0.0 * ones], axis=1)
    mq, mk, mvt, kmean = _moba_proj(x2, w_mb, cos_t, sin_t)
    moba_a = _moba(mq, mk, mvt, kmean.transpose(1, 0, 2))

    hp, res, eidx_t, wts_t = _post(
        x2, gdn_o, moba_a, w_gates, w_gdn_proj.astype(BF16), w_moba_proj.astype(BF16),
        w_out.astype(BF16), ln1_g.reshape(1, D), ln1_b.reshape(1, D),
        router_w.T, router_bias.reshape(N_EXPERTS, 1),
        sh_w_gate.astype(BF16), sh_w_up.astype(BF16), sh_w_down.astype(BF16))

    plan = _dispatch_plan(eidx_t, T)
    y = _experts(hp, *plan, exp_w_gate, exp_w_up, exp_w_down)
    y3 = y.reshape(TOP_K, T * TOK_ROWS, LANES)
    out = _combine(res, y3, wts_t.T, ln2_g.reshape(1, D), ln2_b.reshape(1, D))
    return out.reshape(B, T, D)
```

```python
import functools
import math

import jax
import jax.numpy as jnp
from jax import lax
from jax.experimental import pallas as pl
from jax.experimental.pallas import tpu as pltpu

F32 = jnp.float32
BF16 = jnp.bfloat16

D_MODEL = 1024
DEPTH = 1
GDN_HEADS = 8
GDN_DK = 128
GDN_DV = 128
GDN_CONV = 4
GDN_CHUNK = 64
MB_HEADS = 8
MB_KV_HEADS = 2
MB_REP = MB_HEADS // MB_KV_HEADS
MB_HD = 128
MB_BLOCK = 256
MB_TOPK = 3
ROT_DIM = MB_HD // 4
ROPE_THETA = 500000.0
N_EXPERTS = 256
TOP_K = 8
N_GROUPS = 8
GROUP_SIZE = N_EXPERTS // N_GROUPS
TOPK_GROUPS = 4
EXPERT_DIM = 256
SHARED_DIM = 256
ROUTED_SCALE = 2.5
MOE_BLOCK = 256
DEEPNORM_ALPHA = (2.0 * DEPTH) ** 0.25
LN_EPS = 1e-5
MASK_NEG = -1e30

QK_COLS = GDN_HEADS * GDN_DK
CONV_CH = 2 * QK_COLS + GDN_HEADS * GDN_DV
LANES = 128
SUBLANES = 8
VMEM_LIMIT = 56 * 1024 * 1024

TOK_TILE = 256
POST_TILE = 512
COMBINE_TILE = 256

PK_BETA, PK_GC, PK_EG, PK_EGD, PK_EGL = 0, 8, 16, 24, 32


def _sigmoid(x):
    return 1.0 / (1.0 + jnp.exp(-x))


def _softplus(x):
    return jnp.maximum(x, 0.0) + jnp.log(1.0 + jnp.exp(-jnp.abs(x)))


def _bdot(a, b):
    return jnp.dot(a.astype(BF16), b.astype(BF16), preferred_element_type=F32)


def _bdot_nt(a, b):
    return lax.dot_general(a.astype(BF16), b.astype(BF16), (((1,), (1,)), ((), ())),
                           preferred_element_type=F32)


def _bdot_tn(a, b):
    return lax.dot_general(a.astype(BF16), b.astype(BF16), (((0,), (0,)), ((), ())),
                           preferred_element_type=F32)


def _layer_norm(x, g, b):
    mu = jnp.mean(x, axis=-1, keepdims=True)
    xc = x - mu
    var = jnp.mean(xc * xc, axis=-1, keepdims=True)
    return xc * lax.rsqrt(var + LN_EPS) * g + b


TOK_ROWS = D_MODEL // LANES


HP_ROWS = TOK_ROWS // 2


def _store_token_tiles(ref, x):
    n, r = x.shape[0], x.shape[1] // LANES
    for j in range(r):
        ref[pl.ds(j, n, stride=r), :] = x[:, j * LANES:(j + 1) * LANES]


def _load_token_tiles(ref, n):
    r = ref.shape[0] // n
    return jnp.concatenate([ref[pl.ds(j, n, stride=r), :] for j in range(r)], axis=1)


def _pack_bf16_pairs(x):
    half = x.shape[1] // 2
    bits = pltpu.bitcast(x.astype(BF16).astype(F32), jnp.uint32)
    return (bits[:, :half] >> 16) | (bits[:, half:] & jnp.uint32(0xFFFF0000))


def _unpack_bf16_pairs(w):
    lo = pltpu.bitcast(w << 16, F32)
    hi = pltpu.bitcast(w & jnp.uint32(0xFFFF0000), F32)
    return jnp.concatenate([lo, hi], axis=1)


def _const_spec(shape):
    nd = len(shape)
    return pl.BlockSpec(shape, lambda *_: (0,) * nd, pipeline_mode=pl.Buffered(1))


def _qkv_kernel(x_ref, w_ref, wp_ref, cw_ref, alog_ref, dtb_ref,
                q_ref, k_ref, v_ref, pack_ref, gct_ref, z_ref):
    tm = x_ref.shape[0]
    halo = SUBLANES

    @pl.when(pl.program_id(0) == 0)
    def _():
        z_ref[0:halo, :] = jnp.zeros((halo, CONV_CH), F32)

    xb = x_ref[...].astype(BF16)
    z_ref[halo:halo + tm, :] = jnp.dot(xb, w_ref[...], preferred_element_type=F32)

    for c in range(CONV_CH // LANES):
        cols = slice(c * LANES, (c + 1) * LANES)
        acc = z_ref[halo:halo + tm, cols] * cw_ref[GDN_CONV - 1:GDN_CONV, cols]
        for s in range(1, GDN_CONV):
            acc = acc + z_ref[halo - s:halo - s + tm, cols] * cw_ref[GDN_CONV - 1 - s:GDN_CONV - s, cols]
        y = acc * _sigmoid(acc)
        h = c % GDN_HEADS
        if c < GDN_HEADS:
            y = y * lax.rsqrt(jnp.sum(y * y, axis=-1, keepdims=True) + 1e-6) * (GDN_DK ** -0.5)
            q_ref[h] = y
        elif c < 2 * GDN_HEADS:
            y = y * lax.rsqrt(jnp.sum(y * y, axis=-1, keepdims=True) + 1e-6)
            k_ref[h] = y
        else:
            v_ref[h] = y

    z_ref[0:halo, :] = z_ref[tm:tm + halo, :]

    zp = jnp.dot(xb, wp_ref[...], preferred_element_type=F32)
    beta = _sigmoid(zp)
    g = -jnp.exp(alog_ref[...]) * _softplus(zp + dtb_ref[...])
    row = lax.broadcasted_iota(jnp.int32, (tm, LANES), 0) % GDN_CHUNK
    gc = g
    step = 1
    while step < GDN_CHUNK:
        gc = gc + jnp.where(row >= step, pltpu.roll(gc, step, axis=0), 0.0)
        step *= 2
    gl = gc.reshape(tm // GDN_CHUNK, GDN_CHUNK, LANES)[:, GDN_CHUNK - 1:GDN_CHUNK, :]
    gl = jnp.broadcast_to(gl, (tm // GDN_CHUNK, GDN_CHUNK, LANES)).reshape(tm, LANES)
    lane = lax.broadcasted_iota(jnp.int32, (tm, LANES), 1)
    pack = jnp.where(lane < PK_GC, beta,
           jnp.where(lane < PK_EG, gc,
           jnp.where(lane < PK_EGD, jnp.exp(gc),
           jnp.where(lane < PK_EGL, jnp.exp(gl - gc), jnp.exp(gl)))))
    pack_ref[...] = pack
    gct_ref[...] = pack.T[PK_GC:PK_GC + GDN_HEADS, :]


def _qkv_proj(x2, w_qkv, w_pack, conv_w, alog_row, dtb_row):
    T = x2.shape[0]
    tm = TOK_TILE
    hd_spec = pl.BlockSpec((GDN_HEADS, tm, LANES), lambda i: (0, i, 0))
    hd_shape = jax.ShapeDtypeStruct((GDN_HEADS, T, LANES), F32)
    return pl.pallas_call(
        _qkv_kernel,
        grid=(T // tm,),
        in_specs=[
            pl.BlockSpec((tm, D_MODEL), lambda i: (i, 0)),
            _const_spec((D_MODEL, CONV_CH)),
            _const_spec((D_MODEL, LANES)),
            _const_spec((GDN_CONV, CONV_CH)),
            _const_spec((1, LANES)),
            _const_spec((1, LANES)),
        ],
        out_specs=[hd_spec, hd_spec, hd_spec,
                   pl.BlockSpec((tm, LANES), lambda i: (i, 0)),
                   pl.BlockSpec((GDN_HEADS, tm), lambda i: (0, i))],
        out_shape=[hd_shape, hd_shape, hd_shape,
                   jax.ShapeDtypeStruct((T, LANES), F32),
                   jax.ShapeDtypeStruct((GDN_HEADS, T), F32)],
        scratch_shapes=[pltpu.VMEM((tm + SUBLANES, CONV_CH), F32)],
        compiler_params=pltpu.CompilerParams(dimension_semantics=("arbitrary",),
                                             vmem_limit_bytes=VMEM_LIMIT),
        name="qkv_proj",
    )(x2, w_qkv, w_pack, conv_w, alog_row, dtb_row)


MB_LROWS = 16
MB_VROWS = MB_HD + MB_LROWS


def _rope(xh, cos_t, sin_t, lane):
    half = ROT_DIM // 2
    swapped = jnp.where(lane < half, pltpu.roll(xh, LANES - half, axis=1), pltpu.roll(xh, half, axis=1))
    return xh * cos_t + swapped * sin_t


def _moba_proj_kernel(x_ref, w_ref, cos_ref, sin_ref, q_ref, k_ref, vt_ref, km_ref):
    tm = x_ref.shape[0]
    z = jnp.dot(x_ref[...].astype(BF16), w_ref[...], preferred_element_type=F32)
    cos_t = cos_ref[...]
    sin_t = sin_ref[...]
    lane = lax.broadcasted_iota(jnp.int32, (tm, LANES), 1)
    for h in range(MB_HEADS):
        q_ref[h] = _rope(z[:, h * MB_HD:(h + 1) * MB_HD], cos_t, sin_t, lane)
    koff = MB_HEADS * MB_HD
    voff = koff + MB_KV_HEADS * MB_HD
    blk_onehot = jnp.where(lane == pl.program_id(0), 1.0, 0.0).astype(BF16)
    for g in range(MB_KV_HEADS):
        kr = _rope(z[:, koff + g * MB_HD:koff + (g + 1) * MB_HD], cos_t, sin_t, lane)
        k_ref[0, g] = jnp.concatenate([kr.astype(BF16), blk_onehot], axis=1)
        km_ref[0, g:g + 1, :] = jnp.mean(kr, axis=0, keepdims=True)
        vt_ref[0, g] = jnp.concatenate([z[:, voff + g * MB_HD:voff + (g + 1) * MB_HD].T,
                                        jnp.ones((MB_LROWS, tm), F32)], axis=0).astype(BF16)


def _moba_proj(x2, w_mb, cos_t, sin_t):
    T = x2.shape[0]
    tm = MB_BLOCK
    nb = T // tm
    return pl.pallas_call(
        _moba_proj_kernel,
        grid=(nb,),
        in_specs=[
            pl.BlockSpec((tm, D_MODEL), lambda i: (i, 0)),
            _const_spec(w_mb.shape),
            pl.BlockSpec((tm, LANES), lambda i: (i, 0)),
            pl.BlockSpec((tm, LANES), lambda i: (i, 0)),
        ],
        out_specs=[
            pl.BlockSpec((MB_HEADS, tm, MB_HD), lambda i: (0, i, 0)),
            pl.BlockSpec((1, MB_KV_HEADS, tm, 2 * MB_HD), lambda i: (i, 0, 0, 0)),
            pl.BlockSpec((1, MB_KV_HEADS, MB_VROWS, tm), lambda i: (i, 0, 0, 0)),
            pl.BlockSpec((1, MB_KV_HEADS, MB_HD), lambda i: (i, 0, 0)),
        ],
        out_shape=[
            jax.ShapeDtypeStruct((MB_HEADS, T, MB_HD), F32),
            jax.ShapeDtypeStruct((nb, MB_KV_HEADS, tm, 2 * MB_HD), BF16),
            jax.ShapeDtypeStruct((nb, MB_KV_HEADS, MB_VROWS, tm), BF16),
            jax.ShapeDtypeStruct((nb, MB_KV_HEADS, MB_HD), F32),
        ],
        compiler_params=pltpu.CompilerParams(dimension_semantics=("arbitrary",),
                                             vmem_limit_bytes=VMEM_LIMIT),
        name="moba_proj",
    )(x2, w_mb, cos_t, sin_t)


def _unit_lower_inverse(a_list):
    c = a_list[0].shape[0]
    rr = lax.broadcasted_iota(jnp.int32, (c, c), 0)
    cc = lax.broadcasted_iota(jnp.int32, (c, c), 1)
    eye = jnp.where(rr == cc, 1.0, 0.0).astype(F32)
    inv = [eye - a for a in a_list]
    p = [_bdot(a, a) for a in a_list]
    n = 2
    while True:
        inv = [x + _bdot(x, y) for x, y in zip(inv, p)]
        n *= 2
        if n >= c:
            break
        p = [_bdot(y, y) for y in p]
    return inv


GDN_STEP_CHUNKS = 4


def _gdn_kernel(q_ref, k_ref, v_ref, pack_ref, gct_ref, nw_ref, o_ref, s_ref):
    C = GDN_CHUNK
    H = range(GDN_HEADS)
    P = [(c, h) for c in range(GDN_STEP_CHUNKS) for h in H]

    @pl.when(pl.program_id(0) == 0)
    def _():
        s_ref[...] = jnp.zeros_like(s_ref)

    rr = lax.broadcasted_iota(jnp.int32, (C, C), 0)
    cc = lax.broadcasted_iota(jnp.int32, (C, C), 1)
    tril = rr >= cc
    strict = rr > cc
    pack = pack_ref[...]
    nw = nw_ref[...]

    def rows(c):
        return slice(c * C, (c + 1) * C)

    def col(base, c, h):
        return pack[rows(c), base + h:base + h + 1]

    q = {(c, h): q_ref[h, rows(c), :] for c, h in P}
    k = {(c, h): k_ref[h, rows(c), :] for c, h in P}
    kb = {(c, h): k[c, h] * col(PK_BETA, c, h) for c, h in P}
    decay = {(c, h): jnp.where(tril, jnp.exp(jnp.where(tril, col(PK_GC, c, h) - gct_ref[c, h:h + 1, :], 0.0)), 0.0)
             for c, h in P}
    sc = {p: _bdot_nt(jnp.concatenate([kb[p], q[p]], axis=0), k[p]) for p in P}
    a = {p: jnp.where(strict, sc[p][:C] * decay[p], 0.0) for p in P}
    aqk = {p: sc[p][C:] * decay[p] for p in P}
    tinv = dict(zip(P, _unit_lower_inverse([a[p] for p in P])))
    rhs = {(c, h): jnp.concatenate([v_ref[h, rows(c), :] * col(PK_BETA, c, h), kb[c, h] * col(PK_EG, c, h)],
                                   axis=1) for c, h in P}
    sol = {p: _bdot(tinv[p], rhs[p]) for p in P}

    s = [s_ref[h] for h in H]
    for c in range(GDN_STEP_CHUNKS):
        wq = [_bdot(jnp.concatenate([sol[c, h][:, GDN_DV:], q[c, h] * col(PK_EG, c, h)], axis=0), s[h])
              for h in H]
        v_new = [sol[c, h][:, :GDN_DV] - wq[h][:C] for h in H]
        o = [wq[h][C:] + _bdot(aqk[c, h], v_new[h]) for h in H]
        ds = [_bdot_tn(k[c, h] * col(PK_EGD, c, h), v_new[h]) for h in H]
        for h in H:
            s[h] = s[h] * pack[c * C:c * C + 1, PK_EGL + h:PK_EGL + h + 1] + ds[h]
            on = o[h] * lax.rsqrt(jnp.mean(o[h] * o[h], axis=-1, keepdims=True) + 1e-6) * nw
            o_ref[rows(c), h * GDN_DV:(h + 1) * GDN_DV] = on
    for h in H:
        s_ref[h] = s[h]


def _gdn(q, k, v, pack, gct3, norm_w_row):
    T = q.shape[1]
    C = GDN_CHUNK * GDN_STEP_CHUNKS
    hd_spec = pl.BlockSpec((GDN_HEADS, C, LANES), lambda i: (0, i, 0))
    return pl.pallas_call(
        _gdn_kernel,
        grid=(T // C,),
        in_specs=[hd_spec, hd_spec, hd_spec,
                  pl.BlockSpec((C, LANES), lambda i: (i, 0)),
                  pl.BlockSpec((GDN_STEP_CHUNKS, GDN_HEADS, GDN_CHUNK), lambda i: (i, 0, 0)),
                  _const_spec((1, GDN_DV))],
        out_specs=pl.BlockSpec((C, GDN_HEADS * GDN_DV), lambda i: (i, 0)),
        out_shape=jax.ShapeDtypeStruct((T, GDN_HEADS * GDN_DV), F32),
        scratch_shapes=[pltpu.VMEM((GDN_HEADS, GDN_DK, GDN_DV), F32)],
        compiler_params=pltpu.CompilerParams(dimension_semantics=("arbitrary",)),
        name="gdn",
    )(q, k, v, pack, gct3, norm_w_row)


MB_QSPLIT = 2
MB_TRIP = 4
MB_MAX_BLOCKS = MB_HD


def _moba_kernel(q_ref, k_ref, vt_ref, km_ref, o_ref, qx_ref, sa_ref, sb_ref, acc_ref, m_ref):
    qi = pl.program_id(1)
    nb = k_ref.shape[0]
    bs = MB_BLOCK
    nq = MB_REP * bs
    nbp = MB_MAX_BLOCKS
    c = (MB_HD ** -0.5) * math.log2(math.e)

    qf = q_ref[...].reshape(nq, MB_HD)

    gate = lax.dot_general(km_ref[...], qf, (((1,), (1,)), ((), ())),
                           precision=lax.Precision.HIGHEST, preferred_element_type=F32)
    blk = lax.broadcasted_iota(jnp.int32, (nbp, nq), 0).astype(F32)
    qif = qi.astype(F32)
    gate = jnp.where(blk < qif, gate, -jnp.inf)
    sel = jnp.where(blk == qif, 1.0, 0.0)
    for _ in range(MB_TOPK):
        top = jnp.max(gate, axis=0, keepdims=True)
        first = jnp.min(jnp.where(gate == top, blk, float(nbp)), axis=0, keepdims=True)
        hit = (blk == first) & (top > -jnp.inf)
        sel = jnp.where(hit, 1.0, sel)
        gate = jnp.where(hit, -jnp.inf, gate)
    bias_t = jnp.where(sel > 0.0, 0.0, MASK_NEG)
    qc = qf * c
    qx_ref[0] = jnp.concatenate([qc, bias_t.T], axis=1).astype(BF16)
    qx_ref[1] = jnp.concatenate([qc, jnp.full((nq, nbp), MASK_NEG, F32)], axis=1).astype(BF16)

    w = nq // MB_QSPLIT
    groups = [slice(h * w, (h + 1) * w) for h in range(MB_QSPLIT)]

    def scores_into(dst_ref, t):
        kx = k_ref[jnp.minimum(t, nb - 1)]
        qset = jnp.where(t < qi, 0, 1)
        for lanes in groups:
            dst_ref[:, lanes] = lax.dot_general(kx, qx_ref[qset, lanes, :], (((1,), (1,)), ((), ())),
                                                preferred_element_type=F32)

    st = lax.dot_general(k_ref[qi], qx_ref[0], (((1,), (1,)), ((), ())), preferred_element_type=F32)
    kpos = lax.broadcasted_iota(jnp.int32, (bs, nq), 0)
    qpos = lax.broadcasted_iota(jnp.int32, (bs, nq), 1) % bs
    st = jnp.where(kpos <= qpos, st, MASK_NEG)
    m0 = jnp.max(st, axis=0, keepdims=True)
    p = jnp.exp2(st - m0)
    m_ref[...] = m0
    acc_ref[...] = jnp.dot(vt_ref[qi], p.astype(BF16), preferred_element_type=F32)

    def absorb(src_ref, t):
        vt = vt_ref[jnp.minimum(t, nb - 1)]
        for lanes in groups:
            st = src_ref[:, lanes]
            m_old = m_ref[:, lanes]
            m_new = jnp.maximum(m_old, jnp.max(st, axis=0, keepdims=True))
            alpha = jnp.exp2(m_old - m_new)
            p = jnp.exp2(st - m_new)
            m_ref[:, lanes] = m_new
            acc_ref[:, lanes] = alpha * acc_ref[:, lanes] + jnp.dot(vt, p.astype(BF16),
                                                                    preferred_element_type=F32)

    scores_into(sa_ref, 0)

    def trip(t, blocks):
        for u in range(0, blocks, 2):
            scores_into(sb_ref, t + u + 1)
            absorb(sa_ref, t + u)
            scores_into(sa_ref, t + u + 2)
            absorb(sb_ref, t + u + 1)

    def body(i, carry):
        trip(MB_TRIP * i, MB_TRIP)
        return carry

    def tail(i, carry):
        trip(done + 2 * i, 2)
        return carry

    done = (qi // MB_TRIP) * MB_TRIP
    lax.fori_loop(0, qi // MB_TRIP, body, 0)
    lax.fori_loop(0, (qi - done + 1) // 2, tail, 0)

    out_t = acc_ref[0:MB_HD, :] / acc_ref[MB_HD:MB_HD + 1, :]
    for r in range(MB_REP):
        o_ref[:, r * MB_HD:(r + 1) * MB_HD] = out_t[:, r * bs:(r + 1) * bs].T


def _moba(mq, mk, mvt, kmean):
    T = mq.shape[1]
    bs = MB_BLOCK
    nb = T // bs
    nq = MB_REP * bs
    assert nb <= MB_MAX_BLOCKS
    kmean = jnp.pad(kmean, ((0, 0), (0, MB_MAX_BLOCKS - nb), (0, 0)))
    return pl.pallas_call(
        _moba_kernel,
        grid=(MB_KV_HEADS, nb),
        in_specs=[
            pl.BlockSpec((MB_REP, bs, MB_HD), lambda g, i: (g, i, 0)),
            pl.BlockSpec((nb, None, bs, 2 * MB_HD), lambda g, i: (0, g, 0, 0)),
            pl.BlockSpec((nb, None, MB_VROWS, bs), lambda g, i: (0, g, 0, 0)),
            pl.BlockSpec((None, MB_MAX_BLOCKS, MB_HD), lambda g, i: (g, 0, 0)),
        ],
        out_specs=pl.BlockSpec((bs, MB_REP * MB_HD), lambda g, i: (i, g)),
        out_shape=jax.ShapeDtypeStruct((T, MB_HEADS * MB_HD), F32),
        scratch_shapes=[pltpu.VMEM((2, nq, 2 * MB_HD), BF16),
                        pltpu.VMEM((bs, nq), F32), pltpu.VMEM((bs, nq), F32),
                        pltpu.VMEM((MB_VROWS, nq), F32),
                        pltpu.VMEM((1, nq), F32)],
        compiler_params=pltpu.CompilerParams(dimension_semantics=("arbitrary", "arbitrary"),
                                             vmem_limit_bytes=VMEM_LIMIT),
        name="moba",
    )(mq, mk, mvt, kmean)


def _post_kernel(x_ref, go_ref, ma_ref, wg_ref, wgp_ref, wmp_ref, wo_ref, g1_ref, b1_ref,
                 rwt_ref, rb_ref, sg_ref, su_ref, sd_ref,
                 hp_ref, res_ref, eidx_ref, wts_ref):
    tm = x_ref.shape[0]
    x = x_ref[...]
    zg = jnp.dot(x.astype(BF16), wg_ref[...], preferred_element_type=F32)
    gate = zg[:, :D_MODEL]
    o = go_ref[...] * (gate * _sigmoid(gate))
    y_gdn = jnp.dot(o.astype(BF16), wgp_ref[...], preferred_element_type=F32)
    y_mb = jnp.dot(ma_ref[...].astype(BF16), wmp_ref[...], preferred_element_type=F32)
    m = _sigmoid(zg[:, D_MODEL:2 * D_MODEL]) * y_gdn + _sigmoid(zg[:, 2 * D_MODEL:]) * y_mb
    mix = jnp.dot(m.astype(BF16), wo_ref[...], preferred_element_type=F32)
    h = _layer_norm(DEEPNORM_ALPHA * x + mix, g1_ref[...], b1_ref[...])
    hb = h.astype(BF16)
    _store_token_tiles(hp_ref, _pack_bf16_pairs(h))

    hs = jnp.dot(hb, sg_ref[...], preferred_element_type=F32)
    hs = hs * _sigmoid(hs) * jnp.dot(hb, su_ref[...], preferred_element_type=F32)
    res_ref[...] = DEEPNORM_ALPHA * h + jnp.dot(hs.astype(BF16), sd_ref[...], preferred_element_type=F32)

    logits = lax.dot_general(rwt_ref[...], h, (((1,), (1,)), ((), ())),
                             precision=lax.Precision.HIGHEST, preferred_element_type=F32)
    scores = _sigmoid(logits)
    choice = scores + rb_ref[...]
    neg = -jnp.inf
    gi = lax.broadcasted_iota(jnp.int32, (GROUP_SIZE, tm), 0).astype(F32)
    gscore = []
    for g in range(N_GROUPS):
        cg = choice[g * GROUP_SIZE:(g + 1) * GROUP_SIZE, :]
        m1 = jnp.max(cg, axis=0, keepdims=True)
        i1 = jnp.min(jnp.where(cg == m1, gi, float(GROUP_SIZE)), axis=0, keepdims=True)
        m2 = jnp.max(jnp.where(gi == i1, neg, cg), axis=0, keepdims=True)
        gscore.append(m1 + m2)
    gs = jnp.concatenate(gscore, axis=0)
    gidx = lax.broadcasted_iota(jnp.int32, (N_GROUPS, tm), 0).astype(F32)
    gsel = jnp.zeros((N_GROUPS, tm), F32)
    for _ in range(TOPK_GROUPS):
        top = jnp.max(gs, axis=0, keepdims=True)
        first = jnp.min(jnp.where(gs == top, gidx, float(N_GROUPS)), axis=0, keepdims=True)
        hit = gidx == first
        gsel = jnp.where(hit, 1.0, gsel)
        gs = jnp.where(hit, neg, gs)
    masked = jnp.concatenate(
        [jnp.where(gsel[g:g + 1, :] > 0.0, choice[g * GROUP_SIZE:(g + 1) * GROUP_SIZE, :], neg)
         for g in range(N_GROUPS)], axis=0)
    ei = lax.broadcasted_iota(jnp.int32, (N_EXPERTS, tm), 0).astype(F32)
    idx_rows, w_rows = [], []
    for _ in range(TOP_K):
        top = jnp.max(masked, axis=0, keepdims=True)
        first = jnp.min(jnp.where(masked == top, ei, float(N_EXPERTS)), axis=0, keepdims=True)
        hit = ei == first
        idx_rows.append(first)
        w_rows.append(jnp.sum(jnp.where(hit, scores, 0.0), axis=0, keepdims=True))
        masked = jnp.where(hit, neg, masked)
    w = jnp.concatenate(w_rows, axis=0)
    w = w / (jnp.sum(w, axis=0, keepdims=True) + 1e-20) * ROUTED_SCALE
    eidx_ref[...] = jnp.concatenate(idx_rows, axis=0).astype(jnp.int32)
    wts_ref[...] = w


def _post(x2, gdn_o, moba_a, w_gates, w_gdn_proj, w_moba_proj, w_out, ln_g, ln_b,
          router_wt, router_b_col, sh_g, sh_u, sh_d):
    T = x2.shape[0]
    tm = POST_TILE
    row_spec = pl.BlockSpec((tm, D_MODEL), lambda i: (i, 0))
    consts = [w_gates, w_gdn_proj, w_moba_proj, w_out, ln_g, ln_b, router_wt, router_b_col,
              sh_g, sh_u, sh_d]
    return pl.pallas_call(
        _post_kernel,
        grid=(T // tm,),
        in_specs=[row_spec, row_spec, row_spec] + [_const_spec(c.shape) for c in consts],
        out_specs=[pl.BlockSpec((tm * HP_ROWS, LANES), lambda i: (i, 0)), row_spec,
                   pl.BlockSpec((TOP_K, tm), lambda i: (0, i)),
                   pl.BlockSpec((TOP_K, tm), lambda i: (0, i))],
        out_shape=[jax.ShapeDtypeStruct((T * HP_ROWS, LANES), jnp.uint32),
                   jax.ShapeDtypeStruct((T, D_MODEL), F32),
                   jax.ShapeDtypeStruct((TOP_K, T), jnp.int32),
                   jax.ShapeDtypeStruct((TOP_K, T), F32)],
        compiler_params=pltpu.CompilerParams(dimension_semantics=("arbitrary",),
                                             vmem_limit_bytes=VMEM_LIMIT),
        name="post",
    )(x2, gdn_o, moba_a, *consts)


ROW_UNROLL = 8


def _experts_kernel(iblk_ref, iexp_ref, ilo_ref, ihi_ref, src_ref, dst_ref,
                    hp_ref, wg_ref, wu_ref, wd_ref, out_ref,
                    xbuf, ybuf, wgb, wub, wdb, cast_exp, ssem):
    p = pl.program_id(0)
    n_items = pl.num_programs(0)
    w = iblk_ref[p]
    slot = w % 2
    first = (p == 0) | (w != iblk_ref[jnp.maximum(p - 1, 0)])
    last = (p == n_items - 1) | (w != iblk_ref[jnp.minimum(p + 1, n_items - 1)])
    lo, hi = ilo_ref[p], ihi_ref[p]
    block_rows = MOE_BLOCK * TOK_ROWS

    def tile_at(ref, row0, rows):
        return ref.at[pl.ds(pl.multiple_of(row0, rows), rows)]

    def for_block_rows(fn):
        def group(g, c):
            for u in range(ROW_UNROLL):
                fn(g * ROW_UNROLL + u)
            return c
        lax.fori_loop(0, MOE_BLOCK // ROW_UNROLL, group, 0)

    def gather_block():
        def one(i):
            xbuf[pl.ds(pl.multiple_of(i * HP_ROWS, HP_ROWS), HP_ROWS), :] = (
                hp_ref[pl.ds(pl.multiple_of(src_ref[0, 0, i], HP_ROWS), HP_ROWS), :])
        for_block_rows(one)

    def start_scatter(s):
        for_block_rows(lambda i: pltpu.make_async_copy(
            tile_at(ybuf.at[s], i * TOK_ROWS, TOK_ROWS), tile_at(out_ref, dst_ref[0, 0, i], TOK_ROWS),
            ssem.at[s]).start())

    def wait_scatter(s):
        pltpu.make_async_copy(ybuf.at[s], out_ref.at[pl.ds(0, block_rows)], ssem.at[s]).wait()

    @pl.when(p == 0)
    def _():
        ybuf[...] = jnp.zeros_like(ybuf)
        cast_exp[0] = -1

    @pl.when(first)
    def _():
        @pl.when(w >= 2)
        def _():
            wait_scatter(slot)
        gather_block()

    @pl.when(hi > lo)
    def _():
        @pl.when(iexp_ref[p] != cast_exp[0])
        def _():
            wgb[...] = wg_ref[...].astype(BF16)
            wub[...] = wu_ref[...].astype(BF16)
            wdb[...] = wd_ref[...].astype(BF16)
            cast_exp[0] = iexp_ref[p]

        xb = _unpack_bf16_pairs(_load_token_tiles(xbuf, MOE_BLOCK)).astype(BF16)
        hg = jnp.dot(xb, wgb[...], preferred_element_type=F32)
        hu = jnp.dot(xb, wub[...], preferred_element_type=F32)
        hb = (hg * _sigmoid(hg) * hu).astype(BF16)
        y = jnp.dot(hb, wdb[...], preferred_element_type=F32)
        row = lax.broadcasted_iota(jnp.int32, (MOE_BLOCK, 1), 0)
        mine = (row >= lo) & (row < hi)
        _store_token_tiles(ybuf.at[slot], jnp.where(mine, y, _load_token_tiles(ybuf.at[slot], MOE_BLOCK)))

    @pl.when(last)
    def _():
        start_scatter(slot)

    @pl.when(p == n_items - 1)
    def _():
        @pl.when(w >= 1)
        def _():
            wait_scatter(1 - slot)
        wait_scatter(slot)


def _experts(hp, src_rows, dst_rows, item_blk, item_exp, item_lo, item_hi, wg, wu, wd):
    T = hp.shape[0] // HP_ROWS
    n_items = item_blk.shape[0]
    rows = MOE_BLOCK
    ids_spec = pl.BlockSpec((1, 1, rows), lambda p, ib, ie, il, ih: (ib[p], 0, 0), memory_space=pltpu.SMEM)

    def w_spec(shape):
        return pl.BlockSpec((None,) + shape, lambda p, ib, ie, il, ih: (ie[p], 0, 0))

    grid_spec = pltpu.PrefetchScalarGridSpec(
        num_scalar_prefetch=4,
        grid=(n_items,),
        in_specs=[ids_spec, ids_spec, _const_spec(hp.shape),
                  w_spec((D_MODEL, EXPERT_DIM)), w_spec((D_MODEL, EXPERT_DIM)), w_spec((EXPERT_DIM, D_MODEL))],
        out_specs=pl.BlockSpec(memory_space=pl.ANY),
        scratch_shapes=[pltpu.VMEM((rows * HP_ROWS, LANES), jnp.uint32),
                        pltpu.VMEM((2, rows * TOK_ROWS, LANES), F32),
                        pltpu.VMEM((D_MODEL, EXPERT_DIM), BF16), pltpu.VMEM((D_MODEL, EXPERT_DIM), BF16),
                        pltpu.VMEM((EXPERT_DIM, D_MODEL), BF16),
                        pltpu.SMEM((1,), jnp.int32),
                        pltpu.SemaphoreType.DMA((2,))],
    )
    return pl.pallas_call(
        _experts_kernel,
        grid_spec=grid_spec,
        out_shape=jax.ShapeDtypeStruct((TOP_K * T * TOK_ROWS, LANES), F32),
        compiler_params=pltpu.CompilerParams(dimension_semantics=("arbitrary",),
                                             vmem_limit_bytes=VMEM_LIMIT,
                                             disable_bounds_checks=True),
        name="experts",
    )(item_blk, item_exp, item_lo, item_hi, src_rows, dst_rows, hp, wg, wu, wd)


def _combine_kernel(res_ref, y_ref, w_ref, g_ref, b_ref, o_ref):
    tm = res_ref.shape[0]
    acc = res_ref[...]
    w = w_ref[...]
    for k in range(TOP_K):
        acc = acc + w[:, k:k + 1] * _load_token_tiles(y_ref.at[k], tm)
    o_ref[...] = _layer_norm(acc, g_ref[...], b_ref[...])


def _combine(res, y3, wts, ln_g, ln_b):
    T = res.shape[0]
    tm = COMBINE_TILE
    return pl.pallas_call(
        _combine_kernel,
        grid=(T // tm,),
        in_specs=[pl.BlockSpec((tm, D_MODEL), lambda i: (i, 0)),
                  pl.BlockSpec((TOP_K, tm * TOK_ROWS, LANES), lambda i: (0, i, 0)),
                  pl.BlockSpec((tm, TOP_K), lambda i: (i, 0)),
                  _const_spec((1, D_MODEL)), _const_spec((1, D_MODEL))],
        out_specs=pl.BlockSpec((tm, D_MODEL), lambda i: (i, 0)),
        out_shape=jax.ShapeDtypeStruct((T, D_MODEL), F32),
        compiler_params=pltpu.CompilerParams(dimension_semantics=("arbitrary",),
                                             vmem_limit_bytes=VMEM_LIMIT),
        name="combine",
    )(res, y3, wts, ln_g, ln_b)


def _dispatch_plan(eidx_t, n_tok):
    nk = n_tok * TOP_K
    n_blocks = nk // MOE_BLOCK
    e_flat = eidx_t.T.reshape(nk)
    _, order = lax.sort((e_flat, jnp.arange(nk, dtype=jnp.int32)), num_keys=1)
    experts = jnp.arange(N_EXPERTS, dtype=jnp.int32)
    counts = jnp.sum((e_flat[None, :] == experts[:, None]).astype(jnp.int32), axis=1)
    start = jnp.cumsum(counts) - counts
    pos = jnp.sort(jnp.concatenate([jnp.arange(n_blocks, dtype=jnp.int32) * MOE_BLOCK, start]))
    nxt = jnp.concatenate([pos[1:], jnp.full((1,), nk, jnp.int32)])
    item_blk = jnp.minimum(pos // MOE_BLOCK, n_blocks - 1)
    item_exp = jnp.sum((start[None, :] <= pos[:, None]).astype(jnp.int32), axis=1) - 1
    item_lo = pos - item_blk * MOE_BLOCK
    item_hi = jnp.minimum(nxt, (item_blk + 1) * MOE_BLOCK) - item_blk * MOE_BLOCK
    tok, slot_k = order // TOP_K, order % TOP_K
    shape3 = (n_blocks, 1, MOE_BLOCK)
    src_rows = (tok * HP_ROWS).reshape(shape3)
    dst_rows = ((slot_k * n_tok + tok) * TOK_ROWS).reshape(shape3)
    return src_rows, dst_rows, item_blk, item_exp, item_lo, item_hi


def kernel(x, w_in, conv_w, gdn_a_log, gdn_dt_bias, gdn_norm_w, w_gdn_proj, w_moba_proj, w_out,
           ln1_g, ln1_b, router_w, router_bias, exp_w_gate, exp_w_up, exp_w_down,
           sh_w_gate, sh_w_up, sh_w_down, ln2_g, ln2_b):
    B, T, D = x.shape
    assert B == 1 and D == D_MODEL and T % MB_BLOCK == 0
    x2 = x.reshape(T, D)

    o_gate = CONV_CH
    o_b = o_gate + GDN_HEADS * GDN_DV
    o_a = o_b + GDN_HEADS
    o_mq = o_a + GDN_HEADS
    o_gg = o_mq + (MB_HEADS + 2 * MB_KV_HEADS) * MB_HD
    w_qkv = w_in[:, :CONV_CH].astype(BF16)
    w_pack = jnp.concatenate([w_in[:, o_b:o_a]] + [w_in[:, o_a:o_mq]] * 4
                             + [jnp.zeros((D, LANES - 5 * GDN_HEADS), F32)], axis=1).astype(BF16)
    w_mb = w_in[:, o_mq:o_gg].astype(BF16)
    w_gates = jnp.concatenate([w_in[:, o_gate:o_b], w_in[:, o_gg:]], axis=1).astype(BF16)

    def lane_row(v):
        return jnp.zeros((1, LANES), F32).at[0, PK_GC:PK_GC + 4 * GDN_HEADS].set(jnp.tile(v.astype(F32), 4))

    q, k, v, pack, gct = _qkv_proj(x2, w_qkv, w_pack, conv_w.astype(F32),
                                   lane_row(gdn_a_log), lane_row(gdn_dt_bias))
    gct3 = gct.reshape(GDN_HEADS, T // GDN_CHUNK, GDN_CHUNK).transpose(1, 0, 2)
    gdn_o = _gdn(q, k, v, pack, gct3, gdn_norm_w.astype(F32).reshape(1, GDN_DV))

    half = ROT_DIM // 2
    inv = ROPE_THETA ** (-jnp.arange(half, dtype=F32) / half)
    ang = jnp.arange(T).astype(F32)[:, None] * inv[None, :]
    ones = jnp.ones((T, MB_HD - ROT_DIM), F32)
    cos_t = jnp.concatenate([jnp.cos(ang), jnp.cos(ang), ones], axis=1)
    sin_t = jnp.concatenate([-jnp.sin(ang), jnp.sin(ang), 0.0 * ones], axis=1)
    mq, mk, mvt, kmean = _moba_proj(x2, w_mb, cos_t, sin_t)
    moba_a = _moba(mq, mk, mvt, kmean.transpose(1, 0, 2))

    hp, res, eidx_t, wts_t = _post(
        x2, gdn_o, moba_a, w_gates, w_gdn_proj.astype(BF16), w_moba_proj.astype(BF16),
        w_out.astype(BF16), ln1_g.reshape(1, D), ln1_b.reshape(1, D),
        router_w.T, router_bias.reshape(N_EXPERTS, 1),
        sh_w_gate.astype(BF16), sh_w_up.astype(BF16), sh_w_down.astype(BF16))

    plan = _dispatch_plan(eidx_t, T)
    y = _experts(hp, *plan, exp_w_gate, exp_w_up, exp_w_down)
    y3 = y.reshape(TOP_K, T * TOK_ROWS, LANES)
    out = _combine(res, y3, wts_t.T, ln2_g.reshape(1, D), ln2_b.reshape(1, D))
    return out.reshape(B, T, D)
```

```python
import functools
import math

import jax
import jax.numpy as jnp
from jax import lax
from jax.experimental import pallas as pl
from jax.experimental.pallas import tpu as pltpu

F32 = jnp.float32
BF16 = jnp.bfloat16

D_MODEL = 1024
DEPTH = 1
GDN_HEADS = 8
GDN_DK = 128
GDN_DV = 128
GDN_CONV = 4
GDN_CHUNK = 64
MB_HEADS = 8
MB_KV_HEADS = 2
MB_REP = MB_HEADS // MB_KV_HEADS
MB_HD = 128
MB_BLOCK = 256
MB_TOPK = 3
ROT_DIM = MB_HD // 4
ROPE_THETA = 500000.0
N_EXPERTS = 256
TOP_K = 8
N_GROUPS = 8
GROUP_SIZE = N_EXPERTS // N_GROUPS
TOPK_GROUPS = 4
EXPERT_DIM = 256
SHARED_DIM = 256
ROUTED_SCALE = 2.5
MOE_BLOCK = 256
DEEPNORM_ALPHA = (2.0 * DEPTH) ** 0.25
LN_EPS = 1e-5
MASK_NEG = -1e30

QK_COLS = GDN_HEADS * GDN_DK
CONV_CH = 2 * QK_COLS + GDN_HEADS * GDN_DV
LANES = 128
SUBLANES = 8
VMEM_LIMIT = 56 * 1024 * 1024

TOK_TILE = 256
POST_TILE = 512
COMBINE_TILE = 256

PK_BETA, PK_GC, PK_EG, PK_EGD, PK_EGL = 0, 8, 16, 24, 32


def _sigmoid(x):
    return 1.0 / (1.0 + jnp.exp(-x))


def _softplus(x):
    return jnp.maximum(x, 0.0) + jnp.log(1.0 + jnp.exp(-jnp.abs(x)))


def _bdot(a, b):
    return jnp.dot(a.astype(BF16), b.astype(BF16), preferred_element_type=F32)


def _bdot_nt(a, b):
    return lax.dot_general(a.astype(BF16), b.astype(BF16), (((1,), (1,)), ((), ())),
                           preferred_element_type=F32)


def _bdot_tn(a, b):
    return lax.dot_general(a.astype(BF16), b.astype(BF16), (((0,), (0,)), ((), ())),
                           preferred_element_type=F32)


def _layer_norm(x, g, b):
    mu = jnp.mean(x, axis=-1, keepdims=True)
    xc = x - mu
    var = jnp.mean(xc * xc, axis=-1, keepdims=True)
    return xc * lax.rsqrt(var + LN_EPS) * g + b


TOK_ROWS = D_MODEL // LANES


HP_ROWS = TOK_ROWS // 2


def _store_token_tiles(ref, x):
    n, r = x.shape[0], x.shape[1] // LANES
    for j in range(r):
        ref[pl.ds(j, n, stride=r), :] = x[:, j * LANES:(j + 1) * LANES]


def _load_token_tiles(ref, n):
    r = ref.shape[0] // n
    return jnp.concatenate([ref[pl.ds(j, n, stride=r), :] for j in range(r)], axis=1)


def _pack_bf16_pairs(x):
    half = x.shape[1] // 2
    bits = pltpu.bitcast(x.astype(BF16).astype(F32), jnp.uint32)
    return (bits[:, :half] >> 16) | (bits[:, half:] & jnp.uint32(0xFFFF0000))


def _unpack_bf16_pairs(w):
    lo = pltpu.bitcast(w << 16, F32)
    hi = pltpu.bitcast(w & jnp.uint32(0xFFFF0000), F32)
    return jnp.concatenate([lo, hi], axis=1)


def _const_spec(shape):
    nd = len(shape)
    return pl.BlockSpec(shape, lambda *_: (0,) * nd, pipeline_mode=pl.Buffered(1))


def _qkv_kernel(x_ref, w_ref, wp_ref, cw_ref, alog_ref, dtb_ref,
                q_ref, k_ref, v_ref, pack_ref, gct_ref, z_ref):
    tm = x_ref.shape[0]
    halo = SUBLANES

    @pl.when(pl.program_id(0) == 0)
    def _():
        z_ref[0:halo, :] = jnp.zeros((halo, CONV_CH), F32)

    xb = x_ref[...].astype(BF16)
    z_ref[halo:halo + tm, :] = jnp.dot(xb, w_ref[...], preferred_element_type=F32)

    for c in range(CONV_CH // LANES):
        cols = slice(c * LANES, (c + 1) * LANES)
        acc = z_ref[halo:halo + tm, cols] * cw_ref[GDN_CONV - 1:GDN_CONV, cols]
        for s in range(1, GDN_CONV):
            acc = acc + z_ref[halo - s:halo - s + tm, cols] * cw_ref[GDN_CONV - 1 - s:GDN_CONV - s, cols]
        y = acc * _sigmoid(acc)
        h = c % GDN_HEADS
        if c < GDN_HEADS:
            y = y * lax.rsqrt(jnp.sum(y * y, axis=-1, keepdims=True) + 1e-6) * (GDN_DK ** -0.5)
            q_ref[h] = y
        elif c < 2 * GDN_HEADS:
            y = y * lax.rsqrt(jnp.sum(y * y, axis=-1, keepdims=True) + 1e-6)
            k_ref[h] = y
        else:
            v_ref[h] = y

    z_ref[0:halo, :] = z_ref[tm:tm + halo, :]

    zp = jnp.dot(xb, wp_ref[...], preferred_element_type=F32)
    beta = _sigmoid(zp)
    g = -jnp.exp(alog_ref[...]) * _softplus(zp + dtb_ref[...])
    row = lax.broadcasted_iota(jnp.int32, (tm, LANES), 0) % GDN_CHUNK
    gc = g
    step = 1
    while step < GDN_CHUNK:
        gc = gc + jnp.where(row >= step, pltpu.roll(gc, step, axis=0), 0.0)
        step *= 2
    gl = gc.reshape(tm // GDN_CHUNK, GDN_CHUNK, LANES)[:, GDN_CHUNK - 1:GDN_CHUNK, :]
    gl = jnp.broadcast_to(gl, (tm // GDN_CHUNK, GDN_CHUNK, LANES)).reshape(tm, LANES)
    lane = lax.broadcasted_iota(jnp.int32, (tm, LANES), 1)
    pack = jnp.where(lane < PK_GC, beta,
           jnp.where(lane < PK_EG, gc,
           jnp.where(lane < PK_EGD, jnp.exp(gc),
           jnp.where(lane < PK_EGL, jnp.exp(gl - gc), jnp.exp(gl)))))
    pack_ref[...] = pack
    gct_ref[...] = pack.T[PK_GC:PK_GC + GDN_HEADS, :]


def _qkv_proj(x2, w_qkv, w_pack, conv_w, alog_row, dtb_row):
    T = x2.shape[0]
    tm = TOK_TILE
    hd_spec = pl.BlockSpec((GDN_HEADS, tm, LANES), lambda i: (0, i, 0))
    hd_shape = jax.ShapeDtypeStruct((GDN_HEADS, T, LANES), F32)
    return pl.pallas_call(
        _qkv_kernel,
        grid=(T // tm,),
        in_specs=[
            pl.BlockSpec((tm, D_MODEL), lambda i: (i, 0)),
            _const_spec((D_MODEL, CONV_CH)),
            _const_spec((D_MODEL, LANES)),
            _const_spec((GDN_CONV, CONV_CH)),
            _const_spec((1, LANES)),
            _const_spec((1, LANES)),
        ],
        out_specs=[hd_spec, hd_spec, hd_spec,
                   pl.BlockSpec((tm, LANES), lambda i: (i, 0)),
                   pl.BlockSpec((GDN_HEADS, tm), lambda i: (0, i))],
        out_shape=[hd_shape, hd_shape, hd_shape,
                   jax.ShapeDtypeStruct((T, LANES), F32),
                   jax.ShapeDtypeStruct((GDN_HEADS, T), F32)],
        scratch_shapes=[pltpu.VMEM((tm + SUBLANES, CONV_CH), F32)],
        compiler_params=pltpu.CompilerParams(dimension_semantics=("arbitrary",),
                                             vmem_limit_bytes=VMEM_LIMIT),
        name="qkv_proj",
    )(x2, w_qkv, w_pack, conv_w, alog_row, dtb_row)


MB_LROWS = 16
MB_VROWS = MB_HD + MB_LROWS


def _rope(xh, cos_t, sin_t, lane):
    half = ROT_DIM // 2
    swapped = jnp.where(lane < half, pltpu.roll(xh, LANES - half, axis=1), pltpu.roll(xh, half, axis=1))
    return xh * cos_t + swapped * sin_t


def _moba_proj_kernel(x_ref, w_ref, cos_ref, sin_ref, q_ref, k_ref, vt_ref, km_ref):
    tm = x_ref.shape[0]
    z = jnp.dot(x_ref[...].astype(BF16), w_ref[...], preferred_element_type=F32)
    cos_t = cos_ref[...]
    sin_t = sin_ref[...]
    lane = lax.broadcasted_iota(jnp.int32, (tm, LANES), 1)
    for h in range(MB_HEADS):
        q_ref[h] = _rope(z[:, h * MB_HD:(h + 1) * MB_HD], cos_t, sin_t, lane)
    koff = MB_HEADS * MB_HD
    voff = koff + MB_KV_HEADS * MB_HD
    blk_onehot = jnp.where(lane == pl.program_id(0), 1.0, 0.0).astype(BF16)
    for g in range(MB_KV_HEADS):
        kr = _rope(z[:, koff + g * MB_HD:koff + (g + 1) * MB_HD], cos_t, sin_t, lane)
        k_ref[0, g] = jnp.concatenate([kr.astype(BF16), blk_onehot], axis=1)
        km_ref[0, g:g + 1, :] = jnp.mean(kr, axis=0, keepdims=True)
        vt_ref[0, g] = jnp.concatenate([z[:, voff + g * MB_HD:voff + (g + 1) * MB_HD].T,
                                        jnp.ones((MB_LROWS, tm), F32)], axis=0).astype(BF16)


def _moba_proj(x2, w_mb, cos_t, sin_t):
    T = x2.shape[0]
    tm = MB_BLOCK
    nb = T // tm
    return pl.pallas_call(
        _moba_proj_kernel,
        grid=(nb,),
        in_specs=[
            pl.BlockSpec((tm, D_MODEL), lambda i: (i, 0)),
            _const_spec(w_mb.shape),
            pl.BlockSpec((tm, LANES), lambda i: (i, 0)),
            pl.BlockSpec((tm, LANES), lambda i: (i, 0)),
        ],
        out_specs=[
            pl.BlockSpec((MB_HEADS, tm, MB_HD), lambda i: (0, i, 0)),
            pl.BlockSpec((1, MB_KV_HEADS, tm, 2 * MB_HD), lambda i: (i, 0, 0, 0)),
            pl.BlockSpec((1, MB_KV_HEADS, MB_VROWS, tm), lambda i: (i, 0, 0, 0)),
            pl.BlockSpec((1, MB_KV_HEADS, MB_HD), lambda i: (i, 0, 0)),
        ],
        out_shape=[
            jax.ShapeDtypeStruct((MB_HEADS, T, MB_HD), F32),
            jax.ShapeDtypeStruct((nb, MB_KV_HEADS, tm, 2 * MB_HD), BF16),
            jax.ShapeDtypeStruct((nb, MB_KV_HEADS, MB_VROWS, tm), BF16),
            jax.ShapeDtypeStruct((nb, MB_KV_HEADS, MB_HD), F32),
        ],
        compiler_params=pltpu.CompilerParams(dimension_semantics=("arbitrary",),
                                             vmem_limit_bytes=VMEM_LIMIT),
        name="moba_proj",
    )(x2, w_mb, cos_t, sin_t)


def _unit_lower_inverse(a_list):
    c = a_list[0].shape[0]
    rr = lax.broadcasted_iota(jnp.int32, (c, c), 0)
    cc = lax.broadcasted_iota(jnp.int32, (c, c), 1)
    eye = jnp.where(rr == cc, 1.0, 0.0).astype(F32)
    inv = [eye - a for a in a_list]
    p = [_bdot(a, a) for a in a_list]
    n = 2
    while True:
        inv = [x + _bdot(x, y) for x, y in zip(inv, p)]
        n *= 2
        if n >= c:
            break
        p = [_bdot(y, y) for y in p]
    return inv


GDN_STEP_CHUNKS = 4


def _gdn_kernel(q_ref, k_ref, v_ref, pack_ref, gct_ref, nw_ref, o_ref, s_ref):
    C = GDN_CHUNK
    H = range(GDN_HEADS)
    P = [(c, h) for c in range(GDN_STEP_CHUNKS) for h in H]

    @pl.when(pl.program_id(0) == 0)
    def _():
        s_ref[...] = jnp.zeros_like(s_ref)

    rr = lax.broadcasted_iota(jnp.int32, (C, C), 0)
    cc = lax.broadcasted_iota(jnp.int32, (C, C), 1)
    tril = rr >= cc
    strict = rr > cc
    pack = pack_ref[...]
    nw = nw_ref[...]

    def rows(c):
        return slice(c * C, (c + 1) * C)

    def col(base, c, h):
        return pack[rows(c), base + h:base + h + 1]

    q = {(c, h): q_ref[h, rows(c), :] for c, h in P}
    k = {(c, h): k_ref[h, rows(c), :] for c, h in P}
    kb = {(c, h): k[c, h] * col(PK_BETA, c, h) for c, h in P}
    decay = {(c, h): jnp.where(tril, jnp.exp(jnp.where(tril, col(PK_GC, c, h) - gct_ref[c, h:h + 1, :], 0.0)), 0.0)
             for c, h in P}
    sc = {p: _bdot_nt(jnp.concatenate([kb[p], q[p]], axis=0), k[p]) for p in P}
    a = {p: jnp.where(strict, sc[p][:C] * decay[p], 0.0) for p in P}
    aqk = {p: sc[p][C:] * decay[p] for p in P}
    tinv = dict(zip(P, _unit_lower_inverse([a[p] for p in P])))
    rhs = {(c, h): jnp.concatenate([v_ref[h, rows(c), :] * col(PK_BETA, c, h), kb[c, h] * col(PK_EG, c, h)],
                                   axis=1) for c, h in P}
    sol = {p: _bdot(tinv[p], rhs[p]) for p in P}

    s = [s_ref[h] for h in H]
    for c in range(GDN_STEP_CHUNKS):
        wq = [_bdot(jnp.concatenate([sol[c, h][:, GDN_DV:], q[c, h] * col(PK_EG, c, h)], axis=0), s[h])
              for h in H]
        v_new = [sol[c, h][:, :GDN_DV] - wq[h][:C] for h in H]
        o = [wq[h][C:] + _bdot(aqk[c, h], v_new[h]) for h in H]
        ds = [_bdot_tn(k[c, h] * col(PK_EGD, c, h), v_new[h]) for h in H]
        for h in H:
            s[h] = s[h] * pack[c * C:c * C + 1, PK_EGL + h:PK_EGL + h + 1] + ds[h]
            on = o[h] * lax.rsqrt(jnp.mean(o[h] * o[h], axis=-1, keepdims=True) + 1e-6) * nw
            o_ref[rows(c), h * GDN_DV:(h + 1) * GDN_DV] = on
    for h in H:
        s_ref[h] = s[h]


def _gdn(q, k, v, pack, gct3, norm_w_row):
    T = q.shape[1]
    C = GDN_CHUNK * GDN_STEP_CHUNKS
    hd_spec = pl.BlockSpec((GDN_HEADS, C, LANES), lambda i: (0, i, 0))
    return pl.pallas_call(
        _gdn_kernel,
        grid=(T // C,),
        in_specs=[hd_spec, hd_spec, hd_spec,
                  pl.BlockSpec((C, LANES), lambda i: (i, 0)),
                  pl.BlockSpec((GDN_STEP_CHUNKS, GDN_HEADS, GDN_CHUNK), lambda i: (i, 0, 0)),
                  _const_spec((1, GDN_DV))],
        out_specs=pl.BlockSpec((C, GDN_HEADS * GDN_DV), lambda i: (i, 0)),
        out_shape=jax.ShapeDtypeStruct((T, GDN_HEADS * GDN_DV), F32),
        scratch_shapes=[pltpu.VMEM((GDN_HEADS, GDN_DK, GDN_DV), F32)],
        compiler_params=pltpu.CompilerParams(dimension_semantics=("arbitrary",)),
        name="gdn",
    )(q, k, v, pack, gct3, norm_w_row)


MB_QSPLIT = 2
MB_TRIP = 8
MB_MAX_BLOCKS = MB_HD


def _moba_kernel(q_ref, k_ref, vt_ref, km_ref, o_ref, qx_ref, sa_ref, sb_ref, acc_ref, m_ref):
    qi = pl.program_id(1)
    nb = k_ref.shape[0]
    bs = MB_BLOCK
    nq = MB_REP * bs
    nbp = MB_MAX_BLOCKS
    c = (MB_HD ** -0.5) * math.log2(math.e)

    qf = q_ref[...].reshape(nq, MB_HD)

    gate = lax.dot_general(km_ref[...], qf, (((1,), (1,)), ((), ())),
                           precision=lax.Precision.HIGHEST, preferred_element_type=F32)
    blk = lax.broadcasted_iota(jnp.int32, (nbp, nq), 0).astype(F32)
    qif = qi.astype(F32)
    gate = jnp.where(blk < qif, gate, -jnp.inf)
    sel = jnp.where(blk == qif, 1.0, 0.0)
    for _ in range(MB_TOPK):
        top = jnp.max(gate, axis=0, keepdims=True)
        first = jnp.min(jnp.where(gate == top, blk, float(nbp)), axis=0, keepdims=True)
        hit = (blk == first) & (top > -jnp.inf)
        sel = jnp.where(hit, 1.0, sel)
        gate = jnp.where(hit, -jnp.inf, gate)
    bias_t = jnp.where(sel > 0.0, 0.0, MASK_NEG)
    qc = qf * c
    qx_ref[0] = jnp.concatenate([qc, bias_t.T], axis=1).astype(BF16)
    qx_ref[1] = jnp.concatenate([qc, jnp.full((nq, nbp), MASK_NEG, F32)], axis=1).astype(BF16)

    w = nq // MB_QSPLIT
    groups = [slice(h * w, (h + 1) * w) for h in range(MB_QSPLIT)]

    def scores_into(dst_ref, t):
        kx = k_ref[jnp.minimum(t, nb - 1)]
        qset = jnp.where(t < qi, 0, 1)
        for lanes in groups:
            dst_ref[:, lanes] = lax.dot_general(kx, qx_ref[qset, lanes, :], (((1,), (1,)), ((), ())),
                                                preferred_element_type=F32)

    st = lax.dot_general(k_ref[qi], qx_ref[0], (((1,), (1,)), ((), ())), preferred_element_type=F32)
    kpos = lax.broadcasted_iota(jnp.int32, (bs, nq), 0)
    qpos = lax.broadcasted_iota(jnp.int32, (bs, nq), 1) % bs
    st = jnp.where(kpos <= qpos, st, MASK_NEG)
    m0 = jnp.max(st, axis=0, keepdims=True)
    p = jnp.exp2(st - m0)
    m_ref[...] = m0
    acc_ref[...] = jnp.dot(vt_ref[qi], p.astype(BF16), preferred_element_type=F32)

    def absorb(src_ref, t):
        vt = vt_ref[jnp.minimum(t, nb - 1)]
        for lanes in groups:
            st = src_ref[:, lanes]
            m_old = m_ref[:, lanes]
            m_new = jnp.maximum(m_old, jnp.max(st, axis=0, keepdims=True))
            alpha = jnp.exp2(m_old - m_new)
            p = jnp.exp2(st - m_new)
            m_ref[:, lanes] = m_new
            acc_ref[:, lanes] = alpha * acc_ref[:, lanes] + jnp.dot(vt, p.astype(BF16),
                                                                    preferred_element_type=F32)

    scores_into(sa_ref, 0)

    def trip(t, blocks):
        for u in range(0, blocks, 2):
            scores_into(sb_ref, t + u + 1)
            absorb(sa_ref, t + u)
            scores_into(sa_ref, t + u + 2)
            absorb(sb_ref, t + u + 1)

    def body(i, carry):
        trip(MB_TRIP * i, MB_TRIP)
        return carry

    def tail(i, carry):
        trip(done + 2 * i, 2)
        return carry

    done = (qi // MB_TRIP) * MB_TRIP
    lax.fori_loop(0, qi // MB_TRIP, body, 0)
    lax.fori_loop(0, (qi - done + 1) // 2, tail, 0)

    out_t = acc_ref[0:MB_HD, :] / acc_ref[MB_HD:MB_HD + 1, :]
    for r in range(MB_REP):
        o_ref[:, r * MB_HD:(r + 1) * MB_HD] = out_t[:, r * bs:(r + 1) * bs].T


def _moba(mq, mk, mvt, kmean):
    T = mq.shape[1]
    bs = MB_BLOCK
    nb = T // bs
    nq = MB_REP * bs
    assert nb <= MB_MAX_BLOCKS
    kmean = jnp.pad(kmean, ((0, 0), (0, MB_MAX_BLOCKS - nb), (0, 0)))
    return pl.pallas_call(
        _moba_kernel,
        grid=(MB_KV_HEADS, nb),
        in_specs=[
            pl.BlockSpec((MB_REP, bs, MB_HD), lambda g, i: (g, i, 0)),
            pl.BlockSpec((nb, None, bs, 2 * MB_HD), lambda g, i: (0, g, 0, 0)),
            pl.BlockSpec((nb, None, MB_VROWS, bs), lambda g, i: (0, g, 0, 0)),
            pl.BlockSpec((None, MB_MAX_BLOCKS, MB_HD), lambda g, i: (g, 0, 0)),
        ],
        out_specs=pl.BlockSpec((bs, MB_REP * MB_HD), lambda g, i: (i, g)),
        out_shape=jax.ShapeDtypeStruct((T, MB_HEADS * MB_HD), F32),
        scratch_shapes=[pltpu.VMEM((2, nq, 2 * MB_HD), BF16),
                        pltpu.VMEM((bs, nq), F32), pltpu.VMEM((bs, nq), F32),
                        pltpu.VMEM((MB_VROWS, nq), F32),
                        pltpu.VMEM((1, nq), F32)],
        compiler_params=pltpu.CompilerParams(dimension_semantics=("arbitrary", "arbitrary"),
                                             vmem_limit_bytes=VMEM_LIMIT),
        name="moba",
    )(mq, mk, mvt, kmean)


def _post_kernel(x_ref, go_ref, ma_ref, wg_ref, wgp_ref, wmp_ref, wo_ref, g1_ref, b1_ref,
                 rwt_ref, rb_ref, sg_ref, su_ref, sd_ref,
                 hp_ref, res_ref, eidx_ref, wts_ref):
    tm = x_ref.shape[0]
    x = x_ref[...]
    zg = jnp.dot(x.astype(BF16), wg_ref[...], preferred_element_type=F32)
    gate = zg[:, :D_MODEL]
    o = go_ref[...] * (gate * _sigmoid(gate))
    y_gdn = jnp.dot(o.astype(BF16), wgp_ref[...], preferred_element_type=F32)
    y_mb = jnp.dot(ma_ref[...].astype(BF16), wmp_ref[...], preferred_element_type=F32)
    m = _sigmoid(zg[:, D_MODEL:2 * D_MODEL]) * y_gdn + _sigmoid(zg[:, 2 * D_MODEL:]) * y_mb
    mix = jnp.dot(m.astype(BF16), wo_ref[...], preferred_element_type=F32)
    h = _layer_norm(DEEPNORM_ALPHA * x + mix, g1_ref[...], b1_ref[...])
    hb = h.astype(BF16)
    _store_token_tiles(hp_ref, _pack_bf16_pairs(h))

    hs = jnp.dot(hb, sg_ref[...], preferred_element_type=F32)
    hs = hs * _sigmoid(hs) * jnp.dot(hb, su_ref[...], preferred_element_type=F32)
    res_ref[...] = DEEPNORM_ALPHA * h + jnp.dot(hs.astype(BF16), sd_ref[...], preferred_element_type=F32)

    logits = lax.dot_general(rwt_ref[...], h, (((1,), (1,)), ((), ())),
                             precision=lax.Precision.HIGHEST, preferred_element_type=F32)
    scores = _sigmoid(logits)
    choice = scores + rb_ref[...]
    neg = -jnp.inf
    gi = lax.broadcasted_iota(jnp.int32, (GROUP_SIZE, tm), 0).astype(F32)
    gscore = []
    for g in range(N_GROUPS):
        cg = choice[g * GROUP_SIZE:(g + 1) * GROUP_SIZE, :]
        m1 = jnp.max(cg, axis=0, keepdims=True)
        i1 = jnp.min(jnp.where(cg == m1, gi, float(GROUP_SIZE)), axis=0, keepdims=True)
        m2 = jnp.max(jnp.where(gi == i1, neg, cg), axis=0, keepdims=True)
        gscore.append(m1 + m2)
    gs = jnp.concatenate(gscore, axis=0)
    gidx = lax.broadcasted_iota(jnp.int32, (N_GROUPS, tm), 0).astype(F32)
    gsel = jnp.zeros((N_GROUPS, tm), F32)
    for _ in range(TOPK_GROUPS):
        top = jnp.max(gs, axis=0, keepdims=True)
        first = jnp.min(jnp.where(gs == top, gidx, float(N_GROUPS)), axis=0, keepdims=True)
        hit = gidx == first
        gsel = jnp.where(hit, 1.0, gsel)
        gs = jnp.where(hit, neg, gs)
    masked = jnp.concatenate(
        [jnp.where(gsel[g:g + 1, :] > 0.0, choice[g * GROUP_SIZE:(g + 1) * GROUP_SIZE, :], neg)
         for g in range(N_GROUPS)], axis=0)
    ei = lax.broadcasted_iota(jnp.int32, (N_EXPERTS, tm), 0).astype(F32)
    idx_rows, w_rows = [], []
    for _ in range(TOP_K):
        top = jnp.max(masked, axis=0, keepdims=True)
        first = jnp.min(jnp.where(masked == top, ei, float(N_EXPERTS)), axis=0, keepdims=True)
        hit = ei == first
        idx_rows.append(first)
        w_rows.append(jnp.sum(jnp.where(hit, scores, 0.0), axis=0, keepdims=True))
        masked = jnp.where(hit, neg, masked)
    w = jnp.concatenate(w_rows, axis=0)
    w = w / (jnp.sum(w, axis=0, keepdims=True) + 1e-20) * ROUTED_SCALE
    eidx_ref[...] = jnp.concatenate(idx_rows, axis=0).astype(jnp.int32)
    wts_ref[...] = w


def _post(x2, gdn_o, moba_a, w_gates, w_gdn_proj, w_moba_proj, w_out, ln_g, ln_b,
          router_wt, router_b_col, sh_g, sh_u, sh_d):
    T = x2.shape[0]
    tm = POST_TILE
    row_spec = pl.BlockSpec((tm, D_MODEL), lambda i: (i, 0))
    consts = [w_gates, w_gdn_proj, w_moba_proj, w_out, ln_g, ln_b, router_wt, router_b_col,
              sh_g, sh_u, sh_d]
    return pl.pallas_call(
        _post_kernel,
        grid=(T // tm,),
        in_specs=[row_spec, row_spec, row_spec] + [_const_spec(c.shape) for c in consts],
        out_specs=[pl.BlockSpec((tm * HP_ROWS, LANES), lambda i: (i, 0)), row_spec,
                   pl.BlockSpec((TOP_K, tm), lambda i: (0, i)),
                   pl.BlockSpec((TOP_K, tm), lambda i: (0, i))],
        out_shape=[jax.ShapeDtypeStruct((T * HP_ROWS, LANES), jnp.uint32),
                   jax.ShapeDtypeStruct((T, D_MODEL), F32),
                   jax.ShapeDtypeStruct((TOP_K, T), jnp.int32),
                   jax.ShapeDtypeStruct((TOP_K, T), F32)],
        compiler_params=pltpu.CompilerParams(dimension_semantics=("arbitrary",),
                                             vmem_limit_bytes=VMEM_LIMIT),
        name="post",
    )(x2, gdn_o, moba_a, *consts)


ROW_UNROLL = 8


def _experts_kernel(iblk_ref, iexp_ref, ilo_ref, ihi_ref, src_ref, dst_ref,
                    hp_ref, wg_ref, wu_ref, wd_ref, out_ref,
                    xbuf, ybuf, wgb, wub, wdb, cast_exp, ssem):
    p = pl.program_id(0)
    n_items = pl.num_programs(0)
    w = iblk_ref[p]
    slot = w % 2
    first = (p == 0) | (w != iblk_ref[jnp.maximum(p - 1, 0)])
    last = (p == n_items - 1) | (w != iblk_ref[jnp.minimum(p + 1, n_items - 1)])
    lo, hi = ilo_ref[p], ihi_ref[p]
    block_rows = MOE_BLOCK * TOK_ROWS

    def tile_at(ref, row0, rows):
        return ref.at[pl.ds(pl.multiple_of(row0, rows), rows)]

    def for_block_rows(fn):
        def group(g, c):
            for u in range(ROW_UNROLL):
                fn(g * ROW_UNROLL + u)
            return c
        lax.fori_loop(0, MOE_BLOCK // ROW_UNROLL, group, 0)

    def gather_block():
        def one(i):
            xbuf[pl.ds(pl.multiple_of(i * HP_ROWS, HP_ROWS), HP_ROWS), :] = (
                hp_ref[pl.ds(pl.multiple_of(src_ref[0, 0, i], HP_ROWS), HP_ROWS), :])
        for_block_rows(one)

    def start_scatter(s):
        for_block_rows(lambda i: pltpu.make_async_copy(
            tile_at(ybuf.at[s], i * TOK_ROWS, TOK_ROWS), tile_at(out_ref, dst_ref[0, 0, i], TOK_ROWS),
            ssem.at[s]).start())

    def wait_scatter(s):
        pltpu.make_async_copy(ybuf.at[s], out_ref.at[pl.ds(0, block_rows)], ssem.at[s]).wait()

    @pl.when(p == 0)
    def _():
        ybuf[...] = jnp.zeros_like(ybuf)
        cast_exp[0] = -1

    @pl.when(first)
    def _():
        @pl.when(w >= 2)
        def _():
            wait_scatter(slot)
        gather_block()

    @pl.when(hi > lo)
    def _():
        @pl.when(iexp_ref[p] != cast_exp[0])
        def _():
            wgb[...] = wg_ref[...].astype(BF16)
            wub[...] = wu_ref[...].astype(BF16)
            wdb[...] = wd_ref[...].astype(BF16)
            cast_exp[0] = iexp_ref[p]

        xb = _unpack_bf16_pairs(_load_token_tiles(xbuf, MOE_BLOCK)).astype(BF16)
        hg = jnp.dot(xb, wgb[...], preferred_element_type=F32)
        hu = jnp.dot(xb, wub[...], preferred_element_type=F32)
        hb = (hg * _sigmoid(hg) * hu).astype(BF16)
        y = jnp.dot(hb, wdb[...], preferred_element_type=F32)
        row = lax.broadcasted_iota(jnp.int32, (MOE_BLOCK, 1), 0)
        mine = (row >= lo) & (row < hi)
        _store_token_tiles(ybuf.at[slot], jnp.where(mine, y, _load_token_tiles(ybuf.at[slot], MOE_BLOCK)))

    @pl.when(last)
    def _():
        start_scatter(slot)

    @pl.when(p == n_items - 1)
    def _():
        @pl.when(w >= 1)
        def _():
            wait_scatter(1 - slot)
        wait_scatter(slot)


def _experts(hp, src_rows, dst_rows, item_blk, item_exp, item_lo, item_hi, wg, wu, wd):
    T = hp.shape[0] // HP_ROWS
    n_items = item_blk.shape[0]
    rows = MOE_BLOCK
    ids_spec = pl.BlockSpec((1, 1, rows), lambda p, ib, ie, il, ih: (ib[p], 0, 0), memory_space=pltpu.SMEM)

    def w_spec(shape):
        return pl.BlockSpec((None,) + shape, lambda p, ib, ie, il, ih: (ie[p], 0, 0))

    grid_spec = pltpu.PrefetchScalarGridSpec(
        num_scalar_prefetch=4,
        grid=(n_items,),
        in_specs=[ids_spec, ids_spec, _const_spec(hp.shape),
                  w_spec((D_MODEL, EXPERT_DIM)), w_spec((D_MODEL, EXPERT_DIM)), w_spec((EXPERT_DIM, D_MODEL))],
        out_specs=pl.BlockSpec(memory_space=pl.ANY),
        scratch_shapes=[pltpu.VMEM((rows * HP_ROWS, LANES), jnp.uint32),
                        pltpu.VMEM((2, rows * TOK_ROWS, LANES), F32),
                        pltpu.VMEM((D_MODEL, EXPERT_DIM), BF16), pltpu.VMEM((D_MODEL, EXPERT_DIM), BF16),
                        pltpu.VMEM((EXPERT_DIM, D_MODEL), BF16),
                        pltpu.SMEM((1,), jnp.int32),
                        pltpu.SemaphoreType.DMA((2,))],
    )
    return pl.pallas_call(
        _experts_kernel,
        grid_spec=grid_spec,
        out_shape=jax.ShapeDtypeStruct((TOP_K * T * TOK_ROWS, LANES), F32),
        compiler_params=pltpu.CompilerParams(dimension_semantics=("arbitrary",),
                                             vmem_limit_bytes=VMEM_LIMIT,
                                             disable_bounds_checks=True),
        name="experts",
    )(item_blk, item_exp, item_lo, item_hi, src_rows, dst_rows, hp, wg, wu, wd)


def _combine_kernel(res_ref, y_ref, w_ref, g_ref, b_ref, o_ref):
    tm = res_ref.shape[0]
    acc = res_ref[...]
    w = w_ref[...]
    for k in range(TOP_K):
        acc = acc + w[:, k:k + 1] * _load_token_tiles(y_ref.at[k], tm)
    o_ref[...] = _layer_norm(acc, g_ref[...], b_ref[...])


def _combine(res, y3, wts, ln_g, ln_b):
    T = res.shape[0]
    tm = COMBINE_TILE
    return pl.pallas_call(
        _combine_kernel,
        grid=(T // tm,),
        in_specs=[pl.BlockSpec((tm, D_MODEL), lambda i: (i, 0)),
                  pl.BlockSpec((TOP_K, tm * TOK_ROWS, LANES), lambda i: (0, i, 0)),
                  pl.BlockSpec((tm, TOP_K), lambda i: (i, 0)),
                  _const_spec((1, D_MODEL)), _const_spec((1, D_MODEL))],
        out_specs=pl.BlockSpec((tm, D_MODEL), lambda i: (i, 0)),
        out_shape=jax.ShapeDtypeStruct((T, D_MODEL), F32),
        compiler_params=pltpu.CompilerParams(dimension_semantics=("arbitrary",),
                                             vmem_limit_bytes=VMEM_LIMIT),
        name="combine",
    )(res, y3, wts, ln_g, ln_b)


def _dispatch_plan(eidx_t, n_tok):
    nk = n_tok * TOP_K
    n_blocks = nk // MOE_BLOCK
    e_flat = eidx_t.T.reshape(nk)
    _, order = lax.sort((e_flat, jnp.arange(nk, dtype=jnp.int32)), num_keys=1)
    experts = jnp.arange(N_EXPERTS, dtype=jnp.int32)
    counts = jnp.sum((e_flat[None, :] == experts[:, None]).astype(jnp.int32), axis=1)
    start = jnp.cumsum(counts) - counts
    pos = jnp.sort(jnp.concatenate([jnp.arange(n_blocks, dtype=jnp.int32) * MOE_BLOCK, start]))
    nxt = jnp.concatenate([pos[1:], jnp.full((1,), nk, jnp.int32)])
    item_blk = jnp.minimum(pos // MOE_BLOCK, n_blocks - 1)
    item_exp = jnp.sum((start[None, :] <= pos[:, None]).astype(jnp.int32), axis=1) - 1
    item_lo = pos - item_blk * MOE_BLOCK
    item_hi = jnp.minimum(nxt, (item_blk + 1) * MOE_BLOCK) - item_blk * MOE_BLOCK
    tok, slot_k = order // TOP_K, order % TOP_K
    shape3 = (n_blocks, 1, MOE_BLOCK)
    src_rows = (tok * HP_ROWS).reshape(shape3)
    dst_rows = ((slot_k * n_tok + tok) * TOK_ROWS).reshape(shape3)
    return src_rows, dst_rows, item_blk, item_exp, item_lo, item_hi


def kernel(x, w_in, conv_w, gdn_a_log, gdn_dt_bias, gdn_norm_w, w_gdn_proj, w_moba_proj, w_out,
           ln1_g, ln1_b, router_w, router_bias, exp_w_gate, exp_w_up, exp_w_down,
           sh_w_gate, sh_w_up, sh_w_down, ln2_g, ln2_b):
    B, T, D = x.shape
    assert B == 1 and D == D_MODEL and T % MB_BLOCK == 0
    x2 = x.reshape(T, D)

    o_gate = CONV_CH
    o_b = o_gate + GDN_HEADS * GDN_DV
    o_a = o_b + GDN_HEADS
    o_mq = o_a + GDN_HEADS
    o_gg = o_mq + (MB_HEADS + 2 * MB_KV_HEADS) * MB_HD
    w_qkv = w_in[:, :CONV_CH].astype(BF16)
    w_pack = jnp.concatenate([w_in[:, o_b:o_a]] + [w_in[:, o_a:o_mq]] * 4
                             + [jnp.zeros((D, LANES - 5 * GDN_HEADS), F32)], axis=1).astype(BF16)
    w_mb = w_in[:, o_mq:o_gg].astype(BF16)
    w_gates = jnp.concatenate([w_in[:, o_gate:o_b], w_in[:, o_gg:]], axis=1).astype(BF16)

    def lane_row(v):
        return jnp.zeros((1, LANES), F32).at[0, PK_GC:PK_GC + 4 * GDN_HEADS].set(jnp.tile(v.astype(F32), 4))

    q, k, v, pack, gct = _qkv_proj(x2, w_qkv, w_pack, conv_w.astype(F32),
                                   lane_row(gdn_a_log), lane_row(gdn_dt_bias))
    gct3 = gct.reshape(GDN_HEADS, T // GDN_CHUNK, GDN_CHUNK).transpose(1, 0, 2)
    gdn_o = _gdn(q, k, v, pack, gct3, gdn_norm_w.astype(F32).reshape(1, GDN_DV))

    half = ROT_DIM // 2
    inv = ROPE_THETA ** (-jnp.arange(half, dtype=F32) / half)
    ang = jnp.arange(T).astype(F32)[:, None] * inv[None, :]
    ones = jnp.ones((T, MB_HD - ROT_DIM), F32)
    cos_t = jnp.concatenate([jnp.cos(ang), jnp.cos(ang), ones], axis=1)
    sin_t = jnp.concatenate([-jnp.sin(ang), jnp.sin(ang), 0.0 * ones], axis=1)
    mq, mk, mvt, kmean = _moba_proj(x2, w_mb, cos_t, sin_t)
    moba_a = _moba(mq, mk, mvt, kmean.transpose(1, 0, 2))

    hp, res, eidx_t, wts_t = _post(
        x2, gdn_o, moba_a, w_gates, w_gdn_proj.astype(BF16), w_moba_proj.astype(BF16),
        w_out.astype(BF16), ln1_g.reshape(1, D), ln1_b.reshape(1, D),
        router_w.T, router_bias.reshape(N_EXPERTS, 1),
        sh_w_gate.astype(BF16), sh_w_up.astype(BF16), sh_w_down.astype(BF16))

    plan = _dispatch_plan(eidx_t, T)
    y = _experts(hp, *plan, exp_w_gate, exp_w_up, exp_w_down)
    y3 = y.reshape(TOP_K, T * TOK_ROWS, LANES)
    out = _combine(res, y3, wts_t.T, ln2_g.reshape(1, D), ln2_b.reshape(1, D))
    return out.reshape(B, T, D)
```

```python
import functools
import math

import jax
import jax.numpy as jnp
from jax import lax
from jax.experimental import pallas as pl
from jax.experimental.pallas import tpu as pltpu

F32 = jnp.float32
BF16 = jnp.bfloat16

D_MODEL = 1024
DEPTH = 1
GDN_HEADS = 8
GDN_DK = 128
GDN_DV = 128
GDN_CONV = 4
GDN_CHUNK = 64
MB_HEADS = 8
MB_KV_HEADS = 2
MB_REP = MB_HEADS // MB_KV_HEADS
MB_HD = 128
MB_BLOCK = 256
MB_TOPK = 3
ROT_DIM = MB_HD // 4
ROPE_THETA = 500000.0
N_EXPERTS = 256
TOP_K = 8
N_GROUPS = 8
GROUP_SIZE = N_EXPERTS // N_GROUPS
TOPK_GROUPS = 4
EXPERT_DIM = 256
SHARED_DIM = 256
ROUTED_SCALE = 2.5
MOE_BLOCK = 256
DEEPNORM_ALPHA = (2.0 * DEPTH) ** 0.25
LN_EPS = 1e-5
MASK_NEG = -1e30

QK_COLS = GDN_HEADS * GDN_DK
CONV_CH = 2 * QK_COLS + GDN_HEADS * GDN_DV
LANES = 128
SUBLANES = 8
VMEM_LIMIT = 56 * 1024 * 1024

TOK_TILE = 256
POST_TILE = 512
COMBINE_TILE = 256

PK_BETA, PK_GC, PK_EG, PK_EGD, PK_EGL = 0, 8, 16, 24, 32


def _sigmoid(x):
    return 1.0 / (1.0 + jnp.exp(-x))


def _softplus(x):
    return jnp.maximum(x, 0.0) + jnp.log(1.0 + jnp.exp(-jnp.abs(x)))


def _bdot(a, b):
    return jnp.dot(a.astype(BF16), b.astype(BF16), preferred_element_type=F32)


def _bdot_nt(a, b):
    return lax.dot_general(a.astype(BF16), b.astype(BF16), (((1,), (1,)), ((), ())),
                           preferred_element_type=F32)


def _bdot_tn(a, b):
    return lax.dot_general(a.astype(BF16), b.astype(BF16), (((0,), (0,)), ((), ())),
                           preferred_element_type=F32)


def _layer_norm(x, g, b):
    mu = jnp.mean(x, axis=-1, keepdims=True)
    xc = x - mu
    var = jnp.mean(xc * xc, axis=-1, keepdims=True)
    return xc * lax.rsqrt(var + LN_EPS) * g + b


TOK_ROWS = D_MODEL // LANES


HP_ROWS = TOK_ROWS // 2


def _store_token_tiles(ref, x):
    n, r = x.shape[0], x.shape[1] // LANES
    for j in range(r):
        ref[pl.ds(j, n, stride=r), :] = x[:, j * LANES:(j + 1) * LANES]


def _load_token_tiles(ref, n):
    r = ref.shape[0] // n
    return jnp.concatenate([ref[pl.ds(j, n, stride=r), :] for j in range(r)], axis=1)


def _pack_bf16_pairs(x):
    half = x.shape[1] // 2
    bits = pltpu.bitcast(x.astype(BF16).astype(F32), jnp.uint32)
    return (bits[:, :half] >> 16) | (bits[:, half:] & jnp.uint32(0xFFFF0000))


def _unpack_bf16_pairs(w):
    lo = pltpu.bitcast(w << 16, F32)
    hi = pltpu.bitcast(w & jnp.uint32(0xFFFF0000), F32)
    return jnp.concatenate([lo, hi], axis=1)


def _const_spec(shape):
    nd = len(shape)
    return pl.BlockSpec(shape, lambda *_: (0,) * nd, pipeline_mode=pl.Buffered(1))


def _qkv_kernel(x_ref, w_ref, wp_ref, cw_ref, alog_ref, dtb_ref,
                q_ref, k_ref, v_ref, pack_ref, gct_ref, z_ref):
    tm = x_ref.shape[0]
    halo = SUBLANES

    @pl.when(pl.program_id(0) == 0)
    def _():
        z_ref[0:halo, :] = jnp.zeros((halo, CONV_CH), F32)

    xb = x_ref[...].astype(BF16)
    z_ref[halo:halo + tm, :] = jnp.dot(xb, w_ref[...], preferred_element_type=F32)

    for c in range(CONV_CH // LANES):
        cols = slice(c * LANES, (c + 1) * LANES)
        acc = z_ref[halo:halo + tm, cols] * cw_ref[GDN_CONV - 1:GDN_CONV, cols]
        for s in range(1, GDN_CONV):
            acc = acc + z_ref[halo - s:halo - s + tm, cols] * cw_ref[GDN_CONV - 1 - s:GDN_CONV - s, cols]
        y = acc * _sigmoid(acc)
        h = c % GDN_HEADS
        if c < GDN_HEADS:
            y = y * lax.rsqrt(jnp.sum(y * y, axis=-1, keepdims=True) + 1e-6) * (GDN_DK ** -0.5)
            q_ref[h] = y
        elif c < 2 * GDN_HEADS:
            y = y * lax.rsqrt(jnp.sum(y * y, axis=-1, keepdims=True) + 1e-6)
            k_ref[h] = y
        else:
            v_ref[h] = y

    z_ref[0:halo, :] = z_ref[tm:tm + halo, :]

    zp = jnp.dot(xb, wp_ref[...], preferred_element_type=F32)
    beta = _sigmoid(zp)
    g = -jnp.exp(alog_ref[...]) * _softplus(zp + dtb_ref[...])
    row = lax.broadcasted_iota(jnp.int32, (tm, LANES), 0) % GDN_CHUNK
    gc = g
    step = 1
    while step < GDN_CHUNK:
        gc = gc + jnp.where(row >= step, pltpu.roll(gc, step, axis=0), 0.0)
        step *= 2
    gl = gc.reshape(tm // GDN_CHUNK, GDN_CHUNK, LANES)[:, GDN_CHUNK - 1:GDN_CHUNK, :]
    gl = jnp.broadcast_to(gl, (tm // GDN_CHUNK, GDN_CHUNK, LANES)).reshape(tm, LANES)
    lane = lax.broadcasted_iota(jnp.int32, (tm, LANES), 1)
    pack = jnp.where(lane < PK_GC, beta,
           jnp.where(lane < PK_EG, gc,
           jnp.where(lane < PK_EGD, jnp.exp(gc),
           jnp.where(lane < PK_EGL, jnp.exp(gl - gc), jnp.exp(gl)))))
    pack_ref[...] = pack
    gct_ref[...] = pack.T[PK_GC:PK_GC + GDN_HEADS, :]


def _qkv_proj(x2, w_qkv, w_pack, conv_w, alog_row, dtb_row):
    T = x2.shape[0]
    tm = TOK_TILE
    hd_spec = pl.BlockSpec((GDN_HEADS, tm, LANES), lambda i: (0, i, 0))
    hd_shape = jax.ShapeDtypeStruct((GDN_HEADS, T, LANES), F32)
    return pl.pallas_call(
        _qkv_kernel,
        grid=(T // tm,),
        in_specs=[
            pl.BlockSpec((tm, D_MODEL), lambda i: (i, 0)),
            _const_spec((D_MODEL, CONV_CH)),
            _const_spec((D_MODEL, LANES)),
            _const_spec((GDN_CONV, CONV_CH)),
            _const_spec((1, LANES)),
            _const_spec((1, LANES)),
        ],
        out_specs=[hd_spec, hd_spec, hd_spec,
                   pl.BlockSpec((tm, LANES), lambda i: (i, 0)),
                   pl.BlockSpec((GDN_HEADS, tm), lambda i: (0, i))],
        out_shape=[hd_shape, hd_shape, hd_shape,
                   jax.ShapeDtypeStruct((T, LANES), F32),
                   jax.ShapeDtypeStruct((GDN_HEADS, T), F32)],
        scratch_shapes=[pltpu.VMEM((tm + SUBLANES, CONV_CH), F32)],
        compiler_params=pltpu.CompilerParams(dimension_semantics=("arbitrary",),
                                             vmem_limit_bytes=VMEM_LIMIT),
        name="qkv_proj",
    )(x2, w_qkv, w_pack, conv_w, alog_row, dtb_row)


MB_LROWS = 16
MB_VROWS = MB_HD + MB_LROWS


def _rope(xh, cos_t, sin_t, lane):
    half = ROT_DIM // 2
    swapped = jnp.where(lane < half, pltpu.roll(xh, LANES - half, axis=1), pltpu.roll(xh, half, axis=1))
    return xh * cos_t + swapped * sin_t


def _moba_proj_kernel(x_ref, w_ref, cos_ref, sin_ref, q_ref, k_ref, vt_ref, km_ref):
    tm = x_ref.shape[0]
    z = jnp.dot(x_ref[...].astype(BF16), w_ref[...], preferred_element_type=F32)
    cos_t = cos_ref[...]
    sin_t = sin_ref[...]
    lane = lax.broadcasted_iota(jnp.int32, (tm, LANES), 1)
    for h in range(MB_HEADS):
        q_ref[h] = _rope(z[:, h * MB_HD:(h + 1) * MB_HD], cos_t, sin_t, lane)
    koff = MB_HEADS * MB_HD
    voff = koff + MB_KV_HEADS * MB_HD
    blk_onehot = jnp.where(lane == pl.program_id(0), 1.0, 0.0).astype(BF16)
    for g in range(MB_KV_HEADS):
        kr = _rope(z[:, koff + g * MB_HD:koff + (g + 1) * MB_HD], cos_t, sin_t, lane)
        k_ref[0, g] = jnp.concatenate([kr.astype(BF16), blk_onehot], axis=1)
        km_ref[0, g:g + 1, :] = jnp.mean(kr, axis=0, keepdims=True)
        vt_ref[0, g] = jnp.concatenate([z[:, voff + g * MB_HD:voff + (g + 1) * MB_HD].T,
                                        jnp.ones((MB_LROWS, tm), F32)], axis=0).astype(BF16)


def _moba_proj(x2, w_mb, cos_t, sin_t):
    T = x2.shape[0]
    tm = MB_BLOCK
    nb = T // tm
    return pl.pallas_call(
        _moba_proj_kernel,
        grid=(nb,),
        in_specs=[
            pl.BlockSpec((tm, D_MODEL), lambda i: (i, 0)),
            _const_spec(w_mb.shape),
            pl.BlockSpec((tm, LANES), lambda i: (i, 0)),
            pl.BlockSpec((tm, LANES), lambda i: (i, 0)),
        ],
        out_specs=[
            pl.BlockSpec((MB_HEADS, tm, MB_HD), lambda i: (0, i, 0)),
            pl.BlockSpec((1, MB_KV_HEADS, tm, 2 * MB_HD), lambda i: (i, 0, 0, 0)),
            pl.BlockSpec((1, MB_KV_HEADS, MB_VROWS, tm), lambda i: (i, 0, 0, 0)),
            pl.BlockSpec((1, MB_KV_HEADS, MB_HD), lambda i: (i, 0, 0)),
        ],
        out_shape=[
            jax.ShapeDtypeStruct((MB_HEADS, T, MB_HD), F32),
            jax.ShapeDtypeStruct((nb, MB_KV_HEADS, tm, 2 * MB_HD), BF16),
            jax.ShapeDtypeStruct((nb, MB_KV_HEADS, MB_VROWS, tm), BF16),
            jax.ShapeDtypeStruct((nb, MB_KV_HEADS, MB_HD), F32),
        ],
        compiler_params=pltpu.CompilerParams(dimension_semantics=("arbitrary",),
                                             vmem_limit_bytes=VMEM_LIMIT),
        name="moba_proj",
    )(x2, w_mb, cos_t, sin_t)


def _unit_lower_inverse(a_list):
    c = a_list[0].shape[0]
    rr = lax.broadcasted_iota(jnp.int32, (c, c), 0)
    cc = lax.broadcasted_iota(jnp.int32, (c, c), 1)
    eye = jnp.where(rr == cc, 1.0, 0.0).astype(F32)
    inv = [eye - a for a in a_list]
    p = [_bdot(a, a) for a in a_list]
    n = 2
    while True:
        inv = [x + _bdot(x, y) for x, y in zip(inv, p)]
        n *= 2
        if n >= c:
            break
        p = [_bdot(y, y) for y in p]
    return inv


GDN_STEP_CHUNKS = 4


def _gdn_kernel(q_ref, k_ref, v_ref, pack_ref, gct_ref, nw_ref, o_ref, s_ref):
    C = GDN_CHUNK
    H = range(GDN_HEADS)
    P = [(c, h) for c in range(GDN_STEP_CHUNKS) for h in H]

    @pl.when(pl.program_id(0) == 0)
    def _():
        s_ref[...] = jnp.zeros_like(s_ref)

    rr = lax.broadcasted_iota(jnp.int32, (C, C), 0)
    cc = lax.broadcasted_iota(jnp.int32, (C, C), 1)
    tril = rr >= cc
    strict = rr > cc
    pack = pack_ref[...]
    nw = nw_ref[...]

    def rows(c):
        return slice(c * C, (c + 1) * C)

    def col(base, c, h):
        return pack[rows(c), base + h:base + h + 1]

    q = {(c, h): q_ref[h, rows(c), :] for c, h in P}
    k = {(c, h): k_ref[h, rows(c), :] for c, h in P}
    kb = {(c, h): k[c, h] * col(PK_BETA, c, h) for c, h in P}
    decay = {(c, h): jnp.where(tril, jnp.exp(jnp.where(tril, col(PK_GC, c, h) - gct_ref[c, h:h + 1, :], 0.0)), 0.0)
             for c, h in P}
    sc = {p: _bdot_nt(jnp.concatenate([kb[p], q[p]], axis=0), k[p]) for p in P}
    a = {p: jnp.where(strict, sc[p][:C] * decay[p], 0.0) for p in P}
    aqk = {p: sc[p][C:] * decay[p] for p in P}
    tinv = dict(zip(P, _unit_lower_inverse([a[p] for p in P])))
    rhs = {(c, h): jnp.concatenate([v_ref[h, rows(c), :] * col(PK_BETA, c, h), kb[c, h] * col(PK_EG, c, h)],
                                   axis=1) for c, h in P}
    sol = {p: _bdot(tinv[p], rhs[p]) for p in P}

    s = [s_ref[h] for h in H]
    for c in range(GDN_STEP_CHUNKS):
        wq = [_bdot(jnp.concatenate([sol[c, h][:, GDN_DV:], q[c, h] * col(PK_EG, c, h)], axis=0), s[h])
              for h in H]
        v_new = [sol[c, h][:, :GDN_DV] - wq[h][:C] for h in H]
        o = [wq[h][C:] + _bdot(aqk[c, h], v_new[h]) for h in H]
        ds = [_bdot_tn(k[c, h] * col(PK_EGD, c, h), v_new[h]) for h in H]
        for h in H:
            s[h] = s[h] * pack[c * C:c * C + 1, PK_EGL + h:PK_EGL + h + 1] + ds[h]
            on = o[h] * lax.rsqrt(jnp.mean(o[h] * o[h], axis=-1, keepdims=True) + 1e-6) * nw
            o_ref[rows(c), h * GDN_DV:(h + 1) * GDN_DV] = on
    for h in H:
        s_ref[h] = s[h]


def _gdn(q, k, v, pack, gct3, norm_w_row):
    T = q.shape[1]
    C = GDN_CHUNK * GDN_STEP_CHUNKS
    hd_spec = pl.BlockSpec((GDN_HEADS, C, LANES), lambda i: (0, i, 0))
    return pl.pallas_call(
        _gdn_kernel,
        grid=(T // C,),
        in_specs=[hd_spec, hd_spec, hd_spec,
                  pl.BlockSpec((C, LANES), lambda i: (i, 0)),
                  pl.BlockSpec((GDN_STEP_CHUNKS, GDN_HEADS, GDN_CHUNK), lambda i: (i, 0, 0)),
                  _const_spec((1, GDN_DV))],
        out_specs=pl.BlockSpec((C, GDN_HEADS * GDN_DV), lambda i: (i, 0)),
        out_shape=jax.ShapeDtypeStruct((T, GDN_HEADS * GDN_DV), F32),
        scratch_shapes=[pltpu.VMEM((GDN_HEADS, GDN_DK, GDN_DV), F32)],
        compiler_params=pltpu.CompilerParams(dimension_semantics=("arbitrary",)),
        name="gdn",
    )(q, k, v, pack, gct3, norm_w_row)


MB_QSPLIT = 2
MB_TRIPS = (8, 4, 2)
MB_MAX_BLOCKS = MB_HD


def _moba_kernel(q_ref, k_ref, vt_ref, km_ref, o_ref, qx_ref, sa_ref, sb_ref, acc_ref, m_ref):
    qi = pl.program_id(1)
    nb = k_ref.shape[0]
    bs = MB_BLOCK
    nq = MB_REP * bs
    nbp = MB_MAX_BLOCKS
    c = (MB_HD ** -0.5) * math.log2(math.e)

    qf = q_ref[...].reshape(nq, MB_HD)

    gate = lax.dot_general(km_ref[...], qf, (((1,), (1,)), ((), ())),
                           precision=lax.Precision.HIGHEST, preferred_element_type=F32)
    blk = lax.broadcasted_iota(jnp.int32, (nbp, nq), 0).astype(F32)
    qif = qi.astype(F32)
    gate = jnp.where(blk < qif, gate, -jnp.inf)
    sel = jnp.where(blk == qif, 1.0, 0.0)
    for _ in range(MB_TOPK):
        top = jnp.max(gate, axis=0, keepdims=True)
        first = jnp.min(jnp.where(gate == top, blk, float(nbp)), axis=0, keepdims=True)
        hit = (blk == first) & (top > -jnp.inf)
        sel = jnp.where(hit, 1.0, sel)
        gate = jnp.where(hit, -jnp.inf, gate)
    bias_t = jnp.where(sel > 0.0, 0.0, MASK_NEG)
    qc = qf * c
    qx_ref[0] = jnp.concatenate([qc, bias_t.T], axis=1).astype(BF16)
    qx_ref[1] = jnp.concatenate([qc, jnp.full((nq, nbp), MASK_NEG, F32)], axis=1).astype(BF16)

    w = nq // MB_QSPLIT
    groups = [slice(h * w, (h + 1) * w) for h in range(MB_QSPLIT)]

    def scores_into(dst_ref, t):
        kx = k_ref[jnp.minimum(t, nb - 1)]
        qset = jnp.where(t < qi, 0, 1)
        for lanes in groups:
            dst_ref[:, lanes] = lax.dot_general(kx, qx_ref[qset, lanes, :], (((1,), (1,)), ((), ())),
                                                preferred_element_type=F32)

    st = lax.dot_general(k_ref[qi], qx_ref[0], (((1,), (1,)), ((), ())), preferred_element_type=F32)
    kpos = lax.broadcasted_iota(jnp.int32, (bs, nq), 0)
    qpos = lax.broadcasted_iota(jnp.int32, (bs, nq), 1) % bs
    st = jnp.where(kpos <= qpos, st, MASK_NEG)
    m0 = jnp.max(st, axis=0, keepdims=True)
    p = jnp.exp2(st - m0)
    m_ref[...] = m0
    acc_ref[...] = jnp.dot(vt_ref[qi], p.astype(BF16), preferred_element_type=F32)

    def absorb(src_ref, t):
        vt = vt_ref[jnp.minimum(t, nb - 1)]
        for lanes in groups:
            st = src_ref[:, lanes]
            m_old = m_ref[:, lanes]
            m_new = jnp.maximum(m_old, jnp.max(st, axis=0, keepdims=True))
            alpha = jnp.exp2(m_old - m_new)
            p = jnp.exp2(st - m_new)
            m_ref[:, lanes] = m_new
            acc_ref[:, lanes] = alpha * acc_ref[:, lanes] + jnp.dot(vt, p.astype(BF16),
                                                                    preferred_element_type=F32)

    scores_into(sa_ref, 0)

    def trip(t, blocks):
        for u in range(0, blocks, 2):
            scores_into(sb_ref, t + u + 1)
            absorb(sa_ref, t + u)
            scores_into(sa_ref, t + u + 2)
            absorb(sb_ref, t + u + 1)

    done = 0
    for blocks in MB_TRIPS:
        left = qi - done
        trips = (left + 1) // 2 if blocks == 2 else left // blocks

        def body(i, carry, blocks=blocks, done=done):
            trip(done + blocks * i, blocks)
            return carry

        lax.fori_loop(0, trips, body, 0)
        done = done + trips * blocks

    out_t = acc_ref[0:MB_HD, :] / acc_ref[MB_HD:MB_HD + 1, :]
    for r in range(MB_REP):
        o_ref[:, r * MB_HD:(r + 1) * MB_HD] = out_t[:, r * bs:(r + 1) * bs].T


def _moba(mq, mk, mvt, kmean):
    T = mq.shape[1]
    bs = MB_BLOCK
    nb = T // bs
    nq = MB_REP * bs
    assert nb <= MB_MAX_BLOCKS
    kmean = jnp.pad(kmean, ((0, 0), (0, MB_MAX_BLOCKS - nb), (0, 0)))
    return pl.pallas_call(
        _moba_kernel,
        grid=(MB_KV_HEADS, nb),
        in_specs=[
            pl.BlockSpec((MB_REP, bs, MB_HD), lambda g, i: (g, i, 0)),
            pl.BlockSpec((nb, None, bs, 2 * MB_HD), lambda g, i: (0, g, 0, 0)),
            pl.BlockSpec((nb, None, MB_VROWS, bs), lambda g, i: (0, g, 0, 0)),
            pl.BlockSpec((None, MB_MAX_BLOCKS, MB_HD), lambda g, i: (g, 0, 0)),
        ],
        out_specs=pl.BlockSpec((bs, MB_REP * MB_HD), lambda g, i: (i, g)),
        out_shape=jax.ShapeDtypeStruct((T, MB_HEADS * MB_HD), F32),
        scratch_shapes=[pltpu.VMEM((2, nq, 2 * MB_HD), BF16),
                        pltpu.VMEM((bs, nq), F32), pltpu.VMEM((bs, nq), F32),
                        pltpu.VMEM((MB_VROWS, nq), F32),
                        pltpu.VMEM((1, nq), F32)],
        compiler_params=pltpu.CompilerParams(dimension_semantics=("arbitrary", "arbitrary"),
                                             vmem_limit_bytes=VMEM_LIMIT),
        name="moba",
    )(mq, mk, mvt, kmean)


def _post_kernel(x_ref, go_ref, ma_ref, wg_ref, wgp_ref, wmp_ref, wo_ref, g1_ref, b1_ref,
                 rwt_ref, rb_ref, sg_ref, su_ref, sd_ref,
                 hp_ref, res_ref, eidx_ref, wts_ref):
    tm = x_ref.shape[0]
    x = x_ref[...]
    zg = jnp.dot(x.astype(BF16), wg_ref[...], preferred_element_type=F32)
    gate = zg[:, :D_MODEL]
    o = go_ref[...] * (gate * _sigmoid(gate))
    y_gdn = jnp.dot(o.astype(BF16), wgp_ref[...], preferred_element_type=F32)
    y_mb = jnp.dot(ma_ref[...].astype(BF16), wmp_ref[...], preferred_element_type=F32)
    m = _sigmoid(zg[:, D_MODEL:2 * D_MODEL]) * y_gdn + _sigmoid(zg[:, 2 * D_MODEL:]) * y_mb
    mix = jnp.dot(m.astype(BF16), wo_ref[...], preferred_element_type=F32)
    h = _layer_norm(DEEPNORM_ALPHA * x + mix, g1_ref[...], b1_ref[...])
    hb = h.astype(BF16)
    _store_token_tiles(hp_ref, _pack_bf16_pairs(h))

    hs = jnp.dot(hb, sg_ref[...], preferred_element_type=F32)
    hs = hs * _sigmoid(hs) * jnp.dot(hb, su_ref[...], preferred_element_type=F32)
    res_ref[...] = DEEPNORM_ALPHA * h + jnp.dot(hs.astype(BF16), sd_ref[...], preferred_element_type=F32)

    logits = lax.dot_general(rwt_ref[...], h, (((1,), (1,)), ((), ())),
                             precision=lax.Precision.HIGHEST, preferred_element_type=F32)
    scores = _sigmoid(logits)
    choice = scores + rb_ref[...]
    neg = -jnp.inf
    gi = lax.broadcasted_iota(jnp.int32, (GROUP_SIZE, tm), 0).astype(F32)
    gscore = []
    for g in range(N_GROUPS):
        cg = choice[g * GROUP_SIZE:(g + 1) * GROUP_SIZE, :]
        m1 = jnp.max(cg, axis=0, keepdims=True)
        i1 = jnp.min(jnp.where(cg == m1, gi, float(GROUP_SIZE)), axis=0, keepdims=True)
        m2 = jnp.max(jnp.where(gi == i1, neg, cg), axis=0, keepdims=True)
        gscore.append(m1 + m2)
    gs = jnp.concatenate(gscore, axis=0)
    gidx = lax.broadcasted_iota(jnp.int32, (N_GROUPS, tm), 0).astype(F32)
    gsel = jnp.zeros((N_GROUPS, tm), F32)
    for _ in range(TOPK_GROUPS):
        top = jnp.max(gs, axis=0, keepdims=True)
        first = jnp.min(jnp.where(gs == top, gidx, float(N_GROUPS)), axis=0, keepdims=True)
        hit = gidx == first
        gsel = jnp.where(hit, 1.0, gsel)
        gs = jnp.where(hit, neg, gs)
    masked = jnp.concatenate(
        [jnp.where(gsel[g:g + 1, :] > 0.0, choice[g * GROUP_SIZE:(g + 1) * GROUP_SIZE, :], neg)
         for g in range(N_GROUPS)], axis=0)
    ei = lax.broadcasted_iota(jnp.int32, (N_EXPERTS, tm), 0).astype(F32)
    idx_rows, w_rows = [], []
    for _ in range(TOP_K):
        top = jnp.max(masked, axis=0, keepdims=True)
        first = jnp.min(jnp.where(masked == top, ei, float(N_EXPERTS)), axis=0, keepdims=True)
        hit = ei == first
        idx_rows.append(first)
        w_rows.append(jnp.sum(jnp.where(hit, scores, 0.0), axis=0, keepdims=True))
        masked = jnp.where(hit, neg, masked)
    w = jnp.concatenate(w_rows, axis=0)
    w = w / (jnp.sum(w, axis=0, keepdims=True) + 1e-20) * ROUTED_SCALE
    eidx_ref[...] = jnp.concatenate(idx_rows, axis=0).astype(jnp.int32)
    wts_ref[...] = w


def _post(x2, gdn_o, moba_a, w_gates, w_gdn_proj, w_moba_proj, w_out, ln_g, ln_b,
          router_wt, router_b_col, sh_g, sh_u, sh_d):
    T = x2.shape[0]
    tm = POST_TILE
    row_spec = pl.BlockSpec((tm, D_MODEL), lambda i: (i, 0))
    consts = [w_gates, w_gdn_proj, w_moba_proj, w_out, ln_g, ln_b, router_wt, router_b_col,
              sh_g, sh_u, sh_d]
    return pl.pallas_call(
        _post_kernel,
        grid=(T // tm,),
        in_specs=[row_spec, row_spec, row_spec] + [_const_spec(c.shape) for c in consts],
        out_specs=[pl.BlockSpec((tm * HP_ROWS, LANES), lambda i: (i, 0)), row_spec,
                   pl.BlockSpec((TOP_K, tm), lambda i: (0, i)),
                   pl.BlockSpec((TOP_K, tm), lambda i: (0, i))],
        out_shape=[jax.ShapeDtypeStruct((T * HP_ROWS, LANES), jnp.uint32),
                   jax.ShapeDtypeStruct((T, D_MODEL), F32),
                   jax.ShapeDtypeStruct((TOP_K, T), jnp.int32),
                   jax.ShapeDtypeStruct((TOP_K, T), F32)],
        compiler_params=pltpu.CompilerParams(dimension_semantics=("arbitrary",),
                                             vmem_limit_bytes=VMEM_LIMIT),
        name="post",
    )(x2, gdn_o, moba_a, *consts)


ROW_UNROLL = 8


def _experts_kernel(iblk_ref, iexp_ref, ilo_ref, ihi_ref, src_ref, dst_ref,
                    hp_ref, wg_ref, wu_ref, wd_ref, out_ref,
                    xbuf, ybuf, wgb, wub, wdb, cast_exp, ssem):
    p = pl.program_id(0)
    n_items = pl.num_programs(0)
    w = iblk_ref[p]
    slot = w % 2
    first = (p == 0) | (w != iblk_ref[jnp.maximum(p - 1, 0)])
    last = (p == n_items - 1) | (w != iblk_ref[jnp.minimum(p + 1, n_items - 1)])
    lo, hi = ilo_ref[p], ihi_ref[p]
    block_rows = MOE_BLOCK * TOK_ROWS

    def tile_at(ref, row0, rows):
        return ref.at[pl.ds(pl.multiple_of(row0, rows), rows)]

    def for_block_rows(fn):
        def group(g, c):
            for u in range(ROW_UNROLL):
                fn(g * ROW_UNROLL + u)
            return c
        lax.fori_loop(0, MOE_BLOCK // ROW_UNROLL, group, 0)

    def gather_block():
        def one(i):
            xbuf[pl.ds(pl.multiple_of(i * HP_ROWS, HP_ROWS), HP_ROWS), :] = (
                hp_ref[pl.ds(pl.multiple_of(src_ref[0, 0, i], HP_ROWS), HP_ROWS), :])
        for_block_rows(one)

    def start_scatter(s):
        for_block_rows(lambda i: pltpu.make_async_copy(
            tile_at(ybuf.at[s], i * TOK_ROWS, TOK_ROWS), tile_at(out_ref, dst_ref[0, 0, i], TOK_ROWS),
            ssem.at[s]).start())

    def wait_scatter(s):
        pltpu.make_async_copy(ybuf.at[s], out_ref.at[pl.ds(0, block_rows)], ssem.at[s]).wait()

    @pl.when(p == 0)
    def _():
        ybuf[...] = jnp.zeros_like(ybuf)
        cast_exp[0] = -1

    @pl.when(first)
    def _():
        @pl.when(w >= 2)
        def _():
            wait_scatter(slot)
        gather_block()

    @pl.when(hi > lo)
    def _():
        @pl.when(iexp_ref[p] != cast_exp[0])
        def _():
            wgb[...] = wg_ref[...].astype(BF16)
            wub[...] = wu_ref[...].astype(BF16)
            wdb[...] = wd_ref[...].astype(BF16)
            cast_exp[0] = iexp_ref[p]

        xb = _unpack_bf16_pairs(_load_token_tiles(xbuf, MOE_BLOCK)).astype(BF16)
        hg = jnp.dot(xb, wgb[...], preferred_element_type=F32)
        hu = jnp.dot(xb, wub[...], preferred_element_type=F32)
        hb = (hg * _sigmoid(hg) * hu).astype(BF16)
        y = jnp.dot(hb, wdb[...], preferred_element_type=F32)
        row = lax.broadcasted_iota(jnp.int32, (MOE_BLOCK, 1), 0)
        mine = (row >= lo) & (row < hi)
        _store_token_tiles(ybuf.at[slot], jnp.where(mine, y, _load_token_tiles(ybuf.at[slot], MOE_BLOCK)))

    @pl.when(last)
    def _():
        start_scatter(slot)

    @pl.when(p == n_items - 1)
    def _():
        @pl.when(w >= 1)
        def _():
            wait_scatter(1 - slot)
        wait_scatter(slot)


def _experts(hp, src_rows, dst_rows, item_blk, item_exp, item_lo, item_hi, wg, wu, wd):
    T = hp.shape[0] // HP_ROWS
    n_items = item_blk.shape[0]
    rows = MOE_BLOCK
    ids_spec = pl.BlockSpec((1, 1, rows), lambda p, ib, ie, il, ih: (ib[p], 0, 0), memory_space=pltpu.SMEM)

    def w_spec(shape):
        return pl.BlockSpec((None,) + shape, lambda p, ib, ie, il, ih: (ie[p], 0, 0))

    grid_spec = pltpu.PrefetchScalarGridSpec(
        num_scalar_prefetch=4,
        grid=(n_items,),
        in_specs=[ids_spec, ids_spec, _const_spec(hp.shape),
                  w_spec((D_MODEL, EXPERT_DIM)), w_spec((D_MODEL, EXPERT_DIM)), w_spec((EXPERT_DIM, D_MODEL))],
        out_specs=pl.BlockSpec(memory_space=pl.ANY),
        scratch_shapes=[pltpu.VMEM((rows * HP_ROWS, LANES), jnp.uint32),
                        pltpu.VMEM((2, rows * TOK_ROWS, LANES), F32),
                        pltpu.VMEM((D_MODEL, EXPERT_DIM), BF16), pltpu.VMEM((D_MODEL, EXPERT_DIM), BF16),
                        pltpu.VMEM((EXPERT_DIM, D_MODEL), BF16),
                        pltpu.SMEM((1,), jnp.int32),
                        pltpu.SemaphoreType.DMA((2,))],
    )
    return pl.pallas_call(
        _experts_kernel,
        grid_spec=grid_spec,
        out_shape=jax.ShapeDtypeStruct((TOP_K * T * TOK_ROWS, LANES), F32),
        compiler_params=pltpu.CompilerParams(dimension_semantics=("arbitrary",),
                                             vmem_limit_bytes=VMEM_LIMIT,
                                             disable_bounds_checks=True),
        name="experts",
    )(item_blk, item_exp, item_lo, item_hi, src_rows, dst_rows, hp, wg, wu, wd)


def _combine_kernel(res_ref, y_ref, w_ref, g_ref, b_ref, o_ref):
    tm = res_ref.shape[0]
    acc = res_ref[...]
    w = w_ref[...]
    for k in range(TOP_K):
        acc = acc + w[:, k:k + 1] * _load_token_tiles(y_ref.at[k], tm)
    o_ref[...] = _layer_norm(acc, g_ref[...], b_ref[...])


def _combine(res, y3, wts, ln_g, ln_b):
    T = res.shape[0]
    tm = COMBINE_TILE
    return pl.pallas_call(
        _combine_kernel,
        grid=(T // tm,),
        in_specs=[pl.BlockSpec((tm, D_MODEL), lambda i: (i, 0)),
                  pl.BlockSpec((TOP_K, tm * TOK_ROWS, LANES), lambda i: (0, i, 0)),
                  pl.BlockSpec((tm, TOP_K), lambda i: (i, 0)),
                  _const_spec((1, D_MODEL)), _const_spec((1, D_MODEL))],
        out_specs=pl.BlockSpec((tm, D_MODEL), lambda i: (i, 0)),
        out_shape=jax.ShapeDtypeStruct((T, D_MODEL), F32),
        compiler_params=pltpu.CompilerParams(dimension_semantics=("arbitrary",),
                                             vmem_limit_bytes=VMEM_LIMIT),
        name="combine",
    )(res, y3, wts, ln_g, ln_b)


def _dispatch_plan(eidx_t, n_tok):
    nk = n_tok * TOP_K
    n_blocks = nk // MOE_BLOCK
    e_flat = eidx_t.T.reshape(nk)
    _, order = lax.sort((e_flat, jnp.arange(nk, dtype=jnp.int32)), num_keys=1)
    experts = jnp.arange(N_EXPERTS, dtype=jnp.int32)
    counts = jnp.sum((e_flat[None, :] == experts[:, None]).astype(jnp.int32), axis=1)
    start = jnp.cumsum(counts) - counts
    pos = jnp.sort(jnp.concatenate([jnp.arange(n_blocks, dtype=jnp.int32) * MOE_BLOCK, start]))
    nxt = jnp.concatenate([pos[1:], jnp.full((1,), nk, jnp.int32)])
    item_blk = jnp.minimum(pos // MOE_BLOCK, n_blocks - 1)
    item_exp = jnp.sum((start[None, :] <= pos[:, None]).astype(jnp.int32), axis=1) - 1
    item_lo = pos - item_blk * MOE_BLOCK
    item_hi = jnp.minimum(nxt, (item_blk + 1) * MOE_BLOCK) - item_blk * MOE_BLOCK
    tok, slot_k = order // TOP_K, order % TOP_K
    shape3 = (n_blocks, 1, MOE_BLOCK)
    src_rows = (tok * HP_ROWS).reshape(shape3)
    dst_rows = ((slot_k * n_tok + tok) * TOK_ROWS).reshape(shape3)
    return src_rows, dst_rows, item_blk, item_exp, item_lo, item_hi


def kernel(x, w_in, conv_w, gdn_a_log, gdn_dt_bias, gdn_norm_w, w_gdn_proj, w_moba_proj, w_out,
           ln1_g, ln1_b, router_w, router_bias, exp_w_gate, exp_w_up, exp_w_down,
           sh_w_gate, sh_w_up, sh_w_down, ln2_g, ln2_b):
    B, T, D = x.shape
    assert B == 1 and D == D_MODEL and T % MB_BLOCK == 0
    x2 = x.reshape(T, D)

    o_gate = CONV_CH
    o_b = o_gate + GDN_HEADS * GDN_DV
    o_a = o_b + GDN_HEADS
    o_mq = o_a + GDN_HEADS
    o_gg = o_mq + (MB_HEADS + 2 * MB_KV_HEADS) * MB_HD
    w_qkv = w_in[:, :CONV_CH].astype(BF16)
    w_pack = jnp.concatenate([w_in[:, o_b:o_a]] + [w_in[:, o_a:o_mq]] * 4
                             + [jnp.zeros((D, LANES - 5 * GDN_HEADS), F32)], axis=1).astype(BF16)
    w_mb = w_in[:, o_mq:o_gg].astype(BF16)
    w_gates = jnp.concatenate([w_in[:, o_gate:o_b], w_in[:, o_gg:]], axis=1).astype(BF16)

    def lane_row(v):
        return jnp.zeros((1, LANES), F32).at[0, PK_GC:PK_GC + 4 * GDN_HEADS].set(jnp.tile(v.astype(F32), 4))

    q, k, v, pack, gct = _qkv_proj(x2, w_qkv, w_pack, conv_w.astype(F32),
                                   lane_row(gdn_a_log), lane_row(gdn_dt_bias))
    gct3 = gct.reshape(GDN_HEADS, T // GDN_CHUNK, GDN_CHUNK).transpose(1, 0, 2)
    gdn_o = _gdn(q, k, v, pack, gct3, gdn_norm_w.astype(F32).reshape(1, GDN_DV))

    half = ROT_DIM // 2
    inv = ROPE_THETA ** (-jnp.arange(half, dtype=F32) / half)
    ang = jnp.arange(T).astype(F32)[:, None] * inv[None, :]
    ones = jnp.ones((T, MB_HD - ROT_DIM), F32)
    cos_t = jnp.concatenate([jnp.cos(ang), jnp.cos(ang), ones], axis=1)
    sin_t = jnp.concatenate([-jnp.sin(ang), jnp.sin(ang), 0.0 * ones], axis=1)
    mq, mk, mvt, kmean = _moba_proj(x2, w_mb, cos_t, sin_t)
    moba_a = _moba(mq, mk, mvt, kmean.transpose(1, 0, 2))

    hp, res, eidx_t, wts_t = _post(
        x2, gdn_o, moba_a, w_gates, w_gdn_proj.astype(BF16), w_moba_proj.astype(BF16),
        w_out.astype(BF16), ln1_g.reshape(1, D), ln1_b.reshape(1, D),
        router_w.T, router_bias.reshape(N_EXPERTS, 1),
        sh_w_gate.astype(BF16), sh_w_up.astype(BF16), sh_w_down.astype(BF16))

    plan = _dispatch_plan(eidx_t, T)
    y = _experts(hp, *plan, exp_w_gate, exp_w_up, exp_w_down)
    y3 = y.reshape(TOP_K, T * TOK_ROWS, LANES)
    out = _combine(res, y3, wts_t.T, ln2_g.reshape(1, D), ln2_b.reshape(1, D))
    return out.reshape(B, T, D)
```

```python
import functools
import math

import jax
import jax.numpy as jnp
from jax import lax
from jax.experimental import pallas as pl
from jax.experimental.pallas import tpu as pltpu

F32 = jnp.float32
BF16 = jnp.bfloat16

D_MODEL = 1024
DEPTH = 1
GDN_HEADS = 8
GDN_DK = 128
GDN_DV = 128
GDN_CONV = 4
GDN_CHUNK = 64
MB_HEADS = 8
MB_KV_HEADS = 2
MB_REP = MB_HEADS // MB_KV_HEADS
MB_HD = 128
MB_BLOCK = 256
MB_TOPK = 3
ROT_DIM = MB_HD // 4
ROPE_THETA = 500000.0
N_EXPERTS = 256
TOP_K = 8
N_GROUPS = 8
GROUP_SIZE = N_EXPERTS // N_GROUPS
TOPK_GROUPS = 4
EXPERT_DIM = 256
SHARED_DIM = 256
ROUTED_SCALE = 2.5
MOE_BLOCK = 256
DEEPNORM_ALPHA = (2.0 * DEPTH) ** 0.25
LN_EPS = 1e-5
MASK_NEG = -1e30

QK_COLS = GDN_HEADS * GDN_DK
CONV_CH = 2 * QK_COLS + GDN_HEADS * GDN_DV
LANES = 128
SUBLANES = 8
VMEM_LIMIT = 56 * 1024 * 1024

TOK_TILE = 256
POST_TILE = 512
COMBINE_TILE = 256

PK_BETA, PK_GC, PK_EG, PK_EGD, PK_EGL = 0, 8, 16, 24, 32


def _sigmoid(x):
    return 1.0 / (1.0 + jnp.exp(-x))


def _softplus(x):
    return jnp.maximum(x, 0.0) + jnp.log(1.0 + jnp.exp(-jnp.abs(x)))


def _bdot(a, b):
    return jnp.dot(a.astype(BF16), b.astype(BF16), preferred_element_type=F32)


def _bdot_nt(a, b):
    return lax.dot_general(a.astype(BF16), b.astype(BF16), (((1,), (1,)), ((), ())),
                           preferred_element_type=F32)


def _bdot_tn(a, b):
    return lax.dot_general(a.astype(BF16), b.astype(BF16), (((0,), (0,)), ((), ())),
                           preferred_element_type=F32)


def _layer_norm(x, g, b):
    mu = jnp.mean(x, axis=-1, keepdims=True)
    xc = x - mu
    var = jnp.mean(xc * xc, axis=-1, keepdims=True)
    return xc * lax.rsqrt(var + LN_EPS) * g + b


TOK_ROWS = D_MODEL // LANES


HP_ROWS = TOK_ROWS // 2


def _store_token_tiles(ref, x):
    n, r = x.shape[0], x.shape[1] // LANES
    for j in range(r):
        ref[pl.ds(j, n, stride=r), :] = x[:, j * LANES:(j + 1) * LANES]


def _load_token_tiles(ref, n):
    r = ref.shape[0] // n
    return jnp.concatenate([ref[pl.ds(j, n, stride=r), :] for j in range(r)], axis=1)


def _pack_bf16_pairs(x):
    half = x.shape[1] // 2
    bits = pltpu.bitcast(x.astype(BF16).astype(F32), jnp.uint32)
    return (bits[:, :half] >> 16) | (bits[:, half:] & jnp.uint32(0xFFFF0000))


def _unpack_bf16_pairs(w):
    lo = pltpu.bitcast(w << 16, F32)
    hi = pltpu.bitcast(w & jnp.uint32(0xFFFF0000), F32)
    return jnp.concatenate([lo, hi], axis=1)


def _const_spec(shape):
    nd = len(shape)
    return pl.BlockSpec(shape, lambda *_: (0,) * nd, pipeline_mode=pl.Buffered(1))


def _qkv_kernel(x_ref, w_ref, wp_ref, cw_ref, alog_ref, dtb_ref,
                q_ref, k_ref, v_ref, pack_ref, gct_ref, z_ref):
    tm = x_ref.shape[0]
    halo = SUBLANES

    @pl.when(pl.program_id(0) == 0)
    def _():
        z_ref[0:halo, :] = jnp.zeros((halo, CONV_CH), F32)

    xb = x_ref[...].astype(BF16)
    z_ref[halo:halo + tm, :] = jnp.dot(xb, w_ref[...], preferred_element_type=F32)

    for c in range(CONV_CH // LANES):
        cols = slice(c * LANES, (c + 1) * LANES)
        acc = z_ref[halo:halo + tm, cols] * cw_ref[GDN_CONV - 1:GDN_CONV, cols]
        for s in range(1, GDN_CONV):
            acc = acc + z_ref[halo - s:halo - s + tm, cols] * cw_ref[GDN_CONV - 1 - s:GDN_CONV - s, cols]
        y = acc * _sigmoid(acc)
        h = c % GDN_HEADS
        if c < GDN_HEADS:
            y = y * lax.rsqrt(jnp.sum(y * y, axis=-1, keepdims=True) + 1e-6) * (GDN_DK ** -0.5)
            q_ref[h] = y
        elif c < 2 * GDN_HEADS:
            y = y * lax.rsqrt(jnp.sum(y * y, axis=-1, keepdims=True) + 1e-6)
            k_ref[h] = y
        else:
            v_ref[h] = y

    z_ref[0:halo, :] = z_ref[tm:tm + halo, :]

    zp = jnp.dot(xb, wp_ref[...], preferred_element_type=F32)
    beta = _sigmoid(zp)
    g = -jnp.exp(alog_ref[...]) * _softplus(zp + dtb_ref[...])
    row = lax.broadcasted_iota(jnp.int32, (tm, LANES), 0) % GDN_CHUNK
    gc = g
    step = 1
    while step < GDN_CHUNK:
        gc = gc + jnp.where(row >= step, pltpu.roll(gc, step, axis=0), 0.0)
        step *= 2
    gl = gc.reshape(tm // GDN_CHUNK, GDN_CHUNK, LANES)[:, GDN_CHUNK - 1:GDN_CHUNK, :]
    gl = jnp.broadcast_to(gl, (tm // GDN_CHUNK, GDN_CHUNK, LANES)).reshape(tm, LANES)
    lane = lax.broadcasted_iota(jnp.int32, (tm, LANES), 1)
    pack = jnp.where(lane < PK_GC, beta,
           jnp.where(lane < PK_EG, gc,
           jnp.where(lane < PK_EGD, jnp.exp(gc),
           jnp.where(lane < PK_EGL, jnp.exp(gl - gc), jnp.exp(gl)))))
    pack_ref[...] = pack
    gct_ref[...] = pack.T[PK_GC:PK_GC + GDN_HEADS, :]


def _qkv_proj(x2, w_qkv, w_pack, conv_w, alog_row, dtb_row):
    T = x2.shape[0]
    tm = TOK_TILE
    hd_spec = pl.BlockSpec((GDN_HEADS, tm, LANES), lambda i: (0, i, 0))
    hd_shape = jax.ShapeDtypeStruct((GDN_HEADS, T, LANES), F32)
    return pl.pallas_call(
        _qkv_kernel,
        grid=(T // tm,),
        in_specs=[
            pl.BlockSpec((tm, D_MODEL), lambda i: (i, 0)),
            _const_spec((D_MODEL, CONV_CH)),
            _const_spec((D_MODEL, LANES)),
            _const_spec((GDN_CONV, CONV_CH)),
            _const_spec((1, LANES)),
            _const_spec((1, LANES)),
        ],
        out_specs=[hd_spec, hd_spec, hd_spec,
                   pl.BlockSpec((tm, LANES), lambda i: (i, 0)),
                   pl.BlockSpec((GDN_HEADS, tm), lambda i: (0, i))],
        out_shape=[hd_shape, hd_shape, hd_shape,
                   jax.ShapeDtypeStruct((T, LANES), F32),
                   jax.ShapeDtypeStruct((GDN_HEADS, T), F32)],
        scratch_shapes=[pltpu.VMEM((tm + SUBLANES, CONV_CH), F32)],
        compiler_params=pltpu.CompilerParams(dimension_semantics=("arbitrary",),
                                             vmem_limit_bytes=VMEM_LIMIT),
        name="qkv_proj",
    )(x2, w_qkv, w_pack, conv_w, alog_row, dtb_row)


MB_LROWS = 16
MB_VROWS = MB_HD + MB_LROWS


def _rope(xh, cos_t, sin_t, lane):
    half = ROT_DIM // 2
    swapped = jnp.where(lane < half, pltpu.roll(xh, LANES - half, axis=1), pltpu.roll(xh, half, axis=1))
    return xh * cos_t + swapped * sin_t


def _moba_proj_kernel(x_ref, w_ref, cos_ref, sin_ref, q_ref, k_ref, vt_ref, km_ref):
    tm = x_ref.shape[0]
    z = jnp.dot(x_ref[...].astype(BF16), w_ref[...], preferred_element_type=F32)
    cos_t = cos_ref[...]
    sin_t = sin_ref[...]
    lane = lax.broadcasted_iota(jnp.int32, (tm, LANES), 1)
    for h in range(MB_HEADS):
        q_ref[h] = _rope(z[:, h * MB_HD:(h + 1) * MB_HD], cos_t, sin_t, lane)
    koff = MB_HEADS * MB_HD
    voff = koff + MB_KV_HEADS * MB_HD
    blk_onehot = jnp.where(lane == pl.program_id(0), 1.0, 0.0).astype(BF16)
    for g in range(MB_KV_HEADS):
        kr = _rope(z[:, koff + g * MB_HD:koff + (g + 1) * MB_HD], cos_t, sin_t, lane)
        k_ref[0, g] = jnp.concatenate([kr.astype(BF16), blk_onehot], axis=1)
        km_ref[0, g:g + 1, :] = jnp.mean(kr, axis=0, keepdims=True)
        vt_ref[0, g] = jnp.concatenate([z[:, voff + g * MB_HD:voff + (g + 1) * MB_HD].T,
                                        jnp.ones((MB_LROWS, tm), F32)], axis=0).astype(BF16)


def _moba_proj(x2, w_mb, cos_t, sin_t):
    T = x2.shape[0]
    tm = MB_BLOCK
    nb = T // tm
    return pl.pallas_call(
        _moba_proj_kernel,
        grid=(nb,),
        in_specs=[
            pl.BlockSpec((tm, D_MODEL), lambda i: (i, 0)),
            _const_spec(w_mb.shape),
            pl.BlockSpec((tm, LANES), lambda i: (i, 0)),
            pl.BlockSpec((tm, LANES), lambda i: (i, 0)),
        ],
        out_specs=[
            pl.BlockSpec((MB_HEADS, tm, MB_HD), lambda i: (0, i, 0)),
            pl.BlockSpec((1, MB_KV_HEADS, tm, 2 * MB_HD), lambda i: (i, 0, 0, 0)),
            pl.BlockSpec((1, MB_KV_HEADS, MB_VROWS, tm), lambda i: (i, 0, 0, 0)),
            pl.BlockSpec((1, MB_KV_HEADS, MB_HD), lambda i: (i, 0, 0)),
        ],
        out_shape=[
            jax.ShapeDtypeStruct((MB_HEADS, T, MB_HD), F32),
            jax.ShapeDtypeStruct((nb, MB_KV_HEADS, tm, 2 * MB_HD), BF16),
            jax.ShapeDtypeStruct((nb, MB_KV_HEADS, MB_VROWS, tm), BF16),
            jax.ShapeDtypeStruct((nb, MB_KV_HEADS, MB_HD), F32),
        ],
        compiler_params=pltpu.CompilerParams(dimension_semantics=("arbitrary",),
                                             vmem_limit_bytes=VMEM_LIMIT),
        name="moba_proj",
    )(x2, w_mb, cos_t, sin_t)


def _proj_kernel(x_ref, w_ref, wp_ref, cw_ref, alog_ref, dtb_ref, wmb_ref, cos_ref, sin_ref,
                 q_ref, k_ref, v_ref, pack_ref, gct_ref, mq_ref, mk_ref, vt_ref, km_ref, z_ref):
    _qkv_kernel(x_ref, w_ref, wp_ref, cw_ref, alog_ref, dtb_ref, q_ref, k_ref, v_ref, pack_ref, gct_ref, z_ref)
    _moba_proj_kernel(x_ref, wmb_ref, cos_ref, sin_ref, mq_ref, mk_ref, vt_ref, km_ref)


def _proj(x2, w_qkv, w_pack, conv_w, alog_row, dtb_row, w_mb, cos_t, sin_t):
    T = x2.shape[0]
    tm = MB_BLOCK
    nb = T // tm
    hd_spec = pl.BlockSpec((GDN_HEADS, tm, LANES), lambda i: (0, i, 0))
    hd_shape = jax.ShapeDtypeStruct((GDN_HEADS, T, LANES), F32)
    row_spec = pl.BlockSpec((tm, LANES), lambda i: (i, 0))
    return pl.pallas_call(
        _proj_kernel,
        grid=(nb,),
        in_specs=[pl.BlockSpec((tm, D_MODEL), lambda i: (i, 0)),
                  _const_spec((D_MODEL, CONV_CH)), _const_spec((D_MODEL, LANES)),
                  _const_spec((GDN_CONV, CONV_CH)), _const_spec((1, LANES)), _const_spec((1, LANES)),
                  _const_spec(w_mb.shape), row_spec, row_spec],
        out_specs=[hd_spec, hd_spec, hd_spec, row_spec,
                   pl.BlockSpec((GDN_HEADS, tm), lambda i: (0, i)),
                   pl.BlockSpec((MB_HEADS, tm, MB_HD), lambda i: (0, i, 0)),
                   pl.BlockSpec((1, MB_KV_HEADS, tm, 2 * MB_HD), lambda i: (i, 0, 0, 0)),
                   pl.BlockSpec((1, MB_KV_HEADS, MB_VROWS, tm), lambda i: (i, 0, 0, 0)),
                   pl.BlockSpec((1, MB_KV_HEADS, MB_HD), lambda i: (i, 0, 0))],
        out_shape=[hd_shape, hd_shape, hd_shape,
                   jax.ShapeDtypeStruct((T, LANES), F32),
                   jax.ShapeDtypeStruct((GDN_HEADS, T), F32),
                   jax.ShapeDtypeStruct((MB_HEADS, T, MB_HD), F32),
                   jax.ShapeDtypeStruct((nb, MB_KV_HEADS, tm, 2 * MB_HD), BF16),
                   jax.ShapeDtypeStruct((nb, MB_KV_HEADS, MB_VROWS, tm), BF16),
                   jax.ShapeDtypeStruct((nb, MB_KV_HEADS, MB_HD), F32)],
        scratch_shapes=[pltpu.VMEM((tm + SUBLANES, CONV_CH), F32)],
        compiler_params=pltpu.CompilerParams(dimension_semantics=("arbitrary",),
                                             vmem_limit_bytes=VMEM_LIMIT),
        name="in_proj",
    )(x2, w_qkv, w_pack, conv_w, alog_row, dtb_row, w_mb, cos_t, sin_t)


def _unit_lower_inverse(a_list):
    c = a_list[0].shape[0]
    rr = lax.broadcasted_iota(jnp.int32, (c, c), 0)
    cc = lax.broadcasted_iota(jnp.int32, (c, c), 1)
    eye = jnp.where(rr == cc, 1.0, 0.0).astype(F32)
    inv = [eye - a for a in a_list]
    p = [_bdot(a, a) for a in a_list]
    n = 2
    while True:
        inv = [x + _bdot(x, y) for x, y in zip(inv, p)]
        n *= 2
        if n >= c:
            break
        p = [_bdot(y, y) for y in p]
    return inv


GDN_STEP_CHUNKS = 4


def _gdn_kernel(q_ref, k_ref, v_ref, pack_ref, gct_ref, nw_ref, o_ref, s_ref):
    C = GDN_CHUNK
    H = range(GDN_HEADS)
    P = [(c, h) for c in range(GDN_STEP_CHUNKS) for h in H]

    @pl.when(pl.program_id(0) == 0)
    def _():
        s_ref[...] = jnp.zeros_like(s_ref)

    rr = lax.broadcasted_iota(jnp.int32, (C, C), 0)
    cc = lax.broadcasted_iota(jnp.int32, (C, C), 1)
    tril = rr >= cc
    strict = rr > cc
    pack = pack_ref[...]
    nw = nw_ref[...]

    def rows(c):
        return slice(c * C, (c + 1) * C)

    def col(base, c, h):
        return pack[rows(c), base + h:base + h + 1]

    q = {(c, h): q_ref[h, rows(c), :] for c, h in P}
    k = {(c, h): k_ref[h, rows(c), :] for c, h in P}
    kb = {(c, h): k[c, h] * col(PK_BETA, c, h) for c, h in P}
    decay = {(c, h): jnp.where(tril, jnp.exp(jnp.where(tril, col(PK_GC, c, h) - gct_ref[c, h:h + 1, :], 0.0)), 0.0)
             for c, h in P}
    sc = {p: _bdot_nt(jnp.concatenate([kb[p], q[p]], axis=0), k[p]) for p in P}
    a = {p: jnp.where(strict, sc[p][:C] * decay[p], 0.0) for p in P}
    aqk = {p: sc[p][C:] * decay[p] for p in P}
    tinv = dict(zip(P, _unit_lower_inverse([a[p] for p in P])))
    rhs = {(c, h): jnp.concatenate([v_ref[h, rows(c), :] * col(PK_BETA, c, h), kb[c, h] * col(PK_EG, c, h)],
                                   axis=1) for c, h in P}
    sol = {p: _bdot(tinv[p], rhs[p]) for p in P}

    s = [s_ref[h] for h in H]
    for c in range(GDN_STEP_CHUNKS):
        wq = [_bdot(jnp.concatenate([sol[c, h][:, GDN_DV:], q[c, h] * col(PK_EG, c, h)], axis=0), s[h])
              for h in H]
        v_new = [sol[c, h][:, :GDN_DV] - wq[h][:C] for h in H]
        o = [wq[h][C:] + _bdot(aqk[c, h], v_new[h]) for h in H]
        ds = [_bdot_tn(k[c, h] * col(PK_EGD, c, h), v_new[h]) for h in H]
        for h in H:
            s[h] = s[h] * pack[c * C:c * C + 1, PK_EGL + h:PK_EGL + h + 1] + ds[h]
            on = o[h] * lax.rsqrt(jnp.mean(o[h] * o[h], axis=-1, keepdims=True) + 1e-6) * nw
            o_ref[rows(c), h * GDN_DV:(h + 1) * GDN_DV] = on
    for h in H:
        s_ref[h] = s[h]


def _gdn(q, k, v, pack, gct3, norm_w_row):
    T = q.shape[1]
    C = GDN_CHUNK * GDN_STEP_CHUNKS
    hd_spec = pl.BlockSpec((GDN_HEADS, C, LANES), lambda i: (0, i, 0))
    return pl.pallas_call(
        _gdn_kernel,
        grid=(T // C,),
        in_specs=[hd_spec, hd_spec, hd_spec,
                  pl.BlockSpec((C, LANES), lambda i: (i, 0)),
                  pl.BlockSpec((GDN_STEP_CHUNKS, GDN_HEADS, GDN_CHUNK), lambda i: (i, 0, 0)),
                  _const_spec((1, GDN_DV))],
        out_specs=pl.BlockSpec((C, GDN_HEADS * GDN_DV), lambda i: (i, 0)),
        out_shape=jax.ShapeDtypeStruct((T, GDN_HEADS * GDN_DV), F32),
        scratch_shapes=[pltpu.VMEM((GDN_HEADS, GDN_DK, GDN_DV), F32)],
        compiler_params=pltpu.CompilerParams(dimension_semantics=("arbitrary",)),
        name="gdn",
    )(q, k, v, pack, gct3, norm_w_row)


MB_QSPLIT = 2
MB_TRIPS = (8, 4, 2)
MB_MAX_BLOCKS = MB_HD


def _moba_kernel(q_ref, k_ref, vt_ref, km_ref, o_ref, qx_ref, sa_ref, sb_ref, acc_ref, m_ref):
    qi = pl.program_id(1)
    nb = k_ref.shape[0]
    bs = MB_BLOCK
    nq = MB_REP * bs
    nbp = MB_MAX_BLOCKS
    c = (MB_HD ** -0.5) * math.log2(math.e)

    qf = q_ref[...].reshape(nq, MB_HD)

    gate = lax.dot_general(km_ref[...], qf, (((1,), (1,)), ((), ())),
                           precision=lax.Precision.HIGHEST, preferred_element_type=F32)
    blk = lax.broadcasted_iota(jnp.int32, (nbp, nq), 0).astype(F32)
    qif = qi.astype(F32)
    gate = jnp.where(blk < qif, gate, -jnp.inf)
    sel = jnp.where(blk == qif, 1.0, 0.0)
    for _ in range(MB_TOPK):
        top = jnp.max(gate, axis=0, keepdims=True)
        first = jnp.min(jnp.where(gate == top, blk, float(nbp)), axis=0, keepdims=True)
        hit = (blk == first) & (top > -jnp.inf)
        sel = jnp.where(hit, 1.0, sel)
        gate = jnp.where(hit, -jnp.inf, gate)
    bias_t = jnp.where(sel > 0.0, 0.0, MASK_NEG)
    qc = qf * c
    qx_ref[0] = jnp.concatenate([qc, bias_t.T], axis=1).astype(BF16)
    qx_ref[1] = jnp.concatenate([qc, jnp.full((nq, nbp), MASK_NEG, F32)], axis=1).astype(BF16)

    w = nq // MB_QSPLIT
    groups = [slice(h * w, (h + 1) * w) for h in range(MB_QSPLIT)]

    def scores_into(dst_ref, t):
        kx = k_ref[jnp.minimum(t, nb - 1)]
        qset = jnp.where(t < qi, 0, 1)
        for lanes in groups:
            dst_ref[:, lanes] = lax.dot_general(kx, qx_ref[qset, lanes, :], (((1,), (1,)), ((), ())),
                                                preferred_element_type=F32)

    st = lax.dot_general(k_ref[qi], qx_ref[0], (((1,), (1,)), ((), ())), preferred_element_type=F32)
    kpos = lax.broadcasted_iota(jnp.int32, (bs, nq), 0)
    qpos = lax.broadcasted_iota(jnp.int32, (bs, nq), 1) % bs
    st = jnp.where(kpos <= qpos, st, MASK_NEG)
    m0 = jnp.max(st, axis=0, keepdims=True)
    p = jnp.exp2(st - m0)
    m_ref[...] = m0
    acc_ref[...] = jnp.dot(vt_ref[qi], p.astype(BF16), preferred_element_type=F32)

    def absorb(src_ref, t):
        vt = vt_ref[jnp.minimum(t, nb - 1)]
        for lanes in groups:
            st = src_ref[:, lanes]
            m_old = m_ref[:, lanes]
            m_new = jnp.maximum(m_old, jnp.max(st, axis=0, keepdims=True))
            alpha = jnp.exp2(m_old - m_new)
            p = jnp.exp2(st - m_new)
            m_ref[:, lanes] = m_new
            acc_ref[:, lanes] = alpha * acc_ref[:, lanes] + jnp.dot(vt, p.astype(BF16),
                                                                    preferred_element_type=F32)

    scores_into(sa_ref, 0)

    def trip(t, blocks):
        for u in range(0, blocks, 2):
            scores_into(sb_ref, t + u + 1)
            absorb(sa_ref, t + u)
            scores_into(sa_ref, t + u + 2)
            absorb(sb_ref, t + u + 1)

    done = 0
    for blocks in MB_TRIPS:
        left = qi - done
        trips = (left + 1) // 2 if blocks == 2 else left // blocks

        def body(i, carry, blocks=blocks, done=done):
            trip(done + blocks * i, blocks)
            return carry

        lax.fori_loop(0, trips, body, 0)
        done = done + trips * blocks

    out_t = acc_ref[0:MB_HD, :] / acc_ref[MB_HD:MB_HD + 1, :]
    for r in range(MB_REP):
        o_ref[:, r * MB_HD:(r + 1) * MB_HD] = out_t[:, r * bs:(r + 1) * bs].T


def _moba(mq, mk, mvt, kmean):
    T = mq.shape[1]
    bs = MB_BLOCK
    nb = T // bs
    nq = MB_REP * bs
    assert nb <= MB_MAX_BLOCKS
    kmean = jnp.pad(kmean, ((0, 0), (0, MB_MAX_BLOCKS - nb), (0, 0)))
    return pl.pallas_call(
        _moba_kernel,
        grid=(MB_KV_HEADS, nb),
        in_specs=[
            pl.BlockSpec((MB_REP, bs, MB_HD), lambda g, i: (g, i, 0)),
            pl.BlockSpec((nb, None, bs, 2 * MB_HD), lambda g, i: (0, g, 0, 0)),
            pl.BlockSpec((nb, None, MB_VROWS, bs), lambda g, i: (0, g, 0, 0)),
            pl.BlockSpec((None, MB_MAX_BLOCKS, MB_HD), lambda g, i: (g, 0, 0)),
        ],
        out_specs=pl.BlockSpec((bs, MB_REP * MB_HD), lambda g, i: (i, g)),
        out_shape=jax.ShapeDtypeStruct((T, MB_HEADS * MB_HD), F32),
        scratch_shapes=[pltpu.VMEM((2, nq, 2 * MB_HD), BF16),
                        pltpu.VMEM((bs, nq), F32), pltpu.VMEM((bs, nq), F32),
                        pltpu.VMEM((MB_VROWS, nq), F32),
                        pltpu.VMEM((1, nq), F32)],
        compiler_params=pltpu.CompilerParams(dimension_semantics=("arbitrary", "arbitrary"),
                                             vmem_limit_bytes=VMEM_LIMIT),
        name="moba",
    )(mq, mk, mvt, kmean)


def _post_kernel(x_ref, go_ref, ma_ref, wg_ref, wgp_ref, wmp_ref, wo_ref, g1_ref, b1_ref,
                 rwt_ref, rb_ref, sg_ref, su_ref, sd_ref,
                 hp_ref, res_ref, eidx_ref, wts_ref):
    tm = x_ref.shape[0]
    x = x_ref[...]
    zg = jnp.dot(x.astype(BF16), wg_ref[...], preferred_element_type=F32)
    gate = zg[:, :D_MODEL]
    o = go_ref[...] * (gate * _sigmoid(gate))
    y_gdn = jnp.dot(o.astype(BF16), wgp_ref[...], preferred_element_type=F32)
    y_mb = jnp.dot(ma_ref[...].astype(BF16), wmp_ref[...], preferred_element_type=F32)
    m = _sigmoid(zg[:, D_MODEL:2 * D_MODEL]) * y_gdn + _sigmoid(zg[:, 2 * D_MODEL:]) * y_mb
    mix = jnp.dot(m.astype(BF16), wo_ref[...], preferred_element_type=F32)
    h = _layer_norm(DEEPNORM_ALPHA * x + mix, g1_ref[...], b1_ref[...])
    hb = h.astype(BF16)
    _store_token_tiles(hp_ref, _pack_bf16_pairs(h))

    hs = jnp.dot(hb, sg_ref[...], preferred_element_type=F32)
    hs = hs * _sigmoid(hs) * jnp.dot(hb, su_ref[...], preferred_element_type=F32)
    res_ref[...] = DEEPNORM_ALPHA * h + jnp.dot(hs.astype(BF16), sd_ref[...], preferred_element_type=F32)

    logits = lax.dot_general(rwt_ref[...], h, (((1,), (1,)), ((), ())),
                             precision=lax.Precision.HIGHEST, preferred_element_type=F32)
    scores = _sigmoid(logits)
    choice = scores + rb_ref[...]
    neg = -jnp.inf
    gi = lax.broadcasted_iota(jnp.int32, (GROUP_SIZE, tm), 0).astype(F32)
    gscore = []
    for g in range(N_GROUPS):
        cg = choice[g * GROUP_SIZE:(g + 1) * GROUP_SIZE, :]
        m1 = jnp.max(cg, axis=0, keepdims=True)
        i1 = jnp.min(jnp.where(cg == m1, gi, float(GROUP_SIZE)), axis=0, keepdims=True)
        m2 = jnp.max(jnp.where(gi == i1, neg, cg), axis=0, keepdims=True)
        gscore.append(m1 + m2)
    gs = jnp.concatenate(gscore, axis=0)
    gidx = lax.broadcasted_iota(jnp.int32, (N_GROUPS, tm), 0).astype(F32)
    gsel = jnp.zeros((N_GROUPS, tm), F32)
    for _ in range(TOPK_GROUPS):
        top = jnp.max(gs, axis=0, keepdims=True)
        first = jnp.min(jnp.where(gs == top, gidx, float(N_GROUPS)), axis=0, keepdims=True)
        hit = gidx == first
        gsel = jnp.where(hit, 1.0, gsel)
        gs = jnp.where(hit, neg, gs)
    masked = jnp.concatenate(
        [jnp.where(gsel[g:g + 1, :] > 0.0, choice[g * GROUP_SIZE:(g + 1) * GROUP_SIZE, :], neg)
         for g in range(N_GROUPS)], axis=0)
    ei = lax.broadcasted_iota(jnp.int32, (N_EXPERTS, tm), 0).astype(F32)
    idx_rows, w_rows = [], []
    for _ in range(TOP_K):
        top = jnp.max(masked, axis=0, keepdims=True)
        first = jnp.min(jnp.where(masked == top, ei, float(N_EXPERTS)), axis=0, keepdims=True)
        hit = ei == first
        idx_rows.append(first)
        w_rows.append(jnp.sum(jnp.where(hit, scores, 0.0), axis=0, keepdims=True))
        masked = jnp.where(hit, neg, masked)
    w = jnp.concatenate(w_rows, axis=0)
    w = w / (jnp.sum(w, axis=0, keepdims=True) + 1e-20) * ROUTED_SCALE
    eidx_ref[...] = jnp.concatenate(idx_rows, axis=0).astype(jnp.int32)
    wts_ref[...] = w


def _post(x2, gdn_o, moba_a, w_gates, w_gdn_proj, w_moba_proj, w_out, ln_g, ln_b,
          router_wt, router_b_col, sh_g, sh_u, sh_d):
    T = x2.shape[0]
    tm = POST_TILE
    row_spec = pl.BlockSpec((tm, D_MODEL), lambda i: (i, 0))
    consts = [w_gates, w_gdn_proj, w_moba_proj, w_out, ln_g, ln_b, router_wt, router_b_col,
              sh_g, sh_u, sh_d]
    return pl.pallas_call(
        _post_kernel,
        grid=(T // tm,),
        in_specs=[row_spec, row_spec, row_spec] + [_const_spec(c.shape) for c in consts],
        out_specs=[pl.BlockSpec((tm * HP_ROWS, LANES), lambda i: (i, 0)), row_spec,
                   pl.BlockSpec((TOP_K, tm), lambda i: (0, i)),
                   pl.BlockSpec((TOP_K, tm), lambda i: (0, i))],
        out_shape=[jax.ShapeDtypeStruct((T * HP_ROWS, LANES), jnp.uint32),
                   jax.ShapeDtypeStruct((T, D_MODEL), F32),
                   jax.ShapeDtypeStruct((TOP_K, T), jnp.int32),
                   jax.ShapeDtypeStruct((TOP_K, T), F32)],
        compiler_params=pltpu.CompilerParams(dimension_semantics=("arbitrary",),
                                             vmem_limit_bytes=VMEM_LIMIT),
        name="post",
    )(x2, gdn_o, moba_a, *consts)


ROW_UNROLL = 8


def _experts_kernel(iblk_ref, iexp_ref, ilo_ref, ihi_ref, src_ref, dst_ref,
                    hp_ref, wg_ref, wu_ref, wd_ref, out_ref,
                    xbuf, ybuf, wgb, wub, wdb, cast_exp, ssem):
    p = pl.program_id(0)
    n_items = pl.num_programs(0)
    w = iblk_ref[p]
    slot = w % 2
    first = (p == 0) | (w != iblk_ref[jnp.maximum(p - 1, 0)])
    last = (p == n_items - 1) | (w != iblk_ref[jnp.minimum(p + 1, n_items - 1)])
    lo, hi = ilo_ref[p], ihi_ref[p]
    block_rows = MOE_BLOCK * TOK_ROWS

    def tile_at(ref, row0, rows):
        return ref.at[pl.ds(pl.multiple_of(row0, rows), rows)]

    def for_block_rows(fn):
        def group(g, c):
            for u in range(ROW_UNROLL):
                fn(g * ROW_UNROLL + u)
            return c
        lax.fori_loop(0, MOE_BLOCK // ROW_UNROLL, group, 0)

    def gather_block():
        def one(i):
            xbuf[pl.ds(pl.multiple_of(i * HP_ROWS, HP_ROWS), HP_ROWS), :] = (
                hp_ref[pl.ds(pl.multiple_of(src_ref[0, 0, i], HP_ROWS), HP_ROWS), :])
        for_block_rows(one)

    def start_scatter(s):
        for_block_rows(lambda i: pltpu.make_async_copy(
            tile_at(ybuf.at[s], i * TOK_ROWS, TOK_ROWS), tile_at(out_ref, dst_ref[0, 0, i], TOK_ROWS),
            ssem.at[s]).start())

    def wait_scatter(s):
        pltpu.make_async_copy(ybuf.at[s], out_ref.at[pl.ds(0, block_rows)], ssem.at[s]).wait()

    @pl.when(p == 0)
    def _():
        ybuf[...] = jnp.zeros_like(ybuf)
        cast_exp[0] = -1

    @pl.when(first)
    def _():
        @pl.when(w >= 2)
        def _():
            wait_scatter(slot)
        gather_block()

    @pl.when(hi > lo)
    def _():
        @pl.when(iexp_ref[p] != cast_exp[0])
        def _():
            wgb[...] = wg_ref[...].astype(BF16)
            wub[...] = wu_ref[...].astype(BF16)
            wdb[...] = wd_ref[...].astype(BF16)
            cast_exp[0] = iexp_ref[p]

        xb = _unpack_bf16_pairs(_load_token_tiles(xbuf, MOE_BLOCK)).astype(BF16)
        hg = jnp.dot(xb, wgb[...], preferred_element_type=F32)
        hu = jnp.dot(xb, wub[...], preferred_element_type=F32)
        hb = (hg * _sigmoid(hg) * hu).astype(BF16)
        y = jnp.dot(hb, wdb[...], preferred_element_type=F32)
        row = lax.broadcasted_iota(jnp.int32, (MOE_BLOCK, 1), 0)
        mine = (row >= lo) & (row < hi)
        _store_token_tiles(ybuf.at[slot], jnp.where(mine, y, _load_token_tiles(ybuf.at[slot], MOE_BLOCK)))

    @pl.when(last)
    def _():
        start_scatter(slot)

    @pl.when(p == n_items - 1)
    def _():
        @pl.when(w >= 1)
        def _():
            wait_scatter(1 - slot)
        wait_scatter(slot)


def _experts(hp, src_rows, dst_rows, item_blk, item_exp, item_lo, item_hi, wg, wu, wd):
    T = hp.shape[0] // HP_ROWS
    n_items = item_blk.shape[0]
    rows = MOE_BLOCK
    ids_spec = pl.BlockSpec((1, 1, rows), lambda p, ib, ie, il, ih: (ib[p], 0, 0), memory_space=pltpu.SMEM)

    def w_spec(shape):
        return pl.BlockSpec((None,) + shape, lambda p, ib, ie, il, ih: (ie[p], 0, 0))

    grid_spec = pltpu.PrefetchScalarGridSpec(
        num_scalar_prefetch=4,
        grid=(n_items,),
        in_specs=[ids_spec, ids_spec, _const_spec(hp.shape),
                  w_spec((D_MODEL, EXPERT_DIM)), w_spec((D_MODEL, EXPERT_DIM)), w_spec((EXPERT_DIM, D_MODEL))],
        out_specs=pl.BlockSpec(memory_space=pl.ANY),
        scratch_shapes=[pltpu.VMEM((rows * HP_ROWS, LANES), jnp.uint32),
                        pltpu.VMEM((2, rows * TOK_ROWS, LANES), F32),
                        pltpu.VMEM((D_MODEL, EXPERT_DIM), BF16), pltpu.VMEM((D_MODEL, EXPERT_DIM), BF16),
                        pltpu.VMEM((EXPERT_DIM, D_MODEL), BF16),
                        pltpu.SMEM((1,), jnp.int32),
                        pltpu.SemaphoreType.DMA((2,))],
    )
    return pl.pallas_call(
        _experts_kernel,
        grid_spec=grid_spec,
        out_shape=jax.ShapeDtypeStruct((TOP_K * T * TOK_ROWS, LANES), F32),
        compiler_params=pltpu.CompilerParams(dimension_semantics=("arbitrary",),
                                             vmem_limit_bytes=VMEM_LIMIT,
                                             disable_bounds_checks=True),
        name="experts",
    )(item_blk, item_exp, item_lo, item_hi, src_rows, dst_rows, hp, wg, wu, wd)


def _combine_kernel(res_ref, y_ref, w_ref, g_ref, b_ref, o_ref):
    tm = res_ref.shape[0]
    acc = res_ref[...]
    w = w_ref[...]
    for k in range(TOP_K):
        acc = acc + w[:, k:k + 1] * _load_token_tiles(y_ref.at[k], tm)
    o_ref[...] = _layer_norm(acc, g_ref[...], b_ref[...])


def _combine(res, y3, wts, ln_g, ln_b):
    T = res.shape[0]
    tm = COMBINE_TILE
    return pl.pallas_call(
        _combine_kernel,
        grid=(T // tm,),
        in_specs=[pl.BlockSpec((tm, D_MODEL), lambda i: (i, 0)),
                  pl.BlockSpec((TOP_K, tm * TOK_ROWS, LANES), lambda i: (0, i, 0)),
                  pl.BlockSpec((tm, TOP_K), lambda i: (i, 0)),
                  _const_spec((1, D_MODEL)), _const_spec((1, D_MODEL))],
        out_specs=pl.BlockSpec((tm, D_MODEL), lambda i: (i, 0)),
        out_shape=jax.ShapeDtypeStruct((T, D_MODEL), F32),
        compiler_params=pltpu.CompilerParams(dimension_semantics=("arbitrary",),
                                             vmem_limit_bytes=VMEM_LIMIT),
        name="combine",
    )(res, y3, wts, ln_g, ln_b)


def _dispatch_plan(eidx_t, n_tok):
    nk = n_tok * TOP_K
    n_blocks = nk // MOE_BLOCK
    e_flat = eidx_t.T.reshape(nk)
    _, order = lax.sort((e_flat, jnp.arange(nk, dtype=jnp.int32)), num_keys=1)
    experts = jnp.arange(N_EXPERTS, dtype=jnp.int32)
    counts = jnp.sum((e_flat[None, :] == experts[:, None]).astype(jnp.int32), axis=1)
    start = jnp.cumsum(counts) - counts
    pos = jnp.sort(jnp.concatenate([jnp.arange(n_blocks, dtype=jnp.int32) * MOE_BLOCK, start]))
    nxt = jnp.concatenate([pos[1:], jnp.full((1,), nk, jnp.int32)])
    item_blk = jnp.minimum(pos // MOE_BLOCK, n_blocks - 1)
    item_exp = jnp.sum((start[None, :] <= pos[:, None]).astype(jnp.int32), axis=1) - 1
    item_lo = pos - item_blk * MOE_BLOCK
    item_hi = jnp.minimum(nxt, (item_blk + 1) * MOE_BLOCK) - item_blk * MOE_BLOCK
    tok, slot_k = order // TOP_K, order % TOP_K
    shape3 = (n_blocks, 1, MOE_BLOCK)
    src_rows = (tok * HP_ROWS).reshape(shape3)
    dst_rows = ((slot_k * n_tok + tok) * TOK_ROWS).reshape(shape3)
    return src_rows, dst_rows, item_blk, item_exp, item_lo, item_hi


def kernel(x, w_in, conv_w, gdn_a_log, gdn_dt_bias, gdn_norm_w, w_gdn_proj, w_moba_proj, w_out,
           ln1_g, ln1_b, router_w, router_bias, exp_w_gate, exp_w_up, exp_w_down,
           sh_w_gate, sh_w_up, sh_w_down, ln2_g, ln2_b):
    B, T, D = x.shape
    assert B == 1 and D == D_MODEL and T % MB_BLOCK == 0
    x2 = x.reshape(T, D)

    o_gate = CONV_CH
    o_b = o_gate + GDN_HEADS * GDN_DV
    o_a = o_b + GDN_HEADS
    o_mq = o_a + GDN_HEADS
    o_gg = o_mq + (MB_HEADS + 2 * MB_KV_HEADS) * MB_HD
    w_qkv = w_in[:, :CONV_CH].astype(BF16)
    w_pack = jnp.concatenate([w_in[:, o_b:o_a]] + [w_in[:, o_a:o_mq]] * 4
                             + [jnp.zeros((D, LANES - 5 * GDN_HEADS), F32)], axis=1).astype(BF16)
    w_mb = w_in[:, o_mq:o_gg].astype(BF16)
    w_gates = jnp.concatenate([w_in[:, o_gate:o_b], w_in[:, o_gg:]], axis=1).astype(BF16)

    def lane_row(v):
        return jnp.zeros((1, LANES), F32).at[0, PK_GC:PK_GC + 4 * GDN_HEADS].set(jnp.tile(v.astype(F32), 4))


    half = ROT_DIM // 2
    inv = ROPE_THETA ** (-jnp.arange(half, dtype=F32) / half)
    ang = jnp.arange(T).astype(F32)[:, None] * inv[None, :]
    ones = jnp.ones((T, MB_HD - ROT_DIM), F32)
    cos_t = jnp.concatenate([jnp.cos(ang), jnp.cos(ang), ones], axis=1)
    sin_t = jnp.concatenate([-jnp.sin(ang), jnp.sin(ang), 0.0 * ones], axis=1)
    q, k, v, pack, gct, mq, mk, mvt, kmean = _proj(
        x2, w_qkv, w_pack, conv_w.astype(F32), lane_row(gdn_a_log), lane_row(gdn_dt_bias), w_mb, cos_t, sin_t)
    gct3 = gct.reshape(GDN_HEADS, T // GDN_CHUNK, GDN_CHUNK).transpose(1, 0, 2)
    gdn_o = _gdn(q, k, v, pack, gct3, gdn_norm_w.astype(F32).reshape(1, GDN_DV))
    moba_a = _moba(mq, mk, mvt, kmean.transpose(1, 0, 2))

    hp, res, eidx_t, wts_t = _post(
        x2, gdn_o, moba_a, w_gates, w_gdn_proj.astype(BF16), w_moba_proj.astype(BF16),
        w_out.astype(BF16), ln1_g.reshape(1, D), ln1_b.reshape(1, D),
        router_w.T, router_bias.reshape(N_EXPERTS, 1),
        sh_w_gate.astype(BF16), sh_w_up.astype(BF16), sh_w_down.astype(BF16))

    plan = _dispatch_plan(eidx_t, T)
    y = _experts(hp, *plan, exp_w_gate, exp_w_up, exp_w_down)
    y3 = y.reshape(TOP_K, T * TOK_ROWS, LANES)
    out = _combine(res, y3, wts_t.T, ln2_g.reshape(1, D), ln2_b.reshape(1, D))
    return out.reshape(B, T, D)
```
